```python
import math
import jax, jax.numpy as jnp
from jax import lax
import numpy as np

D_MODEL = 1024
BATCH = 2
SEQ = 8192
DEPTH = 4

HEAD_DIM = 64
N_HEADS_A = 6
DILATED_PATTERNS = ((128, 1), (512, 4), (2048, 16))
N_HEADS_B = 6
Q_LORA_RANK = 384
KV_LORA_RANK = 256
QK_NOPE_DIM = 64
QK_ROPE_DIM = 32
V_HEAD_DIM = 64
ROPE_BASE = 10000.0
MLA_Q_BLOCK = 128
N_HEADS_C = 4
GRID_W = 64
NA_ROWS = 8
NA_COLS = 16
NA_COL_BLOCK = 16
NA_COL_SPAN = 32

WIDTH_A = N_HEADS_A * HEAD_DIM
WIDTH_B = N_HEADS_B * V_HEAD_DIM
WIDTH_C = N_HEADS_C * HEAD_DIM
MIX_WIDTH = WIDTH_A + WIDTH_B + WIDTH_C
SPLIT_SIZES = (WIDTH_A, WIDTH_A, WIDTH_A, Q_LORA_RANK, KV_LORA_RANK, QK_ROPE_DIM, WIDTH_C, WIDTH_C, WIDTH_C)
IN_COLS = sum(SPLIT_SIZES)
D_FF = 2816
N_EXPERTS = 8
TOP_K = 2
D_FF_EXPERT = 3584
MOE_BLOCK = 128
RMS_EPS = 1e-6
NEG_INF = -1e30

kernel_name = "hybrid_dilated_mla_neighbourhood_moe_encoder"


def rms_norm(x, g):
    xf = x.astype(jnp.float32)
    y = xf * lax.rsqrt(jnp.mean(xf * xf, axis=-1, keepdims=True) + RMS_EPS)
    return (y * g.astype(jnp.float32)).astype(x.dtype)


def alibi_slopes(n_heads):
    return 2.0 ** (-8.0 * jnp.arange(1, n_heads + 1, dtype=jnp.float32) / n_heads)


def rope(x, pos):
    r = x.shape[-1]
    inv = ROPE_BASE ** (-jnp.arange(0, r, 2, dtype=jnp.float32) / r)
    ang = pos[:, None].astype(jnp.float32) * inv[None, :]
    cos, sin = jnp.cos(ang), jnp.sin(ang)
    x1 = x[..., : r // 2].astype(jnp.float32)
    x2 = x[..., r // 2:].astype(jnp.float32)
    return jnp.concatenate([x1 * cos - x2 * sin, x1 * sin + x2 * cos], axis=-1).astype(x.dtype)


def dilated_window_branch(q, k, v, dilation, half, slopes):
    B, H, S, hd = q.shape
    n = S // dilation
    nb = -(-n // half)
    npad = nb * half

    def split(a):
        return a.reshape(B, H, n, dilation, hd).transpose(0, 1, 3, 2, 4)

    qs = jnp.pad(split(q), ((0, 0),) * 3 + ((0, npad - n), (0, 0))).reshape(B, H, dilation, nb, half, hd)

    def band(a):
        ap = jnp.pad(split(a), ((0, 0),) * 3 + ((half, npad - n + half), (0, 0)))
        ap = ap.reshape(B, H, dilation, nb + 2, half, hd)
        return jnp.concatenate([ap[:, :, :, 0:nb], ap[:, :, :, 1:nb + 1], ap[:, :, :, 2:nb + 2]], axis=4)

    kb, vb = band(k), band(v)
    qpos = jnp.arange(nb)[:, None] * half + jnp.arange(half)[None, :]
    kpos = jnp.arange(nb)[:, None] * half - half + jnp.arange(3 * half)[None, :]
    rel = kpos[:, None, :] - qpos[:, :, None]
    valid = (jnp.abs(rel) <= half) & (kpos[:, None, :] >= 0) & (kpos[:, None, :] < n)
    dist = (jnp.abs(rel) * dilation).astype(jnp.float32)
    s = jnp.einsum('bhrnqd,bhrnkd->bhrnqk', qs, kb).astype(jnp.float32) * (hd ** -0.5)
    s = s - slopes[None, :, None, None, None, None] * dist
    s = jnp.where(valid, s, NEG_INF)
    m = jnp.max(s, axis=-1, keepdims=True)
    p = jnp.exp(s - m)
    l = jnp.sum(p, axis=-1, keepdims=True)
    o = jnp.einsum('bhrnqk,bhrnkd->bhrnqd', (p / l).astype(v.dtype), vb)

    def merge(a):
        xdim = a.shape[-1]
        a = a.reshape(B, H, dilation, npad, xdim)[:, :, :, :n]
        return a.transpose(0, 1, 3, 2, 4).reshape(B, H, S, xdim)

    return merge(o), merge(m)[..., 0], merge(l)[..., 0]


def dilated_attention(q, k, v):
    slopes = alibi_slopes(q.shape[1])
    outs, ms, ls = [], [], []
    for window, dilation in DILATED_PATTERNS:
        o, m, l = dilated_window_branch(q, k, v, dilation, (window // 2) // dilation, slopes)
        outs.append(o)
        ms.append(m)
        ls.append(l)
    m_all = jnp.stack(ms)
    l_all = jnp.stack(ls)
    w = l_all * jnp.exp(m_all - jnp.max(m_all, axis=0, keepdims=True))
    w = w / jnp.sum(w, axis=0, keepdims=True)
    return jnp.einsum('gbhs,gbhsd->bhsd', w.astype(q.dtype), jnp.stack(outs))


def mla_attention(c_q, c_kv, k_rope_raw, g_q, g_kv, w_uq, w_ukv):
    B, S, _ = c_q.shape
    H = N_HEADS_B
    q = jnp.einsum('bsr,rf->bsf', rms_norm(c_q, g_q), w_uq)
    q = q.reshape(B, S, H, QK_NOPE_DIM + QK_ROPE_DIM).transpose(0, 2, 1, 3)
    kv = jnp.einsum('bsr,rf->bsf', rms_norm(c_kv, g_kv), w_ukv)
    kv = kv.reshape(B, S, H, QK_NOPE_DIM + V_HEAD_DIM).transpose(0, 2, 1, 3)
    pos = jnp.arange(S)
    q_nope, q_rope = q[..., :QK_NOPE_DIM], rope(q[..., QK_NOPE_DIM:], pos)
    k_nope, v = kv[..., :QK_NOPE_DIM], kv[..., QK_NOPE_DIM:]
    k_rope = rope(k_rope_raw, pos)
    scale = (QK_NOPE_DIM + QK_ROPE_DIM) ** -0.5
    nq = S // MLA_Q_BLOCK

    def to_blocks(a):
        return a.reshape(B, H, nq, MLA_Q_BLOCK, a.shape[-1]).transpose(2, 0, 1, 3, 4)

    def block(args):
        qn, qr = args
        s = (jnp.einsum('bhqd,bhkd->bhqk', qn, k_nope)
             + jnp.einsum('bhqr,bkr->bhqk', qr, k_rope)).astype(jnp.float32) * scale
        p = jax.nn.softmax(s, axis=-1).astype(v.dtype)
        return jnp.einsum('bhqk,bhkd->bhqd', p, v)

    o = lax.map(block, (to_blocks(q_nope), to_blocks(q_rope)))
    return o.transpose(1, 2, 0, 3, 4).reshape(B, H, S, V_HEAD_DIM)


def neighbourhood_attention(q, k, v, rpb):
    B, H, S, hd = q.shape
    rows = S // GRID_W
    kr = min(NA_ROWS, rows)
    ncb = GRID_W // NA_COL_BLOCK
    r = jnp.arange(rows)
    row_start = jnp.clip(r - kr // 2, 0, rows - kr)
    key_rows = row_start[:, None] + jnp.arange(kr)[None, :]
    cb = jnp.arange(ncb) * NA_COL_BLOCK
    col_start = jnp.clip(cb - NA_COLS // 2, 0, GRID_W - NA_COL_SPAN)
    key_cols = col_start[:, None] + jnp.arange(NA_COL_SPAN)[None, :]
    flat = key_rows[:, None, :, None] * GRID_W + key_cols[None, :, None, :]
    kg = k[:, :, flat]
    vg = v[:, :, flat]
    qg = q.reshape(B, H, rows, ncb, NA_COL_BLOCK, hd)
    s = jnp.einsum('bhrjqd,bhrjakd->bhrjqak', qg, kg).astype(jnp.float32) * (hd ** -0.5)
    qcol = cb[:, None] + jnp.arange(NA_COL_BLOCK)[None, :]
    win_start = jnp.clip(qcol - NA_COLS // 2, 0, GRID_W - NA_COLS)
    col_ok = (key_cols[:, None, :] >= win_start[:, :, None]) & (key_cols[:, None, :] < win_start[:, :, None] + NA_COLS)
    drow = key_rows - r[:, None]
    dcol = jnp.clip(key_cols[:, None, :] - qcol[:, :, None], -(NA_COLS - 1), NA_COLS - 1)
    bias = rpb[:, drow[:, None, None, :, None] + (NA_ROWS - 1), dcol[None, :, :, None, :] + (NA_COLS - 1)]
    s = s + bias[None].astype(jnp.float32)
    s = jnp.where(col_ok[None, None, None, :, :, None, :], s, NEG_INF)
    shp = s.shape
    p = jax.nn.softmax(s.reshape(shp[:5] + (kr * NA_COL_SPAN,)), axis=-1).reshape(shp)
    o = jnp.einsum('bhrjqak,bhrjakd->bhrjqd', p.astype(v.dtype), vg)
    return o.reshape(B, H, S, hd)


def hybrid_mixer(h, w_in, g_q, g_kv, w_uq, w_ukv, rpb, g_out_a, g_out_b, g_out_c, w_o):
    B, S, _ = h.shape
    proj = jnp.einsum('bsd,df->bsf', h, w_in)
    bounds = np.cumsum(SPLIT_SIZES)[:-1].tolist()
    qa, ka, va, c_q, c_kv, k_rope, qc, kc, vc = jnp.split(proj, bounds, axis=-1)

    def heads(a, n_heads):
        return a.reshape(B, S, n_heads, HEAD_DIM).transpose(0, 2, 1, 3)

    def flat(o):
        return o.transpose(0, 2, 1, 3).reshape(B, S, -1)

    out_a = dilated_attention(heads(qa, N_HEADS_A), heads(ka, N_HEADS_A), heads(va, N_HEADS_A))
    out_b = mla_attention(c_q, c_kv, k_rope, g_q, g_kv, w_uq, w_ukv)
    out_c = neighbourhood_attention(heads(qc, N_HEADS_C), heads(kc, N_HEADS_C), heads(vc, N_HEADS_C), rpb)
    y = jnp.concatenate([rms_norm(flat(out_a), g_out_a),
                         rms_norm(flat(out_b), g_out_b),
                         rms_norm(flat(out_c), g_out_c)], axis=-1)
    return jnp.einsum('bsf,fd->bsd', y, w_o)


def swiglu(h, w1, w3, w2):
    return (jax.nn.silu(h @ w1) * (h @ w3)) @ w2


def moe_swiglu(h, w_router, w1, w3, w2):
    B, S, D = h.shape
    x = h.reshape(-1, D)
    n_tok = x.shape[0]
    logits = jnp.einsum('nd,de->ne', x.astype(jnp.float32), w_router.astype(jnp.float32))
    top_logit, top_e = lax.top_k(logits, TOP_K)
    gates = jax.nn.softmax(top_logit, axis=-1)
    n_assign = n_tok * TOP_K
    flat_e = top_e.reshape(n_assign)
    order = jnp.argsort(flat_e)
    sorted_e = flat_e[order]
    sorted_tok = order // TOP_K
    sorted_gate = gates.reshape(n_assign)[order]
    counts = jnp.bincount(flat_e, length=N_EXPERTS)
    padded = (counts + MOE_BLOCK - 1) // MOE_BLOCK * MOE_BLOCK
    start = jnp.cumsum(counts) - counts
    padded_end = jnp.cumsum(padded)
    padded_start = padded_end - padded
    dest = padded_start[sorted_e] + jnp.arange(n_assign) - start[sorted_e]
    n_blocks = -(-(n_assign + N_EXPERTS * (MOE_BLOCK - 1)) // MOE_BLOCK)
    n_rows = n_blocks * MOE_BLOCK
    row_tok = jnp.zeros((n_rows,), jnp.int32).at[dest].set(sorted_tok.astype(jnp.int32))
    block_e = jnp.minimum(jnp.searchsorted(padded_end, jnp.arange(n_blocks) * MOE_BLOCK, side='right'), N_EXPERTS - 1)
    xb = x[row_tok].reshape(n_blocks, MOE_BLOCK, D)

    def expert_block(args):
        xi, e = args
        return swiglu(xi, w1[e], w3[e], w2[e])

    yb = lax.map(expert_block, (xb, block_e)).reshape(n_rows, D)
    y = jnp.zeros_like(x).at[sorted_tok].add(sorted_gate[:, None].astype(x.dtype) * yb[dest])
    return y.reshape(B, S, D)


def setup_inputs(seed: int = 0) -> dict:
    key = jax.random.key(seed)
    ks = jax.random.split(key, 24)
    f32 = jnp.float32
    n_dense = (DEPTH + 1) // 2
    n_moe = DEPTH // 2
    res = (2 * DEPTH) ** -0.5

    def normal(k, shape, fan_in, gain=1.0):
        return jax.random.normal(k, shape, f32) * (gain * fan_in ** -0.5)

    def norm_gain(k, shape):
        return 1.0 + 0.02 * jax.random.normal(k, shape, f32)

    return {
        "x": jax.random.normal(ks[0], (BATCH, SEQ, D_MODEL), f32),
        "g_mix": norm_gain(ks[1], (DEPTH, D_MODEL)),
        "w_in": normal(ks[2], (DEPTH, D_MODEL, IN_COLS), D_MODEL),
        "g_q": norm_gain(ks[3], (DEPTH, Q_LORA_RANK)),
        "g_kv": norm_gain(ks[4], (DEPTH, KV_LORA_RANK)),
        "w_uq": normal(ks[5], (DEPTH, Q_LORA_RANK, N_HEADS_B * (QK_NOPE_DIM + QK_ROPE_DIM)), Q_LORA_RANK),
        "w_ukv": normal(ks[6], (DEPTH, KV_LORA_RANK, N_HEADS_B * (QK_NOPE_DIM + V_HEAD_DIM)), KV_LORA_RANK),
        "rpb": 0.1 * jax.random.normal(ks[7], (DEPTH, N_HEADS_C, 2 * NA_ROWS - 1, 2 * NA_COLS - 1), f32),
        "g_out_a": norm_gain(ks[8], (DEPTH, WIDTH_A)),
        "g_out_b": norm_gain(ks[9], (DEPTH, WIDTH_B)),
        "g_out_c": norm_gain(ks[10], (DEPTH, WIDTH_C)),
        "w_o": normal(ks[11], (DEPTH, MIX_WIDTH, D_MODEL), MIX_WIDTH, res),
        "g_ffn": norm_gain(ks[12], (DEPTH, D_MODEL)),
        "w1": normal(ks[13], (n_dense, D_MODEL, D_FF), D_MODEL),
        "w3": normal(ks[14], (n_dense, D_MODEL, D_FF), D_MODEL),
        "w2": normal(ks[15], (n_dense, D_FF, D_MODEL), D_FF, res),
        "w_router": normal(ks[16], (n_moe, D_MODEL, N_EXPERTS), D_MODEL),
        "e_w1": normal(ks[17], (n_moe, N_EXPERTS, D_MODEL, D_FF_EXPERT), D_MODEL),
        "e_w3": normal(ks[18], (n_moe, N_EXPERTS, D_MODEL, D_FF_EXPERT), D_MODEL),
        "e_w2": normal(ks[19], (n_moe, N_EXPERTS, D_FF_EXPERT, D_MODEL), D_FF_EXPERT, res),
        "g_final": norm_gain(ks[20], (D_MODEL,)),
    }


def reference(x, g_mix, w_in, g_q, g_kv, w_uq, w_ukv, rpb, g_out_a, g_out_b, g_out_c, w_o,
              g_ffn, w1, w3, w2, w_router, e_w1, e_w3, e_w2, g_final):
    for layer in range(DEPTH):
        h = rms_norm(x, g_mix[layer])
        x = x + hybrid_mixer(h, w_in[layer], g_q[layer], g_kv[layer], w_uq[layer], w_ukv[layer], rpb[layer],
                             g_out_a[layer], g_out_b[layer], g_out_c[layer], w_o[layer])
        h = rms_norm(x, g_ffn[layer])
        j = layer // 2
        if layer % 2 == 0:
            x = x + swiglu(h, w1[j], w3[j], w2[j])
        else:
            x = x + moe_swiglu(h, w_router[j], e_w1[j], e_w3[j], e_w2[j])
    return rms_norm(x, g_final)
```

```python
import functools
import math

import numpy as np
import jax
import jax.numpy as jnp
from jax import lax
from jax.experimental import pallas as pl
from jax.experimental.pallas import tpu as pltpu

F32 = jnp.float32
BF16 = jnp.bfloat16

LANES = 128
V7X_VMEM_LIMIT_BYTES = 52 * 1024 * 1024

HEAD_DIM = 64
N_HEADS_A = 6
DILATIONS = (1, 4, 16)
BAND_HALF = 64
N_HEADS_B = 6
Q_LORA = 384
KV_LORA = 256
QK_NOPE = 64
QK_ROPE = 32
V_HEAD = 64
ROPE_BASE = 10000.0
N_HEADS_C = 4
GRID_W = 64
NA_ROWS = 8
NA_COLS = 16
WIDTH_A = N_HEADS_A * HEAD_DIM
WIDTH_B = N_HEADS_B * V_HEAD
WIDTH_C = N_HEADS_C * HEAD_DIM
N_EXPERTS = 8
TOP_K = 2
RMS_EPS = 1e-6
NEG_INF = -1e30

ROW_TILE = 512
MLA_TQ = 256
MLA_TK = 512
BAND_TQ = 128
NA_TILE_ROWS = 2
NA_KEY_ROWS = 10
MOE_TM = 512
MOE_TF = 896
FFN_TF = 1408
COMBINE_TM = 256


def _cparams(semantics):
    return pltpu.CompilerParams(dimension_semantics=semantics, vmem_limit_bytes=V7X_VMEM_LIMIT_BYTES)


def _rms(x, g):
    return x * lax.rsqrt(jnp.mean(x * x, axis=-1, keepdims=True) + RMS_EPS) * g


def _dot(a, b):
    return jnp.dot(a, b, preferred_element_type=F32)


def _dot_nt(a, b):
    return lax.dot_general(a, b, (((1,), (1,)), ((), ())), preferred_element_type=F32)


def _proj_kernel(x_ref, g_ref, wa_ref, wc_ref, wb_ref, gq_ref, gkv_ref, wq1_ref, wq2_ref, wk_ref, wv_ref,
                 ct_ref, st_ref, pa_ref, pc_ref, qm_ref, km_ref, vm_ref):
    h = _rms(x_ref[...], g_ref[...]).astype(BF16)
    pa_ref[...] = _dot(h, wa_ref[...]).astype(BF16)
    pc_ref[...] = _dot(h, wc_ref[...]).astype(BF16)
    pb = _dot(h, wb_ref[...])
    hq = _rms(pb[:, :Q_LORA], gq_ref[...]).astype(BF16)
    hkv = _rms(pb[:, Q_LORA:Q_LORA + KV_LORA], gkv_ref[...]).astype(BF16)
    r1 = pb[:, Q_LORA + KV_LORA:Q_LORA + KV_LORA + LANES]
    r2 = pb[:, Q_LORA + KV_LORA + LANES:]
    ct = ct_ref[...]
    st = st_ref[...]
    qa = _dot(hq, wq1_ref[...])
    qb = _dot(hq, wq2_ref[...])
    kn = _dot(hkv, wk_ref[...])
    kr = r1 * ct + r2 * st
    for hd in range(N_HEADS_B):
        sl = slice(hd * LANES, (hd + 1) * LANES)
        qm_ref[:, sl] = (qa[:, sl] * ct + qb[:, sl] * st).astype(BF16)
        km_ref[:, sl] = (kn[:, sl] + kr).astype(BF16)
    vm_ref[...] = _dot(hkv, wv_ref[...]).astype(BF16)


def _proj_call(x, g, wa, wc, wb, gq, gkv, wq1, wq2, wk, wv, ctab, stab, seq):
    n, d = x.shape
    tm = ROW_TILE
    tiles_per_seq = seq // tm
    full = lambda a: pl.BlockSpec(a.shape, lambda i: (0,) * a.ndim)
    row = lambda w: pl.BlockSpec((tm, w), lambda i: (i, 0))
    tab = pl.BlockSpec((tm, LANES), lambda i: (i % tiles_per_seq, 0))
    hb = N_HEADS_B * LANES
    return pl.pallas_call(
        _proj_kernel,
        grid=(n // tm,),
        in_specs=[row(d), full(g), full(wa), full(wc), full(wb), full(gq), full(gkv), full(wq1), full(wq2),
                  full(wk), full(wv), tab, tab],
        out_specs=[row(3 * WIDTH_A), row(3 * WIDTH_C), row(hb), row(hb), row(WIDTH_B)],
        out_shape=[jax.ShapeDtypeStruct((n, 3 * WIDTH_A), BF16), jax.ShapeDtypeStruct((n, 3 * WIDTH_C), BF16),
                   jax.ShapeDtypeStruct((n, hb), BF16), jax.ShapeDtypeStruct((n, hb), BF16),
                   jax.ShapeDtypeStruct((n, WIDTH_B), BF16)],
        compiler_params=_cparams(("parallel",)),
        name="proj",
    )(x, g, wa, wc, wb, gq, gkv, wq1, wq2, wk, wv, ctab, stab)


def _band_kernel(q_ref, k_ref, v_ref, o_ref, lse_ref, *, dilation, n):
    tq = BAND_TQ
    tkw = tq + 2 * BAND_HALF
    i0 = pl.program_id(2) * tq
    start = pl.multiple_of(jnp.clip(i0 - BAND_HALF, 0, n - tkw), BAND_HALF)
    q = q_ref[...]
    k = k_ref[pl.ds(start, tkw), :]
    v = v_ref[pl.ds(start, tkw), :]
    rel = (start - i0) + lax.broadcasted_iota(jnp.int32, (tq, tkw), 1) - lax.broadcasted_iota(jnp.int32, (tq, tkw), 0)
    dist_i = jnp.abs(rel)
    valid = dist_i <= BAND_HALF
    dist = dist_i.astype(F32)
    scale = HEAD_DIM ** -0.5
    outs, lses = [], []
    for h in range(N_HEADS_A):
        sl = slice(h * HEAD_DIM, (h + 1) * HEAD_DIM)
        slope = 2.0 ** (-8.0 * (h + 1) / N_HEADS_A)
        s = _dot_nt(q[:, sl], k[:, sl]) * scale - (slope * dilation) * dist
        s = jnp.where(valid, s, NEG_INF)
        m = jnp.max(s, axis=-1, keepdims=True)
        p = jnp.exp(s - m)
        l = jnp.sum(p, axis=-1, keepdims=True)
        o = _dot(p.astype(BF16), v[:, sl]) / l
        outs.append(o)
        lses.append(jnp.broadcast_to(m + jnp.log(l), (tq, HEAD_DIM)))
    o_ref[...] = jnp.concatenate(outs, axis=-1)
    lse_ref[...] = jnp.concatenate(lses, axis=-1)


def _band_call(pa_view, dilation):
    b, n, _ = pa_view.shape
    tq = BAND_TQ
    w = WIDTH_A
    qspec = pl.BlockSpec((None, tq, w), lambda bb, r, i: (bb, i, 3 * r))
    kspec = pl.BlockSpec((None, n, w), lambda bb, r, i: (bb, 0, 3 * r + 1))
    vspec = pl.BlockSpec((None, n, w), lambda bb, r, i: (bb, 0, 3 * r + 2))
    ospec = pl.BlockSpec((None, tq, w), lambda bb, r, i: (bb, i, r))
    shape = jax.ShapeDtypeStruct((b, n, dilation * w), F32)
    return pl.pallas_call(
        functools.partial(_band_kernel, dilation=dilation, n=n),
        grid=(b, dilation, n // tq),
        in_specs=[qspec, kspec, vspec],
        out_specs=[ospec, ospec],
        out_shape=[shape, shape],
        compiler_params=_cparams(("parallel", "parallel", "parallel")),
        name=f"band_d{dilation}",
    )(pa_view, pa_view, pa_view)


def _mla_kernel(q_ref, k_ref, v_ref, o_ref):
    tq = q_ref.shape[0]
    seq = k_ref.shape[0]
    tk = MLA_TK
    scale = (QK_NOPE + QK_ROPE) ** -0.5
    outs = []
    for hh in range(2):
        q = q_ref[:, hh * LANES:(hh + 1) * LANES]

        def body(j, carry, hh=hh, q=q):
            m, l, acc = carry
            ks = pl.multiple_of(j * tk, tk)
            k = k_ref[pl.ds(ks, tk), hh * LANES:(hh + 1) * LANES]
            v = v_ref[pl.ds(ks, tk), hh * V_HEAD:(hh + 1) * V_HEAD]
            s = _dot_nt(q, k) * scale
            m_new = jnp.maximum(m, jnp.max(s, axis=-1, keepdims=True))
            alpha = jnp.exp(m - m_new)
            p = jnp.exp(s - m_new)
            l_new = alpha * l + jnp.sum(p, axis=-1, keepdims=True)
            acc_new = alpha * acc + _dot(p.astype(BF16), v)
            return m_new, l_new, acc_new

        init = (jnp.full((tq, 1), NEG_INF, F32), jnp.zeros((tq, 1), F32), jnp.zeros((tq, V_HEAD), F32))
        m, l, acc = lax.fori_loop(0, seq // tk, body, init)
        outs.append(acc / l)
    o_ref[...] = jnp.concatenate(outs, axis=-1)


def _mla_call(qm, km, vm, batch, seq):
    tq = MLA_TQ
    qspec = pl.BlockSpec((None, tq, 2 * LANES), lambda b, hp, i: (b, i, hp))
    kspec = pl.BlockSpec((None, seq, 2 * LANES), lambda b, hp, i: (b, 0, hp))
    vspec = pl.BlockSpec((None, seq, 2 * V_HEAD), lambda b, hp, i: (b, 0, hp))
    ospec = pl.BlockSpec((None, tq, 2 * V_HEAD), lambda b, hp, i: (b, i, hp))
    return pl.pallas_call(
        _mla_kernel,
        grid=(batch, N_HEADS_B // 2, seq // tq),
        in_specs=[qspec, kspec, vspec],
        out_specs=ospec,
        out_shape=jax.ShapeDtypeStruct((batch, seq, WIDTH_B), F32),
        compiler_params=_cparams(("parallel", "parallel", "parallel")),
        name="mla",
    )(qm.reshape(batch, seq, -1), km.reshape(batch, seq, -1), vm.reshape(batch, seq, -1))


def _na_variant(i, n_tiles):
    return jnp.minimum(i, 2) + jnp.maximum(i - (n_tiles - 3), 0)


def _na_kernel(q_ref, k_ref, v_ref, tab_ref, o_ref, *, rows):
    tq = NA_TILE_ROWS * GRID_W
    tkw = NA_KEY_ROWS * GRID_W
    i = pl.program_id(1)
    base = jnp.clip(i * NA_TILE_ROWS - NA_ROWS // 2, 0, rows - NA_KEY_ROWS)
    start = pl.multiple_of(base * GRID_W, GRID_W)
    q = q_ref[...]
    k = k_ref[pl.ds(start, tkw), :]
    v = v_ref[pl.ds(start, tkw), :]
    scale = HEAD_DIM ** -0.5
    outs = []
    for h in range(N_HEADS_C):
        sl = slice(h * HEAD_DIM, (h + 1) * HEAD_DIM)
        s = _dot_nt(q[:, sl], k[:, sl]) * scale + tab_ref[h]
        m = jnp.max(s, axis=-1, keepdims=True)
        p = jnp.exp(s - m)
        l = jnp.sum(p, axis=-1, keepdims=True)
        outs.append(_dot(p.astype(BF16), v[:, sl]) / l)
    o_ref[...] = jnp.concatenate(outs, axis=-1)


def _na_tables(rpb, rows):
    r0 = np.array([0, 2, 4, rows - 4, rows - 2])
    base = np.clip(r0 - NA_ROWS // 2, 0, rows - NA_KEY_ROWS)
    r = r0[:, None] + np.arange(NA_TILE_ROWS)[None, :]
    row_start = np.clip(r - NA_ROWS // 2, 0, rows - NA_ROWS)
    krow = base[:, None] + np.arange(NA_KEY_ROWS)[None, :]
    drow = krow[:, None, :] - r[:, :, None]
    row_ok = (krow[:, None, :] >= row_start[:, :, None]) & (krow[:, None, :] < row_start[:, :, None] + NA_ROWS)
    c = np.arange(GRID_W)
    win_start = np.clip(c - NA_COLS // 2, 0, GRID_W - NA_COLS)
    col_ok = (c[None, :] >= win_start[:, None]) & (c[None, :] < win_start[:, None] + NA_COLS)
    dcol = np.clip(c[None, :] - c[:, None], -(NA_COLS - 1), NA_COLS - 1)
    ok = row_ok[:, :, None, :, None] & col_ok[None, None, :, None, :]
    di = np.clip(drow, -(NA_ROWS - 1), NA_ROWS - 1) + (NA_ROWS - 1)
    bias = rpb[:, di[:, :, None, :, None], (dcol + NA_COLS - 1)[None, None, :, None, :]]
    tab = jnp.where(jnp.asarray(ok)[None], bias.astype(F32), NEG_INF)
    tab = tab.reshape(N_HEADS_C, 5, NA_TILE_ROWS * GRID_W, NA_KEY_ROWS * GRID_W)
    return tab.transpose(1, 0, 2, 3)


def _na_call(pc, tab, batch, seq):
    rows = seq // GRID_W
    tq = NA_TILE_ROWS * GRID_W
    n_tiles = rows // NA_TILE_ROWS
    w = WIDTH_C
    qspec = pl.BlockSpec((None, tq, w), lambda b, i: (b, i, 0))
    kspec = pl.BlockSpec((None, seq, w), lambda b, i: (b, 0, 1))
    vspec = pl.BlockSpec((None, seq, w), lambda b, i: (b, 0, 2))
    tspec = pl.BlockSpec((None,) + tab.shape[1:], lambda b, i: (_na_variant(i, n_tiles), 0, 0, 0))
    ospec = pl.BlockSpec((None, tq, w), lambda b, i: (b, i, 0))
    pc3 = pc.reshape(batch, seq, 3 * w)
    return pl.pallas_call(
        functools.partial(_na_kernel, rows=rows),
        grid=(batch, n_tiles),
        in_specs=[qspec, kspec, vspec, tspec],
        out_specs=ospec,
        out_shape=jax.ShapeDtypeStruct((batch, seq, w), F32),
        compiler_params=_cparams(("parallel", "parallel")),
        name="natten",
    )(pc3, pc3, pc3, tab)


def _mix_out_kernel(x_ref, o1_ref, o2_ref, o3_ref, l1_ref, l2_ref, l3_ref, ob_ref, oc_ref,
                    ga_ref, gb_ref, gc_ref, wo_ref, out_ref):
    l1, l2, l3 = l1_ref[...], l2_ref[...], l3_ref[...]
    mx = jnp.maximum(jnp.maximum(l1, l2), l3)
    w1, w2, w3 = jnp.exp(l1 - mx), jnp.exp(l2 - mx), jnp.exp(l3 - mx)
    oa = (w1 * o1_ref[...] + w2 * o2_ref[...] + w3 * o3_ref[...]) / (w1 + w2 + w3)
    ya = _rms(oa, ga_ref[...]).astype(BF16)
    yb = _rms(ob_ref[...], gb_ref[...]).astype(BF16)
    yc = _rms(oc_ref[...], gc_ref[...]).astype(BF16)
    y = _dot(ya, wo_ref[:WIDTH_A, :])
    y = y + _dot(yb, wo_ref[WIDTH_A:WIDTH_A + WIDTH_B, :])
    y = y + _dot(yc, wo_ref[WIDTH_A + WIDTH_B:, :])
    out_ref[...] = x_ref[...] + y


def _mix_out_call(x, o_parts, lse_parts, ob, oc, ga, gb, gc, wo):
    n, d = x.shape
    tm = ROW_TILE
    full = lambda a: pl.BlockSpec(a.shape, lambda i: (0,) * a.ndim)
    row = lambda w: pl.BlockSpec((tm, w), lambda i: (i, 0))
    return pl.pallas_call(
        _mix_out_kernel,
        grid=(n // tm,),
        in_specs=[row(d)] + [row(WIDTH_A)] * 6 + [row(WIDTH_B), row(WIDTH_C), full(ga), full(gb), full(gc), full(wo)],
        out_specs=row(d),
        out_shape=jax.ShapeDtypeStruct((n, d), F32),
        compiler_params=_cparams(("parallel",)),
        name="mix_out",
    )(x, *o_parts, *lse_parts, ob, oc, ga, gb, gc, wo)


def _silu(u):
    return u * (1.0 / (1.0 + jnp.exp(-u)))


def _ffn_kernel(x_ref, g_ref, w1_ref, w3_ref, w2_ref, out_ref, h_ref, acc_ref):
    f = pl.program_id(1)

    @pl.when(f == 0)
    def _():
        h_ref[...] = _rms(x_ref[...], g_ref[...]).astype(BF16)
        acc_ref[...] = jnp.zeros_like(acc_ref)

    h = h_ref[...]
    a = (_silu(_dot(h, w1_ref[...])) * _dot(h, w3_ref[...])).astype(BF16)
    acc_ref[...] += _dot(a, w2_ref[...])

    @pl.when(f == pl.num_programs(1) - 1)
    def _():
        out_ref[...] = x_ref[...] + acc_ref[...]


def _ffn_call(x, g, w1, w3, w2):
    n, d = x.shape
    ff = w1.shape[1]
    tm, tf = ROW_TILE, FFN_TF
    return pl.pallas_call(
        _ffn_kernel,
        grid=(n // tm, ff // tf),
        in_specs=[pl.BlockSpec((tm, d), lambda i, f: (i, 0)), pl.BlockSpec((1, d), lambda i, f: (0, 0)),
                  pl.BlockSpec((d, tf), lambda i, f: (0, f)), pl.BlockSpec((d, tf), lambda i, f: (0, f)),
                  pl.BlockSpec((tf, d), lambda i, f: (f, 0))],
        out_specs=pl.BlockSpec((tm, d), lambda i, f: (i, 0)),
        out_shape=jax.ShapeDtypeStruct((n, d), F32),
        scratch_shapes=[pltpu.VMEM((tm, d), BF16), pltpu.VMEM((tm, d), F32)],
        compiler_params=_cparams(("parallel", "arbitrary")),
        name="ffn",
    )(x, g, w1, w3, w2)


def _router_kernel(x_ref, g_ref, wr_ref, h_ref, e_ref, gate_ref):
    h = _rms(x_ref[...], g_ref[...])
    h_ref[...] = h
    logits = jnp.dot(h, wr_ref[...], precision=lax.Precision.HIGHEST, preferred_element_type=F32)
    lane = lax.broadcasted_iota(jnp.int32, logits.shape, 1).astype(F32)
    logits = jnp.where(lane < N_EXPERTS, logits, -jnp.inf)
    m1 = jnp.max(logits, axis=-1, keepdims=True)
    i1 = jnp.min(jnp.where(logits == m1, lane, float(LANES)), axis=-1, keepdims=True)
    rest = jnp.where(lane == i1, -jnp.inf, logits)
    m2 = jnp.max(rest, axis=-1, keepdims=True)
    i2 = jnp.min(jnp.where(rest == m2, lane, float(LANES)), axis=-1, keepdims=True)
    e = jnp.exp(m2 - m1)
    den = 1.0 + e
    e_ref[...] = jnp.where(lane == 0.0, i1, jnp.where(lane == 1.0, i2, 0.0)).astype(jnp.int32)
    gate_ref[...] = jnp.where(lane == 0.0, 1.0 / den, jnp.where(lane == 1.0, e / den, 0.0))


def _router_call(x, g, wr_pad):
    n, d = x.shape
    tm = ROW_TILE
    row = lambda w: pl.BlockSpec((tm, w), lambda i: (i, 0))
    full = lambda a: pl.BlockSpec(a.shape, lambda i: (0,) * a.ndim)
    return pl.pallas_call(
        _router_kernel,
        grid=(n // tm,),
        in_specs=[row(d), full(g), full(wr_pad)],
        out_specs=[row(d), row(LANES), row(LANES)],
        out_shape=[jax.ShapeDtypeStruct((n, d), F32), jax.ShapeDtypeStruct((n, LANES), jnp.int32),
                   jax.ShapeDtypeStruct((n, LANES), F32)],
        compiler_params=_cparams(("parallel",)),
        name="router",
    )(x, g, wr_pad)


def _row_copy(src_hbm, row, dst_ref, r, sem):
    return pltpu.make_async_copy(src_hbm.at[pl.ds(row, 1)], dst_ref.at[pl.ds(r, 1)], sem)


def _start_row_gather(idx_ref, base, src_hbm, dst_ref, sem, count):
    def body(r, c):
        _row_copy(src_hbm, idx_ref[base + r], dst_ref, r, sem).start()
        return c
    lax.fori_loop(0, count, body, 0)


def _wait_row_gather(src_hbm, dst_ref, sem, count):
    def body(r, c):
        _row_copy(src_hbm, 0, dst_ref, r, sem).wait()
        return c
    lax.fori_loop(0, count, body, 0)


def _moe_ffn_kernel(tile_e_ref, tile_ok_ref, row_tok_ref, h_hbm, w1_ref, w3_ref, w2_ref, out_ref,
                    xg_ref, hb_ref, acc_ref, sem):
    i = pl.program_id(0)
    f = pl.program_id(1)
    n_tiles = pl.num_programs(0)
    tm = MOE_TM
    slot = i % 2

    @pl.when(f == 0)
    def _():
        @pl.when(i == 0)
        def _():
            _start_row_gather(row_tok_ref, 0, h_hbm, xg_ref.at[0], sem.at[0], tm)

        _wait_row_gather(h_hbm, xg_ref.at[slot], sem.at[slot], tm)

        @pl.when(i + 1 < n_tiles)
        def _():
            _start_row_gather(row_tok_ref, (i + 1) * tm, h_hbm, xg_ref.at[1 - slot], sem.at[1 - slot], tm)

        hb_ref[...] = xg_ref[slot].astype(BF16)
        acc_ref[...] = jnp.zeros_like(acc_ref)

    @pl.when(tile_ok_ref[i] != 0)
    def _():
        h = hb_ref[...]
        a = (_silu(_dot(h, w1_ref[...])) * _dot(h, w3_ref[...])).astype(BF16)
        acc_ref[...] += _dot(a, w2_ref[...])

    @pl.when(f == pl.num_programs(1) - 1)
    def _():
        out_ref[...] = acc_ref[...]


def _moe_ffn_call(tile_e, tile_ok, row_tok, h, w1, w3, w2):
    n, d = h.shape
    n_tiles = tile_e.shape[0]
    ff = w1.shape[2]
    tm, tf = MOE_TM, MOE_TF
    grid_spec = pltpu.PrefetchScalarGridSpec(
        num_scalar_prefetch=3,
        grid=(n_tiles, ff // tf),
        in_specs=[pl.BlockSpec(memory_space=pl.ANY),
                  pl.BlockSpec((None, d, tf), lambda i, f, te, tv, rt: (te[i], 0, f)),
                  pl.BlockSpec((None, d, tf), lambda i, f, te, tv, rt: (te[i], 0, f)),
                  pl.BlockSpec((None, tf, d), lambda i, f, te, tv, rt: (te[i], f, 0))],
        out_specs=pl.BlockSpec((tm, d), lambda i, f, te, tv, rt: (i, 0)),
        scratch_shapes=[pltpu.VMEM((2, tm, d), F32), pltpu.VMEM((tm, d), BF16), pltpu.VMEM((tm, d), F32),
                        pltpu.SemaphoreType.DMA((2,))],
    )
    return pl.pallas_call(
        _moe_ffn_kernel,
        grid_spec=grid_spec,
        out_shape=jax.ShapeDtypeStruct((n_tiles * tm, d), F32),
        compiler_params=_cparams(("arbitrary", "arbitrary")),
        name="moe_ffn",
    )(tile_e, tile_ok, row_tok, h, w1, w3, w2)


def _combine_kernel(d0_ref, d1_ref, y_hbm, x_ref, gate_ref, out_ref, b0_ref, b1_ref, sem):
    i = pl.program_id(0)
    n_tiles = pl.num_programs(0)
    tm = COMBINE_TM
    slot = i % 2

    def start(tile, s):
        _start_row_gather(d0_ref, tile * tm, y_hbm, b0_ref.at[s], sem.at[0, s], tm)
        _start_row_gather(d1_ref, tile * tm, y_hbm, b1_ref.at[s], sem.at[1, s], tm)

    @pl.when(i == 0)
    def _():
        start(0, 0)

    _wait_row_gather(y_hbm, b0_ref.at[slot], sem.at[0, slot], tm)
    _wait_row_gather(y_hbm, b1_ref.at[slot], sem.at[1, slot], tm)

    @pl.when(i + 1 < n_tiles)
    def _():
        start(i + 1, 1 - slot)

    gates = gate_ref[...]
    out_ref[...] = x_ref[...] + (gates[:, 0:1] * b0_ref[slot] + gates[:, 1:2] * b1_ref[slot])


def _combine_call(d0, d1, y, x, gates):
    n, d = x.shape
    tm = COMBINE_TM
    grid_spec = pltpu.PrefetchScalarGridSpec(
        num_scalar_prefetch=2,
        grid=(n // tm,),
        in_specs=[pl.BlockSpec(memory_space=pl.ANY),
                  pl.BlockSpec((tm, d), lambda i, a, b: (i, 0)),
                  pl.BlockSpec((tm, LANES), lambda i, a, b: (i, 0))],
        out_specs=pl.BlockSpec((tm, d), lambda i, a, b: (i, 0)),
        scratch_shapes=[pltpu.VMEM((2, tm, d), F32), pltpu.VMEM((2, tm, d), F32), pltpu.SemaphoreType.DMA((2, 2))],
    )
    return pl.pallas_call(
        _combine_kernel,
        grid_spec=grid_spec,
        out_shape=jax.ShapeDtypeStruct((n, d), F32),
        compiler_params=_cparams(("arbitrary",)),
        name="moe_combine",
    )(d0, d1, y, x, gates)


def _moe_plan(top_e, n_tiles):
    n_assign = top_e.shape[0] * TOP_K
    flat_e = top_e.reshape(n_assign)
    onehot = (flat_e[:, None] == jnp.arange(N_EXPERTS, dtype=jnp.int32)[None, :]).astype(jnp.int32)
    csum = jnp.cumsum(onehot, axis=0)
    rank = jnp.take_along_axis(csum, flat_e[:, None], axis=1)[:, 0] - 1
    counts = csum[-1]
    padded = (counts + MOE_TM - 1) // MOE_TM * MOE_TM
    pend = jnp.cumsum(padded)
    dest = (pend - padded)[flat_e] + rank
    row_tok = jnp.zeros((n_tiles * MOE_TM,), jnp.int32).at[dest].set(jnp.arange(n_assign, dtype=jnp.int32) // TOP_K)
    tile_start = jnp.arange(n_tiles, dtype=jnp.int32) * MOE_TM
    tile_e = jnp.minimum(jnp.searchsorted(pend, tile_start, side="right"), N_EXPERTS - 1).astype(jnp.int32)
    tile_ok = (tile_start < pend[-1]).astype(jnp.int32)
    dest = dest.reshape(-1, TOP_K).astype(jnp.int32)
    return tile_e, tile_ok, row_tok, dest[:, 0], dest[:, 1]


def _moe_layer(x, g, wr_pad, w1, w3, w2):
    n = x.shape[0]
    h, top_e, gates = _router_call(x, g, wr_pad)
    n_tiles = -(-(n * TOP_K + N_EXPERTS * (MOE_TM - 1)) // MOE_TM)
    tile_e, tile_ok, row_tok, d0, d1 = _moe_plan(top_e[:, :TOP_K], n_tiles)
    y = _moe_ffn_call(tile_e, tile_ok, row_tok, h, w1, w3, w2)
    return _combine_call(d0, d1, y, x, gates)


def _norm_kernel(x_ref, g_ref, o_ref):
    o_ref[...] = _rms(x_ref[...], g_ref[...])


def _norm_call(x, g):
    n, d = x.shape
    tm = ROW_TILE
    return pl.pallas_call(
        _norm_kernel,
        grid=(n // tm,),
        in_specs=[pl.BlockSpec((tm, d), lambda i: (i, 0)), pl.BlockSpec((1, d), lambda i: (0, 0))],
        out_specs=pl.BlockSpec((tm, d), lambda i: (i, 0)),
        out_shape=jax.ShapeDtypeStruct((n, d), F32),
        compiler_params=_cparams(("parallel",)),
        name="final_norm",
    )(x, g)


def _rope_tables(seq):
    inv = ROPE_BASE ** (-jnp.arange(0, QK_ROPE, 2, dtype=F32) / QK_ROPE)
    ang = jnp.arange(seq)[:, None].astype(F32) * inv[None, :]
    cos, sin = jnp.cos(ang), jnp.sin(ang)
    half = QK_ROPE // 2
    pad = LANES - QK_NOPE - QK_ROPE
    ctab = jnp.concatenate([jnp.ones((seq, QK_NOPE), F32), cos, cos, jnp.zeros((seq, pad), F32)], axis=1)
    stab = jnp.concatenate([jnp.zeros((seq, QK_NOPE), F32), sin, sin, jnp.zeros((seq, pad), F32)], axis=1)
    del half
    return ctab, stab


def _split_w_in(w_in):
    d = w_in.shape[0]
    bounds = np.cumsum([WIDTH_A, WIDTH_A, WIDTH_A, Q_LORA, KV_LORA, QK_ROPE, WIDTH_C, WIDTH_C])
    qa, ka, va, cq, ckv, kr, qc, kc, vc = jnp.split(w_in, bounds.tolist(), axis=1)
    half = QK_ROPE // 2
    z_lo = jnp.zeros((d, QK_NOPE), w_in.dtype)
    z_hi = jnp.zeros((d, LANES - QK_NOPE - QK_ROPE), w_in.dtype)
    rope_blk = jnp.concatenate([z_lo, kr, z_hi], axis=1)
    swap_blk = jnp.concatenate([z_lo, -kr[:, half:], kr[:, :half], z_hi], axis=1)
    wa = jnp.concatenate([qa, ka, va], axis=1).astype(BF16)
    wc = jnp.concatenate([qc, kc, vc], axis=1).astype(BF16)
    wb = jnp.concatenate([cq, ckv, rope_blk, swap_blk], axis=1).astype(BF16)
    return wa, wc, wb


def _split_w_uq(w_uq):
    r = w_uq.shape[0]
    w = w_uq.reshape(r, N_HEADS_B, QK_NOPE + QK_ROPE)
    nope, rope = w[..., :QK_NOPE], w[..., QK_NOPE:]
    half = QK_ROPE // 2
    z_hi = jnp.zeros((r, N_HEADS_B, LANES - QK_NOPE - QK_ROPE), w_uq.dtype)
    w1 = jnp.concatenate([nope, rope, z_hi], axis=-1)
    w2 = jnp.concatenate([jnp.zeros_like(nope), -rope[..., half:], rope[..., :half], z_hi], axis=-1)
    return w1.reshape(r, -1).astype(BF16), w2.reshape(r, -1).astype(BF16)


def _split_w_ukv(w_ukv):
    r = w_ukv.shape[0]
    w = w_ukv.reshape(r, N_HEADS_B, QK_NOPE + V_HEAD)
    k_nope, v = w[..., :QK_NOPE], w[..., QK_NOPE:]
    wk = jnp.concatenate([k_nope, jnp.zeros((r, N_HEADS_B, LANES - QK_NOPE), w_ukv.dtype)], axis=-1)
    return wk.reshape(r, -1).astype(BF16), v.reshape(r, -1).astype(BF16)


def kernel(x, g_mix, w_in, g_q, g_kv, w_uq, w_ukv, rpb, g_out_a, g_out_b, g_out_c, w_o, g_ffn, w1, w3, w2,
           w_router, e_w1, e_w3, e_w2, g_final):
    batch, seq, d = x.shape
    n = batch * seq
    depth = g_mix.shape[0]
    rows = seq // GRID_W
    ctab, stab = _rope_tables(seq)
    xf = x.reshape(n, d)
    for layer in range(depth):
        wa, wc, wb = _split_w_in(w_in[layer])
        wq1, wq2 = _split_w_uq(w_uq[layer])
        wk, wv = _split_w_ukv(w_ukv[layer])
        pa, pc, qm, km, vm = _proj_call(xf, g_mix[layer][None], wa, wc, wb, g_q[layer][None], g_kv[layer][None],
                                        wq1, wq2, wk, wv, ctab, stab, seq)
        o_parts, lse_parts = [], []
        for dil in DILATIONS:
            o, lse = _band_call(pa.reshape(batch, seq // dil, dil * 3 * WIDTH_A), dil)
            o_parts.append(o.reshape(n, WIDTH_A))
            lse_parts.append(lse.reshape(n, WIDTH_A))
        ob = _mla_call(qm, km, vm, batch, seq).reshape(n, WIDTH_B)
        oc = _na_call(pc, _na_tables(rpb[layer], rows), batch, seq).reshape(n, WIDTH_C)
        xf = _mix_out_call(xf, o_parts, lse_parts, ob, oc, g_out_a[layer][None], g_out_b[layer][None],
                           g_out_c[layer][None], w_o[layer].astype(BF16))
        j = layer // 2
        if layer % 2 == 0:
            xf = _ffn_call(xf, g_ffn[layer][None], w1[j].astype(BF16), w3[j].astype(BF16), w2[j].astype(BF16))
        else:
            wr_pad = jnp.pad(w_router[j], ((0, 0), (0, LANES - N_EXPERTS)))
            xf = _moe_layer(xf, g_ffn[layer][None], wr_pad, e_w1[j].astype(BF16), e_w3[j].astype(BF16),
                            e_w2[j].astype(BF16))
    return _norm_call(xf, g_final[None]).reshape(batch, seq, d)
```

```python
import functools
import math

import numpy as np
import jax
import jax.numpy as jnp
from jax import lax
from jax.experimental import pallas as pl
from jax.experimental.pallas import tpu as pltpu

F32 = jnp.float32
BF16 = jnp.bfloat16

LANES = 128
V7X_VMEM_LIMIT_BYTES = 52 * 1024 * 1024

HEAD_DIM = 64
N_HEADS_A = 6
DILATIONS = (1, 4, 16)
BAND_HALF = 64
N_HEADS_B = 6
Q_LORA = 384
KV_LORA = 256
QK_NOPE = 64
QK_ROPE = 32
V_HEAD = 64
ROPE_BASE = 10000.0
N_HEADS_C = 4
GRID_W = 64
NA_ROWS = 8
NA_COLS = 16
WIDTH_A = N_HEADS_A * HEAD_DIM
WIDTH_B = N_HEADS_B * V_HEAD
WIDTH_C = N_HEADS_C * HEAD_DIM
N_EXPERTS = 8
TOP_K = 2
RMS_EPS = 1e-6
NEG_INF = -1e30

ROW_TILE = 512
MLA_TQ = 256
MLA_TK = 256
MLA_UNROLL = 8
MLA_LOOKAHEAD = 2
MLA_HEADS_PER_STEP = 2
BAND_TQ = 128
NA_TILE_ROWS = 2
NA_KEY_ROWS = 10
MOE_TM = 512
MOE_TF = 896
FFN_TF = 1408
COMBINE_TM = 256


def _cparams(semantics):
    return pltpu.CompilerParams(dimension_semantics=semantics, vmem_limit_bytes=V7X_VMEM_LIMIT_BYTES)


def _rms(x, g):
    return x * lax.rsqrt(jnp.mean(x * x, axis=-1, keepdims=True) + RMS_EPS) * g


def _dot(a, b):
    return jnp.dot(a, b, preferred_element_type=F32)


def _dot_nt(a, b):
    return lax.dot_general(a, b, (((1,), (1,)), ((), ())), preferred_element_type=F32)


def _proj_kernel(x_ref, g_ref, wa_ref, wc_ref, wb_ref, gq_ref, gkv_ref, wq1_ref, wq2_ref, wk_ref, wvt_ref,
                 ct_ref, st_ref, pa_ref, pc_ref, qm_ref, km_ref, vt_ref):
    h = _rms(x_ref[...], g_ref[...]).astype(BF16)
    pa_ref[...] = _dot(h, wa_ref[...]).astype(BF16)
    pc_ref[...] = _dot(h, wc_ref[...]).astype(BF16)
    pb = _dot(h, wb_ref[...])
    hq = _rms(pb[:, :Q_LORA], gq_ref[...]).astype(BF16)
    hkv = _rms(pb[:, Q_LORA:Q_LORA + KV_LORA], gkv_ref[...]).astype(BF16)
    r1 = pb[:, Q_LORA + KV_LORA:Q_LORA + KV_LORA + LANES]
    r2 = pb[:, Q_LORA + KV_LORA + LANES:]
    ct = ct_ref[...]
    st = st_ref[...]
    qa = _dot(hq, wq1_ref[...])
    qb = _dot(hq, wq2_ref[...])
    kn = _dot(hkv, wk_ref[...])
    kr = r1 * ct + r2 * st
    for hd in range(N_HEADS_B):
        sl = slice(hd * LANES, (hd + 1) * LANES)
        qm_ref[:, sl] = (qa[:, sl] * ct + qb[:, sl] * st).astype(BF16)
        km_ref[:, sl] = (kn[:, sl] + kr).astype(BF16)
    vt_ref[...] = _dot_nt(wvt_ref[...], hkv).astype(BF16)


def _proj_call(x, g, wa, wc, wb, gq, gkv, wq1, wq2, wk, wvt, ctab, stab, seq):
    n, d = x.shape
    tm = ROW_TILE
    tiles_per_seq = seq // tm
    full = lambda a: pl.BlockSpec(a.shape, lambda i: (0,) * a.ndim)
    row = lambda w: pl.BlockSpec((tm, w), lambda i: (i, 0))
    tab = pl.BlockSpec((tm, LANES), lambda i: (i % tiles_per_seq, 0))
    hb = N_HEADS_B * LANES
    return pl.pallas_call(
        _proj_kernel,
        grid=(n // tm,),
        in_specs=[row(d), full(g), full(wa), full(wc), full(wb), full(gq), full(gkv), full(wq1), full(wq2),
                  full(wk), full(wvt), tab, tab],
        out_specs=[row(3 * WIDTH_A), row(3 * WIDTH_C), row(hb), row(hb),
                   pl.BlockSpec((WIDTH_B, tm), lambda i: (0, i))],
        out_shape=[jax.ShapeDtypeStruct((n, 3 * WIDTH_A), BF16), jax.ShapeDtypeStruct((n, 3 * WIDTH_C), BF16),
                   jax.ShapeDtypeStruct((n, hb), BF16), jax.ShapeDtypeStruct((n, hb), BF16),
                   jax.ShapeDtypeStruct((WIDTH_B, n), BF16)],
        compiler_params=_cparams(("parallel",)),
        name="proj",
    )(x, g, wa, wc, wb, gq, gkv, wq1, wq2, wk, wvt, ctab, stab)


def _band_kernel(q_ref, k_ref, v_ref, o_ref, lse_ref, *, dilation, n):
    tq = BAND_TQ
    tkw = tq + 2 * BAND_HALF
    i0 = pl.program_id(2) * tq
    start = pl.multiple_of(jnp.clip(i0 - BAND_HALF, 0, n - tkw), BAND_HALF)
    q = q_ref[...]
    k = k_ref[pl.ds(start, tkw), :]
    v = v_ref[pl.ds(start, tkw), :]
    rel = (start - i0) + lax.broadcasted_iota(jnp.int32, (tq, tkw), 1) - lax.broadcasted_iota(jnp.int32, (tq, tkw), 0)
    dist_i = jnp.abs(rel)
    valid = dist_i <= BAND_HALF
    dist = dist_i.astype(F32)
    scale = HEAD_DIM ** -0.5
    outs, lses = [], []
    for h in range(N_HEADS_A):
        sl = slice(h * HEAD_DIM, (h + 1) * HEAD_DIM)
        slope = 2.0 ** (-8.0 * (h + 1) / N_HEADS_A)
        s = _dot_nt(q[:, sl], k[:, sl]) * scale - (slope * dilation) * dist
        s = jnp.where(valid, s, NEG_INF)
        m = jnp.max(s, axis=-1, keepdims=True)
        p = jnp.exp(s - m)
        l = jnp.sum(p, axis=-1, keepdims=True)
        o = _dot(p.astype(BF16), v[:, sl]) / l
        outs.append(o)
        lses.append(jnp.broadcast_to(m + jnp.log(l), (tq, HEAD_DIM)))
    o_ref[...] = jnp.concatenate(outs, axis=-1)
    lse_ref[...] = jnp.concatenate(lses, axis=-1)


def _band_call(pa_view, dilation):
    b, n, _ = pa_view.shape
    tq = BAND_TQ
    w = WIDTH_A
    qspec = pl.BlockSpec((None, tq, w), lambda bb, r, i: (bb, i, 3 * r))
    kspec = pl.BlockSpec((None, n, w), lambda bb, r, i: (bb, 0, 3 * r + 1))
    vspec = pl.BlockSpec((None, n, w), lambda bb, r, i: (bb, 0, 3 * r + 2))
    ospec = pl.BlockSpec((None, tq, w), lambda bb, r, i: (bb, i, r))
    shape = jax.ShapeDtypeStruct((b, n, dilation * w), F32)
    return pl.pallas_call(
        functools.partial(_band_kernel, dilation=dilation, n=n),
        grid=(b, dilation, n // tq),
        in_specs=[qspec, kspec, vspec],
        out_specs=[ospec, ospec],
        out_shape=[shape, shape],
        compiler_params=_cparams(("parallel", "parallel", "parallel")),
        name=f"band_d{dilation}",
    )(pa_view, pa_view, pa_view)


def _mla_kernel(q_ref, k_ref, vt_ref, o_ref):
    tq = q_ref.shape[0]
    seq = k_ref.shape[0]
    tk = MLA_TK
    nh = MLA_HEADS_PER_STEP
    n_chunks = seq // tk
    items = [(u, h) for u in range(MLA_UNROLL) for h in range(nh)]
    look = MLA_LOOKAHEAD
    c = (QK_NOPE + QK_ROPE) ** -0.5 * math.log2(math.e)

    def key_slice(chunk):
        return pl.ds(pl.multiple_of(chunk * tk, tk), tk)

    def score_matmul(chunk, h):
        k = k_ref[key_slice(chunk), h * LANES:(h + 1) * LANES]
        return _dot_nt(k, q_ref[:, h * LANES:(h + 1) * LANES])

    def value_matmul(chunk, h, p):
        return _dot(vt_ref[h * V_HEAD:(h + 1) * V_HEAD, key_slice(chunk)], p)

    def body(j, carry):
        state = list(carry[:3 * nh])
        scores = dict(zip(items[:look], carry[3 * nh:3 * nh + look]))
        pend_p, pend_alpha = carry[3 * nh + look:]
        pending = (jnp.maximum(j * MLA_UNROLL - 1, 0), nh - 1, pend_p, pend_alpha)
        ahead = []
        for idx, (u, h) in enumerate(items):
            la = idx + look
            if la < len(items):
                lu, lh = items[la]
                scores[lu, lh] = score_matmul(j * MLA_UNROLL + lu, lh)
            else:
                lu, lh = items[la - len(items)]
                ahead.append(score_matmul(jnp.minimum((j + 1) * MLA_UNROLL + lu, n_chunks - 1), lh))
            pc, ph, pp, pa = pending
            state[3 * ph + 2] = pa * state[3 * ph + 2] + value_matmul(pc, ph, pp)
            m, l = state[3 * h:3 * h + 2]
            s = scores.pop((u, h))
            m_new = jnp.maximum(m, jnp.max(s, axis=0, keepdims=True))
            alpha = jnp.exp2((m - m_new) * c)
            p = jnp.exp2((s - m_new) * c)
            state[3 * h:3 * h + 2] = [m_new, alpha * l + jnp.sum(p, axis=0, keepdims=True)]
            pending = (j * MLA_UNROLL + u, h, p.astype(BF16), alpha)
        return tuple(state) + tuple(ahead) + (pending[2], pending[3])

    init = (jnp.full((1, tq), NEG_INF, F32), jnp.zeros((1, tq), F32), jnp.zeros((V_HEAD, tq), F32)) * nh
    init += tuple(score_matmul(u, h) for u, h in items[:look])
    init += (jnp.zeros((tk, tq), BF16), jnp.ones((1, tq), F32))
    res = lax.fori_loop(0, n_chunks // MLA_UNROLL, body, init)
    state = list(res[:3 * nh])
    last_h = nh - 1
    state[3 * last_h + 2] = res[-1] * state[3 * last_h + 2] + value_matmul(n_chunks - 1, last_h, res[-2])
    out_t = jnp.concatenate([state[3 * h + 2] / state[3 * h + 1] for h in range(nh)], axis=0)
    o_ref[...] = out_t.T


def _mla_call(qm, km, vt, batch, seq):
    tq = MLA_TQ
    nh = MLA_HEADS_PER_STEP
    qspec = pl.BlockSpec((None, tq, nh * LANES), lambda b, g, i: (b, i, g))
    kspec = pl.BlockSpec((None, seq, nh * LANES), lambda b, g, i: (b, 0, g))
    vspec = pl.BlockSpec((nh * V_HEAD, seq), lambda b, g, i: (g, b))
    ospec = pl.BlockSpec((None, tq, nh * V_HEAD), lambda b, g, i: (b, i, g))
    return pl.pallas_call(
        _mla_kernel,
        grid=(batch, N_HEADS_B // nh, seq // tq),
        in_specs=[qspec, kspec, vspec],
        out_specs=ospec,
        out_shape=jax.ShapeDtypeStruct((batch, seq, WIDTH_B), F32),
        compiler_params=_cparams(("parallel", "parallel", "parallel")),
        name="mla",
    )(qm.reshape(batch, seq, -1), km.reshape(batch, seq, -1), vt)


def _na_variant(i, n_tiles):
    return jnp.minimum(i, 2) + jnp.maximum(i - (n_tiles - 3), 0)


def _na_kernel(q_ref, k_ref, v_ref, tab_ref, o_ref, *, rows):
    tq = NA_TILE_ROWS * GRID_W
    tkw = NA_KEY_ROWS * GRID_W
    i = pl.program_id(1)
    base = jnp.clip(i * NA_TILE_ROWS - NA_ROWS // 2, 0, rows - NA_KEY_ROWS)
    start = pl.multiple_of(base * GRID_W, GRID_W)
    q = q_ref[...]
    k = k_ref[pl.ds(start, tkw), :]
    v = v_ref[pl.ds(start, tkw), :]
    scale = HEAD_DIM ** -0.5
    outs = []
    for h in range(N_HEADS_C):
        sl = slice(h * HEAD_DIM, (h + 1) * HEAD_DIM)
        s = _dot_nt(q[:, sl], k[:, sl]) * scale + tab_ref[h]
        m = jnp.max(s, axis=-1, keepdims=True)
        p = jnp.exp(s - m)
        l = jnp.sum(p, axis=-1, keepdims=True)
        outs.append(_dot(p.astype(BF16), v[:, sl]) / l)
    o_ref[...] = jnp.concatenate(outs, axis=-1)


def _na_tables(rpb, rows):
    r0 = np.array([0, 2, 4, rows - 4, rows - 2])
    base = np.clip(r0 - NA_ROWS // 2, 0, rows - NA_KEY_ROWS)
    r = r0[:, None] + np.arange(NA_TILE_ROWS)[None, :]
    row_start = np.clip(r - NA_ROWS // 2, 0, rows - NA_ROWS)
    krow = base[:, None] + np.arange(NA_KEY_ROWS)[None, :]
    drow = krow[:, None, :] - r[:, :, None]
    row_ok = (krow[:, None, :] >= row_start[:, :, None]) & (krow[:, None, :] < row_start[:, :, None] + NA_ROWS)
    c = np.arange(GRID_W)
    win_start = np.clip(c - NA_COLS // 2, 0, GRID_W - NA_COLS)
    col_ok = (c[None, :] >= win_start[:, None]) & (c[None, :] < win_start[:, None] + NA_COLS)
    dcol = np.clip(c[None, :] - c[:, None], -(NA_COLS - 1), NA_COLS - 1)
    ok = row_ok[:, :, None, :, None] & col_ok[None, None, :, None, :]
    di = np.clip(drow, -(NA_ROWS - 1), NA_ROWS - 1) + (NA_ROWS - 1)
    pick_col = (dcol[:, :, None] + NA_COLS - 1 == np.arange(2 * NA_COLS - 1)).astype(np.float32)
    pick_row = (di[..., None] == np.arange(2 * NA_ROWS - 1)).astype(np.float32)
    hi = lax.Precision.HIGHEST
    toeplitz = jnp.einsum("hab,cjb->hacj", rpb.astype(F32), pick_col, precision=hi)
    bias = jnp.einsum("vqka,hacj->vhqckj", pick_row, toeplitz, precision=hi)
    tab = jnp.where(jnp.asarray(ok)[:, None], bias, NEG_INF)
    return tab.reshape(5, N_HEADS_C, NA_TILE_ROWS * GRID_W, NA_KEY_ROWS * GRID_W)


def _na_call(pc, tab, batch, seq):
    rows = seq // GRID_W
    tq = NA_TILE_ROWS * GRID_W
    n_tiles = rows // NA_TILE_ROWS
    w = WIDTH_C
    qspec = pl.BlockSpec((None, tq, w), lambda b, i: (b, i, 0))
    kspec = pl.BlockSpec((None, seq, w), lambda b, i: (b, 0, 1))
    vspec = pl.BlockSpec((None, seq, w), lambda b, i: (b, 0, 2))
    tspec = pl.BlockSpec((None,) + tab.shape[1:], lambda b, i: (_na_variant(i, n_tiles), 0, 0, 0))
    ospec = pl.BlockSpec((None, tq, w), lambda b, i: (b, i, 0))
    pc3 = pc.reshape(batch, seq, 3 * w)
    return pl.pallas_call(
        functools.partial(_na_kernel, rows=rows),
        grid=(batch, n_tiles),
        in_specs=[qspec, kspec, vspec, tspec],
        out_specs=ospec,
        out_shape=jax.ShapeDtypeStruct((batch, seq, w), F32),
        compiler_params=_cparams(("parallel", "parallel")),
        name="natten",
    )(pc3, pc3, pc3, tab)


def _mix_out_kernel(x_ref, o1_ref, o2_ref, o3_ref, l1_ref, l2_ref, l3_ref, ob_ref, oc_ref,
                    ga_ref, gb_ref, gc_ref, wo_ref, out_ref):
    l1, l2, l3 = l1_ref[...], l2_ref[...], l3_ref[...]
    mx = jnp.maximum(jnp.maximum(l1, l2), l3)
    w1, w2, w3 = jnp.exp(l1 - mx), jnp.exp(l2 - mx), jnp.exp(l3 - mx)
    oa = (w1 * o1_ref[...] + w2 * o2_ref[...] + w3 * o3_ref[...]) / (w1 + w2 + w3)
    ya = _rms(oa, ga_ref[...]).astype(BF16)
    yb = _rms(ob_ref[...], gb_ref[...]).astype(BF16)
    yc = _rms(oc_ref[...], gc_ref[...]).astype(BF16)
    y = _dot(ya, wo_ref[:WIDTH_A, :])
    y = y + _dot(yb, wo_ref[WIDTH_A:WIDTH_A + WIDTH_B, :])
    y = y + _dot(yc, wo_ref[WIDTH_A + WIDTH_B:, :])
    out_ref[...] = x_ref[...] + y


def _mix_out_call(x, o_parts, lse_parts, ob, oc, ga, gb, gc, wo):
    n, d = x.shape
    tm = ROW_TILE
    full = lambda a: pl.BlockSpec(a.shape, lambda i: (0,) * a.ndim)
    row = lambda w: pl.BlockSpec((tm, w), lambda i: (i, 0))
    return pl.pallas_call(
        _mix_out_kernel,
        grid=(n // tm,),
        in_specs=[row(d)] + [row(WIDTH_A)] * 6 + [row(WIDTH_B), row(WIDTH_C), full(ga), full(gb), full(gc), full(wo)],
        out_specs=row(d),
        out_shape=jax.ShapeDtypeStruct((n, d), F32),
        compiler_params=_cparams(("parallel",)),
        name="mix_out",
    )(x, *o_parts, *lse_parts, ob, oc, ga, gb, gc, wo)


def _silu(u):
    return u * (1.0 / (1.0 + jnp.exp(-u)))


def _ffn_kernel(x_ref, g_ref, w1_ref, w3_ref, w2_ref, out_ref, h_ref, acc_ref):
    f = pl.program_id(1)

    @pl.when(f == 0)
    def _():
        h_ref[...] = _rms(x_ref[...], g_ref[...]).astype(BF16)
        acc_ref[...] = jnp.zeros_like(acc_ref)

    h = h_ref[...]
    a = (_silu(_dot(h, w1_ref[...])) * _dot(h, w3_ref[...])).astype(BF16)
    acc_ref[...] += _dot(a, w2_ref[...])

    @pl.when(f == pl.num_programs(1) - 1)
    def _():
        out_ref[...] = x_ref[...] + acc_ref[...]


def _ffn_call(x, g, w1, w3, w2):
    n, d = x.shape
    ff = w1.shape[1]
    tm, tf = ROW_TILE, FFN_TF
    return pl.pallas_call(
        _ffn_kernel,
        grid=(n // tm, ff // tf),
        in_specs=[pl.BlockSpec((tm, d), lambda i, f: (i, 0)), pl.BlockSpec((1, d), lambda i, f: (0, 0)),
                  pl.BlockSpec((d, tf), lambda i, f: (0, f)), pl.BlockSpec((d, tf), lambda i, f: (0, f)),
                  pl.BlockSpec((tf, d), lambda i, f: (f, 0))],
        out_specs=pl.BlockSpec((tm, d), lambda i, f: (i, 0)),
        out_shape=jax.ShapeDtypeStruct((n, d), F32),
        scratch_shapes=[pltpu.VMEM((tm, d), BF16), pltpu.VMEM((tm, d), F32)],
        compiler_params=_cparams(("parallel", "arbitrary")),
        name="ffn",
    )(x, g, w1, w3, w2)


def _router_kernel(x_ref, g_ref, wr_ref, h_ref, e_ref, gate_ref):
    h = _rms(x_ref[...], g_ref[...])
    h_ref[...] = h
    logits = jnp.dot(h, wr_ref[...], precision=lax.Precision.HIGHEST, preferred_element_type=F32)
    lane = lax.broadcasted_iota(jnp.int32, logits.shape, 1).astype(F32)
    logits = jnp.where(lane < N_EXPERTS, logits, -jnp.inf)
    m1 = jnp.max(logits, axis=-1, keepdims=True)
    i1 = jnp.min(jnp.where(logits == m1, lane, float(LANES)), axis=-1, keepdims=True)
    rest = jnp.where(lane == i1, -jnp.inf, logits)
    m2 = jnp.max(rest, axis=-1, keepdims=True)
    i2 = jnp.min(jnp.where(rest == m2, lane, float(LANES)), axis=-1, keepdims=True)
    e = jnp.exp(m2 - m1)
    den = 1.0 + e
    e_ref[...] = jnp.where(lane == 0.0, i1, jnp.where(lane == 1.0, i2, 0.0)).astype(jnp.int32)
    gate_ref[...] = jnp.where(lane == 0.0, 1.0 / den, jnp.where(lane == 1.0, e / den, 0.0))


def _router_call(x, g, wr_pad):
    n, d = x.shape
    tm = ROW_TILE
    row = lambda w: pl.BlockSpec((tm, w), lambda i: (i, 0))
    full = lambda a: pl.BlockSpec(a.shape, lambda i: (0,) * a.ndim)
    return pl.pallas_call(
        _router_kernel,
        grid=(n // tm,),
        in_specs=[row(d), full(g), full(wr_pad)],
        out_specs=[row(d), row(LANES), row(LANES)],
        out_shape=[jax.ShapeDtypeStruct((n, d), F32), jax.ShapeDtypeStruct((n, LANES), jnp.int32),
                   jax.ShapeDtypeStruct((n, LANES), F32)],
        compiler_params=_cparams(("parallel",)),
        name="router",
    )(x, g, wr_pad)


def _row_copy(src_hbm, row, dst_ref, r, sem):
    return pltpu.make_async_copy(src_hbm.at[pl.ds(row, 1)], dst_ref.at[pl.ds(r, 1)], sem)


def _start_row_gather(idx_ref, base, src_hbm, dst_ref, sem, count):
    def body(r, c):
        _row_copy(src_hbm, idx_ref[base + r], dst_ref, r, sem).start()
        return c
    lax.fori_loop(0, count, body, 0)


def _wait_row_gather(src_hbm, dst_ref, sem, count):
    def body(r, c):
        _row_copy(src_hbm, 0, dst_ref, r, sem).wait()
        return c
    lax.fori_loop(0, count, body, 0)


def _moe_ffn_kernel(tile_e_ref, tile_ok_ref, row_tok_ref, h_hbm, w1_ref, w3_ref, w2_ref, out_ref,
                    xg_ref, hb_ref, acc_ref, sem):
    i = pl.program_id(0)
    f = pl.program_id(1)
    n_tiles = pl.num_programs(0)
    tm = MOE_TM
    slot = i % 2

    @pl.when(f == 0)
    def _():
        @pl.when(i == 0)
        def _():
            _start_row_gather(row_tok_ref, 0, h_hbm, xg_ref.at[0], sem.at[0], tm)

        _wait_row_gather(h_hbm, xg_ref.at[slot], sem.at[slot], tm)

        @pl.when(i + 1 < n_tiles)
        def _():
            _start_row_gather(row_tok_ref, (i + 1) * tm, h_hbm, xg_ref.at[1 - slot], sem.at[1 - slot], tm)

        hb_ref[...] = xg_ref[slot].astype(BF16)
        acc_ref[...] = jnp.zeros_like(acc_ref)

    @pl.when(tile_ok_ref[i] != 0)
    def _():
        h = hb_ref[...]
        a = (_silu(_dot(h, w1_ref[...])) * _dot(h, w3_ref[...])).astype(BF16)
        acc_ref[...] += _dot(a, w2_ref[...])

    @pl.when(f == pl.num_programs(1) - 1)
    def _():
        out_ref[...] = acc_ref[...]


def _moe_ffn_call(tile_e, tile_ok, row_tok, h, w1, w3, w2):
    n, d = h.shape
    n_tiles = tile_e.shape[0]
    ff = w1.shape[2]
    tm, tf = MOE_TM, MOE_TF
    grid_spec = pltpu.PrefetchScalarGridSpec(
        num_scalar_prefetch=3,
        grid=(n_tiles, ff // tf),
        in_specs=[pl.BlockSpec(memory_space=pl.ANY),
                  pl.BlockSpec((None, d, tf), lambda i, f, te, tv, rt: (te[i], 0, f)),
                  pl.BlockSpec((None, d, tf), lambda i, f, te, tv, rt: (te[i], 0, f)),
                  pl.BlockSpec((None, tf, d), lambda i, f, te, tv, rt: (te[i], f, 0))],
        out_specs=pl.BlockSpec((tm, d), lambda i, f, te, tv, rt: (i, 0)),
        scratch_shapes=[pltpu.VMEM((2, tm, d), F32), pltpu.VMEM((tm, d), BF16), pltpu.VMEM((tm, d), F32),
                        pltpu.SemaphoreType.DMA((2,))],
    )
    return pl.pallas_call(
        _moe_ffn_kernel,
        grid_spec=grid_spec,
        out_shape=jax.ShapeDtypeStruct((n_tiles * tm, d), F32),
        compiler_params=_cparams(("arbitrary", "arbitrary")),
        name="moe_ffn",
    )(tile_e, tile_ok, row_tok, h, w1, w3, w2)


def _combine_kernel(d0_ref, d1_ref, y_hbm, x_ref, gate_ref, out_ref, b0_ref, b1_ref, sem):
    i = pl.program_id(0)
    n_tiles = pl.num_programs(0)
    tm = COMBINE_TM
    slot = i % 2

    def start(tile, s):
        _start_row_gather(d0_ref, tile * tm, y_hbm, b0_ref.at[s], sem.at[0, s], tm)
        _start_row_gather(d1_ref, tile * tm, y_hbm, b1_ref.at[s], sem.at[1, s], tm)

    @pl.when(i == 0)
    def _():
        start(0, 0)

    _wait_row_gather(y_hbm, b0_ref.at[slot], sem.at[0, slot], tm)
    _wait_row_gather(y_hbm, b1_ref.at[slot], sem.at[1, slot], tm)

    @pl.when(i + 1 < n_tiles)
    def _():
        start(i + 1, 1 - slot)

    gates = gate_ref[...]
    out_ref[...] = x_ref[...] + (gates[:, 0:1] * b0_ref[slot] + gates[:, 1:2] * b1_ref[slot])


def _combine_call(d0, d1, y, x, gates):
    n, d = x.shape
    tm = COMBINE_TM
    grid_spec = pltpu.PrefetchScalarGridSpec(
        num_scalar_prefetch=2,
        grid=(n // tm,),
        in_specs=[pl.BlockSpec(memory_space=pl.ANY),
                  pl.BlockSpec((tm, d), lambda i, a, b: (i, 0)),
                  pl.BlockSpec((tm, LANES), lambda i, a, b: (i, 0))],
        out_specs=pl.BlockSpec((tm, d), lambda i, a, b: (i, 0)),
        scratch_shapes=[pltpu.VMEM((2, tm, d), F32), pltpu.VMEM((2, tm, d), F32), pltpu.SemaphoreType.DMA((2, 2))],
    )
    return pl.pallas_call(
        _combine_kernel,
        grid_spec=grid_spec,
        out_shape=jax.ShapeDtypeStruct((n, d), F32),
        compiler_params=_cparams(("arbitrary",)),
        name="moe_combine",
    )(d0, d1, y, x, gates)


def _moe_plan(top_e, n_tiles):
    n_assign = top_e.shape[0] * TOP_K
    flat_e = top_e.reshape(n_assign)
    onehot = (flat_e[:, None] == jnp.arange(N_EXPERTS, dtype=jnp.int32)[None, :]).astype(jnp.int32)
    csum = jnp.cumsum(onehot, axis=0)
    rank = jnp.take_along_axis(csum, flat_e[:, None], axis=1)[:, 0] - 1
    counts = csum[-1]
    padded = (counts + MOE_TM - 1) // MOE_TM * MOE_TM
    pend = jnp.cumsum(padded)
    dest = (pend - padded)[flat_e] + rank
    row_tok = jnp.zeros((n_tiles * MOE_TM,), jnp.int32).at[dest].set(jnp.arange(n_assign, dtype=jnp.int32) // TOP_K)
    tile_start = jnp.arange(n_tiles, dtype=jnp.int32) * MOE_TM
    tile_e = jnp.minimum(jnp.searchsorted(pend, tile_start, side="right"), N_EXPERTS - 1).astype(jnp.int32)
    tile_ok = (tile_start < pend[-1]).astype(jnp.int32)
    dest = dest.reshape(-1, TOP_K).astype(jnp.int32)
    return tile_e, tile_ok, row_tok, dest[:, 0], dest[:, 1]


def _moe_layer(x, g, wr_pad, w1, w3, w2):
    n = x.shape[0]
    h, top_e, gates = _router_call(x, g, wr_pad)
    n_tiles = -(-(n * TOP_K + N_EXPERTS * (MOE_TM - 1)) // MOE_TM)
    tile_e, tile_ok, row_tok, d0, d1 = _moe_plan(top_e[:, :TOP_K], n_tiles)
    y = _moe_ffn_call(tile_e, tile_ok, row_tok, h, w1, w3, w2)
    return _combine_call(d0, d1, y, x, gates)


def _norm_kernel(x_ref, g_ref, o_ref):
    o_ref[...] = _rms(x_ref[...], g_ref[...])


def _norm_call(x, g):
    n, d = x.shape
    tm = ROW_TILE
    return pl.pallas_call(
        _norm_kernel,
        grid=(n // tm,),
        in_specs=[pl.BlockSpec((tm, d), lambda i: (i, 0)), pl.BlockSpec((1, d), lambda i: (0, 0))],
        out_specs=pl.BlockSpec((tm, d), lambda i: (i, 0)),
        out_shape=jax.ShapeDtypeStruct((n, d), F32),
        compiler_params=_cparams(("parallel",)),
        name="final_norm",
    )(x, g)


def _rope_tables(seq):
    inv = ROPE_BASE ** (-jnp.arange(0, QK_ROPE, 2, dtype=F32) / QK_ROPE)
    ang = jnp.arange(seq)[:, None].astype(F32) * inv[None, :]
    cos, sin = jnp.cos(ang), jnp.sin(ang)
    pad = LANES - QK_NOPE - QK_ROPE
    ctab = jnp.concatenate([jnp.ones((seq, QK_NOPE), F32), cos, cos, jnp.zeros((seq, pad), F32)], axis=1)
    stab = jnp.concatenate([jnp.zeros((seq, QK_NOPE), F32), sin, sin, jnp.zeros((seq, pad), F32)], axis=1)
    return ctab, stab


def _split_w_in(w_in):
    d = w_in.shape[0]
    bounds = np.cumsum([WIDTH_A, WIDTH_A, WIDTH_A, Q_LORA, KV_LORA, QK_ROPE, WIDTH_C, WIDTH_C])
    qa, ka, va, cq, ckv, kr, qc, kc, vc = jnp.split(w_in, bounds.tolist(), axis=1)
    half = QK_ROPE // 2
    z_lo = jnp.zeros((d, QK_NOPE), w_in.dtype)
    z_hi = jnp.zeros((d, LANES - QK_NOPE - QK_ROPE), w_in.dtype)
    rope_blk = jnp.concatenate([z_lo, kr, z_hi], axis=1)
    swap_blk = jnp.concatenate([z_lo, -kr[:, half:], kr[:, :half], z_hi], axis=1)
    wa = jnp.concatenate([qa, ka, va], axis=1).astype(BF16)
    wc = jnp.concatenate([qc, kc, vc], axis=1).astype(BF16)
    wb = jnp.concatenate([cq, ckv, rope_blk, swap_blk], axis=1).astype(BF16)
    return wa, wc, wb


def _split_w_uq(w_uq):
    r = w_uq.shape[0]
    w = w_uq.reshape(r, N_HEADS_B, QK_NOPE + QK_ROPE)
    nope, rope = w[..., :QK_NOPE], w[..., QK_NOPE:]
    half = QK_ROPE // 2
    z_hi = jnp.zeros((r, N_HEADS_B, LANES - QK_NOPE - QK_ROPE), w_uq.dtype)
    w1 = jnp.concatenate([nope, rope, z_hi], axis=-1)
    w2 = jnp.concatenate([jnp.zeros_like(nope), -rope[..., half:], rope[..., :half], z_hi], axis=-1)
    return w1.reshape(r, -1).astype(BF16), w2.reshape(r, -1).astype(BF16)


def _split_w_ukv(w_ukv):
    r = w_ukv.shape[0]
    w = w_ukv.reshape(r, N_HEADS_B, QK_NOPE + V_HEAD)
    k_nope, v = w[..., :QK_NOPE], w[..., QK_NOPE:]
    wk = jnp.concatenate([k_nope, jnp.zeros((r, N_HEADS_B, LANES - QK_NOPE), w_ukv.dtype)], axis=-1)
    return wk.reshape(r, -1).astype(BF16), v.reshape(r, -1).T.astype(BF16)


def kernel(x, g_mix, w_in, g_q, g_kv, w_uq, w_ukv, rpb, g_out_a, g_out_b, g_out_c, w_o, g_ffn, w1, w3, w2,
           w_router, e_w1, e_w3, e_w2, g_final):
    batch, seq, d = x.shape
    n = batch * seq
    depth = g_mix.shape[0]
    rows = seq // GRID_W
    ctab, stab = _rope_tables(seq)
    xf = x.reshape(n, d)
    for layer in range(depth):
        wa, wc, wb = _split_w_in(w_in[layer])
        wq1, wq2 = _split_w_uq(w_uq[layer])
        wk, wvt = _split_w_ukv(w_ukv[layer])
        pa, pc, qm, km, vt = _proj_call(xf, g_mix[layer][None], wa, wc, wb, g_q[layer][None], g_kv[layer][None],
                                        wq1, wq2, wk, wvt, ctab, stab, seq)
        o_parts, lse_parts = [], []
        for dil in DILATIONS:
            o, lse = _band_call(pa.reshape(batch, seq // dil, dil * 3 * WIDTH_A), dil)
            o_parts.append(o.reshape(n, WIDTH_A))
            lse_parts.append(lse.reshape(n, WIDTH_A))
        ob = _mla_call(qm, km, vt, batch, seq).reshape(n, WIDTH_B)
        oc = _na_call(pc, _na_tables(rpb[layer], rows), batch, seq).reshape(n, WIDTH_C)
        xf = _mix_out_call(xf, o_parts, lse_parts, ob, oc, g_out_a[layer][None], g_out_b[layer][None],
                           g_out_c[layer][None], w_o[layer].astype(BF16))
        j = layer // 2
        if layer % 2 == 0:
            xf = _ffn_call(xf, g_ffn[layer][None], w1[j].astype(BF16), w3[j].astype(BF16), w2[j].astype(BF16))
        else:
            wr_pad = jnp.pad(w_router[j], ((0, 0), (0, LANES - N_EXPERTS)))
            xf = _moe_layer(xf, g_ffn[layer][None], wr_pad, e_w1[j].astype(BF16), e_w3[j].astype(BF16),
                            e_w2[j].astype(BF16))
    return _norm_call(xf, g_final[None]).reshape(batch, seq, d)
```

```python
import functools
import math

import numpy as np
import jax
import jax.numpy as jnp
from jax import lax
from jax.experimental import pallas as pl
from jax.experimental.pallas import tpu as pltpu

F32 = jnp.float32
BF16 = jnp.bfloat16

LANES = 128
V7X_VMEM_LIMIT_BYTES = 52 * 1024 * 1024

HEAD_DIM = 64
N_HEADS_A = 6
DILATIONS = (1, 4, 16)
BAND_HALF = 64
N_HEADS_B = 6
Q_LORA = 384
KV_LORA = 256
QK_NOPE = 64
QK_ROPE = 32
V_HEAD = 64
ROPE_BASE = 10000.0
N_HEADS_C = 4
GRID_W = 64
NA_ROWS = 8
NA_COLS = 16
WIDTH_A = N_HEADS_A * HEAD_DIM
WIDTH_B = N_HEADS_B * V_HEAD
WIDTH_C = N_HEADS_C * HEAD_DIM
N_EXPERTS = 8
TOP_K = 2
RMS_EPS = 1e-6
NEG_INF = -1e30

ROW_TILE = 512
MLA_TQ = 256
MLA_TK = 256
MLA_UNROLL = 8
MLA_LOOKAHEAD = 2
MLA_HEADS_PER_STEP = 2
MLA_DEN_ROWS = 16
MLA_Q_PRESCALE = (QK_NOPE + QK_ROPE) ** -0.5 * math.log2(math.e)
BAND_TQ = 128
NA_TILE_ROWS = 2
NA_KEY_ROWS = 10
MOE_TM = 512
MOE_TF = 896
FFN_TF = 1408
COMBINE_TM = 256
GATHER_UNROLL = 8


def _cparams(semantics):
    return pltpu.CompilerParams(dimension_semantics=semantics, vmem_limit_bytes=V7X_VMEM_LIMIT_BYTES)


def _rms(x, g):
    return x * lax.rsqrt(jnp.mean(x * x, axis=-1, keepdims=True) + RMS_EPS) * g


def _dot(a, b):
    return jnp.dot(a, b, preferred_element_type=F32)


def _dot_nt(a, b):
    return lax.dot_general(a, b, (((1,), (1,)), ((), ())), preferred_element_type=F32)


def _proj_kernel(x_ref, g_ref, wa_ref, wc_ref, wb_ref, gq_ref, gkv_ref, wq1_ref, wq2_ref, wk_ref, wvt_ref,
                 ct_ref, st_ref, pa_ref, pc_ref, qm_ref, km_ref, vt_ref):
    h = _rms(x_ref[...], g_ref[...]).astype(BF16)
    pa_ref[...] = _dot(h, wa_ref[...]).astype(BF16)
    pc_ref[...] = _dot(h, wc_ref[...]).astype(BF16)
    pb = _dot(h, wb_ref[...])
    hq = _rms(pb[:, :Q_LORA], gq_ref[...]).astype(BF16)
    hkv = _rms(pb[:, Q_LORA:Q_LORA + KV_LORA], gkv_ref[...]).astype(BF16)
    r1 = pb[:, Q_LORA + KV_LORA:Q_LORA + KV_LORA + LANES]
    r2 = pb[:, Q_LORA + KV_LORA + LANES:]
    ct = ct_ref[...]
    st = st_ref[...]
    qa = _dot(hq, wq1_ref[...])
    qb = _dot(hq, wq2_ref[...])
    kn = _dot(hkv, wk_ref[...])
    kr = r1 * ct + r2 * st
    for hd in range(N_HEADS_B):
        sl = slice(hd * LANES, (hd + 1) * LANES)
        qm_ref[:, sl] = ((qa[:, sl] * ct + qb[:, sl] * st) * MLA_Q_PRESCALE).astype(BF16)
        km_ref[:, sl] = (kn[:, sl] + kr).astype(BF16)
    vt_ref[...] = _dot_nt(wvt_ref[...], hkv).astype(BF16)


def _proj_call(x, g, wa, wc, wb, gq, gkv, wq1, wq2, wk, wvt, ctab, stab, seq):
    n, d = x.shape
    tm = ROW_TILE
    tiles_per_seq = seq // tm
    full = lambda a: pl.BlockSpec(a.shape, lambda i: (0,) * a.ndim)
    row = lambda w: pl.BlockSpec((tm, w), lambda i: (i, 0))
    tab = pl.BlockSpec((tm, LANES), lambda i: (i % tiles_per_seq, 0))
    hb = N_HEADS_B * LANES
    return pl.pallas_call(
        _proj_kernel,
        grid=(n // tm,),
        in_specs=[row(d), full(g), full(wa), full(wc), full(wb), full(gq), full(gkv), full(wq1), full(wq2),
                  full(wk), full(wvt), tab, tab],
        out_specs=[row(3 * WIDTH_A), row(3 * WIDTH_C), row(hb), row(hb),
                   pl.BlockSpec((WIDTH_B, tm), lambda i: (0, i))],
        out_shape=[jax.ShapeDtypeStruct((n, 3 * WIDTH_A), BF16), jax.ShapeDtypeStruct((n, 3 * WIDTH_C), BF16),
                   jax.ShapeDtypeStruct((n, hb), BF16), jax.ShapeDtypeStruct((n, hb), BF16),
                   jax.ShapeDtypeStruct((WIDTH_B, n), BF16)],
        compiler_params=_cparams(("parallel",)),
        name="proj",
    )(x, g, wa, wc, wb, gq, gkv, wq1, wq2, wk, wvt, ctab, stab)


def _band_kernel(q_ref, k_ref, v_ref, o_ref, lse_ref, *, dilation, n):
    tq = BAND_TQ
    tkw = tq + 2 * BAND_HALF
    i0 = pl.program_id(2) * tq
    start = pl.multiple_of(jnp.clip(i0 - BAND_HALF, 0, n - tkw), BAND_HALF)
    q = q_ref[...]
    k = k_ref[pl.ds(start, tkw), :]
    v = v_ref[pl.ds(start, tkw), :]
    rel = (start - i0) + lax.broadcasted_iota(jnp.int32, (tq, tkw), 1) - lax.broadcasted_iota(jnp.int32, (tq, tkw), 0)
    dist_i = jnp.abs(rel)
    valid = dist_i <= BAND_HALF
    dist = dist_i.astype(F32)
    scale = HEAD_DIM ** -0.5
    head = lambda h: slice(h * HEAD_DIM, (h + 1) * HEAD_DIM)
    scores = [_dot_nt(q[:, head(h)], k[:, head(h)]) for h in range(N_HEADS_A)]
    probs, dens, lses = [], [], []
    for h in range(N_HEADS_A):
        slope = 2.0 ** (-8.0 * (h + 1) / N_HEADS_A)
        s = scores[h] * scale - (slope * dilation) * dist
        s = jnp.where(valid, s, NEG_INF)
        m = jnp.max(s, axis=-1, keepdims=True)
        p = jnp.exp(s - m)
        l = jnp.sum(p, axis=-1, keepdims=True)
        probs.append(p.astype(BF16))
        dens.append(l)
        lses.append(jnp.broadcast_to(m + jnp.log(l), (tq, HEAD_DIM)))
    outs = [_dot(probs[h], v[:, head(h)]) / dens[h] for h in range(N_HEADS_A)]
    o_ref[...] = jnp.concatenate(outs, axis=-1)
    lse_ref[...] = jnp.concatenate(lses, axis=-1)


def _band_call(pa_view, dilation):
    b, n, _ = pa_view.shape
    tq = BAND_TQ
    w = WIDTH_A
    qspec = pl.BlockSpec((None, tq, w), lambda bb, r, i: (bb, i, 3 * r))
    kspec = pl.BlockSpec((None, n, w), lambda bb, r, i: (bb, 0, 3 * r + 1))
    vspec = pl.BlockSpec((None, n, w), lambda bb, r, i: (bb, 0, 3 * r + 2))
    ospec = pl.BlockSpec((None, tq, w), lambda bb, r, i: (bb, i, r))
    shape = jax.ShapeDtypeStruct((b, n, dilation * w), F32)
    return pl.pallas_call(
        functools.partial(_band_kernel, dilation=dilation, n=n),
        grid=(b, dilation, n // tq),
        in_specs=[qspec, kspec, vspec],
        out_specs=[ospec, ospec],
        out_shape=[shape, shape],
        compiler_params=_cparams(("parallel", "parallel", "parallel")),
        name=f"band_d{dilation}",
    )(pa_view, pa_view, pa_view)


def _mla_kernel(q_ref, k_ref, vt_ref, o_ref):
    tq = q_ref.shape[0]
    seq = k_ref.shape[0]
    tk = MLA_TK
    nh = MLA_HEADS_PER_STEP
    n_chunks = seq // tk
    items = [(u, h) for u in range(MLA_UNROLL) for h in range(nh)]
    look = MLA_LOOKAHEAD
    ones_rows = jnp.ones((MLA_DEN_ROWS, tk), BF16)

    def key_slice(chunk):
        return pl.ds(pl.multiple_of(chunk * tk, tk), tk)

    def score_matmul(chunk, h):
        k = k_ref[key_slice(chunk), h * LANES:(h + 1) * LANES]
        return _dot_nt(k, q_ref[:, h * LANES:(h + 1) * LANES])

    def value_matmul(chunk, h, p):
        vt = jnp.concatenate([vt_ref[h * V_HEAD:(h + 1) * V_HEAD, key_slice(chunk)], ones_rows], axis=0)
        return _dot(vt, p)

    def body(j, carry):
        state = list(carry[:2 * nh])
        scores = dict(zip(items[:look], carry[2 * nh:2 * nh + look]))
        pend_p, pend_alpha = carry[2 * nh + look:]
        pending = (jnp.maximum(j * MLA_UNROLL - 1, 0), nh - 1, pend_p, pend_alpha)
        ahead = []
        for idx, (u, h) in enumerate(items):
            la = idx + look
            if la < len(items):
                lu, lh = items[la]
                scores[lu, lh] = score_matmul(j * MLA_UNROLL + lu, lh)
            else:
                lu, lh = items[la - len(items)]
                ahead.append(score_matmul(jnp.minimum((j + 1) * MLA_UNROLL + lu, n_chunks - 1), lh))
            pc, ph, pp, pa = pending
            state[2 * ph + 1] = pa * state[2 * ph + 1] + value_matmul(pc, ph, pp)
            s = scores.pop((u, h))
            m_new = jnp.maximum(state[2 * h], jnp.max(s, axis=0, keepdims=True))
            alpha = jnp.exp2(state[2 * h] - m_new)
            state[2 * h] = m_new
            pending = (j * MLA_UNROLL + u, h, jnp.exp2(s - m_new).astype(BF16), alpha)
        return tuple(state) + tuple(ahead) + (pending[2], pending[3])

    init = (jnp.full((1, tq), NEG_INF, F32), jnp.zeros((V_HEAD + MLA_DEN_ROWS, tq), F32)) * nh
    init += tuple(score_matmul(u, h) for u, h in items[:look])
    init += (jnp.zeros((tk, tq), BF16), jnp.ones((1, tq), F32))
    res = lax.fori_loop(0, n_chunks // MLA_UNROLL, body, init)
    accs = [res[2 * h + 1] for h in range(nh)]
    accs[nh - 1] = res[-1] * accs[nh - 1] + value_matmul(n_chunks - 1, nh - 1, res[-2])
    out_t = jnp.concatenate([a[:V_HEAD] / a[V_HEAD:V_HEAD + 1] for a in accs], axis=0)
    o_ref[...] = out_t.T


def _mla_call(qm, km, vt, batch, seq):
    tq = MLA_TQ
    nh = MLA_HEADS_PER_STEP
    qspec = pl.BlockSpec((None, tq, nh * LANES), lambda b, g, i: (b, i, g))
    kspec = pl.BlockSpec((None, seq, nh * LANES), lambda b, g, i: (b, 0, g))
    vspec = pl.BlockSpec((nh * V_HEAD, seq), lambda b, g, i: (g, b))
    ospec = pl.BlockSpec((None, tq, nh * V_HEAD), lambda b, g, i: (b, i, g))
    return pl.pallas_call(
        _mla_kernel,
        grid=(batch, N_HEADS_B // nh, seq // tq),
        in_specs=[qspec, kspec, vspec],
        out_specs=ospec,
        out_shape=jax.ShapeDtypeStruct((batch, seq, WIDTH_B), F32),
        compiler_params=_cparams(("parallel", "parallel", "parallel")),
        name="mla",
    )(qm.reshape(batch, seq, -1), km.reshape(batch, seq, -1), vt)


def _na_variant(i, n_tiles):
    return jnp.minimum(i, 2) + jnp.maximum(i - (n_tiles - 3), 0)


def _na_kernel(q_ref, k_ref, v_ref, tab_ref, o_ref, *, rows):
    tq = NA_TILE_ROWS * GRID_W
    tkw = NA_KEY_ROWS * GRID_W
    i = pl.program_id(1)
    base = jnp.clip(i * NA_TILE_ROWS - NA_ROWS // 2, 0, rows - NA_KEY_ROWS)
    start = pl.multiple_of(base * GRID_W, GRID_W)
    q = q_ref[...]
    k = k_ref[pl.ds(start, tkw), :]
    v = v_ref[pl.ds(start, tkw), :]
    scale = HEAD_DIM ** -0.5
    head = lambda h: slice(h * HEAD_DIM, (h + 1) * HEAD_DIM)
    scores = [_dot_nt(q[:, head(h)], k[:, head(h)]) for h in range(N_HEADS_C)]
    probs, dens = [], []
    for h in range(N_HEADS_C):
        s = scores[h] * scale + tab_ref[h]
        m = jnp.max(s, axis=-1, keepdims=True)
        p = jnp.exp(s - m)
        dens.append(jnp.sum(p, axis=-1, keepdims=True))
        probs.append(p.astype(BF16))
    outs = [_dot(probs[h], v[:, head(h)]) / dens[h] for h in range(N_HEADS_C)]
    o_ref[...] = jnp.concatenate(outs, axis=-1)


def _na_tables(rpb, rows):
    r0 = np.array([0, 2, 4, rows - 4, rows - 2])
    base = np.clip(r0 - NA_ROWS // 2, 0, rows - NA_KEY_ROWS)
    r = r0[:, None] + np.arange(NA_TILE_ROWS)[None, :]
    row_start = np.clip(r - NA_ROWS // 2, 0, rows - NA_ROWS)
    krow = base[:, None] + np.arange(NA_KEY_ROWS)[None, :]
    drow = krow[:, None, :] - r[:, :, None]
    row_ok = (krow[:, None, :] >= row_start[:, :, None]) & (krow[:, None, :] < row_start[:, :, None] + NA_ROWS)
    c = np.arange(GRID_W)
    win_start = np.clip(c - NA_COLS // 2, 0, GRID_W - NA_COLS)
    col_ok = (c[None, :] >= win_start[:, None]) & (c[None, :] < win_start[:, None] + NA_COLS)
    dcol = np.clip(c[None, :] - c[:, None], -(NA_COLS - 1), NA_COLS - 1)
    ok = row_ok[:, :, None, :, None] & col_ok[None, None, :, None, :]
    di = np.clip(drow, -(NA_ROWS - 1), NA_ROWS - 1) + (NA_ROWS - 1)
    pick_col = (dcol[:, :, None] + NA_COLS - 1 == np.arange(2 * NA_COLS - 1)).astype(np.float32)
    pick_row = (di[..., None] == np.arange(2 * NA_ROWS - 1)).astype(np.float32)
    hi = lax.Precision.HIGHEST
    toeplitz = jnp.einsum("hab,cjb->hacj", rpb.astype(F32), pick_col, precision=hi)
    bias = jnp.einsum("vqka,hacj->vhqckj", pick_row, toeplitz, precision=hi)
    tab = jnp.where(jnp.asarray(ok)[:, None], bias, NEG_INF)
    return tab.reshape(5, N_HEADS_C, NA_TILE_ROWS * GRID_W, NA_KEY_ROWS * GRID_W)


def _na_call(pc, tab, batch, seq):
    rows = seq // GRID_W
    tq = NA_TILE_ROWS * GRID_W
    n_tiles = rows // NA_TILE_ROWS
    w = WIDTH_C
    qspec = pl.BlockSpec((None, tq, w), lambda b, i: (b, i, 0))
    kspec = pl.BlockSpec((None, seq, w), lambda b, i: (b, 0, 1))
    vspec = pl.BlockSpec((None, seq, w), lambda b, i: (b, 0, 2))
    tspec = pl.BlockSpec((None,) + tab.shape[1:], lambda b, i: (_na_variant(i, n_tiles), 0, 0, 0))
    ospec = pl.BlockSpec((None, tq, w), lambda b, i: (b, i, 0))
    pc3 = pc.reshape(batch, seq, 3 * w)
    return pl.pallas_call(
        functools.partial(_na_kernel, rows=rows),
        grid=(batch, n_tiles),
        in_specs=[qspec, kspec, vspec, tspec],
        out_specs=ospec,
        out_shape=jax.ShapeDtypeStruct((batch, seq, w), F32),
        compiler_params=_cparams(("parallel", "parallel")),
        name="natten",
    )(pc3, pc3, pc3, tab)


def _mix_out_kernel(x_ref, o1_ref, o2_ref, o3_ref, l1_ref, l2_ref, l3_ref, ob_ref, oc_ref,
                    ga_ref, gb_ref, gc_ref, wo_ref, out_ref):
    l1, l2, l3 = l1_ref[...], l2_ref[...], l3_ref[...]
    mx = jnp.maximum(jnp.maximum(l1, l2), l3)
    w1, w2, w3 = jnp.exp(l1 - mx), jnp.exp(l2 - mx), jnp.exp(l3 - mx)
    oa = (w1 * o1_ref[...] + w2 * o2_ref[...] + w3 * o3_ref[...]) / (w1 + w2 + w3)
    ya = _rms(oa, ga_ref[...]).astype(BF16)
    yb = _rms(ob_ref[...], gb_ref[...]).astype(BF16)
    yc = _rms(oc_ref[...], gc_ref[...]).astype(BF16)
    y = _dot(ya, wo_ref[:WIDTH_A, :])
    y = y + _dot(yb, wo_ref[WIDTH_A:WIDTH_A + WIDTH_B, :])
    y = y + _dot(yc, wo_ref[WIDTH_A + WIDTH_B:, :])
    out_ref[...] = x_ref[...] + y


def _mix_out_call(x, o_parts, lse_parts, ob, oc, ga, gb, gc, wo):
    n, d = x.shape
    tm = ROW_TILE
    full = lambda a: pl.BlockSpec(a.shape, lambda i: (0,) * a.ndim)
    row = lambda w: pl.BlockSpec((tm, w), lambda i: (i, 0))
    return pl.pallas_call(
        _mix_out_kernel,
        grid=(n // tm,),
        in_specs=[row(d)] + [row(WIDTH_A)] * 6 + [row(WIDTH_B), row(WIDTH_C), full(ga), full(gb), full(gc), full(wo)],
        out_specs=row(d),
        out_shape=jax.ShapeDtypeStruct((n, d), F32),
        compiler_params=_cparams(("parallel",)),
        name="mix_out",
    )(x, *o_parts, *lse_parts, ob, oc, ga, gb, gc, wo)


def _silu(u):
    return u * (1.0 / (1.0 + jnp.exp(-u)))


def _ffn_kernel(x_ref, g_ref, w1_ref, w3_ref, w2_ref, out_ref, h_ref, acc_ref):
    f = pl.program_id(1)

    @pl.when(f == 0)
    def _():
        h_ref[...] = _rms(x_ref[...], g_ref[...]).astype(BF16)
        acc_ref[...] = jnp.zeros_like(acc_ref)

    h = h_ref[...]
    a = (_silu(_dot(h, w1_ref[...])) * _dot(h, w3_ref[...])).astype(BF16)
    acc_ref[...] += _dot(a, w2_ref[...])

    @pl.when(f == pl.num_programs(1) - 1)
    def _():
        out_ref[...] = x_ref[...] + acc_ref[...]


def _ffn_call(x, g, w1, w3, w2):
    n, d = x.shape
    ff = w1.shape[1]
    tm, tf = ROW_TILE, FFN_TF
    return pl.pallas_call(
        _ffn_kernel,
        grid=(n // tm, ff // tf),
        in_specs=[pl.BlockSpec((tm, d), lambda i, f: (i, 0)), pl.BlockSpec((1, d), lambda i, f: (0, 0)),
                  pl.BlockSpec((d, tf), lambda i, f: (0, f)), pl.BlockSpec((d, tf), lambda i, f: (0, f)),
                  pl.BlockSpec((tf, d), lambda i, f: (f, 0))],
        out_specs=pl.BlockSpec((tm, d), lambda i, f: (i, 0)),
        out_shape=jax.ShapeDtypeStruct((n, d), F32),
        scratch_shapes=[pltpu.VMEM((tm, d), BF16), pltpu.VMEM((tm, d), F32)],
        compiler_params=_cparams(("parallel", "arbitrary")),
        name="ffn",
    )(x, g, w1, w3, w2)


def _router_kernel(x_ref, g_ref, wr_ref, h_ref, e_ref, gate_ref):
    h = _rms(x_ref[...], g_ref[...])
    h_ref[...] = h
    logits = jnp.dot(h, wr_ref[...], precision=lax.Precision.HIGHEST, preferred_element_type=F32)
    lane = lax.broadcasted_iota(jnp.int32, logits.shape, 1).astype(F32)
    logits = jnp.where(lane < N_EXPERTS, logits, -jnp.inf)
    m1 = jnp.max(logits, axis=-1, keepdims=True)
    i1 = jnp.min(jnp.where(logits == m1, lane, float(LANES)), axis=-1, keepdims=True)
    rest = jnp.where(lane == i1, -jnp.inf, logits)
    m2 = jnp.max(rest, axis=-1, keepdims=True)
    i2 = jnp.min(jnp.where(rest == m2, lane, float(LANES)), axis=-1, keepdims=True)
    e = jnp.exp(m2 - m1)
    den = 1.0 + e
    e_ref[...] = jnp.where(lane == 0.0, i1, jnp.where(lane == 1.0, i2, 0.0)).astype(jnp.int32)
    gate_ref[...] = jnp.where(lane == 0.0, 1.0 / den, jnp.where(lane == 1.0, e / den, 0.0))


def _router_call(x, g, wr_pad):
    n, d = x.shape
    tm = ROW_TILE
    row = lambda w: pl.BlockSpec((tm, w), lambda i: (i, 0))
    full = lambda a: pl.BlockSpec(a.shape, lambda i: (0,) * a.ndim)
    return pl.pallas_call(
        _router_kernel,
        grid=(n // tm,),
        in_specs=[row(d), full(g), full(wr_pad)],
        out_specs=[row(d), row(LANES), row(LANES)],
        out_shape=[jax.ShapeDtypeStruct((n, d), F32), jax.ShapeDtypeStruct((n, LANES), jnp.int32),
                   jax.ShapeDtypeStruct((n, LANES), F32)],
        compiler_params=_cparams(("parallel",)),
        name="router",
    )(x, g, wr_pad)


def _row_copy(src_hbm, row, dst_ref, r, sem):
    return pltpu.make_async_copy(src_hbm.at[pl.ds(row, 1)], dst_ref.at[pl.ds(r, 1)], sem)


def _start_row_gather(idx_ref, base, src_hbm, dst_ref, sem, count):
    def body(r, c):
        _row_copy(src_hbm, idx_ref[base + r], dst_ref, r, sem).start()
        return c
    lax.fori_loop(0, count, body, 0, unroll=GATHER_UNROLL)


def _wait_row_gather(src_hbm, dst_ref, sem, count):
    pltpu.make_async_copy(src_hbm.at[pl.ds(0, count)], dst_ref, sem).wait()


def _moe_ffn_kernel(tile_e_ref, tile_ok_ref, row_tok_ref, h_hbm, w1_ref, w3_ref, w2_ref, out_ref,
                    xg_ref, hb_ref, acc_ref, sem):
    i = pl.program_id(0)
    f = pl.program_id(1)
    n_tiles = pl.num_programs(0)
    tm = MOE_TM
    slot = i % 2

    @pl.when(f == 0)
    def _():
        @pl.when(i == 0)
        def _():
            _start_row_gather(row_tok_ref, 0, h_hbm, xg_ref.at[0], sem.at[0], tm)

        _wait_row_gather(h_hbm, xg_ref.at[slot], sem.at[slot], tm)

        @pl.when(i + 1 < n_tiles)
        def _():
            _start_row_gather(row_tok_ref, (i + 1) * tm, h_hbm, xg_ref.at[1 - slot], sem.at[1 - slot], tm)

        hb_ref[...] = xg_ref[slot].astype(BF16)
        acc_ref[...] = jnp.zeros_like(acc_ref)

    @pl.when(tile_ok_ref[i] != 0)
    def _():
        h = hb_ref[...]
        a = (_silu(_dot(h, w1_ref[...])) * _dot(h, w3_ref[...])).astype(BF16)
        acc_ref[...] += _dot(a, w2_ref[...])

    @pl.when(f == pl.num_programs(1) - 1)
    def _():
        out_ref[...] = acc_ref[...]


def _moe_ffn_call(tile_e, tile_ok, row_tok, h, w1, w3, w2):
    n, d = h.shape
    n_tiles = tile_e.shape[0]
    ff = w1.shape[2]
    tm, tf = MOE_TM, MOE_TF
    grid_spec = pltpu.PrefetchScalarGridSpec(
        num_scalar_prefetch=3,
        grid=(n_tiles, ff // tf),
        in_specs=[pl.BlockSpec(memory_space=pl.ANY),
                  pl.BlockSpec((None, d, tf), lambda i, f, te, tv, rt: (te[i], 0, f)),
                  pl.BlockSpec((None, d, tf), lambda i, f, te, tv, rt: (te[i], 0, f)),
                  pl.BlockSpec((None, tf, d), lambda i, f, te, tv, rt: (te[i], f, 0))],
        out_specs=pl.BlockSpec((tm, d), lambda i, f, te, tv, rt: (i, 0)),
        scratch_shapes=[pltpu.VMEM((2, tm, d), F32), pltpu.VMEM((tm, d), BF16), pltpu.VMEM((tm, d), F32),
                        pltpu.SemaphoreType.DMA((2,))],
    )
    return pl.pallas_call(
        _moe_ffn_kernel,
        grid_spec=grid_spec,
        out_shape=jax.ShapeDtypeStruct((n_tiles * tm, d), F32),
        compiler_params=_cparams(("arbitrary", "arbitrary")),
        name="moe_ffn",
    )(tile_e, tile_ok, row_tok, h, w1, w3, w2)


def _combine_kernel(d0_ref, d1_ref, y_hbm, x_ref, gate_ref, out_ref, b0_ref, b1_ref, sem):
    i = pl.program_id(0)
    n_tiles = pl.num_programs(0)
    tm = COMBINE_TM
    slot = i % 2

    def start(tile, s):
        _start_row_gather(d0_ref, tile * tm, y_hbm, b0_ref.at[s], sem.at[0, s], tm)
        _start_row_gather(d1_ref, tile * tm, y_hbm, b1_ref.at[s], sem.at[1, s], tm)

    @pl.when(i == 0)
    def _():
        start(0, 0)

    _wait_row_gather(y_hbm, b0_ref.at[slot], sem.at[0, slot], tm)
    _wait_row_gather(y_hbm, b1_ref.at[slot], sem.at[1, slot], tm)

    @pl.when(i + 1 < n_tiles)
    def _():
        start(i + 1, 1 - slot)

    gates = gate_ref[...]
    out_ref[...] = x_ref[...] + (gates[:, 0:1] * b0_ref[slot] + gates[:, 1:2] * b1_ref[slot])


def _combine_call(d0, d1, y, x, gates):
    n, d = x.shape
    tm = COMBINE_TM
    grid_spec = pltpu.PrefetchScalarGridSpec(
        num_scalar_prefetch=2,
        grid=(n // tm,),
        in_specs=[pl.BlockSpec(memory_space=pl.ANY),
                  pl.BlockSpec((tm, d), lambda i, a, b: (i, 0)),
                  pl.BlockSpec((tm, LANES), lambda i, a, b: (i, 0))],
        out_specs=pl.BlockSpec((tm, d), lambda i, a, b: (i, 0)),
        scratch_shapes=[pltpu.VMEM((2, tm, d), F32), pltpu.VMEM((2, tm, d), F32), pltpu.SemaphoreType.DMA((2, 2))],
    )
    return pl.pallas_call(
        _combine_kernel,
        grid_spec=grid_spec,
        out_shape=jax.ShapeDtypeStruct((n, d), F32),
        compiler_params=_cparams(("arbitrary",)),
        name="moe_combine",
    )(d0, d1, y, x, gates)


def _moe_plan(top_e, n_tiles):
    n_assign = top_e.shape[0] * TOP_K
    flat_e = top_e.reshape(n_assign)
    onehot = (flat_e[:, None] == jnp.arange(N_EXPERTS, dtype=jnp.int32)[None, :]).astype(jnp.int32)
    csum = jnp.cumsum(onehot, axis=0)
    rank = jnp.take_along_axis(csum, flat_e[:, None], axis=1)[:, 0] - 1
    counts = csum[-1]
    padded = (counts + MOE_TM - 1) // MOE_TM * MOE_TM
    pend = jnp.cumsum(padded)
    dest = (pend - padded)[flat_e] + rank
    row_tok = jnp.zeros((n_tiles * MOE_TM,), jnp.int32).at[dest].set(jnp.arange(n_assign, dtype=jnp.int32) // TOP_K)
    tile_start = jnp.arange(n_tiles, dtype=jnp.int32) * MOE_TM
    tile_e = jnp.minimum(jnp.searchsorted(pend, tile_start, side="right"), N_EXPERTS - 1).astype(jnp.int32)
    tile_ok = (tile_start < pend[-1]).astype(jnp.int32)
    dest = dest.reshape(-1, TOP_K).astype(jnp.int32)
    return tile_e, tile_ok, row_tok, dest[:, 0], dest[:, 1]


def _moe_layer(x, g, wr_pad, w1, w3, w2):
    n = x.shape[0]
    h, top_e, gates = _router_call(x, g, wr_pad)
    n_tiles = -(-(n * TOP_K + N_EXPERTS * (MOE_TM - 1)) // MOE_TM)
    tile_e, tile_ok, row_tok, d0, d1 = _moe_plan(top_e[:, :TOP_K], n_tiles)
    y = _moe_ffn_call(tile_e, tile_ok, row_tok, h, w1, w3, w2)
    return _combine_call(d0, d1, y, x, gates)


def _norm_kernel(x_ref, g_ref, o_ref):
    o_ref[...] = _rms(x_ref[...], g_ref[...])


def _norm_call(x, g):
    n, d = x.shape
    tm = ROW_TILE
    return pl.pallas_call(
        _norm_kernel,
        grid=(n // tm,),
        in_specs=[pl.BlockSpec((tm, d), lambda i: (i, 0)), pl.BlockSpec((1, d), lambda i: (0, 0))],
        out_specs=pl.BlockSpec((tm, d), lambda i: (i, 0)),
        out_shape=jax.ShapeDtypeStruct((n, d), F32),
        compiler_params=_cparams(("parallel",)),
        name="final_norm",
    )(x, g)


def _rope_tables(seq):
    inv = ROPE_BASE ** (-jnp.arange(0, QK_ROPE, 2, dtype=F32) / QK_ROPE)
    ang = jnp.arange(seq)[:, None].astype(F32) * inv[None, :]
    cos, sin = jnp.cos(ang), jnp.sin(ang)
    pad = LANES - QK_NOPE - QK_ROPE
    ctab = jnp.concatenate([jnp.ones((seq, QK_NOPE), F32), cos, cos, jnp.zeros((seq, pad), F32)], axis=1)
    stab = jnp.concatenate([jnp.zeros((seq, QK_NOPE), F32), sin, sin, jnp.zeros((seq, pad), F32)], axis=1)
    return ctab, stab


def _split_w_in(w_in):
    d = w_in.shape[0]
    bounds = np.cumsum([WIDTH_A, WIDTH_A, WIDTH_A, Q_LORA, KV_LORA, QK_ROPE, WIDTH_C, WIDTH_C])
    qa, ka, va, cq, ckv, kr, qc, kc, vc = jnp.split(w_in, bounds.tolist(), axis=1)
    half = QK_ROPE // 2
    z_lo = jnp.zeros((d, QK_NOPE), w_in.dtype)
    z_hi = jnp.zeros((d, LANES - QK_NOPE - QK_ROPE), w_in.dtype)
    rope_blk = jnp.concatenate([z_lo, kr, z_hi], axis=1)
    swap_blk = jnp.concatenate([z_lo, -kr[:, half:], kr[:, :half], z_hi], axis=1)
    wa = jnp.concatenate([qa, ka, va], axis=1).astype(BF16)
    wc = jnp.concatenate([qc, kc, vc], axis=1).astype(BF16)
    wb = jnp.concatenate([cq, ckv, rope_blk, swap_blk], axis=1).astype(BF16)
    return wa, wc, wb


def _split_w_uq(w_uq):
    r = w_uq.shape[0]
    w = w_uq.reshape(r, N_HEADS_B, QK_NOPE + QK_ROPE)
    nope, rope = w[..., :QK_NOPE], w[..., QK_NOPE:]
    half = QK_ROPE // 2
    z_hi = jnp.zeros((r, N_HEADS_B, LANES - QK_NOPE - QK_ROPE), w_uq.dtype)
    w1 = jnp.concatenate([nope, rope, z_hi], axis=-1)
    w2 = jnp.concatenate([jnp.zeros_like(nope), -rope[..., half:], rope[..., :half], z_hi], axis=-1)
    return w1.reshape(r, -1).astype(BF16), w2.reshape(r, -1).astype(BF16)


def _split_w_ukv(w_ukv):
    r = w_ukv.shape[0]
    w = w_ukv.reshape(r, N_HEADS_B, QK_NOPE + V_HEAD)
    k_nope, v = w[..., :QK_NOPE], w[..., QK_NOPE:]
    wk = jnp.concatenate([k_nope, jnp.zeros((r, N_HEADS_B, LANES - QK_NOPE), w_ukv.dtype)], axis=-1)
    return wk.reshape(r, -1).astype(BF16), v.reshape(r, -1).T.astype(BF16)


def kernel(x, g_mix, w_in, g_q, g_kv, w_uq, w_ukv, rpb, g_out_a, g_out_b, g_out_c, w_o, g_ffn, w1, w3, w2,
           w_router, e_w1, e_w3, e_w2, g_final):
    batch, seq, d = x.shape
    n = batch * seq
    depth = g_mix.shape[0]
    rows = seq // GRID_W
    ctab, stab = _rope_tables(seq)
    xf = x.reshape(n, d)
    for layer in range(depth):
        wa, wc, wb = _split_w_in(w_in[layer])
        wq1, wq2 = _split_w_uq(w_uq[layer])
        wk, wvt = _split_w_ukv(w_ukv[layer])
        pa, pc, qm, km, vt = _proj_call(xf, g_mix[layer][None], wa, wc, wb, g_q[layer][None], g_kv[layer][None],
                                        wq1, wq2, wk, wvt, ctab, stab, seq)
        o_parts, lse_parts = [], []
        for dil in DILATIONS:
            o, lse = _band_call(pa.reshape(batch, seq // dil, dil * 3 * WIDTH_A), dil)
            o_parts.append(o.reshape(n, WIDTH_A))
            lse_parts.append(lse.reshape(n, WIDTH_A))
        ob = _mla_call(qm, km, vt, batch, seq).reshape(n, WIDTH_B)
        oc = _na_call(pc, _na_tables(rpb[layer], rows), batch, seq).reshape(n, WIDTH_C)
        xf = _mix_out_call(xf, o_parts, lse_parts, ob, oc, g_out_a[layer][None], g_out_b[layer][None],
                           g_out_c[layer][None], w_o[layer].astype(BF16))
        j = layer // 2
        if layer % 2 == 0:
            xf = _ffn_call(xf, g_ffn[layer][None], w1[j].astype(BF16), w3[j].astype(BF16), w2[j].astype(BF16))
        else:
            wr_pad = jnp.pad(w_router[j], ((0, 0), (0, LANES - N_EXPERTS)))
            xf = _moe_layer(xf, g_ffn[layer][None], wr_pad, e_w1[j].astype(BF16), e_w3[j].astype(BF16),
                            e_w2[j].astype(BF16))
    return _norm_call(xf, g_final[None]).reshape(batch, seq, d)
```

```python
import functools
import math

import numpy as np
import jax
import jax.numpy as jnp
from jax import lax
from jax.experimental import pallas as pl
from jax.experimental.pallas import tpu as pltpu

F32 = jnp.float32
BF16 = jnp.bfloat16

LANES = 128
V7X_VMEM_LIMIT_BYTES = 52 * 1024 * 1024

HEAD_DIM = 64
N_HEADS_A = 6
DILATIONS = (1, 4, 16)
BAND_HALF = 64
N_HEADS_B = 6
Q_LORA = 384
KV_LORA = 256
QK_NOPE = 64
QK_ROPE = 32
V_HEAD = 64
ROPE_BASE = 10000.0
N_HEADS_C = 4
GRID_W = 64
NA_ROWS = 8
NA_COLS = 16
WIDTH_A = N_HEADS_A * HEAD_DIM
WIDTH_B = N_HEADS_B * V_HEAD
WIDTH_C = N_HEADS_C * HEAD_DIM
N_EXPERTS = 8
TOP_K = 2
RMS_EPS = 1e-6
NEG_INF = -1e30

ROW_TILE = 512
MLA_TQ = 256
MLA_TK = 256
MLA_UNROLL = 8
MLA_LOOKAHEAD = 2
MLA_HEADS_PER_STEP = 6
MLA_DEN_ROWS = 16
MLA_Q_PRESCALE = (QK_NOPE + QK_ROPE) ** -0.5 * math.log2(math.e)
BAND_TQ = 128
NA_TILE_ROWS = 2
NA_KEY_ROWS = 10
MOE_TM = 512
MOE_TF = 896
FFN_TF = 1408
COMBINE_TM = 256
GATHER_UNROLL = 8


def _cparams(semantics):
    return pltpu.CompilerParams(dimension_semantics=semantics, vmem_limit_bytes=V7X_VMEM_LIMIT_BYTES)


def _rms(x, g):
    return x * lax.rsqrt(jnp.mean(x * x, axis=-1, keepdims=True) + RMS_EPS) * g


def _dot(a, b):
    return jnp.dot(a, b, preferred_element_type=F32)


def _dot_nt(a, b):
    return lax.dot_general(a, b, (((1,), (1,)), ((), ())), preferred_element_type=F32)


def _proj_kernel(x_ref, g_ref, wa_ref, wc_ref, wb_ref, gq_ref, gkv_ref, wq1_ref, wq2_ref, wk_ref, wvt_ref,
                 ct_ref, st_ref, pa_ref, pc_ref, qm_ref, km_ref, vt_ref):
    h = _rms(x_ref[...], g_ref[...]).astype(BF16)
    pa_ref[...] = _dot(h, wa_ref[...]).astype(BF16)
    pc_ref[...] = _dot(h, wc_ref[...]).astype(BF16)
    pb = _dot(h, wb_ref[...])
    hq = _rms(pb[:, :Q_LORA], gq_ref[...]).astype(BF16)
    hkv = _rms(pb[:, Q_LORA:Q_LORA + KV_LORA], gkv_ref[...]).astype(BF16)
    r1 = pb[:, Q_LORA + KV_LORA:Q_LORA + KV_LORA + LANES]
    r2 = pb[:, Q_LORA + KV_LORA + LANES:]
    ct = ct_ref[...]
    st = st_ref[...]
    qa = _dot(hq, wq1_ref[...])
    qb = _dot(hq, wq2_ref[...])
    kn = _dot(hkv, wk_ref[...])
    kr = r1 * ct + r2 * st
    for hd in range(N_HEADS_B):
        sl = slice(hd * LANES, (hd + 1) * LANES)
        qm_ref[:, sl] = ((qa[:, sl] * ct + qb[:, sl] * st) * MLA_Q_PRESCALE).astype(BF16)
        km_ref[:, sl] = (kn[:, sl] + kr).astype(BF16)
    vt_ref[...] = _dot_nt(wvt_ref[...], hkv).astype(BF16)


def _proj_call(x, g, wa, wc, wb, gq, gkv, wq1, wq2, wk, wvt, ctab, stab, seq):
    n, d = x.shape
    tm = ROW_TILE
    tiles_per_seq = seq // tm
    full = lambda a: pl.BlockSpec(a.shape, lambda i: (0,) * a.ndim)
    row = lambda w: pl.BlockSpec((tm, w), lambda i: (i, 0))
    tab = pl.BlockSpec((tm, LANES), lambda i: (i % tiles_per_seq, 0))
    hb = N_HEADS_B * LANES
    return pl.pallas_call(
        _proj_kernel,
        grid=(n // tm,),
        in_specs=[row(d), full(g), full(wa), full(wc), full(wb), full(gq), full(gkv), full(wq1), full(wq2),
                  full(wk), full(wvt), tab, tab],
        out_specs=[row(3 * WIDTH_A), row(3 * WIDTH_C), row(hb), row(hb),
                   pl.BlockSpec((WIDTH_B, tm), lambda i: (0, i))],
        out_shape=[jax.ShapeDtypeStruct((n, 3 * WIDTH_A), BF16), jax.ShapeDtypeStruct((n, 3 * WIDTH_C), BF16),
                   jax.ShapeDtypeStruct((n, hb), BF16), jax.ShapeDtypeStruct((n, hb), BF16),
                   jax.ShapeDtypeStruct((WIDTH_B, n), BF16)],
        compiler_params=_cparams(("parallel",)),
        name="proj",
    )(x, g, wa, wc, wb, gq, gkv, wq1, wq2, wk, wvt, ctab, stab)


def _band_kernel(q_ref, k_ref, v_ref, o_ref, lse_ref, *, dilation, n):
    tq = BAND_TQ
    tkw = tq + 2 * BAND_HALF
    i0 = pl.program_id(2) * tq
    start = pl.multiple_of(jnp.clip(i0 - BAND_HALF, 0, n - tkw), BAND_HALF)
    q = q_ref[...]
    k = k_ref[pl.ds(start, tkw), :]
    v = v_ref[pl.ds(start, tkw), :]
    rel = (start - i0) + lax.broadcasted_iota(jnp.int32, (tq, tkw), 1) - lax.broadcasted_iota(jnp.int32, (tq, tkw), 0)
    dist_i = jnp.abs(rel)
    valid = dist_i <= BAND_HALF
    dist = dist_i.astype(F32)
    scale = HEAD_DIM ** -0.5
    head = lambda h: slice(h * HEAD_DIM, (h + 1) * HEAD_DIM)
    scores = [_dot_nt(q[:, head(h)], k[:, head(h)]) for h in range(N_HEADS_A)]
    probs, dens, lses = [], [], []
    for h in range(N_HEADS_A):
        slope = 2.0 ** (-8.0 * (h + 1) / N_HEADS_A)
        s = scores[h] * scale - (slope * dilation) * dist
        s = jnp.where(valid, s, NEG_INF)
        m = jnp.max(s, axis=-1, keepdims=True)
        p = jnp.exp(s - m)
        l = jnp.sum(p, axis=-1, keepdims=True)
        probs.append(p.astype(BF16))
        dens.append(l)
        lses.append(jnp.broadcast_to(m + jnp.log(l), (tq, HEAD_DIM)))
    outs = [_dot(probs[h], v[:, head(h)]) / dens[h] for h in range(N_HEADS_A)]
    o_ref[...] = jnp.concatenate(outs, axis=-1)
    lse_ref[...] = jnp.concatenate(lses, axis=-1)


def _band_call(pa_view, dilation):
    b, n, _ = pa_view.shape
    tq = BAND_TQ
    w = WIDTH_A
    qspec = pl.BlockSpec((None, tq, w), lambda bb, r, i: (bb, i, 3 * r))
    kspec = pl.BlockSpec((None, n, w), lambda bb, r, i: (bb, 0, 3 * r + 1))
    vspec = pl.BlockSpec((None, n, w), lambda bb, r, i: (bb, 0, 3 * r + 2))
    ospec = pl.BlockSpec((None, tq, w), lambda bb, r, i: (bb, i, r))
    shape = jax.ShapeDtypeStruct((b, n, dilation * w), F32)
    return pl.pallas_call(
        functools.partial(_band_kernel, dilation=dilation, n=n),
        grid=(b, dilation, n // tq),
        in_specs=[qspec, kspec, vspec],
        out_specs=[ospec, ospec],
        out_shape=[shape, shape],
        compiler_params=_cparams(("parallel", "parallel", "parallel")),
        name=f"band_d{dilation}",
    )(pa_view, pa_view, pa_view)


def _mla_kernel(q_ref, k_ref, vt_ref, o_ref):
    tq = q_ref.shape[0]
    seq = k_ref.shape[0]
    tk = MLA_TK
    nh = MLA_HEADS_PER_STEP
    n_chunks = seq // tk
    items = [(u, h) for u in range(MLA_UNROLL) for h in range(nh)]
    look = MLA_LOOKAHEAD
    ones_rows = jnp.ones((MLA_DEN_ROWS, tk), BF16)

    def key_slice(chunk):
        return pl.ds(pl.multiple_of(chunk * tk, tk), tk)

    def score_matmul(chunk, h):
        k = k_ref[key_slice(chunk), h * LANES:(h + 1) * LANES]
        return _dot_nt(k, q_ref[:, h * LANES:(h + 1) * LANES])

    def value_matmul(chunk, h, p):
        vt = jnp.concatenate([vt_ref[h * V_HEAD:(h + 1) * V_HEAD, key_slice(chunk)], ones_rows], axis=0)
        return _dot(vt, p)

    def body(j, carry):
        state = list(carry[:2 * nh])
        scores = dict(zip(items[:look], carry[2 * nh:2 * nh + look]))
        pend_p, pend_alpha = carry[2 * nh + look:]
        pending = (jnp.maximum(j * MLA_UNROLL - 1, 0), nh - 1, pend_p, pend_alpha)
        ahead = []
        for idx, (u, h) in enumerate(items):
            la = idx + look
            if la < len(items):
                lu, lh = items[la]
                scores[lu, lh] = score_matmul(j * MLA_UNROLL + lu, lh)
            else:
                lu, lh = items[la - len(items)]
                ahead.append(score_matmul(jnp.minimum((j + 1) * MLA_UNROLL + lu, n_chunks - 1), lh))
            pc, ph, pp, pa = pending
            state[2 * ph + 1] = pa * state[2 * ph + 1] + value_matmul(pc, ph, pp)
            s = scores.pop((u, h))
            m_new = jnp.maximum(state[2 * h], jnp.max(s, axis=0, keepdims=True))
            alpha = jnp.exp2(state[2 * h] - m_new)
            state[2 * h] = m_new
            pending = (j * MLA_UNROLL + u, h, jnp.exp2(s - m_new).astype(BF16), alpha)
        return tuple(state) + tuple(ahead) + (pending[2], pending[3])

    init = (jnp.full((1, tq), NEG_INF, F32), jnp.zeros((V_HEAD + MLA_DEN_ROWS, tq), F32)) * nh
    init += tuple(score_matmul(u, h) for u, h in items[:look])
    init += (jnp.zeros((tk, tq), BF16), jnp.ones((1, tq), F32))
    res = lax.fori_loop(0, n_chunks // MLA_UNROLL, body, init)
    accs = [res[2 * h + 1] for h in range(nh)]
    accs[nh - 1] = res[-1] * accs[nh - 1] + value_matmul(n_chunks - 1, nh - 1, res[-2])
    out_t = jnp.concatenate([a[:V_HEAD] / a[V_HEAD:V_HEAD + 1] for a in accs], axis=0)
    o_ref[...] = out_t.T


def _mla_call(qm, km, vt, batch, seq):
    tq = MLA_TQ
    nh = MLA_HEADS_PER_STEP
    qspec = pl.BlockSpec((None, tq, nh * LANES), lambda b, g, i: (b, i, g))
    kspec = pl.BlockSpec((None, seq, nh * LANES), lambda b, g, i: (b, 0, g))
    vspec = pl.BlockSpec((nh * V_HEAD, seq), lambda b, g, i: (g, b))
    ospec = pl.BlockSpec((None, tq, nh * V_HEAD), lambda b, g, i: (b, i, g))
    return pl.pallas_call(
        _mla_kernel,
        grid=(batch, N_HEADS_B // nh, seq // tq),
        in_specs=[qspec, kspec, vspec],
        out_specs=ospec,
        out_shape=jax.ShapeDtypeStruct((batch, seq, WIDTH_B), F32),
        compiler_params=_cparams(("parallel", "parallel", "parallel")),
        name="mla",
    )(qm.reshape(batch, seq, -1), km.reshape(batch, seq, -1), vt)


def _na_variant(i, n_tiles):
    return jnp.minimum(i, 2) + jnp.maximum(i - (n_tiles - 3), 0)


def _na_kernel(q_ref, k_ref, v_ref, tab_ref, o_ref, *, rows):
    tq = NA_TILE_ROWS * GRID_W
    tkw = NA_KEY_ROWS * GRID_W
    i = pl.program_id(1)
    base = jnp.clip(i * NA_TILE_ROWS - NA_ROWS // 2, 0, rows - NA_KEY_ROWS)
    start = pl.multiple_of(base * GRID_W, GRID_W)
    q = q_ref[...]
    k = k_ref[pl.ds(start, tkw), :]
    v = v_ref[pl.ds(start, tkw), :]
    scale = HEAD_DIM ** -0.5
    head = lambda h: slice(h * HEAD_DIM, (h + 1) * HEAD_DIM)
    scores = [_dot_nt(q[:, head(h)], k[:, head(h)]) for h in range(N_HEADS_C)]
    probs, dens = [], []
    for h in range(N_HEADS_C):
        s = scores[h] * scale + tab_ref[h]
        m = jnp.max(s, axis=-1, keepdims=True)
        p = jnp.exp(s - m)
        dens.append(jnp.sum(p, axis=-1, keepdims=True))
        probs.append(p.astype(BF16))
    outs = [_dot(probs[h], v[:, head(h)]) / dens[h] for h in range(N_HEADS_C)]
    o_ref[...] = jnp.concatenate(outs, axis=-1)


def _na_tables(rpb, rows):
    r0 = np.array([0, 2, 4, rows - 4, rows - 2])
    base = np.clip(r0 - NA_ROWS // 2, 0, rows - NA_KEY_ROWS)
    r = r0[:, None] + np.arange(NA_TILE_ROWS)[None, :]
    row_start = np.clip(r - NA_ROWS // 2, 0, rows - NA_ROWS)
    krow = base[:, None] + np.arange(NA_KEY_ROWS)[None, :]
    drow = krow[:, None, :] - r[:, :, None]
    row_ok = (krow[:, None, :] >= row_start[:, :, None]) & (krow[:, None, :] < row_start[:, :, None] + NA_ROWS)
    c = np.arange(GRID_W)
    win_start = np.clip(c - NA_COLS // 2, 0, GRID_W - NA_COLS)
    col_ok = (c[None, :] >= win_start[:, None]) & (c[None, :] < win_start[:, None] + NA_COLS)
    dcol = np.clip(c[None, :] - c[:, None], -(NA_COLS - 1), NA_COLS - 1)
    ok = row_ok[:, :, None, :, None] & col_ok[None, None, :, None, :]
    di = np.clip(drow, -(NA_ROWS - 1), NA_ROWS - 1) + (NA_ROWS - 1)
    pick_col = (dcol[:, :, None] + NA_COLS - 1 == np.arange(2 * NA_COLS - 1)).astype(np.float32)
    pick_row = (di[..., None] == np.arange(2 * NA_ROWS - 1)).astype(np.float32)
    hi = lax.Precision.HIGHEST
    toeplitz = jnp.einsum("hab,cjb->hacj", rpb.astype(F32), pick_col, precision=hi)
    bias = jnp.einsum("vqka,hacj->vhqckj", pick_row, toeplitz, precision=hi)
    tab = jnp.where(jnp.asarray(ok)[:, None], bias, NEG_INF)
    return tab.reshape(5, N_HEADS_C, NA_TILE_ROWS * GRID_W, NA_KEY_ROWS * GRID_W)


def _na_call(pc, tab, batch, seq):
    rows = seq // GRID_W
    tq = NA_TILE_ROWS * GRID_W
    n_tiles = rows // NA_TILE_ROWS
    w = WIDTH_C
    qspec = pl.BlockSpec((None, tq, w), lambda b, i: (b, i, 0))
    kspec = pl.BlockSpec((None, seq, w), lambda b, i: (b, 0, 1))
    vspec = pl.BlockSpec((None, seq, w), lambda b, i: (b, 0, 2))
    tspec = pl.BlockSpec((None,) + tab.shape[1:], lambda b, i: (_na_variant(i, n_tiles), 0, 0, 0))
    ospec = pl.BlockSpec((None, tq, w), lambda b, i: (b, i, 0))
    pc3 = pc.reshape(batch, seq, 3 * w)
    return pl.pallas_call(
        functools.partial(_na_kernel, rows=rows),
        grid=(batch, n_tiles),
        in_specs=[qspec, kspec, vspec, tspec],
        out_specs=ospec,
        out_shape=jax.ShapeDtypeStruct((batch, seq, w), F32),
        compiler_params=_cparams(("parallel", "parallel")),
        name="natten",
    )(pc3, pc3, pc3, tab)


def _mix_out_kernel(x_ref, o1_ref, o2_ref, o3_ref, l1_ref, l2_ref, l3_ref, ob_ref, oc_ref,
                    ga_ref, gb_ref, gc_ref, wo_ref, out_ref):
    l1, l2, l3 = l1_ref[...], l2_ref[...], l3_ref[...]
    mx = jnp.maximum(jnp.maximum(l1, l2), l3)
    w1, w2, w3 = jnp.exp(l1 - mx), jnp.exp(l2 - mx), jnp.exp(l3 - mx)
    oa = (w1 * o1_ref[...] + w2 * o2_ref[...] + w3 * o3_ref[...]) / (w1 + w2 + w3)
    ya = _rms(oa, ga_ref[...]).astype(BF16)
    yb = _rms(ob_ref[...], gb_ref[...]).astype(BF16)
    yc = _rms(oc_ref[...], gc_ref[...]).astype(BF16)
    y = _dot(ya, wo_ref[:WIDTH_A, :])
    y = y + _dot(yb, wo_ref[WIDTH_A:WIDTH_A + WIDTH_B, :])
    y = y + _dot(yc, wo_ref[WIDTH_A + WIDTH_B:, :])
    out_ref[...] = x_ref[...] + y


def _mix_out_call(x, o_parts, lse_parts, ob, oc, ga, gb, gc, wo):
    n, d = x.shape
    tm = ROW_TILE
    full = lambda a: pl.BlockSpec(a.shape, lambda i: (0,) * a.ndim)
    row = lambda w: pl.BlockSpec((tm, w), lambda i: (i, 0))
    return pl.pallas_call(
        _mix_out_kernel,
        grid=(n // tm,),
        in_specs=[row(d)] + [row(WIDTH_A)] * 6 + [row(WIDTH_B), row(WIDTH_C), full(ga), full(gb), full(gc), full(wo)],
        out_specs=row(d),
        out_shape=jax.ShapeDtypeStruct((n, d), F32),
        compiler_params=_cparams(("parallel",)),
        name="mix_out",
    )(x, *o_parts, *lse_parts, ob, oc, ga, gb, gc, wo)


def _silu(u):
    return u * (1.0 / (1.0 + jnp.exp(-u)))


def _ffn_kernel(x_ref, g_ref, w1_ref, w3_ref, w2_ref, out_ref, h_ref, acc_ref):
    f = pl.program_id(1)

    @pl.when(f == 0)
    def _():
        h_ref[...] = _rms(x_ref[...], g_ref[...]).astype(BF16)
        acc_ref[...] = jnp.zeros_like(acc_ref)

    h = h_ref[...]
    a = (_silu(_dot(h, w1_ref[...])) * _dot(h, w3_ref[...])).astype(BF16)
    acc_ref[...] += _dot(a, w2_ref[...])

    @pl.when(f == pl.num_programs(1) - 1)
    def _():
        out_ref[...] = x_ref[...] + acc_ref[...]


def _ffn_call(x, g, w1, w3, w2):
    n, d = x.shape
    ff = w1.shape[1]
    tm, tf = ROW_TILE, FFN_TF
    return pl.pallas_call(
        _ffn_kernel,
        grid=(n // tm, ff // tf),
        in_specs=[pl.BlockSpec((tm, d), lambda i, f: (i, 0)), pl.BlockSpec((1, d), lambda i, f: (0, 0)),
                  pl.BlockSpec((d, tf), lambda i, f: (0, f)), pl.BlockSpec((d, tf), lambda i, f: (0, f)),
                  pl.BlockSpec((tf, d), lambda i, f: (f, 0))],
        out_specs=pl.BlockSpec((tm, d), lambda i, f: (i, 0)),
        out_shape=jax.ShapeDtypeStruct((n, d), F32),
        scratch_shapes=[pltpu.VMEM((tm, d), BF16), pltpu.VMEM((tm, d), F32)],
        compiler_params=_cparams(("parallel", "arbitrary")),
        name="ffn",
    )(x, g, w1, w3, w2)


def _router_kernel(x_ref, g_ref, wr_ref, h_ref, e_ref, gate_ref):
    h = _rms(x_ref[...], g_ref[...])
    h_ref[...] = h
    logits = jnp.dot(h, wr_ref[...], precision=lax.Precision.HIGHEST, preferred_element_type=F32)
    lane = lax.broadcasted_iota(jnp.int32, logits.shape, 1).astype(F32)
    logits = jnp.where(lane < N_EXPERTS, logits, -jnp.inf)
    m1 = jnp.max(logits, axis=-1, keepdims=True)
    i1 = jnp.min(jnp.where(logits == m1, lane, float(LANES)), axis=-1, keepdims=True)
    rest = jnp.where(lane == i1, -jnp.inf, logits)
    m2 = jnp.max(rest, axis=-1, keepdims=True)
    i2 = jnp.min(jnp.where(rest == m2, lane, float(LANES)), axis=-1, keepdims=True)
    e = jnp.exp(m2 - m1)
    den = 1.0 + e
    e_ref[...] = jnp.where(lane == 0.0, i1, jnp.where(lane == 1.0, i2, 0.0)).astype(jnp.int32)
    gate_ref[...] = jnp.where(lane == 0.0, 1.0 / den, jnp.where(lane == 1.0, e / den, 0.0))


def _router_call(x, g, wr_pad):
    n, d = x.shape
    tm = ROW_TILE
    row = lambda w: pl.BlockSpec((tm, w), lambda i: (i, 0))
    full = lambda a: pl.BlockSpec(a.shape, lambda i: (0,) * a.ndim)
    return pl.pallas_call(
        _router_kernel,
        grid=(n // tm,),
        in_specs=[row(d), full(g), full(wr_pad)],
        out_specs=[row(d), row(LANES), row(LANES)],
        out_shape=[jax.ShapeDtypeStruct((n, d), F32), jax.ShapeDtypeStruct((n, LANES), jnp.int32),
                   jax.ShapeDtypeStruct((n, LANES), F32)],
        compiler_params=_cparams(("parallel",)),
        name="router",
    )(x, g, wr_pad)


def _row_copy(src_hbm, row, dst_ref, r, sem):
    return pltpu.make_async_copy(src_hbm.at[pl.ds(row, 1)], dst_ref.at[pl.ds(r, 1)], sem)


def _start_row_gather(idx_ref, base, src_hbm, dst_ref, sem, count):
    def body(r, c):
        _row_copy(src_hbm, idx_ref[base + r], dst_ref, r, sem).start()
        return c
    lax.fori_loop(0, count, body, 0, unroll=GATHER_UNROLL)


def _wait_row_gather(src_hbm, dst_ref, sem, count):
    pltpu.make_async_copy(src_hbm.at[pl.ds(0, count)], dst_ref, sem).wait()


def _moe_ffn_kernel(tile_e_ref, tile_ok_ref, row_tok_ref, h_hbm, w1_ref, w3_ref, w2_ref, out_ref,
                    xg_ref, hb_ref, acc_ref, sem):
    i = pl.program_id(0)
    f = pl.program_id(1)
    n_tiles = pl.num_programs(0)
    tm = MOE_TM
    slot = i % 2

    @pl.when(f == 0)
    def _():
        @pl.when(i == 0)
        def _():
            _start_row_gather(row_tok_ref, 0, h_hbm, xg_ref.at[0], sem.at[0], tm)

        _wait_row_gather(h_hbm, xg_ref.at[slot], sem.at[slot], tm)

        @pl.when(i + 1 < n_tiles)
        def _():
            _start_row_gather(row_tok_ref, (i + 1) * tm, h_hbm, xg_ref.at[1 - slot], sem.at[1 - slot], tm)

        hb_ref[...] = xg_ref[slot].astype(BF16)
        acc_ref[...] = jnp.zeros_like(acc_ref)

    @pl.when(tile_ok_ref[i] != 0)
    def _():
        h = hb_ref[...]
        a = (_silu(_dot(h, w1_ref[...])) * _dot(h, w3_ref[...])).astype(BF16)
        acc_ref[...] += _dot(a, w2_ref[...])

    @pl.when(f == pl.num_programs(1) - 1)
    def _():
        out_ref[...] = acc_ref[...]


def _moe_ffn_call(tile_e, tile_ok, row_tok, h, w1, w3, w2):
    n, d = h.shape
    n_tiles = tile_e.shape[0]
    ff = w1.shape[2]
    tm, tf = MOE_TM, MOE_TF
    grid_spec = pltpu.PrefetchScalarGridSpec(
        num_scalar_prefetch=3,
        grid=(n_tiles, ff // tf),
        in_specs=[pl.BlockSpec(memory_space=pl.ANY),
                  pl.BlockSpec((None, d, tf), lambda i, f, te, tv, rt: (te[i], 0, f)),
                  pl.BlockSpec((None, d, tf), lambda i, f, te, tv, rt: (te[i], 0, f)),
                  pl.BlockSpec((None, tf, d), lambda i, f, te, tv, rt: (te[i], f, 0))],
        out_specs=pl.BlockSpec((tm, d), lambda i, f, te, tv, rt: (i, 0)),
        scratch_shapes=[pltpu.VMEM((2, tm, d), F32), pltpu.VMEM((tm, d), BF16), pltpu.VMEM((tm, d), F32),
                        pltpu.SemaphoreType.DMA((2,))],
    )
    return pl.pallas_call(
        _moe_ffn_kernel,
        grid_spec=grid_spec,
        out_shape=jax.ShapeDtypeStruct((n_tiles * tm, d), F32),
        compiler_params=_cparams(("arbitrary", "arbitrary")),
        name="moe_ffn",
    )(tile_e, tile_ok, row_tok, h, w1, w3, w2)


def _combine_kernel(d0_ref, d1_ref, y_hbm, x_ref, gate_ref, out_ref, b0_ref, b1_ref, sem):
    i = pl.program_id(0)
    n_tiles = pl.num_programs(0)
    tm = COMBINE_TM
    slot = i % 2

    def start(tile, s):
        _start_row_gather(d0_ref, tile * tm, y_hbm, b0_ref.at[s], sem.at[0, s], tm)
        _start_row_gather(d1_ref, tile * tm, y_hbm, b1_ref.at[s], sem.at[1, s], tm)

    @pl.when(i == 0)
    def _():
        start(0, 0)

    _wait_row_gather(y_hbm, b0_ref.at[slot], sem.at[0, slot], tm)
    _wait_row_gather(y_hbm, b1_ref.at[slot], sem.at[1, slot], tm)

    @pl.when(i + 1 < n_tiles)
    def _():
        start(i + 1, 1 - slot)

    gates = gate_ref[...]
    out_ref[...] = x_ref[...] + (gates[:, 0:1] * b0_ref[slot] + gates[:, 1:2] * b1_ref[slot])


def _combine_call(d0, d1, y, x, gates):
    n, d = x.shape
    tm = COMBINE_TM
    grid_spec = pltpu.PrefetchScalarGridSpec(
        num_scalar_prefetch=2,
        grid=(n // tm,),
        in_specs=[pl.BlockSpec(memory_space=pl.ANY),
                  pl.BlockSpec((tm, d), lambda i, a, b: (i, 0)),
                  pl.BlockSpec((tm, LANES), lambda i, a, b: (i, 0))],
        out_specs=pl.BlockSpec((tm, d), lambda i, a, b: (i, 0)),
        scratch_shapes=[pltpu.VMEM((2, tm, d), F32), pltpu.VMEM((2, tm, d), F32), pltpu.SemaphoreType.DMA((2, 2))],
    )
    return pl.pallas_call(
        _combine_kernel,
        grid_spec=grid_spec,
        out_shape=jax.ShapeDtypeStruct((n, d), F32),
        compiler_params=_cparams(("arbitrary",)),
        name="moe_combine",
    )(d0, d1, y, x, gates)


def _moe_plan(top_e, n_tiles):
    n_assign = top_e.shape[0] * TOP_K
    flat_e = top_e.reshape(n_assign)
    onehot = (flat_e[:, None] == jnp.arange(N_EXPERTS, dtype=jnp.int32)[None, :]).astype(jnp.int32)
    csum = jnp.cumsum(onehot, axis=0)
    rank = jnp.take_along_axis(csum, flat_e[:, None], axis=1)[:, 0] - 1
    counts = csum[-1]
    padded = (counts + MOE_TM - 1) // MOE_TM * MOE_TM
    pend = jnp.cumsum(padded)
    dest = (pend - padded)[flat_e] + rank
    row_tok = jnp.zeros((n_tiles * MOE_TM,), jnp.int32).at[dest].set(jnp.arange(n_assign, dtype=jnp.int32) // TOP_K)
    tile_start = jnp.arange(n_tiles, dtype=jnp.int32) * MOE_TM
    tile_e = jnp.minimum(jnp.searchsorted(pend, tile_start, side="right"), N_EXPERTS - 1).astype(jnp.int32)
    tile_ok = (tile_start < pend[-1]).astype(jnp.int32)
    dest = dest.reshape(-1, TOP_K).astype(jnp.int32)
    return tile_e, tile_ok, row_tok, dest[:, 0], dest[:, 1]


def _moe_layer(x, g, wr_pad, w1, w3, w2):
    n = x.shape[0]
    h, top_e, gates = _router_call(x, g, wr_pad)
    n_tiles = -(-(n * TOP_K + N_EXPERTS * (MOE_TM - 1)) // MOE_TM)
    tile_e, tile_ok, row_tok, d0, d1 = _moe_plan(top_e[:, :TOP_K], n_tiles)
    y = _moe_ffn_call(tile_e, tile_ok, row_tok, h, w1, w3, w2)
    return _combine_call(d0, d1, y, x, gates)


def _norm_kernel(x_ref, g_ref, o_ref):
    o_ref[...] = _rms(x_ref[...], g_ref[...])


def _norm_call(x, g):
    n, d = x.shape
    tm = ROW_TILE
    return pl.pallas_call(
        _norm_kernel,
        grid=(n // tm,),
        in_specs=[pl.BlockSpec((tm, d), lambda i: (i, 0)), pl.BlockSpec((1, d), lambda i: (0, 0))],
        out_specs=pl.BlockSpec((tm, d), lambda i: (i, 0)),
        out_shape=jax.ShapeDtypeStruct((n, d), F32),
        compiler_params=_cparams(("parallel",)),
        name="final_norm",
    )(x, g)


def _rope_tables(seq):
    inv = ROPE_BASE ** (-jnp.arange(0, QK_ROPE, 2, dtype=F32) / QK_ROPE)
    ang = jnp.arange(seq)[:, None].astype(F32) * inv[None, :]
    cos, sin = jnp.cos(ang), jnp.sin(ang)
    pad = LANES - QK_NOPE - QK_ROPE
    ctab = jnp.concatenate([jnp.ones((seq, QK_NOPE), F32), cos, cos, jnp.zeros((seq, pad), F32)], axis=1)
    stab = jnp.concatenate([jnp.zeros((seq, QK_NOPE), F32), sin, sin, jnp.zeros((seq, pad), F32)], axis=1)
    return ctab, stab


def _split_w_in(w_in):
    d = w_in.shape[0]
    bounds = np.cumsum([WIDTH_A, WIDTH_A, WIDTH_A, Q_LORA, KV_LORA, QK_ROPE, WIDTH_C, WIDTH_C])
    qa, ka, va, cq, ckv, kr, qc, kc, vc = jnp.split(w_in, bounds.tolist(), axis=1)
    half = QK_ROPE // 2
    z_lo = jnp.zeros((d, QK_NOPE), w_in.dtype)
    z_hi = jnp.zeros((d, LANES - QK_NOPE - QK_ROPE), w_in.dtype)
    rope_blk = jnp.concatenate([z_lo, kr, z_hi], axis=1)
    swap_blk = jnp.concatenate([z_lo, -kr[:, half:], kr[:, :half], z_hi], axis=1)
    wa = jnp.concatenate([qa, ka, va], axis=1).astype(BF16)
    wc = jnp.concatenate([qc, kc, vc], axis=1).astype(BF16)
    wb = jnp.concatenate([cq, ckv, rope_blk, swap_blk], axis=1).astype(BF16)
    return wa, wc, wb


def _split_w_uq(w_uq):
    r = w_uq.shape[0]
    w = w_uq.reshape(r, N_HEADS_B, QK_NOPE + QK_ROPE)
    nope, rope = w[..., :QK_NOPE], w[..., QK_NOPE:]
    half = QK_ROPE // 2
    z_hi = jnp.zeros((r, N_HEADS_B, LANES - QK_NOPE - QK_ROPE), w_uq.dtype)
    w1 = jnp.concatenate([nope, rope, z_hi], axis=-1)
    w2 = jnp.concatenate([jnp.zeros_like(nope), -rope[..., half:], rope[..., :half], z_hi], axis=-1)
    return w1.reshape(r, -1).astype(BF16), w2.reshape(r, -1).astype(BF16)


def _split_w_ukv(w_ukv):
    r = w_ukv.shape[0]
    w = w_ukv.reshape(r, N_HEADS_B, QK_NOPE + V_HEAD)
    k_nope, v = w[..., :QK_NOPE], w[..., QK_NOPE:]
    wk = jnp.concatenate([k_nope, jnp.zeros((r, N_HEADS_B, LANES - QK_NOPE), w_ukv.dtype)], axis=-1)
    return wk.reshape(r, -1).astype(BF16), v.reshape(r, -1).T.astype(BF16)


def kernel(x, g_mix, w_in, g_q, g_kv, w_uq, w_ukv, rpb, g_out_a, g_out_b, g_out_c, w_o, g_ffn, w1, w3, w2,
           w_router, e_w1, e_w3, e_w2, g_final):
    batch, seq, d = x.shape
    n = batch * seq
    depth = g_mix.shape[0]
    rows = seq // GRID_W
    ctab, stab = _rope_tables(seq)
    xf = x.reshape(n, d)
    for layer in range(depth):
        wa, wc, wb = _split_w_in(w_in[layer])
        wq1, wq2 = _split_w_uq(w_uq[layer])
        wk, wvt = _split_w_ukv(w_ukv[layer])
        pa, pc, qm, km, vt = _proj_call(xf, g_mix[layer][None], wa, wc, wb, g_q[layer][None], g_kv[layer][None],
                                        wq1, wq2, wk, wvt, ctab, stab, seq)
        o_parts, lse_parts = [], []
        for dil in DILATIONS:
            o, lse = _band_call(pa.reshape(batch, seq // dil, dil * 3 * WIDTH_A), dil)
            o_parts.append(o.reshape(n, WIDTH_A))
            lse_parts.append(lse.reshape(n, WIDTH_A))
        ob = _mla_call(qm, km, vt, batch, seq).reshape(n, WIDTH_B)
        oc = _na_call(pc, _na_tables(rpb[layer], rows), batch, seq).reshape(n, WIDTH_C)
        xf = _mix_out_call(xf, o_parts, lse_parts, ob, oc, g_out_a[layer][None], g_out_b[layer][None],
                           g_out_c[layer][None], w_o[layer].astype(BF16))
        j = layer // 2
        if layer % 2 == 0:
            xf = _ffn_call(xf, g_ffn[layer][None], w1[j].astype(BF16), w3[j].astype(BF16), w2[j].astype(BF16))
        else:
            wr_pad = jnp.pad(w_router[j], ((0, 0), (0, LANES - N_EXPERTS)))
            xf = _moe_layer(xf, g_ffn[layer][None], wr_pad, e_w1[j].astype(BF16), e_w3[j].astype(BF16),
                            e_w2[j].astype(BF16))
    return _norm_call(xf, g_final[None]).reshape(batch, seq, d)
```

```python
import functools
import math

import numpy as np
import jax
import jax.numpy as jnp
from jax import lax
from jax.experimental import pallas as pl
from jax.experimental.pallas import tpu as pltpu

F32 = jnp.float32
BF16 = jnp.bfloat16

LANES = 128
V7X_VMEM_LIMIT_BYTES = 52 * 1024 * 1024

HEAD_DIM = 64
N_HEADS_A = 6
DILATIONS = (1, 4, 16)
BAND_HALF = 64
N_HEADS_B = 6
Q_LORA = 384
KV_LORA = 256
QK_NOPE = 64
QK_ROPE = 32
V_HEAD = 64
ROPE_BASE = 10000.0
N_HEADS_C = 4
GRID_W = 64
NA_ROWS = 8
NA_COLS = 16
WIDTH_A = N_HEADS_A * HEAD_DIM
WIDTH_B = N_HEADS_B * V_HEAD
WIDTH_C = N_HEADS_C * HEAD_DIM
N_EXPERTS = 8
TOP_K = 2
RMS_EPS = 1e-6
NEG_INF = -1e30

ROW_TILE = 512
MLA_TQ = 256
MLA_TK = 256
MLA_UNROLL = 8
MLA_LOOKAHEAD = 2
MLA_HEADS_PER_STEP = 6
MLA_DEN_ROWS = 16
MLA_Q_PRESCALE = (QK_NOPE + QK_ROPE) ** -0.5 * math.log2(math.e)
BAND_TQ = 128
NA_TILE_ROWS = 2
NA_KEY_ROWS = 10
MOE_TM = 512
MOE_TF = 896
FFN_TF = 1408
COMBINE_TM = 256
GATHER_UNROLL = 8


def _cparams(semantics):
    return pltpu.CompilerParams(dimension_semantics=semantics, vmem_limit_bytes=V7X_VMEM_LIMIT_BYTES)


def _rms(x, g):
    return x * lax.rsqrt(jnp.mean(x * x, axis=-1, keepdims=True) + RMS_EPS) * g


def _dot(a, b):
    return jnp.dot(a, b, preferred_element_type=F32)


def _dot_nt(a, b):
    return lax.dot_general(a, b, (((1,), (1,)), ((), ())), preferred_element_type=F32)


def _proj_kernel(x_ref, g_ref, wa_ref, wc_ref, wb_ref, gq_ref, gkv_ref, wq1_ref, wq2_ref, wk_ref, wvt_ref,
                 ct_ref, st_ref, pa_ref, pa4_ref, pa16_ref, pc_ref, qm_ref, km_ref, vt_ref, pa_scr):
    tm = x_ref.shape[0]
    h = _rms(x_ref[...], g_ref[...]).astype(BF16)
    pa = _dot(h, wa_ref[...])
    pa_ref[...] = pa.astype(BF16)
    n_chunks = pa.shape[1] // LANES
    for c in range(n_chunks):
        pa_scr[c] = pa[:, c * LANES:(c + 1) * LANES]
    for dil, view_ref in ((DILATIONS[1], pa4_ref), (DILATIONS[2], pa16_ref)):
        for r in range(dil):
            for c in range(n_chunks):
                col = r * 3 * WIDTH_A + c * LANES
                view_ref[:, col:col + LANES] = pa_scr[c, pl.ds(r, tm // dil, stride=dil), :].astype(BF16)
    pc_ref[...] = _dot(h, wc_ref[...]).astype(BF16)
    pb = _dot(h, wb_ref[...])
    hq = _rms(pb[:, :Q_LORA], gq_ref[...]).astype(BF16)
    hkv = _rms(pb[:, Q_LORA:Q_LORA + KV_LORA], gkv_ref[...]).astype(BF16)
    r1 = pb[:, Q_LORA + KV_LORA:Q_LORA + KV_LORA + LANES]
    r2 = pb[:, Q_LORA + KV_LORA + LANES:]
    ct = ct_ref[...]
    st = st_ref[...]
    qa = _dot(hq, wq1_ref[...])
    qb = _dot(hq, wq2_ref[...])
    kn = _dot(hkv, wk_ref[...])
    kr = r1 * ct + r2 * st
    for hd in range(N_HEADS_B):
        sl = slice(hd * LANES, (hd + 1) * LANES)
        qm_ref[:, sl] = ((qa[:, sl] * ct + qb[:, sl] * st) * MLA_Q_PRESCALE).astype(BF16)
        km_ref[:, sl] = (kn[:, sl] + kr).astype(BF16)
    vt_ref[...] = _dot_nt(wvt_ref[...], hkv).astype(BF16)


def _proj_call(x, g, wa, wc, wb, gq, gkv, wq1, wq2, wk, wvt, ctab, stab, seq):
    n, d = x.shape
    tm = ROW_TILE
    tiles_per_seq = seq // tm
    full = lambda a: pl.BlockSpec(a.shape, lambda i: (0,) * a.ndim)
    row = lambda w: pl.BlockSpec((tm, w), lambda i: (i, 0))
    tab = pl.BlockSpec((tm, LANES), lambda i: (i % tiles_per_seq, 0))
    view = lambda dil: pl.BlockSpec((tm // dil, dil * 3 * WIDTH_A), lambda i: (i, 0))
    hb = N_HEADS_B * LANES
    return pl.pallas_call(
        _proj_kernel,
        grid=(n // tm,),
        in_specs=[row(d), full(g), full(wa), full(wc), full(wb), full(gq), full(gkv), full(wq1), full(wq2),
                  full(wk), full(wvt), tab, tab],
        out_specs=[row(3 * WIDTH_A)] + [view(dil) for dil in DILATIONS[1:]] + [row(3 * WIDTH_C), row(hb), row(hb),
                   pl.BlockSpec((WIDTH_B, tm), lambda i: (0, i))],
        out_shape=[jax.ShapeDtypeStruct((n, 3 * WIDTH_A), BF16)]
                  + [jax.ShapeDtypeStruct((n // dil, dil * 3 * WIDTH_A), BF16) for dil in DILATIONS[1:]]
                  + [jax.ShapeDtypeStruct((n, 3 * WIDTH_C), BF16),
                   jax.ShapeDtypeStruct((n, hb), BF16), jax.ShapeDtypeStruct((n, hb), BF16),
                   jax.ShapeDtypeStruct((WIDTH_B, n), BF16)],
        scratch_shapes=[pltpu.VMEM((3 * WIDTH_A // LANES, tm, LANES), F32)],
        compiler_params=_cparams(("parallel",)),
        name="proj",
    )(x, g, wa, wc, wb, gq, gkv, wq1, wq2, wk, wvt, ctab, stab)


def _band_kernel(q_ref, k_ref, v_ref, o_ref, lse_ref, *, dilation, n):
    tq = BAND_TQ
    tkw = tq + 2 * BAND_HALF
    i0 = pl.program_id(2) * tq
    start = pl.multiple_of(jnp.clip(i0 - BAND_HALF, 0, n - tkw), BAND_HALF)
    q = q_ref[...]
    k = k_ref[pl.ds(start, tkw), :]
    v = v_ref[pl.ds(start, tkw), :]
    rel = (start - i0) + lax.broadcasted_iota(jnp.int32, (tq, tkw), 1) - lax.broadcasted_iota(jnp.int32, (tq, tkw), 0)
    dist_i = jnp.abs(rel)
    valid = dist_i <= BAND_HALF
    dist = dist_i.astype(F32)
    scale = HEAD_DIM ** -0.5
    head = lambda h: slice(h * HEAD_DIM, (h + 1) * HEAD_DIM)
    scores = [_dot_nt(q[:, head(h)], k[:, head(h)]) for h in range(N_HEADS_A)]
    probs, dens, lses = [], [], []
    for h in range(N_HEADS_A):
        slope = 2.0 ** (-8.0 * (h + 1) / N_HEADS_A)
        s = scores[h] * scale - (slope * dilation) * dist
        s = jnp.where(valid, s, NEG_INF)
        m = jnp.max(s, axis=-1, keepdims=True)
        p = jnp.exp(s - m)
        l = jnp.sum(p, axis=-1, keepdims=True)
        probs.append(p.astype(BF16))
        dens.append(l)
        lses.append(jnp.broadcast_to(m + jnp.log(l), (tq, HEAD_DIM)))
    outs = [_dot(probs[h], v[:, head(h)]) / dens[h] for h in range(N_HEADS_A)]
    o_ref[...] = jnp.concatenate(outs, axis=-1)
    lse_ref[...] = jnp.concatenate(lses, axis=-1)


def _band_call(pa_view, dilation):
    b, n, _ = pa_view.shape
    tq = BAND_TQ
    w = WIDTH_A
    qspec = pl.BlockSpec((None, tq, w), lambda bb, r, i: (bb, i, 3 * r))
    kspec = pl.BlockSpec((None, n, w), lambda bb, r, i: (bb, 0, 3 * r + 1))
    vspec = pl.BlockSpec((None, n, w), lambda bb, r, i: (bb, 0, 3 * r + 2))
    ospec = pl.BlockSpec((None, tq, w), lambda bb, r, i: (bb, i, r))
    shape = jax.ShapeDtypeStruct((b, n, dilation * w), F32)
    return pl.pallas_call(
        functools.partial(_band_kernel, dilation=dilation, n=n),
        grid=(b, dilation, n // tq),
        in_specs=[qspec, kspec, vspec],
        out_specs=[ospec, ospec],
        out_shape=[shape, shape],
        compiler_params=_cparams(("parallel", "parallel", "parallel")),
        name=f"band_d{dilation}",
    )(pa_view, pa_view, pa_view)


def _mla_kernel(q_ref, k_ref, vt_ref, o_ref):
    tq = q_ref.shape[0]
    seq = k_ref.shape[0]
    tk = MLA_TK
    nh = MLA_HEADS_PER_STEP
    n_chunks = seq // tk
    items = [(u, h) for u in range(MLA_UNROLL) for h in range(nh)]
    look = MLA_LOOKAHEAD
    ones_rows = jnp.ones((MLA_DEN_ROWS, tk), BF16)

    def key_slice(chunk):
        return pl.ds(pl.multiple_of(chunk * tk, tk), tk)

    def score_matmul(chunk, h):
        k = k_ref[key_slice(chunk), h * LANES:(h + 1) * LANES]
        return _dot_nt(k, q_ref[:, h * LANES:(h + 1) * LANES])

    def value_matmul(chunk, h, p):
        vt = jnp.concatenate([vt_ref[h * V_HEAD:(h + 1) * V_HEAD, key_slice(chunk)], ones_rows], axis=0)
        return _dot(vt, p)

    def body(j, carry):
        state = list(carry[:2 * nh])
        scores = dict(zip(items[:look], carry[2 * nh:2 * nh + look]))
        pend_p, pend_alpha = carry[2 * nh + look:]
        pending = (jnp.maximum(j * MLA_UNROLL - 1, 0), nh - 1, pend_p, pend_alpha)
        ahead = []
        for idx, (u, h) in enumerate(items):
            la = idx + look
            if la < len(items):
                lu, lh = items[la]
                scores[lu, lh] = score_matmul(j * MLA_UNROLL + lu, lh)
            else:
                lu, lh = items[la - len(items)]
                ahead.append(score_matmul(jnp.minimum((j + 1) * MLA_UNROLL + lu, n_chunks - 1), lh))
            pc, ph, pp, pa = pending
            state[2 * ph + 1] = pa * state[2 * ph + 1] + value_matmul(pc, ph, pp)
            s = scores.pop((u, h))
            m_new = jnp.maximum(state[2 * h], jnp.max(s, axis=0, keepdims=True))
            alpha = jnp.exp2(state[2 * h] - m_new)
            state[2 * h] = m_new
            pending = (j * MLA_UNROLL + u, h, jnp.exp2(s - m_new).astype(BF16), alpha)
        return tuple(state) + tuple(ahead) + (pending[2], pending[3])

    init = (jnp.full((1, tq), NEG_INF, F32), jnp.zeros((V_HEAD + MLA_DEN_ROWS, tq), F32)) * nh
    init += tuple(score_matmul(u, h) for u, h in items[:look])
    init += (jnp.zeros((tk, tq), BF16), jnp.ones((1, tq), F32))
    res = lax.fori_loop(0, n_chunks // MLA_UNROLL, body, init)
    accs = [res[2 * h + 1] for h in range(nh)]
    accs[nh - 1] = res[-1] * accs[nh - 1] + value_matmul(n_chunks - 1, nh - 1, res[-2])
    out_t = jnp.concatenate([a[:V_HEAD] / a[V_HEAD:V_HEAD + 1] for a in accs], axis=0)
    o_ref[...] = out_t.T


def _mla_call(qm, km, vt, batch, seq):
    tq = MLA_TQ
    nh = MLA_HEADS_PER_STEP
    qspec = pl.BlockSpec((None, tq, nh * LANES), lambda b, g, i: (b, i, g))
    kspec = pl.BlockSpec((None, seq, nh * LANES), lambda b, g, i: (b, 0, g))
    vspec = pl.BlockSpec((nh * V_HEAD, seq), lambda b, g, i: (g, b))
    ospec = pl.BlockSpec((None, tq, nh * V_HEAD), lambda b, g, i: (b, i, g))
    return pl.pallas_call(
        _mla_kernel,
        grid=(batch, N_HEADS_B // nh, seq // tq),
        in_specs=[qspec, kspec, vspec],
        out_specs=ospec,
        out_shape=jax.ShapeDtypeStruct((batch, seq, WIDTH_B), F32),
        compiler_params=_cparams(("parallel", "parallel", "parallel")),
        name="mla",
    )(qm.reshape(batch, seq, -1), km.reshape(batch, seq, -1), vt)


def _na_variant(i, n_tiles):
    return jnp.minimum(i, 2) + jnp.maximum(i - (n_tiles - 3), 0)


def _na_kernel(q_ref, k_ref, v_ref, tab_ref, o_ref, *, rows):
    tq = NA_TILE_ROWS * GRID_W
    tkw = NA_KEY_ROWS * GRID_W
    i = pl.program_id(1)
    base = jnp.clip(i * NA_TILE_ROWS - NA_ROWS // 2, 0, rows - NA_KEY_ROWS)
    start = pl.multiple_of(base * GRID_W, GRID_W)
    q = q_ref[...]
    k = k_ref[pl.ds(start, tkw), :]
    v = v_ref[pl.ds(start, tkw), :]
    scale = HEAD_DIM ** -0.5
    head = lambda h: slice(h * HEAD_DIM, (h + 1) * HEAD_DIM)
    scores = [_dot_nt(q[:, head(h)], k[:, head(h)]) for h in range(N_HEADS_C)]
    probs, dens = [], []
    for h in range(N_HEADS_C):
        s = scores[h] * scale + tab_ref[h]
        m = jnp.max(s, axis=-1, keepdims=True)
        p = jnp.exp(s - m)
        dens.append(jnp.sum(p, axis=-1, keepdims=True))
        probs.append(p.astype(BF16))
    outs = [_dot(probs[h], v[:, head(h)]) / dens[h] for h in range(N_HEADS_C)]
    o_ref[...] = jnp.concatenate(outs, axis=-1)


def _na_tables(rpb, rows):
    r0 = np.array([0, 2, 4, rows - 4, rows - 2])
    base = np.clip(r0 - NA_ROWS // 2, 0, rows - NA_KEY_ROWS)
    r = r0[:, None] + np.arange(NA_TILE_ROWS)[None, :]
    row_start = np.clip(r - NA_ROWS // 2, 0, rows - NA_ROWS)
    krow = base[:, None] + np.arange(NA_KEY_ROWS)[None, :]
    drow = krow[:, None, :] - r[:, :, None]
    row_ok = (krow[:, None, :] >= row_start[:, :, None]) & (krow[:, None, :] < row_start[:, :, None] + NA_ROWS)
    c = np.arange(GRID_W)
    win_start = np.clip(c - NA_COLS // 2, 0, GRID_W - NA_COLS)
    col_ok = (c[None, :] >= win_start[:, None]) & (c[None, :] < win_start[:, None] + NA_COLS)
    dcol = np.clip(c[None, :] - c[:, None], -(NA_COLS - 1), NA_COLS - 1)
    ok = row_ok[:, :, None, :, None] & col_ok[None, None, :, None, :]
    di = np.clip(drow, -(NA_ROWS - 1), NA_ROWS - 1) + (NA_ROWS - 1)
    pick_col = (dcol[:, :, None] + NA_COLS - 1 == np.arange(2 * NA_COLS - 1)).astype(np.float32)
    pick_row = (di[..., None] == np.arange(2 * NA_ROWS - 1)).astype(np.float32)
    hi = lax.Precision.HIGHEST
    toeplitz = jnp.einsum("hab,cjb->hacj", rpb.astype(F32), pick_col, precision=hi)
    bias = jnp.einsum("vqka,hacj->vhqckj", pick_row, toeplitz, precision=hi)
    tab = jnp.where(jnp.asarray(ok)[:, None], bias, NEG_INF)
    return tab.reshape(5, N_HEADS_C, NA_TILE_ROWS * GRID_W, NA_KEY_ROWS * GRID_W)


def _na_call(pc, tab, batch, seq):
    rows = seq // GRID_W
    tq = NA_TILE_ROWS * GRID_W
    n_tiles = rows // NA_TILE_ROWS
    w = WIDTH_C
    qspec = pl.BlockSpec((None, tq, w), lambda b, i: (b, i, 0))
    kspec = pl.BlockSpec((None, seq, w), lambda b, i: (b, 0, 1))
    vspec = pl.BlockSpec((None, seq, w), lambda b, i: (b, 0, 2))
    tspec = pl.BlockSpec((None,) + tab.shape[1:], lambda b, i: (_na_variant(i, n_tiles), 0, 0, 0))
    ospec = pl.BlockSpec((None, tq, w), lambda b, i: (b, i, 0))
    pc3 = pc.reshape(batch, seq, 3 * w)
    return pl.pallas_call(
        functools.partial(_na_kernel, rows=rows),
        grid=(batch, n_tiles),
        in_specs=[qspec, kspec, vspec, tspec],
        out_specs=ospec,
        out_shape=jax.ShapeDtypeStruct((batch, seq, w), F32),
        compiler_params=_cparams(("parallel", "parallel")),
        name="natten",
    )(pc3, pc3, pc3, tab)


def _mix_out_kernel(x_ref, o1_ref, o2_ref, o3_ref, l1_ref, l2_ref, l3_ref, ob_ref, oc_ref,
                    ga_ref, gb_ref, gc_ref, wo_ref, out_ref, o2_scr, o3_scr, l2_scr, l3_scr):
    tm = x_ref.shape[0]
    n_chunks = WIDTH_A // LANES

    def natural(view_ref, scr, dil):
        for r in range(dil):
            for c in range(n_chunks):
                col = r * WIDTH_A + c * LANES
                scr[c, pl.ds(r, tm // dil, stride=dil), :] = view_ref[:, col:col + LANES]
        return jnp.concatenate([scr[c] for c in range(n_chunks)], axis=-1)

    l1, o1 = l1_ref[...], o1_ref[...]
    l2, o2 = natural(l2_ref, l2_scr, DILATIONS[1]), natural(o2_ref, o2_scr, DILATIONS[1])
    l3, o3 = natural(l3_ref, l3_scr, DILATIONS[2]), natural(o3_ref, o3_scr, DILATIONS[2])
    mx = jnp.maximum(jnp.maximum(l1, l2), l3)
    w1, w2, w3 = jnp.exp(l1 - mx), jnp.exp(l2 - mx), jnp.exp(l3 - mx)
    oa = (w1 * o1 + w2 * o2 + w3 * o3) / (w1 + w2 + w3)
    ya = _rms(oa, ga_ref[...]).astype(BF16)
    yb = _rms(ob_ref[...], gb_ref[...]).astype(BF16)
    yc = _rms(oc_ref[...], gc_ref[...]).astype(BF16)
    y = _dot(ya, wo_ref[:WIDTH_A, :])
    y = y + _dot(yb, wo_ref[WIDTH_A:WIDTH_A + WIDTH_B, :])
    y = y + _dot(yc, wo_ref[WIDTH_A + WIDTH_B:, :])
    out_ref[...] = x_ref[...] + y


def _mix_out_call(x, o_parts, lse_parts, ob, oc, ga, gb, gc, wo):
    n, d = x.shape
    tm = ROW_TILE
    full = lambda a: pl.BlockSpec(a.shape, lambda i: (0,) * a.ndim)
    row = lambda w: pl.BlockSpec((tm, w), lambda i: (i, 0))
    view = lambda dil: pl.BlockSpec((tm // dil, dil * WIDTH_A), lambda i: (i, 0))
    return pl.pallas_call(
        _mix_out_kernel,
        grid=(n // tm,),
        in_specs=[row(d)] + [view(dil) for dil in DILATIONS] * 2
                 + [row(WIDTH_B), row(WIDTH_C), full(ga), full(gb), full(gc), full(wo)],
        out_specs=row(d),
        out_shape=jax.ShapeDtypeStruct((n, d), F32),
        scratch_shapes=[pltpu.VMEM((WIDTH_A // LANES, tm, LANES), F32)] * 4,
        compiler_params=_cparams(("parallel",)),
        name="mix_out",
    )(x, *o_parts, *lse_parts, ob, oc, ga, gb, gc, wo)


def _silu(u):
    return u * (1.0 / (1.0 + jnp.exp(-u)))


def _ffn_kernel(x_ref, g_ref, w1_ref, w3_ref, w2_ref, out_ref, h_ref, acc_ref):
    f = pl.program_id(1)

    @pl.when(f == 0)
    def _():
        h_ref[...] = _rms(x_ref[...], g_ref[...]).astype(BF16)
        acc_ref[...] = jnp.zeros_like(acc_ref)

    h = h_ref[...]
    a = (_silu(_dot(h, w1_ref[...])) * _dot(h, w3_ref[...])).astype(BF16)
    acc_ref[...] += _dot(a, w2_ref[...])

    @pl.when(f == pl.num_programs(1) - 1)
    def _():
        out_ref[...] = x_ref[...] + acc_ref[...]


def _ffn_call(x, g, w1, w3, w2):
    n, d = x.shape
    ff = w1.shape[1]
    tm, tf = ROW_TILE, FFN_TF
    return pl.pallas_call(
        _ffn_kernel,
        grid=(n // tm, ff // tf),
        in_specs=[pl.BlockSpec((tm, d), lambda i, f: (i, 0)), pl.BlockSpec((1, d), lambda i, f: (0, 0)),
                  pl.BlockSpec((d, tf), lambda i, f: (0, f)), pl.BlockSpec((d, tf), lambda i, f: (0, f)),
                  pl.BlockSpec((tf, d), lambda i, f: (f, 0))],
        out_specs=pl.BlockSpec((tm, d), lambda i, f: (i, 0)),
        out_shape=jax.ShapeDtypeStruct((n, d), F32),
        scratch_shapes=[pltpu.VMEM((tm, d), BF16), pltpu.VMEM((tm, d), F32)],
        compiler_params=_cparams(("parallel", "arbitrary")),
        name="ffn",
    )(x, g, w1, w3, w2)


def _router_kernel(x_ref, g_ref, wr_ref, h_ref, e_ref, gate_ref):
    h = _rms(x_ref[...], g_ref[...])
    h_ref[...] = h
    logits = jnp.dot(h, wr_ref[...], precision=lax.Precision.HIGHEST, preferred_element_type=F32)
    lane = lax.broadcasted_iota(jnp.int32, logits.shape, 1).astype(F32)
    logits = jnp.where(lane < N_EXPERTS, logits, -jnp.inf)
    m1 = jnp.max(logits, axis=-1, keepdims=True)
    i1 = jnp.min(jnp.where(logits == m1, lane, float(LANES)), axis=-1, keepdims=True)
    rest = jnp.where(lane == i1, -jnp.inf, logits)
    m2 = jnp.max(rest, axis=-1, keepdims=True)
    i2 = jnp.min(jnp.where(rest == m2, lane, float(LANES)), axis=-1, keepdims=True)
    e = jnp.exp(m2 - m1)
    den = 1.0 + e
    e_ref[...] = jnp.where(lane == 0.0, i1, jnp.where(lane == 1.0, i2, 0.0)).astype(jnp.int32)
    gate_ref[...] = jnp.where(lane == 0.0, 1.0 / den, jnp.where(lane == 1.0, e / den, 0.0))


def _router_call(x, g, wr_pad):
    n, d = x.shape
    tm = ROW_TILE
    row = lambda w: pl.BlockSpec((tm, w), lambda i: (i, 0))
    full = lambda a: pl.BlockSpec(a.shape, lambda i: (0,) * a.ndim)
    return pl.pallas_call(
        _router_kernel,
        grid=(n // tm,),
        in_specs=[row(d), full(g), full(wr_pad)],
        out_specs=[row(d), row(LANES), row(LANES)],
        out_shape=[jax.ShapeDtypeStruct((n, d), F32), jax.ShapeDtypeStruct((n, LANES), jnp.int32),
                   jax.ShapeDtypeStruct((n, LANES), F32)],
        compiler_params=_cparams(("parallel",)),
        name="router",
    )(x, g, wr_pad)


def _row_copy(src_hbm, row, dst_ref, r, sem):
    return pltpu.make_async_copy(src_hbm.at[pl.ds(row, 1)], dst_ref.at[pl.ds(r, 1)], sem)


def _start_row_gather(idx_ref, base, src_hbm, dst_ref, sem, count):
    def body(r, c):
        _row_copy(src_hbm, idx_ref[base + r], dst_ref, r, sem).start()
        return c
    lax.fori_loop(0, count, body, 0, unroll=GATHER_UNROLL)


def _wait_row_gather(src_hbm, dst_ref, sem, count):
    pltpu.make_async_copy(src_hbm.at[pl.ds(0, count)], dst_ref, sem).wait()


def _moe_ffn_kernel(tile_e_ref, tile_ok_ref, row_tok_ref, h_hbm, w1_ref, w3_ref, w2_ref, out_ref,
                    xg_ref, hb_ref, acc_ref, sem):
    i = pl.program_id(0)
    f = pl.program_id(1)
    n_tiles = pl.num_programs(0)
    tm = MOE_TM
    slot = i % 2

    @pl.when(f == 0)
    def _():
        @pl.when(i == 0)
        def _():
            _start_row_gather(row_tok_ref, 0, h_hbm, xg_ref.at[0], sem.at[0], tm)

        _wait_row_gather(h_hbm, xg_ref.at[slot], sem.at[slot], tm)

        @pl.when(i + 1 < n_tiles)
        def _():
            _start_row_gather(row_tok_ref, (i + 1) * tm, h_hbm, xg_ref.at[1 - slot], sem.at[1 - slot], tm)

        hb_ref[...] = xg_ref[slot].astype(BF16)
        acc_ref[...] = jnp.zeros_like(acc_ref)

    @pl.when(tile_ok_ref[i] != 0)
    def _():
        h = hb_ref[...]
        a = (_silu(_dot(h, w1_ref[...])) * _dot(h, w3_ref[...])).astype(BF16)
        acc_ref[...] += _dot(a, w2_ref[...])

    @pl.when(f == pl.num_programs(1) - 1)
    def _():
        out_ref[...] = acc_ref[...]


def _moe_ffn_call(tile_e, tile_ok, row_tok, h, w1, w3, w2):
    n, d = h.shape
    n_tiles = tile_e.shape[0]
    ff = w1.shape[2]
    tm, tf = MOE_TM, MOE_TF
    grid_spec = pltpu.PrefetchScalarGridSpec(
        num_scalar_prefetch=3,
        grid=(n_tiles, ff // tf),
        in_specs=[pl.BlockSpec(memory_space=pl.ANY),
                  pl.BlockSpec((None, d, tf), lambda i, f, te, tv, rt: (te[i], 0, f)),
                  pl.BlockSpec((None, d, tf), lambda i, f, te, tv, rt: (te[i], 0, f)),
                  pl.BlockSpec((None, tf, d), lambda i, f, te, tv, rt: (te[i], f, 0))],
        out_specs=pl.BlockSpec((tm, d), lambda i, f, te, tv, rt: (i, 0)),
        scratch_shapes=[pltpu.VMEM((2, tm, d), F32), pltpu.VMEM((tm, d), BF16), pltpu.VMEM((tm, d), F32),
                        pltpu.SemaphoreType.DMA((2,))],
    )
    return pl.pallas_call(
        _moe_ffn_kernel,
        grid_spec=grid_spec,
        out_shape=jax.ShapeDtypeStruct((n_tiles * tm, d), F32),
        compiler_params=_cparams(("arbitrary", "arbitrary")),
        name="moe_ffn",
    )(tile_e, tile_ok, row_tok, h, w1, w3, w2)


def _combine_kernel(d0_ref, d1_ref, y_hbm, x_ref, gate_ref, out_ref, b0_ref, b1_ref, sem):
    i = pl.program_id(0)
    n_tiles = pl.num_programs(0)
    tm = COMBINE_TM
    slot = i % 2

    def start(tile, s):
        _start_row_gather(d0_ref, tile * tm, y_hbm, b0_ref.at[s], sem.at[0, s], tm)
        _start_row_gather(d1_ref, tile * tm, y_hbm, b1_ref.at[s], sem.at[1, s], tm)

    @pl.when(i == 0)
    def _():
        start(0, 0)

    _wait_row_gather(y_hbm, b0_ref.at[slot], sem.at[0, slot], tm)
    _wait_row_gather(y_hbm, b1_ref.at[slot], sem.at[1, slot], tm)

    @pl.when(i + 1 < n_tiles)
    def _():
        start(i + 1, 1 - slot)

    gates = gate_ref[...]
    out_ref[...] = x_ref[...] + (gates[:, 0:1] * b0_ref[slot] + gates[:, 1:2] * b1_ref[slot])


def _combine_call(d0, d1, y, x, gates):
    n, d = x.shape
    tm = COMBINE_TM
    grid_spec = pltpu.PrefetchScalarGridSpec(
        num_scalar_prefetch=2,
        grid=(n // tm,),
        in_specs=[pl.BlockSpec(memory_space=pl.ANY),
                  pl.BlockSpec((tm, d), lambda i, a, b: (i, 0)),
                  pl.BlockSpec((tm, LANES), lambda i, a, b: (i, 0))],
        out_specs=pl.BlockSpec((tm, d), lambda i, a, b: (i, 0)),
        scratch_shapes=[pltpu.VMEM((2, tm, d), F32), pltpu.VMEM((2, tm, d), F32), pltpu.SemaphoreType.DMA((2, 2))],
    )
    return pl.pallas_call(
        _combine_kernel,
        grid_spec=grid_spec,
        out_shape=jax.ShapeDtypeStruct((n, d), F32),
        compiler_params=_cparams(("arbitrary",)),
        name="moe_combine",
    )(d0, d1, y, x, gates)


def _moe_plan(top_e, n_tiles):
    n_assign = top_e.shape[0] * TOP_K
    flat_e = top_e.reshape(n_assign)
    onehot = (flat_e[:, None] == jnp.arange(N_EXPERTS, dtype=jnp.int32)[None, :]).astype(jnp.int32)
    csum = jnp.cumsum(onehot, axis=0)
    rank = jnp.take_along_axis(csum, flat_e[:, None], axis=1)[:, 0] - 1
    counts = csum[-1]
    padded = (counts + MOE_TM - 1) // MOE_TM * MOE_TM
    pend = jnp.cumsum(padded)
    dest = (pend - padded)[flat_e] + rank
    row_tok = jnp.zeros((n_tiles * MOE_TM,), jnp.int32).at[dest].set(jnp.arange(n_assign, dtype=jnp.int32) // TOP_K)
    tile_start = jnp.arange(n_tiles, dtype=jnp.int32) * MOE_TM
    tile_e = jnp.minimum(jnp.searchsorted(pend, tile_start, side="right"), N_EXPERTS - 1).astype(jnp.int32)
    tile_ok = (tile_start < pend[-1]).astype(jnp.int32)
    dest = dest.reshape(-1, TOP_K).astype(jnp.int32)
    return tile_e, tile_ok, row_tok, dest[:, 0], dest[:, 1]


def _moe_layer(x, g, wr_pad, w1, w3, w2):
    n = x.shape[0]
    h, top_e, gates = _router_call(x, g, wr_pad)
    n_tiles = -(-(n * TOP_K + N_EXPERTS * (MOE_TM - 1)) // MOE_TM)
    tile_e, tile_ok, row_tok, d0, d1 = _moe_plan(top_e[:, :TOP_K], n_tiles)
    y = _moe_ffn_call(tile_e, tile_ok, row_tok, h, w1, w3, w2)
    return _combine_call(d0, d1, y, x, gates)


def _norm_kernel(x_ref, g_ref, o_ref):
    o_ref[...] = _rms(x_ref[...], g_ref[...])


def _norm_call(x, g):
    n, d = x.shape
    tm = ROW_TILE
    return pl.pallas_call(
        _norm_kernel,
        grid=(n // tm,),
        in_specs=[pl.BlockSpec((tm, d), lambda i: (i, 0)), pl.BlockSpec((1, d), lambda i: (0, 0))],
        out_specs=pl.BlockSpec((tm, d), lambda i: (i, 0)),
        out_shape=jax.ShapeDtypeStruct((n, d), F32),
        compiler_params=_cparams(("parallel",)),
        name="final_norm",
    )(x, g)


def _rope_tables(seq):
    inv = ROPE_BASE ** (-jnp.arange(0, QK_ROPE, 2, dtype=F32) / QK_ROPE)
    ang = jnp.arange(seq)[:, None].astype(F32) * inv[None, :]
    cos, sin = jnp.cos(ang), jnp.sin(ang)
    pad = LANES - QK_NOPE - QK_ROPE
    ctab = jnp.concatenate([jnp.ones((seq, QK_NOPE), F32), cos, cos, jnp.zeros((seq, pad), F32)], axis=1)
    stab = jnp.concatenate([jnp.zeros((seq, QK_NOPE), F32), sin, sin, jnp.zeros((seq, pad), F32)], axis=1)
    return ctab, stab


def _split_w_in(w_in):
    d = w_in.shape[0]
    bounds = np.cumsum([WIDTH_A, WIDTH_A, WIDTH_A, Q_LORA, KV_LORA, QK_ROPE, WIDTH_C, WIDTH_C])
    qa, ka, va, cq, ckv, kr, qc, kc, vc = jnp.split(w_in, bounds.tolist(), axis=1)
    half = QK_ROPE // 2
    z_lo = jnp.zeros((d, QK_NOPE), w_in.dtype)
    z_hi = jnp.zeros((d, LANES - QK_NOPE - QK_ROPE), w_in.dtype)
    rope_blk = jnp.concatenate([z_lo, kr, z_hi], axis=1)
    swap_blk = jnp.concatenate([z_lo, -kr[:, half:], kr[:, :half], z_hi], axis=1)
    wa = jnp.concatenate([qa, ka, va], axis=1).astype(BF16)
    wc = jnp.concatenate([qc, kc, vc], axis=1).astype(BF16)
    wb = jnp.concatenate([cq, ckv, rope_blk, swap_blk], axis=1).astype(BF16)
    return wa, wc, wb


def _split_w_uq(w_uq):
    r = w_uq.shape[0]
    w = w_uq.reshape(r, N_HEADS_B, QK_NOPE + QK_ROPE)
    nope, rope = w[..., :QK_NOPE], w[..., QK_NOPE:]
    half = QK_ROPE // 2
    z_hi = jnp.zeros((r, N_HEADS_B, LANES - QK_NOPE - QK_ROPE), w_uq.dtype)
    w1 = jnp.concatenate([nope, rope, z_hi], axis=-1)
    w2 = jnp.concatenate([jnp.zeros_like(nope), -rope[..., half:], rope[..., :half], z_hi], axis=-1)
    return w1.reshape(r, -1).astype(BF16), w2.reshape(r, -1).astype(BF16)


def _split_w_ukv(w_ukv):
    r = w_ukv.shape[0]
    w = w_ukv.reshape(r, N_HEADS_B, QK_NOPE + V_HEAD)
    k_nope, v = w[..., :QK_NOPE], w[..., QK_NOPE:]
    wk = jnp.concatenate([k_nope, jnp.zeros((r, N_HEADS_B, LANES - QK_NOPE), w_ukv.dtype)], axis=-1)
    return wk.reshape(r, -1).astype(BF16), v.reshape(r, -1).T.astype(BF16)


def kernel(x, g_mix, w_in, g_q, g_kv, w_uq, w_ukv, rpb, g_out_a, g_out_b, g_out_c, w_o, g_ffn, w1, w3, w2,
           w_router, e_w1, e_w3, e_w2, g_final):
    batch, seq, d = x.shape
    n = batch * seq
    depth = g_mix.shape[0]
    rows = seq // GRID_W
    ctab, stab = _rope_tables(seq)
    xf = x.reshape(n, d)
    for layer in range(depth):
        wa, wc, wb = _split_w_in(w_in[layer])
        wq1, wq2 = _split_w_uq(w_uq[layer])
        wk, wvt = _split_w_ukv(w_ukv[layer])
        pa, pa4, pa16, pc, qm, km, vt = _proj_call(xf, g_mix[layer][None], wa, wc, wb, g_q[layer][None], g_kv[layer][None],
                                        wq1, wq2, wk, wvt, ctab, stab, seq)
        o_parts, lse_parts = [], []
        for dil, view in zip(DILATIONS, (pa, pa4, pa16)):
            o, lse = _band_call(view.reshape(batch, seq // dil, dil * 3 * WIDTH_A), dil)
            o_parts.append(o.reshape(n // dil, dil * WIDTH_A))
            lse_parts.append(lse.reshape(n // dil, dil * WIDTH_A))
        ob = _mla_call(qm, km, vt, batch, seq).reshape(n, WIDTH_B)
        oc = _na_call(pc, _na_tables(rpb[layer], rows), batch, seq).reshape(n, WIDTH_C)
        xf = _mix_out_call(xf, o_parts, lse_parts, ob, oc, g_out_a[layer][None], g_out_b[layer][None],
                           g_out_c[layer][None], w_o[layer].astype(BF16))
        j = layer // 2
        if layer % 2 == 0:
            xf = _ffn_call(xf, g_ffn[layer][None], w1[j].astype(BF16), w3[j].astype(BF16), w2[j].astype(BF16))
        else:
            wr_pad = jnp.pad(w_router[j], ((0, 0), (0, LANES - N_EXPERTS)))
            xf = _moe_layer(xf, g_ffn[layer][None], wr_pad, e_w1[j].astype(BF16), e_w3[j].astype(BF16),
                            e_w2[j].astype(BF16))
    return _norm_call(xf, g_final[None]).reshape(batch, seq, d)
```

```python
import functools
import math

import numpy as np
import jax
import jax.numpy as jnp
from jax import lax
from jax.experimental import pallas as pl
from jax.experimental.pallas import tpu as pltpu

F32 = jnp.float32
BF16 = jnp.bfloat16

LANES = 128
V7X_VMEM_LIMIT_BYTES = 52 * 1024 * 1024

HEAD_DIM = 64
N_HEADS_A = 6
DILATIONS = (1, 4, 16)
BAND_HALF = 64
N_HEADS_B = 6
Q_LORA = 384
KV_LORA = 256
QK_NOPE = 64
QK_ROPE = 32
V_HEAD = 64
ROPE_BASE = 10000.0
N_HEADS_C = 4
GRID_W = 64
NA_ROWS = 8
NA_COLS = 16
WIDTH_A = N_HEADS_A * HEAD_DIM
WIDTH_B = N_HEADS_B * V_HEAD
WIDTH_C = N_HEADS_C * HEAD_DIM
N_EXPERTS = 8
TOP_K = 2
RMS_EPS = 1e-6
NEG_INF = -1e30

ROW_TILE = 512
MLA_TQ = 256
MLA_TK = 256
MLA_UNROLL = 8
MLA_LOOKAHEAD = 4
MLA_HEADS_PER_STEP = 6
MLA_DEN_ROWS = 16
MLA_Q_PRESCALE = (QK_NOPE + QK_ROPE) ** -0.5 * math.log2(math.e)
BAND_TQ = 128
NA_TILE_ROWS = 2
NA_KEY_ROWS = 10
MOE_TM = 512
MOE_TF = 896
FFN_TF = 1408
COMBINE_TM = 256
GATHER_UNROLL = 8


def _cparams(semantics):
    return pltpu.CompilerParams(dimension_semantics=semantics, vmem_limit_bytes=V7X_VMEM_LIMIT_BYTES)


def _rms(x, g):
    return x * lax.rsqrt(jnp.mean(x * x, axis=-1, keepdims=True) + RMS_EPS) * g


def _dot(a, b):
    return jnp.dot(a, b, preferred_element_type=F32)


def _dot_nt(a, b):
    return lax.dot_general(a, b, (((1,), (1,)), ((), ())), preferred_element_type=F32)


def _proj_kernel(x_ref, g_ref, wa_ref, wc_ref, wb_ref, gq_ref, gkv_ref, wq1_ref, wq2_ref, wk_ref, wvt_ref,
                 ct_ref, st_ref, pa_ref, pa4_ref, pa16_ref, pc_ref, qm_ref, km_ref, vt_ref, pa_scr):
    tm = x_ref.shape[0]
    h = _rms(x_ref[...], g_ref[...]).astype(BF16)
    pa = _dot(h, wa_ref[...])
    pa_ref[...] = pa.astype(BF16)
    n_chunks = pa.shape[1] // LANES
    for c in range(n_chunks):
        pa_scr[c] = pa[:, c * LANES:(c + 1) * LANES]
    for dil, view_ref in ((DILATIONS[1], pa4_ref), (DILATIONS[2], pa16_ref)):
        for r in range(dil):
            for c in range(n_chunks):
                col = r * 3 * WIDTH_A + c * LANES
                view_ref[:, col:col + LANES] = pa_scr[c, pl.ds(r, tm // dil, stride=dil), :].astype(BF16)
    pc_ref[...] = _dot(h, wc_ref[...]).astype(BF16)
    pb = _dot(h, wb_ref[...])
    hq = _rms(pb[:, :Q_LORA], gq_ref[...]).astype(BF16)
    hkv = _rms(pb[:, Q_LORA:Q_LORA + KV_LORA], gkv_ref[...]).astype(BF16)
    r1 = pb[:, Q_LORA + KV_LORA:Q_LORA + KV_LORA + LANES]
    r2 = pb[:, Q_LORA + KV_LORA + LANES:]
    ct = ct_ref[...]
    st = st_ref[...]
    qa = _dot(hq, wq1_ref[...])
    qb = _dot(hq, wq2_ref[...])
    kn = _dot(hkv, wk_ref[...])
    kr = r1 * ct + r2 * st
    for hd in range(N_HEADS_B):
        sl = slice(hd * LANES, (hd + 1) * LANES)
        qm_ref[:, sl] = ((qa[:, sl] * ct + qb[:, sl] * st) * MLA_Q_PRESCALE).astype(BF16)
        km_ref[:, sl] = (kn[:, sl] + kr).astype(BF16)
    vt_ref[...] = _dot_nt(wvt_ref[...], hkv).astype(BF16)


def _proj_call(x, g, wa, wc, wb, gq, gkv, wq1, wq2, wk, wvt, ctab, stab, seq):
    n, d = x.shape
    tm = ROW_TILE
    tiles_per_seq = seq // tm
    full = lambda a: pl.BlockSpec(a.shape, lambda i: (0,) * a.ndim)
    row = lambda w: pl.BlockSpec((tm, w), lambda i: (i, 0))
    tab = pl.BlockSpec((tm, LANES), lambda i: (i % tiles_per_seq, 0))
    view = lambda dil: pl.BlockSpec((tm // dil, dil * 3 * WIDTH_A), lambda i: (i, 0))
    hb = N_HEADS_B * LANES
    return pl.pallas_call(
        _proj_kernel,
        grid=(n // tm,),
        in_specs=[row(d), full(g), full(wa), full(wc), full(wb), full(gq), full(gkv), full(wq1), full(wq2),
                  full(wk), full(wvt), tab, tab],
        out_specs=[row(3 * WIDTH_A)] + [view(dil) for dil in DILATIONS[1:]] + [row(3 * WIDTH_C), row(hb), row(hb),
                   pl.BlockSpec((WIDTH_B, tm), lambda i: (0, i))],
        out_shape=[jax.ShapeDtypeStruct((n, 3 * WIDTH_A), BF16)]
                  + [jax.ShapeDtypeStruct((n // dil, dil * 3 * WIDTH_A), BF16) for dil in DILATIONS[1:]]
                  + [jax.ShapeDtypeStruct((n, 3 * WIDTH_C), BF16),
                   jax.ShapeDtypeStruct((n, hb), BF16), jax.ShapeDtypeStruct((n, hb), BF16),
                   jax.ShapeDtypeStruct((WIDTH_B, n), BF16)],
        scratch_shapes=[pltpu.VMEM((3 * WIDTH_A // LANES, tm, LANES), F32)],
        compiler_params=_cparams(("parallel",)),
        name="proj",
    )(x, g, wa, wc, wb, gq, gkv, wq1, wq2, wk, wvt, ctab, stab)


def _band_kernel(q_ref, k_ref, v_ref, o_ref, lse_ref, *, dilation, n):
    tq = BAND_TQ
    tkw = tq + 2 * BAND_HALF
    i0 = pl.program_id(2) * tq
    start = pl.multiple_of(jnp.clip(i0 - BAND_HALF, 0, n - tkw), BAND_HALF)
    q = q_ref[...]
    k = k_ref[pl.ds(start, tkw), :]
    v = v_ref[pl.ds(start, tkw), :]
    rel = (start - i0) + lax.broadcasted_iota(jnp.int32, (tq, tkw), 1) - lax.broadcasted_iota(jnp.int32, (tq, tkw), 0)
    dist_i = jnp.abs(rel)
    valid = dist_i <= BAND_HALF
    dist = dist_i.astype(F32)
    scale = HEAD_DIM ** -0.5
    head = lambda h: slice(h * HEAD_DIM, (h + 1) * HEAD_DIM)
    scores = [_dot_nt(q[:, head(h)], k[:, head(h)]) for h in range(N_HEADS_A)]
    probs, dens, lses = [], [], []
    for h in range(N_HEADS_A):
        slope = 2.0 ** (-8.0 * (h + 1) / N_HEADS_A)
        s = scores[h] * scale - (slope * dilation) * dist
        s = jnp.where(valid, s, NEG_INF)
        m = jnp.max(s, axis=-1, keepdims=True)
        p = jnp.exp(s - m)
        l = jnp.sum(p, axis=-1, keepdims=True)
        probs.append(p.astype(BF16))
        dens.append(l)
        lses.append(jnp.broadcast_to(m + jnp.log(l), (tq, HEAD_DIM)))
    outs = [_dot(probs[h], v[:, head(h)]) / dens[h] for h in range(N_HEADS_A)]
    o_ref[...] = jnp.concatenate(outs, axis=-1)
    lse_ref[...] = jnp.concatenate(lses, axis=-1)


def _band_call(pa_view, dilation):
    b, n, _ = pa_view.shape
    tq = BAND_TQ
    w = WIDTH_A
    qspec = pl.BlockSpec((None, tq, w), lambda bb, r, i: (bb, i, 3 * r))
    kspec = pl.BlockSpec((None, n, w), lambda bb, r, i: (bb, 0, 3 * r + 1))
    vspec = pl.BlockSpec((None, n, w), lambda bb, r, i: (bb, 0, 3 * r + 2))
    ospec = pl.BlockSpec((None, tq, w), lambda bb, r, i: (bb, i, r))
    shape = jax.ShapeDtypeStruct((b, n, dilation * w), F32)
    return pl.pallas_call(
        functools.partial(_band_kernel, dilation=dilation, n=n),
        grid=(b, dilation, n // tq),
        in_specs=[qspec, kspec, vspec],
        out_specs=[ospec, ospec],
        out_shape=[shape, shape],
        compiler_params=_cparams(("parallel", "parallel", "parallel")),
        name=f"band_d{dilation}",
    )(pa_view, pa_view, pa_view)


def _mla_kernel(q_ref, k_ref, vt_ref, o_ref):
    tq = q_ref.shape[0]
    seq = k_ref.shape[0]
    tk = MLA_TK
    nh = MLA_HEADS_PER_STEP
    n_chunks = seq // tk
    items = [(u, h) for u in range(MLA_UNROLL) for h in range(nh)]
    look = MLA_LOOKAHEAD
    ones_rows = jnp.ones((MLA_DEN_ROWS, tk), BF16)

    def key_slice(chunk):
        return pl.ds(pl.multiple_of(chunk * tk, tk), tk)

    def score_matmul(chunk, h):
        k = k_ref[key_slice(chunk), h * LANES:(h + 1) * LANES]
        return _dot_nt(k, q_ref[:, h * LANES:(h + 1) * LANES])

    def value_matmul(chunk, h, p):
        vt = jnp.concatenate([vt_ref[h * V_HEAD:(h + 1) * V_HEAD, key_slice(chunk)], ones_rows], axis=0)
        return _dot(vt, p)

    def body(j, carry):
        state = list(carry[:2 * nh])
        scores = dict(zip(items[:look], carry[2 * nh:2 * nh + look]))
        pend_p, pend_alpha = carry[2 * nh + look:]
        pending = (jnp.maximum(j * MLA_UNROLL - 1, 0), nh - 1, pend_p, pend_alpha)
        ahead = []
        for idx, (u, h) in enumerate(items):
            la = idx + look
            if la < len(items):
                lu, lh = items[la]
                scores[lu, lh] = score_matmul(j * MLA_UNROLL + lu, lh)
            else:
                lu, lh = items[la - len(items)]
                ahead.append(score_matmul(jnp.minimum((j + 1) * MLA_UNROLL + lu, n_chunks - 1), lh))
            pc, ph, pp, pa = pending
            state[2 * ph + 1] = pa * state[2 * ph + 1] + value_matmul(pc, ph, pp)
            s = scores.pop((u, h))
            m_new = jnp.maximum(state[2 * h], jnp.max(s, axis=0, keepdims=True))
            alpha = jnp.exp2(state[2 * h] - m_new)
            state[2 * h] = m_new
            pending = (j * MLA_UNROLL + u, h, jnp.exp2(s - m_new).astype(BF16), alpha)
        return tuple(state) + tuple(ahead) + (pending[2], pending[3])

    init = (jnp.full((1, tq), NEG_INF, F32), jnp.zeros((V_HEAD + MLA_DEN_ROWS, tq), F32)) * nh
    init += tuple(score_matmul(u, h) for u, h in items[:look])
    init += (jnp.zeros((tk, tq), BF16), jnp.ones((1, tq), F32))
    res = lax.fori_loop(0, n_chunks // MLA_UNROLL, body, init)
    accs = [res[2 * h + 1] for h in range(nh)]
    accs[nh - 1] = res[-1] * accs[nh - 1] + value_matmul(n_chunks - 1, nh - 1, res[-2])
    out_t = jnp.concatenate([a[:V_HEAD] / a[V_HEAD:V_HEAD + 1] for a in accs], axis=0)
    o_ref[...] = out_t.T


def _mla_call(qm, km, vt, batch, seq):
    tq = MLA_TQ
    nh = MLA_HEADS_PER_STEP
    qspec = pl.BlockSpec((None, tq, nh * LANES), lambda b, g, i: (b, i, g))
    kspec = pl.BlockSpec((None, seq, nh * LANES), lambda b, g, i: (b, 0, g))
    vspec = pl.BlockSpec((nh * V_HEAD, seq), lambda b, g, i: (g, b))
    ospec = pl.BlockSpec((None, tq, nh * V_HEAD), lambda b, g, i: (b, i, g))
    return pl.pallas_call(
        _mla_kernel,
        grid=(batch, N_HEADS_B // nh, seq // tq),
        in_specs=[qspec, kspec, vspec],
        out_specs=ospec,
        out_shape=jax.ShapeDtypeStruct((batch, seq, WIDTH_B), F32),
        compiler_params=_cparams(("parallel", "parallel", "parallel")),
        name="mla",
    )(qm.reshape(batch, seq, -1), km.reshape(batch, seq, -1), vt)


def _na_variant(i, n_tiles):
    return jnp.minimum(i, 2) + jnp.maximum(i - (n_tiles - 3), 0)


def _na_kernel(q_ref, k_ref, v_ref, tab_ref, o_ref, *, rows):
    tq = NA_TILE_ROWS * GRID_W
    tkw = NA_KEY_ROWS * GRID_W
    i = pl.program_id(1)
    base = jnp.clip(i * NA_TILE_ROWS - NA_ROWS // 2, 0, rows - NA_KEY_ROWS)
    start = pl.multiple_of(base * GRID_W, GRID_W)
    q = q_ref[...]
    k = k_ref[pl.ds(start, tkw), :]
    v = v_ref[pl.ds(start, tkw), :]
    scale = HEAD_DIM ** -0.5
    head = lambda h: slice(h * HEAD_DIM, (h + 1) * HEAD_DIM)
    scores = [_dot_nt(q[:, head(h)], k[:, head(h)]) for h in range(N_HEADS_C)]
    probs, dens = [], []
    for h in range(N_HEADS_C):
        s = scores[h] * scale + tab_ref[h]
        m = jnp.max(s, axis=-1, keepdims=True)
        p = jnp.exp(s - m)
        dens.append(jnp.sum(p, axis=-1, keepdims=True))
        probs.append(p.astype(BF16))
    outs = [_dot(probs[h], v[:, head(h)]) / dens[h] for h in range(N_HEADS_C)]
    o_ref[...] = jnp.concatenate(outs, axis=-1)


def _na_tables(rpb, rows):
    r0 = np.array([0, 2, 4, rows - 4, rows - 2])
    base = np.clip(r0 - NA_ROWS // 2, 0, rows - NA_KEY_ROWS)
    r = r0[:, None] + np.arange(NA_TILE_ROWS)[None, :]
    row_start = np.clip(r - NA_ROWS // 2, 0, rows - NA_ROWS)
    krow = base[:, None] + np.arange(NA_KEY_ROWS)[None, :]
    drow = krow[:, None, :] - r[:, :, None]
    row_ok = (krow[:, None, :] >= row_start[:, :, None]) & (krow[:, None, :] < row_start[:, :, None] + NA_ROWS)
    c = np.arange(GRID_W)
    win_start = np.clip(c - NA_COLS // 2, 0, GRID_W - NA_COLS)
    col_ok = (c[None, :] >= win_start[:, None]) & (c[None, :] < win_start[:, None] + NA_COLS)
    dcol = np.clip(c[None, :] - c[:, None], -(NA_COLS - 1), NA_COLS - 1)
    ok = row_ok[:, :, None, :, None] & col_ok[None, None, :, None, :]
    di = np.clip(drow, -(NA_ROWS - 1), NA_ROWS - 1) + (NA_ROWS - 1)
    pick_col = (dcol[:, :, None] + NA_COLS - 1 == np.arange(2 * NA_COLS - 1)).astype(np.float32)
    pick_row = (di[..., None] == np.arange(2 * NA_ROWS - 1)).astype(np.float32)
    hi = lax.Precision.HIGHEST
    toeplitz = jnp.einsum("hab,cjb->hacj", rpb.astype(F32), pick_col, precision=hi)
    bias = jnp.einsum("vqka,hacj->vhqckj", pick_row, toeplitz, precision=hi)
    tab = jnp.where(jnp.asarray(ok)[:, None], bias, NEG_INF)
    return tab.reshape(5, N_HEADS_C, NA_TILE_ROWS * GRID_W, NA_KEY_ROWS * GRID_W)


def _na_call(pc, tab, batch, seq):
    rows = seq // GRID_W
    tq = NA_TILE_ROWS * GRID_W
    n_tiles = rows // NA_TILE_ROWS
    w = WIDTH_C
    qspec = pl.BlockSpec((None, tq, w), lambda b, i: (b, i, 0))
    kspec = pl.BlockSpec((None, seq, w), lambda b, i: (b, 0, 1))
    vspec = pl.BlockSpec((None, seq, w), lambda b, i: (b, 0, 2))
    tspec = pl.BlockSpec((None,) + tab.shape[1:], lambda b, i: (_na_variant(i, n_tiles), 0, 0, 0))
    ospec = pl.BlockSpec((None, tq, w), lambda b, i: (b, i, 0))
    pc3 = pc.reshape(batch, seq, 3 * w)
    return pl.pallas_call(
        functools.partial(_na_kernel, rows=rows),
        grid=(batch, n_tiles),
        in_specs=[qspec, kspec, vspec, tspec],
        out_specs=ospec,
        out_shape=jax.ShapeDtypeStruct((batch, seq, w), F32),
        compiler_params=_cparams(("parallel", "parallel")),
        name="natten",
    )(pc3, pc3, pc3, tab)


def _mix_out_kernel(x_ref, o1_ref, o2_ref, o3_ref, l1_ref, l2_ref, l3_ref, ob_ref, oc_ref,
                    ga_ref, gb_ref, gc_ref, wo_ref, out_ref, o2_scr, o3_scr, l2_scr, l3_scr):
    tm = x_ref.shape[0]
    n_chunks = WIDTH_A // LANES

    def natural(view_ref, scr, dil):
        for r in range(dil):
            for c in range(n_chunks):
                col = r * WIDTH_A + c * LANES
                scr[c, pl.ds(r, tm // dil, stride=dil), :] = view_ref[:, col:col + LANES]
        return jnp.concatenate([scr[c] for c in range(n_chunks)], axis=-1)

    l1, o1 = l1_ref[...], o1_ref[...]
    l2, o2 = natural(l2_ref, l2_scr, DILATIONS[1]), natural(o2_ref, o2_scr, DILATIONS[1])
    l3, o3 = natural(l3_ref, l3_scr, DILATIONS[2]), natural(o3_ref, o3_scr, DILATIONS[2])
    mx = jnp.maximum(jnp.maximum(l1, l2), l3)
    w1, w2, w3 = jnp.exp(l1 - mx), jnp.exp(l2 - mx), jnp.exp(l3 - mx)
    oa = (w1 * o1 + w2 * o2 + w3 * o3) / (w1 + w2 + w3)
    ya = _rms(oa, ga_ref[...]).astype(BF16)
    yb = _rms(ob_ref[...], gb_ref[...]).astype(BF16)
    yc = _rms(oc_ref[...], gc_ref[...]).astype(BF16)
    y = _dot(ya, wo_ref[:WIDTH_A, :])
    y = y + _dot(yb, wo_ref[WIDTH_A:WIDTH_A + WIDTH_B, :])
    y = y + _dot(yc, wo_ref[WIDTH_A + WIDTH_B:, :])
    out_ref[...] = x_ref[...] + y


def _mix_out_call(x, o_parts, lse_parts, ob, oc, ga, gb, gc, wo):
    n, d = x.shape
    tm = ROW_TILE
    full = lambda a: pl.BlockSpec(a.shape, lambda i: (0,) * a.ndim)
    row = lambda w: pl.BlockSpec((tm, w), lambda i: (i, 0))
    view = lambda dil: pl.BlockSpec((tm // dil, dil * WIDTH_A), lambda i: (i, 0))
    return pl.pallas_call(
        _mix_out_kernel,
        grid=(n // tm,),
        in_specs=[row(d)] + [view(dil) for dil in DILATIONS] * 2
                 + [row(WIDTH_B), row(WIDTH_C), full(ga), full(gb), full(gc), full(wo)],
        out_specs=row(d),
        out_shape=jax.ShapeDtypeStruct((n, d), F32),
        scratch_shapes=[pltpu.VMEM((WIDTH_A // LANES, tm, LANES), F32)] * 4,
        compiler_params=_cparams(("parallel",)),
        name="mix_out",
    )(x, *o_parts, *lse_parts, ob, oc, ga, gb, gc, wo)


def _silu(u):
    return u * (1.0 / (1.0 + jnp.exp(-u)))


def _ffn_kernel(x_ref, g_ref, w1_ref, w3_ref, w2_ref, out_ref, h_ref, acc_ref):
    f = pl.program_id(1)

    @pl.when(f == 0)
    def _():
        h_ref[...] = _rms(x_ref[...], g_ref[...]).astype(BF16)
        acc_ref[...] = jnp.zeros_like(acc_ref)

    h = h_ref[...]
    a = (_silu(_dot(h, w1_ref[...])) * _dot(h, w3_ref[...])).astype(BF16)
    acc_ref[...] += _dot(a, w2_ref[...])

    @pl.when(f == pl.num_programs(1) - 1)
    def _():
        out_ref[...] = x_ref[...] + acc_ref[...]


def _ffn_call(x, g, w1, w3, w2):
    n, d = x.shape
    ff = w1.shape[1]
    tm, tf = ROW_TILE, FFN_TF
    return pl.pallas_call(
        _ffn_kernel,
        grid=(n // tm, ff // tf),
        in_specs=[pl.BlockSpec((tm, d), lambda i, f: (i, 0)), pl.BlockSpec((1, d), lambda i, f: (0, 0)),
                  pl.BlockSpec((d, tf), lambda i, f: (0, f)), pl.BlockSpec((d, tf), lambda i, f: (0, f)),
                  pl.BlockSpec((tf, d), lambda i, f: (f, 0))],
        out_specs=pl.BlockSpec((tm, d), lambda i, f: (i, 0)),
        out_shape=jax.ShapeDtypeStruct((n, d), F32),
        scratch_shapes=[pltpu.VMEM((tm, d), BF16), pltpu.VMEM((tm, d), F32)],
        compiler_params=_cparams(("parallel", "arbitrary")),
        name="ffn",
    )(x, g, w1, w3, w2)


def _router_kernel(x_ref, g_ref, wr_ref, h_ref, e_ref, gate_ref):
    h = _rms(x_ref[...], g_ref[...])
    h_ref[...] = h
    logits = jnp.dot(h, wr_ref[...], precision=lax.Precision.HIGHEST, preferred_element_type=F32)
    lane = lax.broadcasted_iota(jnp.int32, logits.shape, 1).astype(F32)
    logits = jnp.where(lane < N_EXPERTS, logits, -jnp.inf)
    m1 = jnp.max(logits, axis=-1, keepdims=True)
    i1 = jnp.min(jnp.where(logits == m1, lane, float(LANES)), axis=-1, keepdims=True)
    rest = jnp.where(lane == i1, -jnp.inf, logits)
    m2 = jnp.max(rest, axis=-1, keepdims=True)
    i2 = jnp.min(jnp.where(rest == m2, lane, float(LANES)), axis=-1, keepdims=True)
    e = jnp.exp(m2 - m1)
    den = 1.0 + e
    e_ref[...] = jnp.where(lane == 0.0, i1, jnp.where(lane == 1.0, i2, 0.0)).astype(jnp.int32)
    gate_ref[...] = jnp.where(lane == 0.0, 1.0 / den, jnp.where(lane == 1.0, e / den, 0.0))


def _router_call(x, g, wr_pad):
    n, d = x.shape
    tm = ROW_TILE
    row = lambda w: pl.BlockSpec((tm, w), lambda i: (i, 0))
    full = lambda a: pl.BlockSpec(a.shape, lambda i: (0,) * a.ndim)
    return pl.pallas_call(
        _router_kernel,
        grid=(n // tm,),
        in_specs=[row(d), full(g), full(wr_pad)],
        out_specs=[row(d), row(LANES), row(LANES)],
        out_shape=[jax.ShapeDtypeStruct((n, d), F32), jax.ShapeDtypeStruct((n, LANES), jnp.int32),
                   jax.ShapeDtypeStruct((n, LANES), F32)],
        compiler_params=_cparams(("parallel",)),
        name="router",
    )(x, g, wr_pad)


def _row_copy(src_hbm, row, dst_ref, r, sem):
    return pltpu.make_async_copy(src_hbm.at[pl.ds(row, 1)], dst_ref.at[pl.ds(r, 1)], sem)


def _start_row_gather(idx_ref, base, src_hbm, dst_ref, sem, count):
    def body(r, c):
        _row_copy(src_hbm, idx_ref[base + r], dst_ref, r, sem).start()
        return c
    lax.fori_loop(0, count, body, 0, unroll=GATHER_UNROLL)


def _wait_row_gather(src_hbm, dst_ref, sem, count):
    pltpu.make_async_copy(src_hbm.at[pl.ds(0, count)], dst_ref, sem).wait()


def _moe_ffn_kernel(tile_e_ref, tile_ok_ref, row_tok_ref, h_hbm, w1_ref, w3_ref, w2_ref, out_ref,
                    xg_ref, hb_ref, acc_ref, sem):
    i = pl.program_id(0)
    f = pl.program_id(1)
    n_tiles = pl.num_programs(0)
    tm = MOE_TM
    slot = i % 2

    @pl.when(f == 0)
    def _():
        @pl.when(i == 0)
        def _():
            _start_row_gather(row_tok_ref, 0, h_hbm, xg_ref.at[0], sem.at[0], tm)

        _wait_row_gather(h_hbm, xg_ref.at[slot], sem.at[slot], tm)

        @pl.when(i + 1 < n_tiles)
        def _():
            _start_row_gather(row_tok_ref, (i + 1) * tm, h_hbm, xg_ref.at[1 - slot], sem.at[1 - slot], tm)

        hb_ref[...] = xg_ref[slot].astype(BF16)
        acc_ref[...] = jnp.zeros_like(acc_ref)

    @pl.when(tile_ok_ref[i] != 0)
    def _():
        h = hb_ref[...]
        a = (_silu(_dot(h, w1_ref[...])) * _dot(h, w3_ref[...])).astype(BF16)
        acc_ref[...] += _dot(a, w2_ref[...])

    @pl.when(f == pl.num_programs(1) - 1)
    def _():
        out_ref[...] = acc_ref[...]


def _moe_ffn_call(tile_e, tile_ok, row_tok, h, w1, w3, w2):
    n, d = h.shape
    n_tiles = tile_e.shape[0]
    ff = w1.shape[2]
    tm, tf = MOE_TM, MOE_TF
    grid_spec = pltpu.PrefetchScalarGridSpec(
        num_scalar_prefetch=3,
        grid=(n_tiles, ff // tf),
        in_specs=[pl.BlockSpec(memory_space=pl.ANY),
                  pl.BlockSpec((None, d, tf), lambda i, f, te, tv, rt: (te[i], 0, f)),
                  pl.BlockSpec((None, d, tf), lambda i, f, te, tv, rt: (te[i], 0, f)),
                  pl.BlockSpec((None, tf, d), lambda i, f, te, tv, rt: (te[i], f, 0))],
        out_specs=pl.BlockSpec((tm, d), lambda i, f, te, tv, rt: (i, 0)),
        scratch_shapes=[pltpu.VMEM((2, tm, d), F32), pltpu.VMEM((tm, d), BF16), pltpu.VMEM((tm, d), F32),
                        pltpu.SemaphoreType.DMA((2,))],
    )
    return pl.pallas_call(
        _moe_ffn_kernel,
        grid_spec=grid_spec,
        out_shape=jax.ShapeDtypeStruct((n_tiles * tm, d), F32),
        compiler_params=_cparams(("arbitrary", "arbitrary")),
        name="moe_ffn",
    )(tile_e, tile_ok, row_tok, h, w1, w3, w2)


def _combine_kernel(d0_ref, d1_ref, y_hbm, x_ref, gate_ref, out_ref, b0_ref, b1_ref, sem):
    i = pl.program_id(0)
    n_tiles = pl.num_programs(0)
    tm = COMBINE_TM
    slot = i % 2

    def start(tile, s):
        _start_row_gather(d0_ref, tile * tm, y_hbm, b0_ref.at[s], sem.at[0, s], tm)
        _start_row_gather(d1_ref, tile * tm, y_hbm, b1_ref.at[s], sem.at[1, s], tm)

    @pl.when(i == 0)
    def _():
        start(0, 0)

    _wait_row_gather(y_hbm, b0_ref.at[slot], sem.at[0, slot], tm)
    _wait_row_gather(y_hbm, b1_ref.at[slot], sem.at[1, slot], tm)

    @pl.when(i + 1 < n_tiles)
    def _():
        start(i + 1, 1 - slot)

    gates = gate_ref[...]
    out_ref[...] = x_ref[...] + (gates[:, 0:1] * b0_ref[slot] + gates[:, 1:2] * b1_ref[slot])


def _combine_call(d0, d1, y, x, gates):
    n, d = x.shape
    tm = COMBINE_TM
    grid_spec = pltpu.PrefetchScalarGridSpec(
        num_scalar_prefetch=2,
        grid=(n // tm,),
        in_specs=[pl.BlockSpec(memory_space=pl.ANY),
                  pl.BlockSpec((tm, d), lambda i, a, b: (i, 0)),
                  pl.BlockSpec((tm, LANES), lambda i, a, b: (i, 0))],
        out_specs=pl.BlockSpec((tm, d), lambda i, a, b: (i, 0)),
        scratch_shapes=[pltpu.VMEM((2, tm, d), F32), pltpu.VMEM((2, tm, d), F32), pltpu.SemaphoreType.DMA((2, 2))],
    )
    return pl.pallas_call(
        _combine_kernel,
        grid_spec=grid_spec,
        out_shape=jax.ShapeDtypeStruct((n, d), F32),
        compiler_params=_cparams(("arbitrary",)),
        name="moe_combine",
    )(d0, d1, y, x, gates)


def _moe_plan(top_e, n_tiles):
    n_assign = top_e.shape[0] * TOP_K
    flat_e = top_e.reshape(n_assign)
    onehot = (flat_e[:, None] == jnp.arange(N_EXPERTS, dtype=jnp.int32)[None, :]).astype(jnp.int32)
    csum = jnp.cumsum(onehot, axis=0)
    rank = jnp.take_along_axis(csum, flat_e[:, None], axis=1)[:, 0] - 1
    counts = csum[-1]
    padded = (counts + MOE_TM - 1) // MOE_TM * MOE_TM
    pend = jnp.cumsum(padded)
    dest = (pend - padded)[flat_e] + rank
    row_tok = jnp.zeros((n_tiles * MOE_TM,), jnp.int32).at[dest].set(jnp.arange(n_assign, dtype=jnp.int32) // TOP_K)
    tile_start = jnp.arange(n_tiles, dtype=jnp.int32) * MOE_TM
    tile_e = jnp.minimum(jnp.searchsorted(pend, tile_start, side="right"), N_EXPERTS - 1).astype(jnp.int32)
    tile_ok = (tile_start < pend[-1]).astype(jnp.int32)
    dest = dest.reshape(-1, TOP_K).astype(jnp.int32)
    return tile_e, tile_ok, row_tok, dest[:, 0], dest[:, 1]


def _moe_layer(x, g, wr_pad, w1, w3, w2):
    n = x.shape[0]
    h, top_e, gates = _router_call(x, g, wr_pad)
    n_tiles = -(-(n * TOP_K + N_EXPERTS * (MOE_TM - 1)) // MOE_TM)
    tile_e, tile_ok, row_tok, d0, d1 = _moe_plan(top_e[:, :TOP_K], n_tiles)
    y = _moe_ffn_call(tile_e, tile_ok, row_tok, h, w1, w3, w2)
    return _combine_call(d0, d1, y, x, gates)


def _norm_kernel(x_ref, g_ref, o_ref):
    o_ref[...] = _rms(x_ref[...], g_ref[...])


def _norm_call(x, g):
    n, d = x.shape
    tm = ROW_TILE
    return pl.pallas_call(
        _norm_kernel,
        grid=(n // tm,),
        in_specs=[pl.BlockSpec((tm, d), lambda i: (i, 0)), pl.BlockSpec((1, d), lambda i: (0, 0))],
        out_specs=pl.BlockSpec((tm, d), lambda i: (i, 0)),
        out_shape=jax.ShapeDtypeStruct((n, d), F32),
        compiler_params=_cparams(("parallel",)),
        name="final_norm",
    )(x, g)


def _rope_tables(seq):
    inv = ROPE_BASE ** (-jnp.arange(0, QK_ROPE, 2, dtype=F32) / QK_ROPE)
    ang = jnp.arange(seq)[:, None].astype(F32) * inv[None, :]
    cos, sin = jnp.cos(ang), jnp.sin(ang)
    pad = LANES - QK_NOPE - QK_ROPE
    ctab = jnp.concatenate([jnp.ones((seq, QK_NOPE), F32), cos, cos, jnp.zeros((seq, pad), F32)], axis=1)
    stab = jnp.concatenate([jnp.zeros((seq, QK_NOPE), F32), sin, sin, jnp.zeros((seq, pad), F32)], axis=1)
    return ctab, stab


def _split_w_in(w_in):
    d = w_in.shape[0]
    bounds = np.cumsum([WIDTH_A, WIDTH_A, WIDTH_A, Q_LORA, KV_LORA, QK_ROPE, WIDTH_C, WIDTH_C])
    qa, ka, va, cq, ckv, kr, qc, kc, vc = jnp.split(w_in, bounds.tolist(), axis=1)
    half = QK_ROPE // 2
    z_lo = jnp.zeros((d, QK_NOPE), w_in.dtype)
    z_hi = jnp.zeros((d, LANES - QK_NOPE - QK_ROPE), w_in.dtype)
    rope_blk = jnp.concatenate([z_lo, kr, z_hi], axis=1)
    swap_blk = jnp.concatenate([z_lo, -kr[:, half:], kr[:, :half], z_hi], axis=1)
    wa = jnp.concatenate([qa, ka, va], axis=1).astype(BF16)
    wc = jnp.concatenate([qc, kc, vc], axis=1).astype(BF16)
    wb = jnp.concatenate([cq, ckv, rope_blk, swap_blk], axis=1).astype(BF16)
    return wa, wc, wb


def _split_w_uq(w_uq):
    r = w_uq.shape[0]
    w = w_uq.reshape(r, N_HEADS_B, QK_NOPE + QK_ROPE)
    nope, rope = w[..., :QK_NOPE], w[..., QK_NOPE:]
    half = QK_ROPE // 2
    z_hi = jnp.zeros((r, N_HEADS_B, LANES - QK_NOPE - QK_ROPE), w_uq.dtype)
    w1 = jnp.concatenate([nope, rope, z_hi], axis=-1)
    w2 = jnp.concatenate([jnp.zeros_like(nope), -rope[..., half:], rope[..., :half], z_hi], axis=-1)
    return w1.reshape(r, -1).astype(BF16), w2.reshape(r, -1).astype(BF16)


def _split_w_ukv(w_ukv):
    r = w_ukv.shape[0]
    w = w_ukv.reshape(r, N_HEADS_B, QK_NOPE + V_HEAD)
    k_nope, v = w[..., :QK_NOPE], w[..., QK_NOPE:]
    wk = jnp.concatenate([k_nope, jnp.zeros((r, N_HEADS_B, LANES - QK_NOPE), w_ukv.dtype)], axis=-1)
    return wk.reshape(r, -1).astype(BF16), v.reshape(r, -1).T.astype(BF16)


def kernel(x, g_mix, w_in, g_q, g_kv, w_uq, w_ukv, rpb, g_out_a, g_out_b, g_out_c, w_o, g_ffn, w1, w3, w2,
           w_router, e_w1, e_w3, e_w2, g_final):
    batch, seq, d = x.shape
    n = batch * seq
    depth = g_mix.shape[0]
    rows = seq // GRID_W
    ctab, stab = _rope_tables(seq)
    xf = x.reshape(n, d)
    for layer in range(depth):
        wa, wc, wb = _split_w_in(w_in[layer])
        wq1, wq2 = _split_w_uq(w_uq[layer])
        wk, wvt = _split_w_ukv(w_ukv[layer])
        pa, pa4, pa16, pc, qm, km, vt = _proj_call(xf, g_mix[layer][None], wa, wc, wb, g_q[layer][None], g_kv[layer][None],
                                        wq1, wq2, wk, wvt, ctab, stab, seq)
        o_parts, lse_parts = [], []
        for dil, view in zip(DILATIONS, (pa, pa4, pa16)):
            o, lse = _band_call(view.reshape(batch, seq // dil, dil * 3 * WIDTH_A), dil)
            o_parts.append(o.reshape(n // dil, dil * WIDTH_A))
            lse_parts.append(lse.reshape(n // dil, dil * WIDTH_A))
        ob = _mla_call(qm, km, vt, batch, seq).reshape(n, WIDTH_B)
        oc = _na_call(pc, _na_tables(rpb[layer], rows), batch, seq).reshape(n, WIDTH_C)
        xf = _mix_out_call(xf, o_parts, lse_parts, ob, oc, g_out_a[layer][None], g_out_b[layer][None],
                           g_out_c[layer][None], w_o[layer].astype(BF16))
        j = layer // 2
        if layer % 2 == 0:
            xf = _ffn_call(xf, g_ffn[layer][None], w1[j].astype(BF16), w3[j].astype(BF16), w2[j].astype(BF16))
        else:
            wr_pad = jnp.pad(w_router[j], ((0, 0), (0, LANES - N_EXPERTS)))
            xf = _moe_layer(xf, g_ffn[layer][None], wr_pad, e_w1[j].astype(BF16), e_w3[j].astype(BF16),
                            e_w2[j].astype(BF16))
    return _norm_call(xf, g_final[None]).reshape(batch, seq, d)
```

```python
import functools
import math

import numpy as np
import jax
import jax.numpy as jnp
from jax import lax
from jax.experimental import pallas as pl
from jax.experimental.pallas import tpu as pltpu

F32 = jnp.float32
BF16 = jnp.bfloat16

LANES = 128
V7X_VMEM_LIMIT_BYTES = 52 * 1024 * 1024

HEAD_DIM = 64
N_HEADS_A = 6
DILATIONS = (1, 4, 16)
BAND_HALF = 64
N_HEADS_B = 6
Q_LORA = 384
KV_LORA = 256
QK_NOPE = 64
QK_ROPE = 32
V_HEAD = 64
ROPE_BASE = 10000.0
N_HEADS_C = 4
GRID_W = 64
NA_ROWS = 8
NA_COLS = 16
WIDTH_A = N_HEADS_A * HEAD_DIM
WIDTH_B = N_HEADS_B * V_HEAD
WIDTH_C = N_HEADS_C * HEAD_DIM
N_EXPERTS = 8
TOP_K = 2
RMS_EPS = 1e-6
NEG_INF = -1e30

ROW_TILE = 512
MLA_TQ = 256
MLA_TK = 256
MLA_UNROLL = 8
MLA_LOOKAHEAD = 4
MLA_HEADS_PER_STEP = 6
MLA_DEN_ROWS = 16
MLA_Q_PRESCALE = (QK_NOPE + QK_ROPE) ** -0.5 * math.log2(math.e)
BAND_TQ = 128
BAND_TILES_PER_STEP = 4
BAND_LOOKAHEAD = 6
NA_TILE_ROWS = 2
NA_KEY_ROWS = 10
NA_TILES_PER_STEP = 4
NA_LOOKAHEAD = 16
MOE_TM = 512
MOE_TF = 896
FFN_TF = 1408
COMBINE_TM = 256
GATHER_UNROLL = 8


def _cparams(semantics):
    return pltpu.CompilerParams(dimension_semantics=semantics, vmem_limit_bytes=V7X_VMEM_LIMIT_BYTES)


def _rms(x, g):
    return x * lax.rsqrt(jnp.mean(x * x, axis=-1, keepdims=True) + RMS_EPS) * g


def _dot(a, b):
    return jnp.dot(a, b, preferred_element_type=F32)


def _dot_nt(a, b):
    return lax.dot_general(a, b, (((1,), (1,)), ((), ())), preferred_element_type=F32)


def _proj_kernel(x_ref, g_ref, wa_ref, wc_ref, wb_ref, gq_ref, gkv_ref, wq1_ref, wq2_ref, wk_ref, wvt_ref,
                 ct_ref, st_ref, pa_ref, pa4_ref, pa16_ref, pc_ref, qm_ref, km_ref, vt_ref, pa_scr):
    tm = x_ref.shape[0]
    h = _rms(x_ref[...], g_ref[...]).astype(BF16)
    pa = _dot(h, wa_ref[...])
    pa_ref[...] = pa.astype(BF16)
    n_chunks = pa.shape[1] // LANES
    for c in range(n_chunks):
        pa_scr[c] = pa[:, c * LANES:(c + 1) * LANES]
    for dil, view_ref in ((DILATIONS[1], pa4_ref), (DILATIONS[2], pa16_ref)):
        for r in range(dil):
            for c in range(n_chunks):
                col = r * 3 * WIDTH_A + c * LANES
                view_ref[:, col:col + LANES] = pa_scr[c, pl.ds(r, tm // dil, stride=dil), :].astype(BF16)
    pc_ref[...] = _dot(h, wc_ref[...]).astype(BF16)
    pb = _dot(h, wb_ref[...])
    hq = _rms(pb[:, :Q_LORA], gq_ref[...]).astype(BF16)
    hkv = _rms(pb[:, Q_LORA:Q_LORA + KV_LORA], gkv_ref[...]).astype(BF16)
    r1 = pb[:, Q_LORA + KV_LORA:Q_LORA + KV_LORA + LANES]
    r2 = pb[:, Q_LORA + KV_LORA + LANES:]
    ct = ct_ref[...]
    st = st_ref[...]
    qa = _dot(hq, wq1_ref[...])
    qb = _dot(hq, wq2_ref[...])
    kn = _dot(hkv, wk_ref[...])
    kr = r1 * ct + r2 * st
    for hd in range(N_HEADS_B):
        sl = slice(hd * LANES, (hd + 1) * LANES)
        qm_ref[:, sl] = ((qa[:, sl] * ct + qb[:, sl] * st) * MLA_Q_PRESCALE).astype(BF16)
        km_ref[:, sl] = (kn[:, sl] + kr).astype(BF16)
    vt_ref[...] = _dot_nt(wvt_ref[...], hkv).astype(BF16)


def _proj_call(x, g, wa, wc, wb, gq, gkv, wq1, wq2, wk, wvt, ctab, stab, seq):
    n, d = x.shape
    tm = ROW_TILE
    tiles_per_seq = seq // tm
    full = lambda a: pl.BlockSpec(a.shape, lambda i: (0,) * a.ndim)
    row = lambda w: pl.BlockSpec((tm, w), lambda i: (i, 0))
    tab = pl.BlockSpec((tm, LANES), lambda i: (i % tiles_per_seq, 0))
    view = lambda dil: pl.BlockSpec((tm // dil, dil * 3 * WIDTH_A), lambda i: (i, 0))
    hb = N_HEADS_B * LANES
    return pl.pallas_call(
        _proj_kernel,
        grid=(n // tm,),
        in_specs=[row(d), full(g), full(wa), full(wc), full(wb), full(gq), full(gkv), full(wq1), full(wq2),
                  full(wk), full(wvt), tab, tab],
        out_specs=[row(3 * WIDTH_A)] + [view(dil) for dil in DILATIONS[1:]] + [row(3 * WIDTH_C), row(hb), row(hb),
                   pl.BlockSpec((WIDTH_B, tm), lambda i: (0, i))],
        out_shape=[jax.ShapeDtypeStruct((n, 3 * WIDTH_A), BF16)]
                  + [jax.ShapeDtypeStruct((n // dil, dil * 3 * WIDTH_A), BF16) for dil in DILATIONS[1:]]
                  + [jax.ShapeDtypeStruct((n, 3 * WIDTH_C), BF16),
                   jax.ShapeDtypeStruct((n, hb), BF16), jax.ShapeDtypeStruct((n, hb), BF16),
                   jax.ShapeDtypeStruct((WIDTH_B, n), BF16)],
        scratch_shapes=[pltpu.VMEM((3 * WIDTH_A // LANES, tm, LANES), F32)],
        compiler_params=_cparams(("parallel",)),
        name="proj",
    )(x, g, wa, wc, wb, gq, gkv, wq1, wq2, wk, wvt, ctab, stab)


def _run_pipeline(n_items, look, issue, consume, finish):
    if look >= n_items:
        results = [consume(i, s) for i, s in enumerate([issue(i) for i in range(n_items)])]
        for i, res in enumerate(results):
            finish(i, res)
        return
    inflight = {i: issue(i) for i in range(min(look, n_items))}
    pending = None
    for i in range(n_items):
        if i + look < n_items:
            inflight[i + look] = issue(i + look)
        if pending is not None:
            finish(*pending)
        pending = (i, consume(i, inflight.pop(i)))
    finish(*pending)


class _HeadPairs:
    def __init__(self, q_ref, k_ref, v_ref, tq, tkw):
        self.q_ref, self.k_ref, self.v_ref, self.tq, self.tkw = q_ref, k_ref, v_ref, tq, tkw
        lane = lax.broadcasted_iota(jnp.int32, (tq, LANES), 1)
        self.low = lane < HEAD_DIM
        self.ones = jnp.ones((tkw, LANES), BF16)

    @staticmethod
    def cols(h):
        return slice((h // 2) * LANES, (h // 2 + 1) * LANES)

    def scores(self, q_start, k_start, h):
        q = self.q_ref[q_start:q_start + self.tq, self.cols(h)]
        q = jnp.where(self.low if h % 2 == 0 else ~self.low, q, jnp.zeros_like(q))
        return _dot_nt(q, self.k_ref[pl.ds(k_start, self.tkw), self.cols(h)])

    def values(self, k_start, h, p):
        v = jnp.concatenate([self.v_ref[pl.ds(k_start, self.tkw), self.cols(h)], self.ones], axis=-1)
        o = _dot(p, v)
        return o[:, :LANES], o[:, LANES:]

    def merge(self, even, odd):
        return jnp.where(self.low, even, odd)


def _band_tables(dilation):
    tq, tkw = BAND_TQ, BAND_TQ + 2 * BAND_HALF
    shift = np.array([0, -BAND_HALF, -2 * BAND_HALF])
    rel = shift[:, None, None] + np.arange(tkw)[None, None, :] - np.arange(tq)[None, :, None]
    dist = np.abs(rel)
    slopes = 2.0 ** (-8.0 * np.arange(1, N_HEADS_A + 1) / N_HEADS_A)
    bias = -(slopes[None, :, None, None] * dilation) * dist[:, None].astype(np.float64)
    tab = np.where(dist[:, None] <= BAND_HALF, bias, NEG_INF)
    return jnp.asarray(tab, F32)


def _band_kernel(q_ref, k_ref, v_ref, tab_ref, o_ref, lse_ref, *, n):
    tq = BAND_TQ
    tkw = tq + 2 * BAND_HALF
    tiles = q_ref.shape[0] // tq
    last_tile = n // tq - 1
    items = [(t, h) for t in range(tiles) for h in range(N_HEADS_A)]
    info = []
    for t in range(tiles):
        i = pl.program_id(2) * tiles + t
        start = pl.multiple_of(jnp.clip(i * tq - BAND_HALF, 0, n - tkw), BAND_HALF)
        variant = jnp.minimum(i, 1) + (i == last_tile).astype(jnp.int32)
        info.append((start, variant))
    pairs = _HeadPairs(q_ref, k_ref, v_ref, tq, tkw)
    held = {}

    def issue(idx):
        t, h = items[idx]
        return pairs.scores(t * tq, info[t][0], h)

    def consume(idx, s):
        t, h = items[idx]
        s = s + tab_ref[info[t][1], h]
        m = jnp.max(s, axis=-1, keepdims=True)
        return jnp.exp(s - m).astype(BF16), m

    def finish(idx, res):
        t, h = items[idx]
        p, m = res
        out, den = pairs.values(info[t][0], h, p)
        held[h % 2] = (out / den, m + jnp.log(den))
        if h % 2 == 1:
            cols = pairs.cols(h)
            o_ref[t * tq:(t + 1) * tq, cols] = pairs.merge(held[0][0], held[1][0])
            lse_ref[t * tq:(t + 1) * tq, cols] = pairs.merge(held[0][1], held[1][1])

    _run_pipeline(len(items), BAND_LOOKAHEAD, issue, consume, finish)


def _band_call(pa_view, dilation):
    b, n, _ = pa_view.shape
    rows = BAND_TQ * BAND_TILES_PER_STEP
    w = WIDTH_A
    tab = _band_tables(dilation)
    qspec = pl.BlockSpec((None, rows, w), lambda bb, r, i: (bb, i, 3 * r))
    kspec = pl.BlockSpec((None, n, w), lambda bb, r, i: (bb, 0, 3 * r + 1))
    vspec = pl.BlockSpec((None, n, w), lambda bb, r, i: (bb, 0, 3 * r + 2))
    tspec = pl.BlockSpec(tab.shape, lambda bb, r, i: (0, 0, 0, 0))
    ospec = pl.BlockSpec((None, rows, w), lambda bb, r, i: (bb, i, r))
    shape = jax.ShapeDtypeStruct((b, n, dilation * w), F32)
    return pl.pallas_call(
        functools.partial(_band_kernel, n=n),
        grid=(b, dilation, n // rows),
        in_specs=[qspec, kspec, vspec, tspec],
        out_specs=[ospec, ospec],
        out_shape=[shape, shape],
        compiler_params=_cparams(("parallel", "parallel", "parallel")),
        name=f"band_d{dilation}",
    )(pa_view, pa_view, pa_view, tab)


def _mla_kernel(q_ref, k_ref, vt_ref, o_ref):
    tq = q_ref.shape[0]
    seq = k_ref.shape[0]
    tk = MLA_TK
    nh = MLA_HEADS_PER_STEP
    n_chunks = seq // tk
    items = [(u, h) for u in range(MLA_UNROLL) for h in range(nh)]
    look = MLA_LOOKAHEAD
    ones_rows = jnp.ones((MLA_DEN_ROWS, tk), BF16)

    def key_slice(chunk):
        return pl.ds(pl.multiple_of(chunk * tk, tk), tk)

    def score_matmul(chunk, h):
        k = k_ref[key_slice(chunk), h * LANES:(h + 1) * LANES]
        return _dot_nt(k, q_ref[:, h * LANES:(h + 1) * LANES])

    def value_matmul(chunk, h, p):
        vt = jnp.concatenate([vt_ref[h * V_HEAD:(h + 1) * V_HEAD, key_slice(chunk)], ones_rows], axis=0)
        return _dot(vt, p)

    def body(j, carry):
        state = list(carry[:2 * nh])
        scores = dict(zip(items[:look], carry[2 * nh:2 * nh + look]))
        pend_p, pend_alpha = carry[2 * nh + look:]
        pending = (jnp.maximum(j * MLA_UNROLL - 1, 0), nh - 1, pend_p, pend_alpha)
        ahead = []
        for idx, (u, h) in enumerate(items):
            la = idx + look
            if la < len(items):
                lu, lh = items[la]
                scores[lu, lh] = score_matmul(j * MLA_UNROLL + lu, lh)
            else:
                lu, lh = items[la - len(items)]
                ahead.append(score_matmul(jnp.minimum((j + 1) * MLA_UNROLL + lu, n_chunks - 1), lh))
            pc, ph, pp, pa = pending
            state[2 * ph + 1] = pa * state[2 * ph + 1] + value_matmul(pc, ph, pp)
            s = scores.pop((u, h))
            m_new = jnp.maximum(state[2 * h], jnp.max(s, axis=0, keepdims=True))
            alpha = jnp.exp2(state[2 * h] - m_new)
            state[2 * h] = m_new
            pending = (j * MLA_UNROLL + u, h, jnp.exp2(s - m_new).astype(BF16), alpha)
        return tuple(state) + tuple(ahead) + (pending[2], pending[3])

    init = (jnp.full((1, tq), NEG_INF, F32), jnp.zeros((V_HEAD + MLA_DEN_ROWS, tq), F32)) * nh
    init += tuple(score_matmul(u, h) for u, h in items[:look])
    init += (jnp.zeros((tk, tq), BF16), jnp.ones((1, tq), F32))
    res = lax.fori_loop(0, n_chunks // MLA_UNROLL, body, init)
    accs = [res[2 * h + 1] for h in range(nh)]
    accs[nh - 1] = res[-1] * accs[nh - 1] + value_matmul(n_chunks - 1, nh - 1, res[-2])
    out_t = jnp.concatenate([a[:V_HEAD] / a[V_HEAD:V_HEAD + 1] for a in accs], axis=0)
    o_ref[...] = out_t.T


def _mla_call(qm, km, vt, batch, seq):
    tq = MLA_TQ
    nh = MLA_HEADS_PER_STEP
    qspec = pl.BlockSpec((None, tq, nh * LANES), lambda b, g, i: (b, i, g))
    kspec = pl.BlockSpec((None, seq, nh * LANES), lambda b, g, i: (b, 0, g))
    vspec = pl.BlockSpec((nh * V_HEAD, seq), lambda b, g, i: (g, b))
    ospec = pl.BlockSpec((None, tq, nh * V_HEAD), lambda b, g, i: (b, i, g))
    return pl.pallas_call(
        _mla_kernel,
        grid=(batch, N_HEADS_B // nh, seq // tq),
        in_specs=[qspec, kspec, vspec],
        out_specs=ospec,
        out_shape=jax.ShapeDtypeStruct((batch, seq, WIDTH_B), F32),
        compiler_params=_cparams(("parallel", "parallel", "parallel")),
        name="mla",
    )(qm.reshape(batch, seq, -1), km.reshape(batch, seq, -1), vt)


def _na_variant(i, n_tiles):
    return jnp.minimum(i, 2) + jnp.maximum(i - (n_tiles - 3), 0)


def _na_kernel(q_ref, k_ref, v_ref, tab_ref, o_ref, *, rows):
    tq = NA_TILE_ROWS * GRID_W
    tkw = NA_KEY_ROWS * GRID_W
    tiles = q_ref.shape[0] // tq
    n_tiles = rows // NA_TILE_ROWS
    items = [(t, h) for t in range(tiles) for h in range(N_HEADS_C)]
    info = []
    for t in range(tiles):
        i = pl.program_id(1) * tiles + t
        base = jnp.clip(i * NA_TILE_ROWS - NA_ROWS // 2, 0, rows - NA_KEY_ROWS)
        info.append((pl.multiple_of(base * GRID_W, GRID_W), _na_variant(i, n_tiles)))
    pairs = _HeadPairs(q_ref, k_ref, v_ref, tq, tkw)
    held = {}

    def issue(idx):
        t, h = items[idx]
        return pairs.scores(t * tq, info[t][0], h)

    def consume(idx, s):
        t, h = items[idx]
        s = s + tab_ref[info[t][1], h]
        return jnp.exp(s - jnp.max(s, axis=-1, keepdims=True)).astype(BF16)

    def finish(idx, p):
        t, h = items[idx]
        out, den = pairs.values(info[t][0], h, p)
        held[h % 2] = out / den
        if h % 2 == 1:
            o_ref[t * tq:(t + 1) * tq, pairs.cols(h)] = pairs.merge(held[0], held[1])

    _run_pipeline(len(items), NA_LOOKAHEAD, issue, consume, finish)


def _na_tables(rpb, rows):
    r0 = np.array([0, 2, 4, rows - 4, rows - 2])
    base = np.clip(r0 - NA_ROWS // 2, 0, rows - NA_KEY_ROWS)
    r = r0[:, None] + np.arange(NA_TILE_ROWS)[None, :]
    row_start = np.clip(r - NA_ROWS // 2, 0, rows - NA_ROWS)
    krow = base[:, None] + np.arange(NA_KEY_ROWS)[None, :]
    drow = krow[:, None, :] - r[:, :, None]
    row_ok = (krow[:, None, :] >= row_start[:, :, None]) & (krow[:, None, :] < row_start[:, :, None] + NA_ROWS)
    c = np.arange(GRID_W)
    win_start = np.clip(c - NA_COLS // 2, 0, GRID_W - NA_COLS)
    col_ok = (c[None, :] >= win_start[:, None]) & (c[None, :] < win_start[:, None] + NA_COLS)
    dcol = np.clip(c[None, :] - c[:, None], -(NA_COLS - 1), NA_COLS - 1)
    ok = row_ok[:, :, None, :, None] & col_ok[None, None, :, None, :]
    di = np.clip(drow, -(NA_ROWS - 1), NA_ROWS - 1) + (NA_ROWS - 1)
    pick_col = (dcol[:, :, None] + NA_COLS - 1 == np.arange(2 * NA_COLS - 1)).astype(np.float32)
    pick_row = (di[..., None] == np.arange(2 * NA_ROWS - 1)).astype(np.float32)
    hi = lax.Precision.HIGHEST
    toeplitz = jnp.einsum("hab,cjb->hacj", rpb.astype(F32), pick_col, precision=hi)
    bias = jnp.einsum("vqka,hacj->vhqckj", pick_row, toeplitz, precision=hi)
    tab = jnp.where(jnp.asarray(ok)[:, None], bias, NEG_INF)
    return tab.reshape(5, N_HEADS_C, NA_TILE_ROWS * GRID_W, NA_KEY_ROWS * GRID_W)


def _na_call(pc, tab, batch, seq):
    rows = seq // GRID_W
    tq = NA_TILE_ROWS * GRID_W * NA_TILES_PER_STEP
    w = WIDTH_C
    qspec = pl.BlockSpec((None, tq, w), lambda b, i: (b, i, 0))
    kspec = pl.BlockSpec((None, seq, w), lambda b, i: (b, 0, 1))
    vspec = pl.BlockSpec((None, seq, w), lambda b, i: (b, 0, 2))
    tspec = pl.BlockSpec(tab.shape, lambda b, i: (0, 0, 0, 0))
    ospec = pl.BlockSpec((None, tq, w), lambda b, i: (b, i, 0))
    pc3 = pc.reshape(batch, seq, 3 * w)
    return pl.pallas_call(
        functools.partial(_na_kernel, rows=rows),
        grid=(batch, seq // tq),
        in_specs=[qspec, kspec, vspec, tspec],
        out_specs=ospec,
        out_shape=jax.ShapeDtypeStruct((batch, seq, w), F32),
        compiler_params=_cparams(("parallel", "parallel")),
        name="natten",
    )(pc3, pc3, pc3, tab)


def _mix_out_kernel(x_ref, o1_ref, o2_ref, o3_ref, l1_ref, l2_ref, l3_ref, ob_ref, oc_ref,
                    ga_ref, gb_ref, gc_ref, wo_ref, out_ref, o2_scr, o3_scr, l2_scr, l3_scr):
    tm = x_ref.shape[0]
    n_chunks = WIDTH_A // LANES

    def natural(view_ref, scr, dil):
        for r in range(dil):
            for c in range(n_chunks):
                col = r * WIDTH_A + c * LANES
                scr[c, pl.ds(r, tm // dil, stride=dil), :] = view_ref[:, col:col + LANES]
        return jnp.concatenate([scr[c] for c in range(n_chunks)], axis=-1)

    l1, o1 = l1_ref[...], o1_ref[...]
    l2, o2 = natural(l2_ref, l2_scr, DILATIONS[1]), natural(o2_ref, o2_scr, DILATIONS[1])
    l3, o3 = natural(l3_ref, l3_scr, DILATIONS[2]), natural(o3_ref, o3_scr, DILATIONS[2])
    mx = jnp.maximum(jnp.maximum(l1, l2), l3)
    w1, w2, w3 = jnp.exp(l1 - mx), jnp.exp(l2 - mx), jnp.exp(l3 - mx)
    oa = (w1 * o1 + w2 * o2 + w3 * o3) / (w1 + w2 + w3)
    ya = _rms(oa, ga_ref[...]).astype(BF16)
    yb = _rms(ob_ref[...], gb_ref[...]).astype(BF16)
    yc = _rms(oc_ref[...], gc_ref[...]).astype(BF16)
    y = _dot(ya, wo_ref[:WIDTH_A, :])
    y = y + _dot(yb, wo_ref[WIDTH_A:WIDTH_A + WIDTH_B, :])
    y = y + _dot(yc, wo_ref[WIDTH_A + WIDTH_B:, :])
    out_ref[...] = x_ref[...] + y


def _mix_out_call(x, o_parts, lse_parts, ob, oc, ga, gb, gc, wo):
    n, d = x.shape
    tm = ROW_TILE
    full = lambda a: pl.BlockSpec(a.shape, lambda i: (0,) * a.ndim)
    row = lambda w: pl.BlockSpec((tm, w), lambda i: (i, 0))
    view = lambda dil: pl.BlockSpec((tm // dil, dil * WIDTH_A), lambda i: (i, 0))
    return pl.pallas_call(
        _mix_out_kernel,
        grid=(n // tm,),
        in_specs=[row(d)] + [view(dil) for dil in DILATIONS] * 2
                 + [row(WIDTH_B), row(WIDTH_C), full(ga), full(gb), full(gc), full(wo)],
        out_specs=row(d),
        out_shape=jax.ShapeDtypeStruct((n, d), F32),
        scratch_shapes=[pltpu.VMEM((WIDTH_A // LANES, tm, LANES), F32)] * 4,
        compiler_params=_cparams(("parallel",)),
        name="mix_out",
    )(x, *o_parts, *lse_parts, ob, oc, ga, gb, gc, wo)


def _silu(u):
    return u * (1.0 / (1.0 + jnp.exp(-u)))


def _ffn_kernel(x_ref, g_ref, w1_ref, w3_ref, w2_ref, out_ref, h_ref, acc_ref):
    f = pl.program_id(1)

    @pl.when(f == 0)
    def _():
        h_ref[...] = _rms(x_ref[...], g_ref[...]).astype(BF16)
        acc_ref[...] = jnp.zeros_like(acc_ref)

    h = h_ref[...]
    a = (_silu(_dot(h, w1_ref[...])) * _dot(h, w3_ref[...])).astype(BF16)
    acc_ref[...] += _dot(a, w2_ref[...])

    @pl.when(f == pl.num_programs(1) - 1)
    def _():
        out_ref[...] = x_ref[...] + acc_ref[...]


def _ffn_call(x, g, w1, w3, w2):
    n, d = x.shape
    ff = w1.shape[1]
    tm, tf = ROW_TILE, FFN_TF
    return pl.pallas_call(
        _ffn_kernel,
        grid=(n // tm, ff // tf),
        in_specs=[pl.BlockSpec((tm, d), lambda i, f: (i, 0)), pl.BlockSpec((1, d), lambda i, f: (0, 0)),
                  pl.BlockSpec((d, tf), lambda i, f: (0, f)), pl.BlockSpec((d, tf), lambda i, f: (0, f)),
                  pl.BlockSpec((tf, d), lambda i, f: (f, 0))],
        out_specs=pl.BlockSpec((tm, d), lambda i, f: (i, 0)),
        out_shape=jax.ShapeDtypeStruct((n, d), F32),
        scratch_shapes=[pltpu.VMEM((tm, d), BF16), pltpu.VMEM((tm, d), F32)],
        compiler_params=_cparams(("parallel", "arbitrary")),
        name="ffn",
    )(x, g, w1, w3, w2)


def _router_kernel(x_ref, g_ref, wr_ref, h_ref, e_ref, gate_ref):
    h = _rms(x_ref[...], g_ref[...])
    h_ref[...] = h
    logits = jnp.dot(h, wr_ref[...], precision=lax.Precision.HIGHEST, preferred_element_type=F32)
    lane = lax.broadcasted_iota(jnp.int32, logits.shape, 1).astype(F32)
    logits = jnp.where(lane < N_EXPERTS, logits, -jnp.inf)
    m1 = jnp.max(logits, axis=-1, keepdims=True)
    i1 = jnp.min(jnp.where(logits == m1, lane, float(LANES)), axis=-1, keepdims=True)
    rest = jnp.where(lane == i1, -jnp.inf, logits)
    m2 = jnp.max(rest, axis=-1, keepdims=True)
    i2 = jnp.min(jnp.where(rest == m2, lane, float(LANES)), axis=-1, keepdims=True)
    e = jnp.exp(m2 - m1)
    den = 1.0 + e
    e_ref[...] = jnp.where(lane == 0.0, i1, jnp.where(lane == 1.0, i2, 0.0)).astype(jnp.int32)
    gate_ref[...] = jnp.where(lane == 0.0, 1.0 / den, jnp.where(lane == 1.0, e / den, 0.0))


def _router_call(x, g, wr_pad):
    n, d = x.shape
    tm = ROW_TILE
    row = lambda w: pl.BlockSpec((tm, w), lambda i: (i, 0))
    full = lambda a: pl.BlockSpec(a.shape, lambda i: (0,) * a.ndim)
    return pl.pallas_call(
        _router_kernel,
        grid=(n // tm,),
        in_specs=[row(d), full(g), full(wr_pad)],
        out_specs=[row(d), row(LANES), row(LANES)],
        out_shape=[jax.ShapeDtypeStruct((n, d), F32), jax.ShapeDtypeStruct((n, LANES), jnp.int32),
                   jax.ShapeDtypeStruct((n, LANES), F32)],
        compiler_params=_cparams(("parallel",)),
        name="router",
    )(x, g, wr_pad)


def _row_copy(src_hbm, row, dst_ref, r, sem):
    return pltpu.make_async_copy(src_hbm.at[pl.ds(row, 1)], dst_ref.at[pl.ds(r, 1)], sem)


def _start_row_gather(idx_ref, base, src_hbm, dst_ref, sem, count):
    def body(r, c):
        _row_copy(src_hbm, idx_ref[base + r], dst_ref, r, sem).start()
        return c
    lax.fori_loop(0, count, body, 0, unroll=GATHER_UNROLL)


def _wait_row_gather(src_hbm, dst_ref, sem, count):
    pltpu.make_async_copy(src_hbm.at[pl.ds(0, count)], dst_ref, sem).wait()


def _moe_ffn_kernel(tile_e_ref, tile_ok_ref, row_tok_ref, h_hbm, w1_ref, w3_ref, w2_ref, out_ref,
                    xg_ref, hb_ref, acc_ref, sem):
    i = pl.program_id(0)
    f = pl.program_id(1)
    n_tiles = pl.num_programs(0)
    tm = MOE_TM
    slot = i % 2

    @pl.when(f == 0)
    def _():
        @pl.when(i == 0)
        def _():
            _start_row_gather(row_tok_ref, 0, h_hbm, xg_ref.at[0], sem.at[0], tm)

        _wait_row_gather(h_hbm, xg_ref.at[slot], sem.at[slot], tm)

        @pl.when(i + 1 < n_tiles)
        def _():
            _start_row_gather(row_tok_ref, (i + 1) * tm, h_hbm, xg_ref.at[1 - slot], sem.at[1 - slot], tm)

        hb_ref[...] = xg_ref[slot].astype(BF16)
        acc_ref[...] = jnp.zeros_like(acc_ref)

    @pl.when(tile_ok_ref[i] != 0)
    def _():
        h = hb_ref[...]
        a = (_silu(_dot(h, w1_ref[...])) * _dot(h, w3_ref[...])).astype(BF16)
        acc_ref[...] += _dot(a, w2_ref[...])

    @pl.when(f == pl.num_programs(1) - 1)
    def _():
        out_ref[...] = acc_ref[...]


def _moe_ffn_call(tile_e, tile_ok, row_tok, h, w1, w3, w2):
    n, d = h.shape
    n_tiles = tile_e.shape[0]
    ff = w1.shape[2]
    tm, tf = MOE_TM, MOE_TF
    grid_spec = pltpu.PrefetchScalarGridSpec(
        num_scalar_prefetch=3,
        grid=(n_tiles, ff // tf),
        in_specs=[pl.BlockSpec(memory_space=pl.ANY),
                  pl.BlockSpec((None, d, tf), lambda i, f, te, tv, rt: (te[i], 0, f)),
                  pl.BlockSpec((None, d, tf), lambda i, f, te, tv, rt: (te[i], 0, f)),
                  pl.BlockSpec((None, tf, d), lambda i, f, te, tv, rt: (te[i], f, 0))],
        out_specs=pl.BlockSpec((tm, d), lambda i, f, te, tv, rt: (i, 0)),
        scratch_shapes=[pltpu.VMEM((2, tm, d), F32), pltpu.VMEM((tm, d), BF16), pltpu.VMEM((tm, d), F32),
                        pltpu.SemaphoreType.DMA((2,))],
    )
    return pl.pallas_call(
        _moe_ffn_kernel,
        grid_spec=grid_spec,
        out_shape=jax.ShapeDtypeStruct((n_tiles * tm, d), F32),
        compiler_params=_cparams(("arbitrary", "arbitrary")),
        name="moe_ffn",
    )(tile_e, tile_ok, row_tok, h, w1, w3, w2)


def _combine_kernel(d0_ref, d1_ref, y_hbm, x_ref, gate_ref, out_ref, b0_ref, b1_ref, sem):
    i = pl.program_id(0)
    n_tiles = pl.num_programs(0)
    tm = COMBINE_TM
    slot = i % 2

    def start(tile, s):
        _start_row_gather(d0_ref, tile * tm, y_hbm, b0_ref.at[s], sem.at[0, s], tm)
        _start_row_gather(d1_ref, tile * tm, y_hbm, b1_ref.at[s], sem.at[1, s], tm)

    @pl.when(i == 0)
    def _():
        start(0, 0)

    _wait_row_gather(y_hbm, b0_ref.at[slot], sem.at[0, slot], tm)
    _wait_row_gather(y_hbm, b1_ref.at[slot], sem.at[1, slot], tm)

    @pl.when(i + 1 < n_tiles)
    def _():
        start(i + 1, 1 - slot)

    gates = gate_ref[...]
    out_ref[...] = x_ref[...] + (gates[:, 0:1] * b0_ref[slot] + gates[:, 1:2] * b1_ref[slot])


def _combine_call(d0, d1, y, x, gates):
    n, d = x.shape
    tm = COMBINE_TM
    grid_spec = pltpu.PrefetchScalarGridSpec(
        num_scalar_prefetch=2,
        grid=(n // tm,),
        in_specs=[pl.BlockSpec(memory_space=pl.ANY),
                  pl.BlockSpec((tm, d), lambda i, a, b: (i, 0)),
                  pl.BlockSpec((tm, LANES), lambda i, a, b: (i, 0))],
        out_specs=pl.BlockSpec((tm, d), lambda i, a, b: (i, 0)),
        scratch_shapes=[pltpu.VMEM((2, tm, d), F32), pltpu.VMEM((2, tm, d), F32), pltpu.SemaphoreType.DMA((2, 2))],
    )
    return pl.pallas_call(
        _combine_kernel,
        grid_spec=grid_spec,
        out_shape=jax.ShapeDtypeStruct((n, d), F32),
        compiler_params=_cparams(("arbitrary",)),
        name="moe_combine",
    )(d0, d1, y, x, gates)


def _moe_plan(top_e, n_tiles):
    n_assign = top_e.shape[0] * TOP_K
    flat_e = top_e.reshape(n_assign)
    onehot = (flat_e[:, None] == jnp.arange(N_EXPERTS, dtype=jnp.int32)[None, :]).astype(jnp.int32)
    csum = jnp.cumsum(onehot, axis=0)
    rank = jnp.take_along_axis(csum, flat_e[:, None], axis=1)[:, 0] - 1
    counts = csum[-1]
    padded = (counts + MOE_TM - 1) // MOE_TM * MOE_TM
    pend = jnp.cumsum(padded)
    dest = (pend - padded)[flat_e] + rank
    row_tok = jnp.zeros((n_tiles * MOE_TM,), jnp.int32).at[dest].set(jnp.arange(n_assign, dtype=jnp.int32) // TOP_K)
    tile_start = jnp.arange(n_tiles, dtype=jnp.int32) * MOE_TM
    tile_e = jnp.minimum(jnp.searchsorted(pend, tile_start, side="right"), N_EXPERTS - 1).astype(jnp.int32)
    tile_ok = (tile_start < pend[-1]).astype(jnp.int32)
    dest = dest.reshape(-1, TOP_K).astype(jnp.int32)
    return tile_e, tile_ok, row_tok, dest[:, 0], dest[:, 1]


def _moe_layer(x, g, wr_pad, w1, w3, w2):
    n = x.shape[0]
    h, top_e, gates = _router_call(x, g, wr_pad)
    n_tiles = -(-(n * TOP_K + N_EXPERTS * (MOE_TM - 1)) // MOE_TM)
    tile_e, tile_ok, row_tok, d0, d1 = _moe_plan(top_e[:, :TOP_K], n_tiles)
    y = _moe_ffn_call(tile_e, tile_ok, row_tok, h, w1, w3, w2)
    return _combine_call(d0, d1, y, x, gates)


def _norm_kernel(x_ref, g_ref, o_ref):
    o_ref[...] = _rms(x_ref[...], g_ref[...])


def _norm_call(x, g):
    n, d = x.shape
    tm = ROW_TILE
    return pl.pallas_call(
        _norm_kernel,
        grid=(n // tm,),
        in_specs=[pl.BlockSpec((tm, d), lambda i: (i, 0)), pl.BlockSpec((1, d), lambda i: (0, 0))],
        out_specs=pl.BlockSpec((tm, d), lambda i: (i, 0)),
        out_shape=jax.ShapeDtypeStruct((n, d), F32),
        compiler_params=_cparams(("parallel",)),
        name="final_norm",
    )(x, g)


def _rope_tables(seq):
    inv = ROPE_BASE ** (-jnp.arange(0, QK_ROPE, 2, dtype=F32) / QK_ROPE)
    ang = jnp.arange(seq)[:, None].astype(F32) * inv[None, :]
    cos, sin = jnp.cos(ang), jnp.sin(ang)
    pad = LANES - QK_NOPE - QK_ROPE
    ctab = jnp.concatenate([jnp.ones((seq, QK_NOPE), F32), cos, cos, jnp.zeros((seq, pad), F32)], axis=1)
    stab = jnp.concatenate([jnp.zeros((seq, QK_NOPE), F32), sin, sin, jnp.zeros((seq, pad), F32)], axis=1)
    return ctab, stab


def _split_w_in(w_in):
    d = w_in.shape[0]
    bounds = np.cumsum([WIDTH_A, WIDTH_A, WIDTH_A, Q_LORA, KV_LORA, QK_ROPE, WIDTH_C, WIDTH_C])
    qa, ka, va, cq, ckv, kr, qc, kc, vc = jnp.split(w_in, bounds.tolist(), axis=1)
    half = QK_ROPE // 2
    z_lo = jnp.zeros((d, QK_NOPE), w_in.dtype)
    z_hi = jnp.zeros((d, LANES - QK_NOPE - QK_ROPE), w_in.dtype)
    rope_blk = jnp.concatenate([z_lo, kr, z_hi], axis=1)
    swap_blk = jnp.concatenate([z_lo, -kr[:, half:], kr[:, :half], z_hi], axis=1)
    score_scale = HEAD_DIM ** -0.5
    wa = jnp.concatenate([qa * score_scale, ka, va], axis=1).astype(BF16)
    wc = jnp.concatenate([qc * score_scale, kc, vc], axis=1).astype(BF16)
    wb = jnp.concatenate([cq, ckv, rope_blk, swap_blk], axis=1).astype(BF16)
    return wa, wc, wb


def _split_w_uq(w_uq):
    r = w_uq.shape[0]
    w = w_uq.reshape(r, N_HEADS_B, QK_NOPE + QK_ROPE)
    nope, rope = w[..., :QK_NOPE], w[..., QK_NOPE:]
    half = QK_ROPE // 2
    z_hi = jnp.zeros((r, N_HEADS_B, LANES - QK_NOPE - QK_ROPE), w_uq.dtype)
    w1 = jnp.concatenate([nope, rope, z_hi], axis=-1)
    w2 = jnp.concatenate([jnp.zeros_like(nope), -rope[..., half:], rope[..., :half], z_hi], axis=-1)
    return w1.reshape(r, -1).astype(BF16), w2.reshape(r, -1).astype(BF16)


def _split_w_ukv(w_ukv):
    r = w_ukv.shape[0]
    w = w_ukv.reshape(r, N_HEADS_B, QK_NOPE + V_HEAD)
    k_nope, v = w[..., :QK_NOPE], w[..., QK_NOPE:]
    wk = jnp.concatenate([k_nope, jnp.zeros((r, N_HEADS_B, LANES - QK_NOPE), w_ukv.dtype)], axis=-1)
    return wk.reshape(r, -1).astype(BF16), v.reshape(r, -1).T.astype(BF16)


def kernel(x, g_mix, w_in, g_q, g_kv, w_uq, w_ukv, rpb, g_out_a, g_out_b, g_out_c, w_o, g_ffn, w1, w3, w2,
           w_router, e_w1, e_w3, e_w2, g_final):
    batch, seq, d = x.shape
    n = batch * seq
    depth = g_mix.shape[0]
    rows = seq // GRID_W
    ctab, stab = _rope_tables(seq)
    xf = x.reshape(n, d)
    for layer in range(depth):
        wa, wc, wb = _split_w_in(w_in[layer])
        wq1, wq2 = _split_w_uq(w_uq[layer])
        wk, wvt = _split_w_ukv(w_ukv[layer])
        pa, pa4, pa16, pc, qm, km, vt = _proj_call(xf, g_mix[layer][None], wa, wc, wb, g_q[layer][None], g_kv[layer][None],
                                        wq1, wq2, wk, wvt, ctab, stab, seq)
        o_parts, lse_parts = [], []
        for dil, view in zip(DILATIONS, (pa, pa4, pa16)):
            o, lse = _band_call(view.reshape(batch, seq // dil, dil * 3 * WIDTH_A), dil)
            o_parts.append(o.reshape(n // dil, dil * WIDTH_A))
            lse_parts.append(lse.reshape(n // dil, dil * WIDTH_A))
        ob = _mla_call(qm, km, vt, batch, seq).reshape(n, WIDTH_B)
        oc = _na_call(pc, _na_tables(rpb[layer], rows), batch, seq).reshape(n, WIDTH_C)
        xf = _mix_out_call(xf, o_parts, lse_parts, ob, oc, g_out_a[layer][None], g_out_b[layer][None],
                           g_out_c[layer][None], w_o[layer].astype(BF16))
        j = layer // 2
        if layer % 2 == 0:
            xf = _ffn_call(xf, g_ffn[layer][None], w1[j].astype(BF16), w3[j].astype(BF16), w2[j].astype(BF16))
        else:
            wr_pad = jnp.pad(w_router[j], ((0, 0), (0, LANES - N_EXPERTS)))
            xf = _moe_layer(xf, g_ffn[layer][None], wr_pad, e_w1[j].astype(BF16), e_w3[j].astype(BF16),
                            e_w2[j].astype(BF16))
    return _norm_call(xf, g_final[None]).reshape(batch, seq, d)
```

```python
import functools
import math

import numpy as np
import jax
import jax.numpy as jnp
from jax import lax
from jax.experimental import pallas as pl
from jax.experimental.pallas import tpu as pltpu

F32 = jnp.float32
BF16 = jnp.bfloat16

LANES = 128
V7X_VMEM_LIMIT_BYTES = 52 * 1024 * 1024

HEAD_DIM = 64
N_HEADS_A = 6
DILATIONS = (1, 4, 16)
BAND_HALF = 64
N_HEADS_B = 6
Q_LORA = 384
KV_LORA = 256
QK_NOPE = 64
QK_ROPE = 32
V_HEAD = 64
ROPE_BASE = 10000.0
N_HEADS_C = 4
GRID_W = 64
NA_ROWS = 8
NA_COLS = 16
WIDTH_A = N_HEADS_A * HEAD_DIM
WIDTH_B = N_HEADS_B * V_HEAD
WIDTH_C = N_HEADS_C * HEAD_DIM
N_EXPERTS = 8
TOP_K = 2
RMS_EPS = 1e-6
NEG_INF = -1e30

ROW_TILE = 512
MLA_TQ = 256
MLA_TK = 256
MLA_UNROLL = 8
MLA_LOOKAHEAD = 4
MLA_HEADS_PER_STEP = 6
MLA_DEN_ROWS = 16
MLA_Q_PRESCALE = (QK_NOPE + QK_ROPE) ** -0.5 * math.log2(math.e)
BAND_TQ = 128
BAND_TILES_PER_STEP = 4
BAND_LOOKAHEAD = 6
NA_TILE_ROWS = 2
NA_KEY_ROWS = 10
NA_TILES_PER_STEP = 4
NA_LOOKAHEAD = 16
MOE_TM = 512
MOE_TF = 1792
COMBINE_TM = 256
GATHER_UNROLL = 8


def _cparams(semantics):
    return pltpu.CompilerParams(dimension_semantics=semantics, vmem_limit_bytes=V7X_VMEM_LIMIT_BYTES)


def _rms(x, g):
    return x * lax.rsqrt(jnp.mean(x * x, axis=-1, keepdims=True) + RMS_EPS) * g


def _dot(a, b):
    return jnp.dot(a, b, preferred_element_type=F32)


def _dot_nt(a, b):
    return lax.dot_general(a, b, (((1,), (1,)), ((), ())), preferred_element_type=F32)


def _proj_kernel(x_ref, g_ref, wa_ref, wc_ref, wb_ref, gq_ref, gkv_ref, wq1_ref, wq2_ref, wk_ref, wvt_ref,
                 ct_ref, st_ref, pa_ref, pa4_ref, pa16_ref, pc_ref, qm_ref, km_ref, vt_ref, pa_scr):
    tm = x_ref.shape[0]
    h = _rms(x_ref[...], g_ref[...]).astype(BF16)
    pa = _dot(h, wa_ref[...])
    pa_ref[...] = pa.astype(BF16)
    n_chunks = pa.shape[1] // LANES
    for c in range(n_chunks):
        pa_scr[c] = pa[:, c * LANES:(c + 1) * LANES]
    for dil, view_ref in ((DILATIONS[1], pa4_ref), (DILATIONS[2], pa16_ref)):
        for r in range(dil):
            for c in range(n_chunks):
                col = r * 3 * WIDTH_A + c * LANES
                view_ref[:, col:col + LANES] = pa_scr[c, pl.ds(r, tm // dil, stride=dil), :].astype(BF16)
    pc_ref[...] = _dot(h, wc_ref[...]).astype(BF16)
    pb = _dot(h, wb_ref[...])
    hq = _rms(pb[:, :Q_LORA], gq_ref[...]).astype(BF16)
    hkv = _rms(pb[:, Q_LORA:Q_LORA + KV_LORA], gkv_ref[...]).astype(BF16)
    r1 = pb[:, Q_LORA + KV_LORA:Q_LORA + KV_LORA + LANES]
    r2 = pb[:, Q_LORA + KV_LORA + LANES:]
    ct = ct_ref[...]
    st = st_ref[...]
    qa = _dot(hq, wq1_ref[...])
    qb = _dot(hq, wq2_ref[...])
    kn = _dot(hkv, wk_ref[...])
    kr = r1 * ct + r2 * st
    for hd in range(N_HEADS_B):
        sl = slice(hd * LANES, (hd + 1) * LANES)
        qm_ref[:, sl] = ((qa[:, sl] * ct + qb[:, sl] * st) * MLA_Q_PRESCALE).astype(BF16)
        km_ref[:, sl] = (kn[:, sl] + kr).astype(BF16)
    vt_ref[...] = _dot_nt(wvt_ref[...], hkv).astype(BF16)


def _proj_call(x, g, wa, wc, wb, gq, gkv, wq1, wq2, wk, wvt, ctab, stab, seq):
    n, d = x.shape
    tm = ROW_TILE
    tiles_per_seq = seq // tm
    full = lambda a: pl.BlockSpec(a.shape, lambda i: (0,) * a.ndim)
    row = lambda w: pl.BlockSpec((tm, w), lambda i: (i, 0))
    tab = pl.BlockSpec((tm, LANES), lambda i: (i % tiles_per_seq, 0))
    view = lambda dil: pl.BlockSpec((tm // dil, dil * 3 * WIDTH_A), lambda i: (i, 0))
    hb = N_HEADS_B * LANES
    return pl.pallas_call(
        _proj_kernel,
        grid=(n // tm,),
        in_specs=[row(d), full(g), full(wa), full(wc), full(wb), full(gq), full(gkv), full(wq1), full(wq2),
                  full(wk), full(wvt), tab, tab],
        out_specs=[row(3 * WIDTH_A)] + [view(dil) for dil in DILATIONS[1:]] + [row(3 * WIDTH_C), row(hb), row(hb),
                   pl.BlockSpec((WIDTH_B, tm), lambda i: (0, i))],
        out_shape=[jax.ShapeDtypeStruct((n, 3 * WIDTH_A), BF16)]
                  + [jax.ShapeDtypeStruct((n // dil, dil * 3 * WIDTH_A), BF16) for dil in DILATIONS[1:]]
                  + [jax.ShapeDtypeStruct((n, 3 * WIDTH_C), BF16),
                   jax.ShapeDtypeStruct((n, hb), BF16), jax.ShapeDtypeStruct((n, hb), BF16),
                   jax.ShapeDtypeStruct((WIDTH_B, n), BF16)],
        scratch_shapes=[pltpu.VMEM((3 * WIDTH_A // LANES, tm, LANES), F32)],
        compiler_params=_cparams(("parallel",)),
        name="proj",
    )(x, g, wa, wc, wb, gq, gkv, wq1, wq2, wk, wvt, ctab, stab)


def _run_pipeline(n_items, look, issue, consume, finish):
    if look >= n_items:
        results = [consume(i, s) for i, s in enumerate([issue(i) for i in range(n_items)])]
        for i, res in enumerate(results):
            finish(i, res)
        return
    inflight = {i: issue(i) for i in range(min(look, n_items))}
    pending = None
    for i in range(n_items):
        if i + look < n_items:
            inflight[i + look] = issue(i + look)
        if pending is not None:
            finish(*pending)
        pending = (i, consume(i, inflight.pop(i)))
    finish(*pending)


class _HeadPairs:
    def __init__(self, q_ref, k_ref, v_ref, tq, tkw):
        self.q_ref, self.k_ref, self.v_ref, self.tq, self.tkw = q_ref, k_ref, v_ref, tq, tkw
        lane = lax.broadcasted_iota(jnp.int32, (tq, LANES), 1)
        self.low = lane < HEAD_DIM
        self.ones = jnp.ones((tkw, LANES), BF16)

    @staticmethod
    def cols(h):
        return slice((h // 2) * LANES, (h // 2 + 1) * LANES)

    def scores(self, q_start, k_start, h):
        q = self.q_ref[q_start:q_start + self.tq, self.cols(h)]
        q = jnp.where(self.low if h % 2 == 0 else ~self.low, q, jnp.zeros_like(q))
        return _dot_nt(q, self.k_ref[pl.ds(k_start, self.tkw), self.cols(h)])

    def values(self, k_start, h, p):
        v = jnp.concatenate([self.v_ref[pl.ds(k_start, self.tkw), self.cols(h)], self.ones], axis=-1)
        o = _dot(p, v)
        return o[:, :LANES], o[:, LANES:]

    def merge(self, even, odd):
        return jnp.where(self.low, even, odd)


def _band_tables(dilation):
    tq, tkw = BAND_TQ, BAND_TQ + 2 * BAND_HALF
    shift = np.array([0, -BAND_HALF, -2 * BAND_HALF])
    rel = shift[:, None, None] + np.arange(tkw)[None, None, :] - np.arange(tq)[None, :, None]
    dist = np.abs(rel)
    slopes = 2.0 ** (-8.0 * np.arange(1, N_HEADS_A + 1) / N_HEADS_A)
    bias = -(slopes[None, :, None, None] * dilation) * dist[:, None].astype(np.float64)
    tab = np.where(dist[:, None] <= BAND_HALF, bias, NEG_INF)
    return jnp.asarray(tab, F32)


def _band_kernel(q_ref, k_ref, v_ref, tab_ref, o_ref, lse_ref, *, n):
    tq = BAND_TQ
    tkw = tq + 2 * BAND_HALF
    tiles = q_ref.shape[0] // tq
    last_tile = n // tq - 1
    items = [(t, h) for t in range(tiles) for h in range(N_HEADS_A)]
    info = []
    for t in range(tiles):
        i = pl.program_id(2) * tiles + t
        start = pl.multiple_of(jnp.clip(i * tq - BAND_HALF, 0, n - tkw), BAND_HALF)
        variant = jnp.minimum(i, 1) + (i == last_tile).astype(jnp.int32)
        info.append((start, variant))
    pairs = _HeadPairs(q_ref, k_ref, v_ref, tq, tkw)
    held = {}

    def issue(idx):
        t, h = items[idx]
        return pairs.scores(t * tq, info[t][0], h)

    def consume(idx, s):
        t, h = items[idx]
        s = s + tab_ref[info[t][1], h]
        m = jnp.max(s, axis=-1, keepdims=True)
        return jnp.exp(s - m).astype(BF16), m

    def finish(idx, res):
        t, h = items[idx]
        p, m = res
        out, den = pairs.values(info[t][0], h, p)
        held[h % 2] = (out / den, m + jnp.log(den))
        if h % 2 == 1:
            cols = pairs.cols(h)
            o_ref[t * tq:(t + 1) * tq, cols] = pairs.merge(held[0][0], held[1][0])
            lse_ref[t * tq:(t + 1) * tq, cols] = pairs.merge(held[0][1], held[1][1])

    _run_pipeline(len(items), BAND_LOOKAHEAD, issue, consume, finish)


def _band_call(pa_view, dilation):
    b, n, _ = pa_view.shape
    rows = BAND_TQ * BAND_TILES_PER_STEP
    w = WIDTH_A
    tab = _band_tables(dilation)
    qspec = pl.BlockSpec((None, rows, w), lambda bb, r, i: (bb, i, 3 * r))
    kspec = pl.BlockSpec((None, n, w), lambda bb, r, i: (bb, 0, 3 * r + 1))
    vspec = pl.BlockSpec((None, n, w), lambda bb, r, i: (bb, 0, 3 * r + 2))
    tspec = pl.BlockSpec(tab.shape, lambda bb, r, i: (0, 0, 0, 0))
    ospec = pl.BlockSpec((None, rows, w), lambda bb, r, i: (bb, i, r))
    shape = jax.ShapeDtypeStruct((b, n, dilation * w), F32)
    return pl.pallas_call(
        functools.partial(_band_kernel, n=n),
        grid=(b, dilation, n // rows),
        in_specs=[qspec, kspec, vspec, tspec],
        out_specs=[ospec, ospec],
        out_shape=[shape, shape],
        compiler_params=_cparams(("parallel", "parallel", "parallel")),
        name=f"band_d{dilation}",
    )(pa_view, pa_view, pa_view, tab)


def _mla_kernel(q_ref, k_ref, vt_ref, o_ref):
    tq = q_ref.shape[0]
    seq = k_ref.shape[0]
    tk = MLA_TK
    nh = MLA_HEADS_PER_STEP
    n_chunks = seq // tk
    items = [(u, h) for u in range(MLA_UNROLL) for h in range(nh)]
    look = MLA_LOOKAHEAD
    ones_rows = jnp.ones((MLA_DEN_ROWS, tk), BF16)

    def key_slice(chunk):
        return pl.ds(pl.multiple_of(chunk * tk, tk), tk)

    def score_matmul(chunk, h):
        k = k_ref[key_slice(chunk), h * LANES:(h + 1) * LANES]
        return _dot_nt(k, q_ref[:, h * LANES:(h + 1) * LANES])

    def value_matmul(chunk, h, p):
        vt = jnp.concatenate([vt_ref[h * V_HEAD:(h + 1) * V_HEAD, key_slice(chunk)], ones_rows], axis=0)
        return _dot(vt, p)

    def body(j, carry):
        state = list(carry[:2 * nh])
        scores = dict(zip(items[:look], carry[2 * nh:2 * nh + look]))
        pend_p, pend_alpha = carry[2 * nh + look:]
        pending = (jnp.maximum(j * MLA_UNROLL - 1, 0), nh - 1, pend_p, pend_alpha)
        ahead = []
        for idx, (u, h) in enumerate(items):
            la = idx + look
            if la < len(items):
                lu, lh = items[la]
                scores[lu, lh] = score_matmul(j * MLA_UNROLL + lu, lh)
            else:
                lu, lh = items[la - len(items)]
                ahead.append(score_matmul(jnp.minimum((j + 1) * MLA_UNROLL + lu, n_chunks - 1), lh))
            pc, ph, pp, pa = pending
            state[2 * ph + 1] = pa * state[2 * ph + 1] + value_matmul(pc, ph, pp)
            s = scores.pop((u, h))
            m_new = jnp.maximum(state[2 * h], jnp.max(s, axis=0, keepdims=True))
            alpha = jnp.exp2(state[2 * h] - m_new)
            state[2 * h] = m_new
            pending = (j * MLA_UNROLL + u, h, jnp.exp2(s - m_new).astype(BF16), alpha)
        return tuple(state) + tuple(ahead) + (pending[2], pending[3])

    init = (jnp.full((1, tq), NEG_INF, F32), jnp.zeros((V_HEAD + MLA_DEN_ROWS, tq), F32)) * nh
    init += tuple(score_matmul(u, h) for u, h in items[:look])
    init += (jnp.zeros((tk, tq), BF16), jnp.ones((1, tq), F32))
    res = lax.fori_loop(0, n_chunks // MLA_UNROLL, body, init)
    accs = [res[2 * h + 1] for h in range(nh)]
    accs[nh - 1] = res[-1] * accs[nh - 1] + value_matmul(n_chunks - 1, nh - 1, res[-2])
    out_t = jnp.concatenate([a[:V_HEAD] / a[V_HEAD:V_HEAD + 1] for a in accs], axis=0)
    o_ref[...] = out_t.T


def _mla_call(qm, km, vt, batch, seq):
    tq = MLA_TQ
    nh = MLA_HEADS_PER_STEP
    qspec = pl.BlockSpec((None, tq, nh * LANES), lambda b, g, i: (b, i, g))
    kspec = pl.BlockSpec((None, seq, nh * LANES), lambda b, g, i: (b, 0, g))
    vspec = pl.BlockSpec((nh * V_HEAD, seq), lambda b, g, i: (g, b))
    ospec = pl.BlockSpec((None, tq, nh * V_HEAD), lambda b, g, i: (b, i, g))
    return pl.pallas_call(
        _mla_kernel,
        grid=(batch, N_HEADS_B // nh, seq // tq),
        in_specs=[qspec, kspec, vspec],
        out_specs=ospec,
        out_shape=jax.ShapeDtypeStruct((batch, seq, WIDTH_B), F32),
        compiler_params=_cparams(("parallel", "parallel", "parallel")),
        name="mla",
    )(qm.reshape(batch, seq, -1), km.reshape(batch, seq, -1), vt)


def _na_variant(i, n_tiles):
    return jnp.minimum(i, 2) + jnp.maximum(i - (n_tiles - 3), 0)


def _na_kernel(q_ref, k_ref, v_ref, tab_ref, o_ref, *, rows):
    tq = NA_TILE_ROWS * GRID_W
    tkw = NA_KEY_ROWS * GRID_W
    tiles = q_ref.shape[0] // tq
    n_tiles = rows // NA_TILE_ROWS
    items = [(t, h) for t in range(tiles) for h in range(N_HEADS_C)]
    info = []
    for t in range(tiles):
        i = pl.program_id(1) * tiles + t
        base = jnp.clip(i * NA_TILE_ROWS - NA_ROWS // 2, 0, rows - NA_KEY_ROWS)
        info.append((pl.multiple_of(base * GRID_W, GRID_W), _na_variant(i, n_tiles)))
    pairs = _HeadPairs(q_ref, k_ref, v_ref, tq, tkw)
    held = {}

    def issue(idx):
        t, h = items[idx]
        return pairs.scores(t * tq, info[t][0], h)

    def consume(idx, s):
        t, h = items[idx]
        s = s + tab_ref[info[t][1], h]
        return jnp.exp(s - jnp.max(s, axis=-1, keepdims=True)).astype(BF16)

    def finish(idx, p):
        t, h = items[idx]
        out, den = pairs.values(info[t][0], h, p)
        held[h % 2] = out / den
        if h % 2 == 1:
            o_ref[t * tq:(t + 1) * tq, pairs.cols(h)] = pairs.merge(held[0], held[1])

    _run_pipeline(len(items), NA_LOOKAHEAD, issue, consume, finish)


def _na_tables(rpb, rows):
    r0 = np.array([0, 2, 4, rows - 4, rows - 2])
    base = np.clip(r0 - NA_ROWS // 2, 0, rows - NA_KEY_ROWS)
    r = r0[:, None] + np.arange(NA_TILE_ROWS)[None, :]
    row_start = np.clip(r - NA_ROWS // 2, 0, rows - NA_ROWS)
    krow = base[:, None] + np.arange(NA_KEY_ROWS)[None, :]
    drow = krow[:, None, :] - r[:, :, None]
    row_ok = (krow[:, None, :] >= row_start[:, :, None]) & (krow[:, None, :] < row_start[:, :, None] + NA_ROWS)
    c = np.arange(GRID_W)
    win_start = np.clip(c - NA_COLS // 2, 0, GRID_W - NA_COLS)
    col_ok = (c[None, :] >= win_start[:, None]) & (c[None, :] < win_start[:, None] + NA_COLS)
    dcol = np.clip(c[None, :] - c[:, None], -(NA_COLS - 1), NA_COLS - 1)
    ok = row_ok[:, :, None, :, None] & col_ok[None, None, :, None, :]
    di = np.clip(drow, -(NA_ROWS - 1), NA_ROWS - 1) + (NA_ROWS - 1)
    pick_col = (dcol[:, :, None] + NA_COLS - 1 == np.arange(2 * NA_COLS - 1)).astype(np.float32)
    pick_row = (di[..., None] == np.arange(2 * NA_ROWS - 1)).astype(np.float32)
    hi = lax.Precision.HIGHEST
    toeplitz = jnp.einsum("hab,cjb->hacj", rpb.astype(F32), pick_col, precision=hi)
    bias = jnp.einsum("vqka,hacj->vhqckj", pick_row, toeplitz, precision=hi)
    tab = jnp.where(jnp.asarray(ok)[:, None], bias, NEG_INF)
    return tab.reshape(5, N_HEADS_C, NA_TILE_ROWS * GRID_W, NA_KEY_ROWS * GRID_W)


def _na_call(pc, tab, batch, seq):
    rows = seq // GRID_W
    tq = NA_TILE_ROWS * GRID_W * NA_TILES_PER_STEP
    w = WIDTH_C
    qspec = pl.BlockSpec((None, tq, w), lambda b, i: (b, i, 0))
    kspec = pl.BlockSpec((None, seq, w), lambda b, i: (b, 0, 1))
    vspec = pl.BlockSpec((None, seq, w), lambda b, i: (b, 0, 2))
    tspec = pl.BlockSpec(tab.shape, lambda b, i: (0, 0, 0, 0))
    ospec = pl.BlockSpec((None, tq, w), lambda b, i: (b, i, 0))
    pc3 = pc.reshape(batch, seq, 3 * w)
    return pl.pallas_call(
        functools.partial(_na_kernel, rows=rows),
        grid=(batch, seq // tq),
        in_specs=[qspec, kspec, vspec, tspec],
        out_specs=ospec,
        out_shape=jax.ShapeDtypeStruct((batch, seq, w), F32),
        compiler_params=_cparams(("parallel", "parallel")),
        name="natten",
    )(pc3, pc3, pc3, tab)


def _mix_out_kernel(x_ref, o1_ref, o2_ref, o3_ref, l1_ref, l2_ref, l3_ref, ob_ref, oc_ref,
                    ga_ref, gb_ref, gc_ref, wo_ref, out_ref, o2_scr, o3_scr, l2_scr, l3_scr):
    tm = x_ref.shape[0]
    n_chunks = WIDTH_A // LANES

    def natural(view_ref, scr, dil):
        for r in range(dil):
            for c in range(n_chunks):
                col = r * WIDTH_A + c * LANES
                scr[c, pl.ds(r, tm // dil, stride=dil), :] = view_ref[:, col:col + LANES]
        return jnp.concatenate([scr[c] for c in range(n_chunks)], axis=-1)

    l1, o1 = l1_ref[...], o1_ref[...]
    l2, o2 = natural(l2_ref, l2_scr, DILATIONS[1]), natural(o2_ref, o2_scr, DILATIONS[1])
    l3, o3 = natural(l3_ref, l3_scr, DILATIONS[2]), natural(o3_ref, o3_scr, DILATIONS[2])
    mx = jnp.maximum(jnp.maximum(l1, l2), l3)
    w1, w2, w3 = jnp.exp(l1 - mx), jnp.exp(l2 - mx), jnp.exp(l3 - mx)
    oa = (w1 * o1 + w2 * o2 + w3 * o3) / (w1 + w2 + w3)
    ya = _rms(oa, ga_ref[...]).astype(BF16)
    yb = _rms(ob_ref[...], gb_ref[...]).astype(BF16)
    yc = _rms(oc_ref[...], gc_ref[...]).astype(BF16)
    y = _dot(ya, wo_ref[:WIDTH_A, :])
    y = y + _dot(yb, wo_ref[WIDTH_A:WIDTH_A + WIDTH_B, :])
    y = y + _dot(yc, wo_ref[WIDTH_A + WIDTH_B:, :])
    out_ref[...] = x_ref[...] + y


def _mix_out_call(x, o_parts, lse_parts, ob, oc, ga, gb, gc, wo):
    n, d = x.shape
    tm = ROW_TILE
    full = lambda a: pl.BlockSpec(a.shape, lambda i: (0,) * a.ndim)
    row = lambda w: pl.BlockSpec((tm, w), lambda i: (i, 0))
    view = lambda dil: pl.BlockSpec((tm // dil, dil * WIDTH_A), lambda i: (i, 0))
    return pl.pallas_call(
        _mix_out_kernel,
        grid=(n // tm,),
        in_specs=[row(d)] + [view(dil) for dil in DILATIONS] * 2
                 + [row(WIDTH_B), row(WIDTH_C), full(ga), full(gb), full(gc), full(wo)],
        out_specs=row(d),
        out_shape=jax.ShapeDtypeStruct((n, d), F32),
        scratch_shapes=[pltpu.VMEM((WIDTH_A // LANES, tm, LANES), F32)] * 4,
        compiler_params=_cparams(("parallel",)),
        name="mix_out",
    )(x, *o_parts, *lse_parts, ob, oc, ga, gb, gc, wo)


def _silu(u):
    return u * (1.0 / (1.0 + jnp.exp(-u)))


def _ffn_kernel(x_ref, g_ref, w1_ref, w3_ref, w2_ref, out_ref):
    x = x_ref[...]
    h = _rms(x, g_ref[...]).astype(BF16)
    a = (_silu(_dot(h, w1_ref[...])) * _dot(h, w3_ref[...])).astype(BF16)
    out_ref[...] = x + _dot(a, w2_ref[...])


def _ffn_call(x, g, w1, w3, w2):
    n, d = x.shape
    tm = ROW_TILE
    resident = lambda a: pl.BlockSpec(a.shape, lambda i: (0, 0), pipeline_mode=pl.Buffered(1))
    return pl.pallas_call(
        _ffn_kernel,
        grid=(n // tm,),
        in_specs=[pl.BlockSpec((tm, d), lambda i: (i, 0)), pl.BlockSpec((1, d), lambda i: (0, 0)),
                  resident(w1), resident(w3), resident(w2)],
        out_specs=pl.BlockSpec((tm, d), lambda i: (i, 0)),
        out_shape=jax.ShapeDtypeStruct((n, d), F32),
        compiler_params=_cparams(("parallel",)),
        name="ffn",
    )(x, g, w1, w3, w2)


def _router_kernel(x_ref, g_ref, wr_ref, h_ref, e_ref, gate_ref):
    h = _rms(x_ref[...], g_ref[...])
    h_ref[...] = h
    logits = jnp.dot(h, wr_ref[...], precision=lax.Precision.HIGHEST, preferred_element_type=F32)
    lane = lax.broadcasted_iota(jnp.int32, logits.shape, 1).astype(F32)
    logits = jnp.where(lane < N_EXPERTS, logits, -jnp.inf)
    m1 = jnp.max(logits, axis=-1, keepdims=True)
    i1 = jnp.min(jnp.where(logits == m1, lane, float(LANES)), axis=-1, keepdims=True)
    rest = jnp.where(lane == i1, -jnp.inf, logits)
    m2 = jnp.max(rest, axis=-1, keepdims=True)
    i2 = jnp.min(jnp.where(rest == m2, lane, float(LANES)), axis=-1, keepdims=True)
    e = jnp.exp(m2 - m1)
    den = 1.0 + e
    e_ref[...] = jnp.where(lane == 0.0, i1, jnp.where(lane == 1.0, i2, 0.0)).astype(jnp.int32)
    gate_ref[...] = jnp.where(lane == 0.0, 1.0 / den, jnp.where(lane == 1.0, e / den, 0.0))


def _router_call(x, g, wr_pad):
    n, d = x.shape
    tm = ROW_TILE
    row = lambda w: pl.BlockSpec((tm, w), lambda i: (i, 0))
    full = lambda a: pl.BlockSpec(a.shape, lambda i: (0,) * a.ndim)
    return pl.pallas_call(
        _router_kernel,
        grid=(n // tm,),
        in_specs=[row(d), full(g), full(wr_pad)],
        out_specs=[row(d), row(LANES), row(LANES)],
        out_shape=[jax.ShapeDtypeStruct((n, d), F32), jax.ShapeDtypeStruct((n, LANES), jnp.int32),
                   jax.ShapeDtypeStruct((n, LANES), F32)],
        compiler_params=_cparams(("parallel",)),
        name="router",
    )(x, g, wr_pad)


def _row_copy(src_hbm, row, dst_ref, r, sem):
    return pltpu.make_async_copy(src_hbm.at[pl.ds(row, 1)], dst_ref.at[pl.ds(r, 1)], sem)


def _start_row_gather(idx_ref, base, src_hbm, dst_ref, sem, count):
    def body(r, c):
        _row_copy(src_hbm, idx_ref[base + r], dst_ref, r, sem).start()
        return c
    lax.fori_loop(0, count, body, 0, unroll=GATHER_UNROLL)


def _wait_row_gather(src_hbm, dst_ref, sem, count):
    pltpu.make_async_copy(src_hbm.at[pl.ds(0, count)], dst_ref, sem).wait()


def _moe_ffn_kernel(tile_e_ref, tile_ok_ref, row_tok_ref, h_hbm, w1_ref, w3_ref, w2_ref, out_ref,
                    xg_ref, hb_ref, acc_ref, sem):
    i = pl.program_id(0)
    f = pl.program_id(1)
    n_tiles = pl.num_programs(0)
    tm = MOE_TM
    slot = i % 2

    @pl.when(f == 0)
    def _():
        @pl.when(i == 0)
        def _():
            _start_row_gather(row_tok_ref, 0, h_hbm, xg_ref.at[0], sem.at[0], tm)

        _wait_row_gather(h_hbm, xg_ref.at[slot], sem.at[slot], tm)

        @pl.when(i + 1 < n_tiles)
        def _():
            _start_row_gather(row_tok_ref, (i + 1) * tm, h_hbm, xg_ref.at[1 - slot], sem.at[1 - slot], tm)

        hb_ref[...] = xg_ref[slot].astype(BF16)
        acc_ref[...] = jnp.zeros_like(acc_ref)

    @pl.when(tile_ok_ref[i] != 0)
    def _():
        h = hb_ref[...]
        a = (_silu(_dot(h, w1_ref[...])) * _dot(h, w3_ref[...])).astype(BF16)
        acc_ref[...] += _dot(a, w2_ref[...])

    @pl.when(f == pl.num_programs(1) - 1)
    def _():
        out_ref[...] = acc_ref[...]


def _moe_ffn_call(tile_e, tile_ok, row_tok, h, w1, w3, w2, layer):
    n, d = h.shape
    n_tiles = tile_e.shape[0]
    ff = w1.shape[3]
    tm, tf = MOE_TM, MOE_TF
    grid_spec = pltpu.PrefetchScalarGridSpec(
        num_scalar_prefetch=3,
        grid=(n_tiles, ff // tf),
        in_specs=[pl.BlockSpec(memory_space=pl.ANY),
                  pl.BlockSpec((None, None, d, tf), lambda i, f, te, tv, rt: (layer, te[i], 0, f)),
                  pl.BlockSpec((None, None, d, tf), lambda i, f, te, tv, rt: (layer, te[i], 0, f)),
                  pl.BlockSpec((None, None, tf, d), lambda i, f, te, tv, rt: (layer, te[i], f, 0))],
        out_specs=pl.BlockSpec((tm, d), lambda i, f, te, tv, rt: (i, 0)),
        scratch_shapes=[pltpu.VMEM((2, tm, d), F32), pltpu.VMEM((tm, d), BF16), pltpu.VMEM((tm, d), F32),
                        pltpu.SemaphoreType.DMA((2,))],
    )
    return pl.pallas_call(
        _moe_ffn_kernel,
        grid_spec=grid_spec,
        out_shape=jax.ShapeDtypeStruct((n_tiles * tm, d), F32),
        compiler_params=_cparams(("arbitrary", "arbitrary")),
        name="moe_ffn",
    )(tile_e, tile_ok, row_tok, h, w1, w3, w2)


def _combine_kernel(d0_ref, d1_ref, y_hbm, x_ref, gate_ref, out_ref, b0_ref, b1_ref, sem):
    i = pl.program_id(0)
    n_tiles = pl.num_programs(0)
    tm = COMBINE_TM
    slot = i % 2

    def start(tile, s):
        _start_row_gather(d0_ref, tile * tm, y_hbm, b0_ref.at[s], sem.at[0, s], tm)
        _start_row_gather(d1_ref, tile * tm, y_hbm, b1_ref.at[s], sem.at[1, s], tm)

    @pl.when(i == 0)
    def _():
        start(0, 0)

    _wait_row_gather(y_hbm, b0_ref.at[slot], sem.at[0, slot], tm)
    _wait_row_gather(y_hbm, b1_ref.at[slot], sem.at[1, slot], tm)

    @pl.when(i + 1 < n_tiles)
    def _():
        start(i + 1, 1 - slot)

    gates = gate_ref[...]
    out_ref[...] = x_ref[...] + (gates[:, 0:1] * b0_ref[slot] + gates[:, 1:2] * b1_ref[slot])


def _combine_call(d0, d1, y, x, gates):
    n, d = x.shape
    tm = COMBINE_TM
    grid_spec = pltpu.PrefetchScalarGridSpec(
        num_scalar_prefetch=2,
        grid=(n // tm,),
        in_specs=[pl.BlockSpec(memory_space=pl.ANY),
                  pl.BlockSpec((tm, d), lambda i, a, b: (i, 0)),
                  pl.BlockSpec((tm, LANES), lambda i, a, b: (i, 0))],
        out_specs=pl.BlockSpec((tm, d), lambda i, a, b: (i, 0)),
        scratch_shapes=[pltpu.VMEM((2, tm, d), F32), pltpu.VMEM((2, tm, d), F32), pltpu.SemaphoreType.DMA((2, 2))],
    )
    return pl.pallas_call(
        _combine_kernel,
        grid_spec=grid_spec,
        out_shape=jax.ShapeDtypeStruct((n, d), F32),
        compiler_params=_cparams(("arbitrary",)),
        name="moe_combine",
    )(d0, d1, y, x, gates)


def _moe_plan(top_e, n_tiles):
    n_assign = top_e.shape[0] * TOP_K
    flat_e = top_e.reshape(n_assign)
    onehot = (flat_e[:, None] == jnp.arange(N_EXPERTS, dtype=jnp.int32)[None, :]).astype(jnp.int32)
    csum = jnp.cumsum(onehot, axis=0)
    rank = jnp.take_along_axis(csum, flat_e[:, None], axis=1)[:, 0] - 1
    counts = csum[-1]
    padded = (counts + MOE_TM - 1) // MOE_TM * MOE_TM
    pend = jnp.cumsum(padded)
    dest = (pend - padded)[flat_e] + rank
    row_tok = jnp.zeros((n_tiles * MOE_TM,), jnp.int32).at[dest].set(jnp.arange(n_assign, dtype=jnp.int32) // TOP_K)
    tile_start = jnp.arange(n_tiles, dtype=jnp.int32) * MOE_TM
    tile_e = jnp.minimum(jnp.searchsorted(pend, tile_start, side="right"), N_EXPERTS - 1).astype(jnp.int32)
    tile_ok = (tile_start < pend[-1]).astype(jnp.int32)
    dest = dest.reshape(-1, TOP_K).astype(jnp.int32)
    return tile_e, tile_ok, row_tok, dest[:, 0], dest[:, 1]


def _moe_layer(x, g, wr_pad, w1, w3, w2, layer):
    n = x.shape[0]
    h, top_e, gates = _router_call(x, g, wr_pad)
    n_tiles = -(-(n * TOP_K + N_EXPERTS * (MOE_TM - 1)) // MOE_TM)
    tile_e, tile_ok, row_tok, d0, d1 = _moe_plan(top_e[:, :TOP_K], n_tiles)
    y = _moe_ffn_call(tile_e, tile_ok, row_tok, h, w1, w3, w2, layer)
    return _combine_call(d0, d1, y, x, gates)


def _norm_kernel(x_ref, g_ref, o_ref):
    o_ref[...] = _rms(x_ref[...], g_ref[...])


def _norm_call(x, g):
    n, d = x.shape
    tm = ROW_TILE
    return pl.pallas_call(
        _norm_kernel,
        grid=(n // tm,),
        in_specs=[pl.BlockSpec((tm, d), lambda i: (i, 0)), pl.BlockSpec((1, d), lambda i: (0, 0))],
        out_specs=pl.BlockSpec((tm, d), lambda i: (i, 0)),
        out_shape=jax.ShapeDtypeStruct((n, d), F32),
        compiler_params=_cparams(("parallel",)),
        name="final_norm",
    )(x, g)


def _rope_tables(seq):
    inv = ROPE_BASE ** (-jnp.arange(0, QK_ROPE, 2, dtype=F32) / QK_ROPE)
    ang = jnp.arange(seq)[:, None].astype(F32) * inv[None, :]
    cos, sin = jnp.cos(ang), jnp.sin(ang)
    pad = LANES - QK_NOPE - QK_ROPE
    ctab = jnp.concatenate([jnp.ones((seq, QK_NOPE), F32), cos, cos, jnp.zeros((seq, pad), F32)], axis=1)
    stab = jnp.concatenate([jnp.zeros((seq, QK_NOPE), F32), sin, sin, jnp.zeros((seq, pad), F32)], axis=1)
    return ctab, stab


def _split_w_in(w_in):
    d = w_in.shape[0]
    bounds = np.cumsum([WIDTH_A, WIDTH_A, WIDTH_A, Q_LORA, KV_LORA, QK_ROPE, WIDTH_C, WIDTH_C])
    qa, ka, va, cq, ckv, kr, qc, kc, vc = jnp.split(w_in, bounds.tolist(), axis=1)
    half = QK_ROPE // 2
    z_lo = jnp.zeros((d, QK_NOPE), w_in.dtype)
    z_hi = jnp.zeros((d, LANES - QK_NOPE - QK_ROPE), w_in.dtype)
    rope_blk = jnp.concatenate([z_lo, kr, z_hi], axis=1)
    swap_blk = jnp.concatenate([z_lo, -kr[:, half:], kr[:, :half], z_hi], axis=1)
    score_scale = HEAD_DIM ** -0.5
    wa = jnp.concatenate([qa * score_scale, ka, va], axis=1).astype(BF16)
    wc = jnp.concatenate([qc * score_scale, kc, vc], axis=1).astype(BF16)
    wb = jnp.concatenate([cq, ckv, rope_blk, swap_blk], axis=1).astype(BF16)
    return wa, wc, wb


def _split_w_uq(w_uq):
    r = w_uq.shape[0]
    w = w_uq.reshape(r, N_HEADS_B, QK_NOPE + QK_ROPE)
    nope, rope = w[..., :QK_NOPE], w[..., QK_NOPE:]
    half = QK_ROPE // 2
    z_hi = jnp.zeros((r, N_HEADS_B, LANES - QK_NOPE - QK_ROPE), w_uq.dtype)
    w1 = jnp.concatenate([nope, rope, z_hi], axis=-1)
    w2 = jnp.concatenate([jnp.zeros_like(nope), -rope[..., half:], rope[..., :half], z_hi], axis=-1)
    return w1.reshape(r, -1).astype(BF16), w2.reshape(r, -1).astype(BF16)


def _split_w_ukv(w_ukv):
    r = w_ukv.shape[0]
    w = w_ukv.reshape(r, N_HEADS_B, QK_NOPE + V_HEAD)
    k_nope, v = w[..., :QK_NOPE], w[..., QK_NOPE:]
    wk = jnp.concatenate([k_nope, jnp.zeros((r, N_HEADS_B, LANES - QK_NOPE), w_ukv.dtype)], axis=-1)
    return wk.reshape(r, -1).astype(BF16), v.reshape(r, -1).T.astype(BF16)


def kernel(x, g_mix, w_in, g_q, g_kv, w_uq, w_ukv, rpb, g_out_a, g_out_b, g_out_c, w_o, g_ffn, w1, w3, w2,
           w_router, e_w1, e_w3, e_w2, g_final):
    batch, seq, d = x.shape
    n = batch * seq
    depth = g_mix.shape[0]
    rows = seq // GRID_W
    ctab, stab = _rope_tables(seq)
    ew1, ew3, ew2 = e_w1.astype(BF16), e_w3.astype(BF16), e_w2.astype(BF16)
    xf = x.reshape(n, d)
    for layer in range(depth):
        wa, wc, wb = _split_w_in(w_in[layer])
        wq1, wq2 = _split_w_uq(w_uq[layer])
        wk, wvt = _split_w_ukv(w_ukv[layer])
        pa, pa4, pa16, pc, qm, km, vt = _proj_call(xf, g_mix[layer][None], wa, wc, wb, g_q[layer][None], g_kv[layer][None],
                                        wq1, wq2, wk, wvt, ctab, stab, seq)
        o_parts, lse_parts = [], []
        for dil, view in zip(DILATIONS, (pa, pa4, pa16)):
            o, lse = _band_call(view.reshape(batch, seq // dil, dil * 3 * WIDTH_A), dil)
            o_parts.append(o.reshape(n // dil, dil * WIDTH_A))
            lse_parts.append(lse.reshape(n // dil, dil * WIDTH_A))
        ob = _mla_call(qm, km, vt, batch, seq).reshape(n, WIDTH_B)
        oc = _na_call(pc, _na_tables(rpb[layer], rows), batch, seq).reshape(n, WIDTH_C)
        xf = _mix_out_call(xf, o_parts, lse_parts, ob, oc, g_out_a[layer][None], g_out_b[layer][None],
                           g_out_c[layer][None], w_o[layer].astype(BF16))
        j = layer // 2
        if layer % 2 == 0:
            xf = _ffn_call(xf, g_ffn[layer][None], w1[j].astype(BF16), w3[j].astype(BF16), w2[j].astype(BF16))
        else:
            wr_pad = jnp.pad(w_router[j], ((0, 0), (0, LANES - N_EXPERTS)))
            xf = _moe_layer(xf, g_ffn[layer][None], wr_pad, ew1, ew3, ew2, j)
    return _norm_call(xf, g_final[None]).reshape(batch, seq, d)
```

```python
import functools
import math

import numpy as np
import jax
import jax.numpy as jnp
from jax import lax
from jax.experimental import pallas as pl
from jax.experimental.pallas import tpu as pltpu

F32 = jnp.float32
BF16 = jnp.bfloat16

LANES = 128
V7X_VMEM_LIMIT_BYTES = 52 * 1024 * 1024

HEAD_DIM = 64
N_HEADS_A = 6
DILATIONS = (1, 4, 16)
BAND_HALF = 64
N_HEADS_B = 6
Q_LORA = 384
KV_LORA = 256
QK_NOPE = 64
QK_ROPE = 32
V_HEAD = 64
ROPE_BASE = 10000.0
N_HEADS_C = 4
GRID_W = 64
NA_ROWS = 8
NA_COLS = 16
WIDTH_A = N_HEADS_A * HEAD_DIM
WIDTH_B = N_HEADS_B * V_HEAD
WIDTH_C = N_HEADS_C * HEAD_DIM
N_EXPERTS = 8
TOP_K = 2
RMS_EPS = 1e-6
NEG_INF = -1e30

ROW_TILE = 512
MLA_TQ = 256
MLA_TK = 256
MLA_UNROLL = 8
MLA_LOOKAHEAD = 4
MLA_HEADS_PER_STEP = 6
MLA_DEN_ROWS = 16
MLA_Q_PRESCALE = (QK_NOPE + QK_ROPE) ** -0.5 * math.log2(math.e)
BAND_TQ = 128
BAND_TILES_PER_STEP = 4
BAND_LOOKAHEAD = 6
NA_TILE_ROWS = 2
NA_KEY_ROWS = 10
NA_TILES_PER_STEP = 4
NA_LOOKAHEAD = 16
MOE_TM = 512
MOE_TF = 1792
COMBINE_TM = 256
GATHER_UNROLL = 8


def _cparams(semantics):
    return pltpu.CompilerParams(dimension_semantics=semantics, vmem_limit_bytes=V7X_VMEM_LIMIT_BYTES)


def _rms(x, g):
    return x * lax.rsqrt(jnp.mean(x * x, axis=-1, keepdims=True) + RMS_EPS) * g


def _dot(a, b):
    return jnp.dot(a, b, preferred_element_type=F32)


def _dot_nt(a, b):
    return lax.dot_general(a, b, (((1,), (1,)), ((), ())), preferred_element_type=F32)


def _proj_kernel(x_ref, g_ref, wa_ref, wc_ref, wb_ref, gq_ref, gkv_ref, wq1_ref, wq2_ref, wk_ref, wvt_ref,
                 ct_ref, st_ref, pa_ref, pa4_ref, pa16_ref, pc_ref, qm_ref, km_ref, vt_ref, pa_scr):
    tm = x_ref.shape[0]
    h = _rms(x_ref[...], g_ref[...]).astype(BF16)
    pa = _dot(h, wa_ref[...])
    pa_ref[...] = pa.astype(BF16)
    n_chunks = pa.shape[1] // LANES
    for c in range(n_chunks):
        pa_scr[c] = pa[:, c * LANES:(c + 1) * LANES]
    for dil, view_ref in ((DILATIONS[1], pa4_ref), (DILATIONS[2], pa16_ref)):
        for r in range(dil):
            for c in range(n_chunks):
                col = r * 3 * WIDTH_A + c * LANES
                view_ref[:, col:col + LANES] = pa_scr[c, pl.ds(r, tm // dil, stride=dil), :].astype(BF16)
    pc_ref[...] = _dot(h, wc_ref[...]).astype(BF16)
    pb = _dot(h, wb_ref[...])
    hq = _rms(pb[:, :Q_LORA], gq_ref[...]).astype(BF16)
    hkv = _rms(pb[:, Q_LORA:Q_LORA + KV_LORA], gkv_ref[...]).astype(BF16)
    r1 = pb[:, Q_LORA + KV_LORA:Q_LORA + KV_LORA + LANES]
    r2 = pb[:, Q_LORA + KV_LORA + LANES:]
    ct = ct_ref[...]
    st = st_ref[...]
    qa = _dot(hq, wq1_ref[...])
    qb = _dot(hq, wq2_ref[...])
    kn = _dot(hkv, wk_ref[...])
    kr = r1 * ct + r2 * st
    for hd in range(N_HEADS_B):
        sl = slice(hd * LANES, (hd + 1) * LANES)
        qm_ref[:, sl] = ((qa[:, sl] * ct + qb[:, sl] * st) * MLA_Q_PRESCALE).astype(BF16)
        km_ref[:, sl] = (kn[:, sl] + kr).astype(BF16)
    vt_ref[...] = _dot_nt(wvt_ref[...], hkv).astype(BF16)


def _proj_call(x, g, wa, wc, wb, gq, gkv, wq1, wq2, wk, wvt, ctab, stab, seq):
    n, d = x.shape
    tm = ROW_TILE
    tiles_per_seq = seq // tm
    full = lambda a: pl.BlockSpec(a.shape, lambda i: (0,) * a.ndim)
    row = lambda w: pl.BlockSpec((tm, w), lambda i: (i, 0))
    tab = pl.BlockSpec((tm, LANES), lambda i: (i % tiles_per_seq, 0))
    view = lambda dil: pl.BlockSpec((tm // dil, dil * 3 * WIDTH_A), lambda i: (i, 0))
    hb = N_HEADS_B * LANES
    return pl.pallas_call(
        _proj_kernel,
        grid=(n // tm,),
        in_specs=[row(d), full(g), full(wa), full(wc), full(wb), full(gq), full(gkv), full(wq1), full(wq2),
                  full(wk), full(wvt), tab, tab],
        out_specs=[row(3 * WIDTH_A)] + [view(dil) for dil in DILATIONS[1:]] + [row(3 * WIDTH_C), row(hb), row(hb),
                   pl.BlockSpec((WIDTH_B, tm), lambda i: (0, i))],
        out_shape=[jax.ShapeDtypeStruct((n, 3 * WIDTH_A), BF16)]
                  + [jax.ShapeDtypeStruct((n // dil, dil * 3 * WIDTH_A), BF16) for dil in DILATIONS[1:]]
                  + [jax.ShapeDtypeStruct((n, 3 * WIDTH_C), BF16),
                   jax.ShapeDtypeStruct((n, hb), BF16), jax.ShapeDtypeStruct((n, hb), BF16),
                   jax.ShapeDtypeStruct((WIDTH_B, n), BF16)],
        scratch_shapes=[pltpu.VMEM((3 * WIDTH_A // LANES, tm, LANES), F32)],
        compiler_params=_cparams(("parallel",)),
        name="proj",
    )(x, g, wa, wc, wb, gq, gkv, wq1, wq2, wk, wvt, ctab, stab)


def _run_pipeline(n_items, look, issue, consume, finish):
    if look >= n_items:
        results = [consume(i, s) for i, s in enumerate([issue(i) for i in range(n_items)])]
        for i, res in enumerate(results):
            finish(i, res)
        return
    inflight = {i: issue(i) for i in range(min(look, n_items))}
    pending = None
    for i in range(n_items):
        if i + look < n_items:
            inflight[i + look] = issue(i + look)
        if pending is not None:
            finish(*pending)
        pending = (i, consume(i, inflight.pop(i)))
    finish(*pending)


class _HeadPairs:
    def __init__(self, q_ref, k_ref, v_ref, tq, tkw):
        self.q_ref, self.k_ref, self.v_ref, self.tq, self.tkw = q_ref, k_ref, v_ref, tq, tkw
        lane = lax.broadcasted_iota(jnp.int32, (tq, LANES), 1)
        self.low = lane < HEAD_DIM
        self.ones = jnp.ones((tkw, LANES), BF16)

    @staticmethod
    def cols(h):
        return slice((h // 2) * LANES, (h // 2 + 1) * LANES)

    def scores(self, q_start, k_start, h):
        q = self.q_ref[q_start:q_start + self.tq, self.cols(h)]
        q = jnp.where(self.low if h % 2 == 0 else ~self.low, q, jnp.zeros_like(q))
        return _dot_nt(q, self.k_ref[pl.ds(k_start, self.tkw), self.cols(h)])

    def values(self, k_start, h, p):
        v = jnp.concatenate([self.v_ref[pl.ds(k_start, self.tkw), self.cols(h)], self.ones], axis=-1)
        o = _dot(p, v)
        return o[:, :LANES], o[:, LANES:]

    def merge(self, even, odd):
        return jnp.where(self.low, even, odd)


def _band_tables(dilation):
    tq, tkw = BAND_TQ, BAND_TQ + 2 * BAND_HALF
    shift = np.array([0, -BAND_HALF, -2 * BAND_HALF])
    rel = shift[:, None, None] + np.arange(tkw)[None, None, :] - np.arange(tq)[None, :, None]
    dist = np.abs(rel)
    slopes = 2.0 ** (-8.0 * np.arange(1, N_HEADS_A + 1) / N_HEADS_A)
    bias = -(slopes[None, :, None, None] * dilation) * dist[:, None].astype(np.float64)
    tab = np.where(dist[:, None] <= BAND_HALF, bias, NEG_INF)
    return jnp.asarray(tab, F32)


def _band_kernel(q_ref, k_ref, v_ref, tab_ref, o_ref, lse_ref, *, n):
    tq = BAND_TQ
    tkw = tq + 2 * BAND_HALF
    tiles = q_ref.shape[0] // tq
    last_tile = n // tq - 1
    items = [(t, h) for t in range(tiles) for h in range(N_HEADS_A)]
    info = []
    for t in range(tiles):
        i = pl.program_id(2) * tiles + t
        start = pl.multiple_of(jnp.clip(i * tq - BAND_HALF, 0, n - tkw), BAND_HALF)
        variant = jnp.minimum(i, 1) + (i == last_tile).astype(jnp.int32)
        info.append((start, variant))
    pairs = _HeadPairs(q_ref, k_ref, v_ref, tq, tkw)
    held = {}

    def issue(idx):
        t, h = items[idx]
        return pairs.scores(t * tq, info[t][0], h)

    def consume(idx, s):
        t, h = items[idx]
        s = s + tab_ref[info[t][1], h]
        m = jnp.max(s, axis=-1, keepdims=True)
        return jnp.exp(s - m).astype(BF16), m

    def finish(idx, res):
        t, h = items[idx]
        p, m = res
        out, den = pairs.values(info[t][0], h, p)
        held[h % 2] = (out / den, m + jnp.log(den))
        if h % 2 == 1:
            cols = pairs.cols(h)
            o_ref[t * tq:(t + 1) * tq, cols] = pairs.merge(held[0][0], held[1][0])
            lse_ref[t * tq:(t + 1) * tq, cols] = pairs.merge(held[0][1], held[1][1])

    _run_pipeline(len(items), BAND_LOOKAHEAD, issue, consume, finish)


def _band_call(pa_view, dilation):
    b, n, _ = pa_view.shape
    rows = BAND_TQ * BAND_TILES_PER_STEP
    w = WIDTH_A
    tab = _band_tables(dilation)
    qspec = pl.BlockSpec((None, rows, w), lambda bb, r, i: (bb, i, 3 * r))
    kspec = pl.BlockSpec((None, n, w), lambda bb, r, i: (bb, 0, 3 * r + 1))
    vspec = pl.BlockSpec((None, n, w), lambda bb, r, i: (bb, 0, 3 * r + 2))
    tspec = pl.BlockSpec(tab.shape, lambda bb, r, i: (0, 0, 0, 0))
    ospec = pl.BlockSpec((None, rows, w), lambda bb, r, i: (bb, i, r))
    shape = jax.ShapeDtypeStruct((b, n, dilation * w), F32)
    return pl.pallas_call(
        functools.partial(_band_kernel, n=n),
        grid=(b, dilation, n // rows),
        in_specs=[qspec, kspec, vspec, tspec],
        out_specs=[ospec, ospec],
        out_shape=[shape, shape],
        compiler_params=_cparams(("parallel", "parallel", "parallel")),
        name=f"band_d{dilation}",
    )(pa_view, pa_view, pa_view, tab)


def _mla_kernel(q_ref, k_ref, vt_ref, o_ref):
    tq = q_ref.shape[0]
    seq = k_ref.shape[0]
    tk = MLA_TK
    nh = MLA_HEADS_PER_STEP
    n_chunks = seq // tk
    items = [(u, h) for u in range(MLA_UNROLL) for h in range(nh)]
    look = MLA_LOOKAHEAD
    ones_rows = jnp.ones((MLA_DEN_ROWS, tk), BF16)

    def key_slice(chunk):
        return pl.ds(pl.multiple_of(chunk * tk, tk), tk)

    def score_matmul(chunk, h):
        k = k_ref[key_slice(chunk), h * LANES:(h + 1) * LANES]
        return _dot_nt(k, q_ref[:, h * LANES:(h + 1) * LANES])

    def value_matmul(chunk, h, p):
        vt = jnp.concatenate([vt_ref[h * V_HEAD:(h + 1) * V_HEAD, key_slice(chunk)], ones_rows], axis=0)
        return _dot(vt, p)

    def body(j, carry):
        state = list(carry[:2 * nh])
        scores = dict(zip(items[:look], carry[2 * nh:2 * nh + look]))
        pend_p, pend_alpha = carry[2 * nh + look:]
        pending = (jnp.maximum(j * MLA_UNROLL - 1, 0), nh - 1, pend_p, pend_alpha)
        ahead = []
        for idx, (u, h) in enumerate(items):
            la = idx + look
            if la < len(items):
                lu, lh = items[la]
                scores[lu, lh] = score_matmul(j * MLA_UNROLL + lu, lh)
            else:
                lu, lh = items[la - len(items)]
                ahead.append(score_matmul(jnp.minimum((j + 1) * MLA_UNROLL + lu, n_chunks - 1), lh))
            pc, ph, pp, pa = pending
            state[2 * ph + 1] = pa * state[2 * ph + 1] + value_matmul(pc, ph, pp)
            s = scores.pop((u, h))
            m_new = jnp.maximum(state[2 * h], jnp.max(s, axis=0, keepdims=True))
            alpha = jnp.exp2(state[2 * h] - m_new)
            state[2 * h] = m_new
            pending = (j * MLA_UNROLL + u, h, jnp.exp2(s - m_new).astype(BF16), alpha)
        return tuple(state) + tuple(ahead) + (pending[2], pending[3])

    init = (jnp.full((1, tq), NEG_INF, F32), jnp.zeros((V_HEAD + MLA_DEN_ROWS, tq), F32)) * nh
    init += tuple(score_matmul(u, h) for u, h in items[:look])
    init += (jnp.zeros((tk, tq), BF16), jnp.ones((1, tq), F32))
    res = lax.fori_loop(0, n_chunks // MLA_UNROLL, body, init)
    accs = [res[2 * h + 1] for h in range(nh)]
    accs[nh - 1] = res[-1] * accs[nh - 1] + value_matmul(n_chunks - 1, nh - 1, res[-2])
    out_t = jnp.concatenate([a[:V_HEAD] / a[V_HEAD:V_HEAD + 1] for a in accs], axis=0)
    o_ref[...] = out_t.T


def _mla_call(qm, km, vt, batch, seq):
    tq = MLA_TQ
    nh = MLA_HEADS_PER_STEP
    qspec = pl.BlockSpec((None, tq, nh * LANES), lambda b, g, i: (b, i, g))
    kspec = pl.BlockSpec((None, seq, nh * LANES), lambda b, g, i: (b, 0, g))
    vspec = pl.BlockSpec((nh * V_HEAD, seq), lambda b, g, i: (g, b))
    ospec = pl.BlockSpec((None, tq, nh * V_HEAD), lambda b, g, i: (b, i, g))
    return pl.pallas_call(
        _mla_kernel,
        grid=(batch, N_HEADS_B // nh, seq // tq),
        in_specs=[qspec, kspec, vspec],
        out_specs=ospec,
        out_shape=jax.ShapeDtypeStruct((batch, seq, WIDTH_B), F32),
        compiler_params=_cparams(("parallel", "parallel", "parallel")),
        name="mla",
    )(qm.reshape(batch, seq, -1), km.reshape(batch, seq, -1), vt)


def _na_variant(i, n_tiles):
    return jnp.minimum(i, 2) + jnp.maximum(i - (n_tiles - 3), 0)


def _na_kernel(q_ref, k_ref, v_ref, tab_ref, o_ref, *, rows):
    tq = NA_TILE_ROWS * GRID_W
    tkw = NA_KEY_ROWS * GRID_W
    tiles = q_ref.shape[0] // tq
    n_tiles = rows // NA_TILE_ROWS
    items = [(t, h) for t in range(tiles) for h in range(N_HEADS_C)]
    info = []
    for t in range(tiles):
        i = pl.program_id(1) * tiles + t
        base = jnp.clip(i * NA_TILE_ROWS - NA_ROWS // 2, 0, rows - NA_KEY_ROWS)
        info.append((pl.multiple_of(base * GRID_W, GRID_W), _na_variant(i, n_tiles)))
    pairs = _HeadPairs(q_ref, k_ref, v_ref, tq, tkw)
    held = {}

    def issue(idx):
        t, h = items[idx]
        return pairs.scores(t * tq, info[t][0], h)

    def consume(idx, s):
        t, h = items[idx]
        s = s + tab_ref[info[t][1], h]
        return jnp.exp(s - jnp.max(s, axis=-1, keepdims=True)).astype(BF16)

    def finish(idx, p):
        t, h = items[idx]
        out, den = pairs.values(info[t][0], h, p)
        held[h % 2] = out / den
        if h % 2 == 1:
            o_ref[t * tq:(t + 1) * tq, pairs.cols(h)] = pairs.merge(held[0], held[1])

    _run_pipeline(len(items), NA_LOOKAHEAD, issue, consume, finish)


def _na_tables(rpb, rows):
    r0 = np.array([0, 2, 4, rows - 4, rows - 2])
    base = np.clip(r0 - NA_ROWS // 2, 0, rows - NA_KEY_ROWS)
    r = r0[:, None] + np.arange(NA_TILE_ROWS)[None, :]
    row_start = np.clip(r - NA_ROWS // 2, 0, rows - NA_ROWS)
    krow = base[:, None] + np.arange(NA_KEY_ROWS)[None, :]
    drow = krow[:, None, :] - r[:, :, None]
    row_ok = (krow[:, None, :] >= row_start[:, :, None]) & (krow[:, None, :] < row_start[:, :, None] + NA_ROWS)
    c = np.arange(GRID_W)
    win_start = np.clip(c - NA_COLS // 2, 0, GRID_W - NA_COLS)
    col_ok = (c[None, :] >= win_start[:, None]) & (c[None, :] < win_start[:, None] + NA_COLS)
    dcol = np.clip(c[None, :] - c[:, None], -(NA_COLS - 1), NA_COLS - 1)
    ok = row_ok[:, :, None, :, None] & col_ok[None, None, :, None, :]
    di = np.clip(drow, -(NA_ROWS - 1), NA_ROWS - 1) + (NA_ROWS - 1)
    pick_col = (dcol[:, :, None] + NA_COLS - 1 == np.arange(2 * NA_COLS - 1)).astype(np.float32)
    pick_row = (di[..., None] == np.arange(2 * NA_ROWS - 1)).astype(np.float32)
    hi = lax.Precision.HIGHEST
    toeplitz = jnp.einsum("hab,cjb->hacj", rpb.astype(F32), pick_col, precision=hi)
    bias = jnp.einsum("vqka,hacj->vhqckj", pick_row, toeplitz, precision=hi)
    tab = jnp.where(jnp.asarray(ok)[:, None], bias, NEG_INF)
    return tab.reshape(5, N_HEADS_C, NA_TILE_ROWS * GRID_W, NA_KEY_ROWS * GRID_W)


def _na_call(pc, tab, batch, seq):
    rows = seq // GRID_W
    tq = NA_TILE_ROWS * GRID_W * NA_TILES_PER_STEP
    w = WIDTH_C
    qspec = pl.BlockSpec((None, tq, w), lambda b, i: (b, i, 0))
    kspec = pl.BlockSpec((None, seq, w), lambda b, i: (b, 0, 1))
    vspec = pl.BlockSpec((None, seq, w), lambda b, i: (b, 0, 2))
    tspec = pl.BlockSpec(tab.shape, lambda b, i: (0, 0, 0, 0))
    ospec = pl.BlockSpec((None, tq, w), lambda b, i: (b, i, 0))
    pc3 = pc.reshape(batch, seq, 3 * w)
    return pl.pallas_call(
        functools.partial(_na_kernel, rows=rows),
        grid=(batch, seq // tq),
        in_specs=[qspec, kspec, vspec, tspec],
        out_specs=ospec,
        out_shape=jax.ShapeDtypeStruct((batch, seq, w), F32),
        compiler_params=_cparams(("parallel", "parallel")),
        name="natten",
    )(pc3, pc3, pc3, tab)


def _mix_out_kernel(x_ref, o1_ref, o2_ref, o3_ref, l1_ref, l2_ref, l3_ref, ob_ref, oc_ref,
                    ga_ref, gb_ref, gc_ref, wo_ref, out_ref, o2_scr, o3_scr, l2_scr, l3_scr):
    tm = x_ref.shape[0]
    n_chunks = WIDTH_A // LANES

    def natural(view_ref, scr, dil):
        for r in range(dil):
            for c in range(n_chunks):
                col = r * WIDTH_A + c * LANES
                scr[c, pl.ds(r, tm // dil, stride=dil), :] = view_ref[:, col:col + LANES]
        return jnp.concatenate([scr[c] for c in range(n_chunks)], axis=-1)

    l1, o1 = l1_ref[...], o1_ref[...]
    l2, o2 = natural(l2_ref, l2_scr, DILATIONS[1]), natural(o2_ref, o2_scr, DILATIONS[1])
    l3, o3 = natural(l3_ref, l3_scr, DILATIONS[2]), natural(o3_ref, o3_scr, DILATIONS[2])
    mx = jnp.maximum(jnp.maximum(l1, l2), l3)
    w1, w2, w3 = jnp.exp(l1 - mx), jnp.exp(l2 - mx), jnp.exp(l3 - mx)
    oa = (w1 * o1 + w2 * o2 + w3 * o3) / (w1 + w2 + w3)
    ya = _rms(oa, ga_ref[...]).astype(BF16)
    yb = _rms(ob_ref[...], gb_ref[...]).astype(BF16)
    yc = _rms(oc_ref[...], gc_ref[...]).astype(BF16)
    y = _dot(ya, wo_ref[:WIDTH_A, :])
    y = y + _dot(yb, wo_ref[WIDTH_A:WIDTH_A + WIDTH_B, :])
    y = y + _dot(yc, wo_ref[WIDTH_A + WIDTH_B:, :])
    out_ref[...] = x_ref[...] + y


def _mix_out_call(x, o_parts, lse_parts, ob, oc, ga, gb, gc, wo):
    n, d = x.shape
    tm = ROW_TILE
    full = lambda a: pl.BlockSpec(a.shape, lambda i: (0,) * a.ndim)
    row = lambda w: pl.BlockSpec((tm, w), lambda i: (i, 0))
    view = lambda dil: pl.BlockSpec((tm // dil, dil * WIDTH_A), lambda i: (i, 0))
    return pl.pallas_call(
        _mix_out_kernel,
        grid=(n // tm,),
        in_specs=[row(d)] + [view(dil) for dil in DILATIONS] * 2
                 + [row(WIDTH_B), row(WIDTH_C), full(ga), full(gb), full(gc), full(wo)],
        out_specs=row(d),
        out_shape=jax.ShapeDtypeStruct((n, d), F32),
        scratch_shapes=[pltpu.VMEM((WIDTH_A // LANES, tm, LANES), F32)] * 4,
        compiler_params=_cparams(("parallel",)),
        name="mix_out",
    )(x, *o_parts, *lse_parts, ob, oc, ga, gb, gc, wo)


def _silu(u):
    return u * (1.0 / (1.0 + jnp.exp(-u)))


def _ffn_kernel(x_ref, g_ref, w1_ref, w3_ref, w2_ref, out_ref):
    x = x_ref[...]
    h = _rms(x, g_ref[...]).astype(BF16)
    a = (_silu(_dot(h, w1_ref[...])) * _dot(h, w3_ref[...])).astype(BF16)
    out_ref[...] = x + _dot(a, w2_ref[...])


def _ffn_call(x, g, w1, w3, w2):
    n, d = x.shape
    tm = ROW_TILE
    resident = lambda a: pl.BlockSpec(a.shape, lambda i: (0, 0), pipeline_mode=pl.Buffered(1))
    return pl.pallas_call(
        _ffn_kernel,
        grid=(n // tm,),
        in_specs=[pl.BlockSpec((tm, d), lambda i: (i, 0)), pl.BlockSpec((1, d), lambda i: (0, 0)),
                  resident(w1), resident(w3), resident(w2)],
        out_specs=pl.BlockSpec((tm, d), lambda i: (i, 0)),
        out_shape=jax.ShapeDtypeStruct((n, d), F32),
        compiler_params=_cparams(("parallel",)),
        name="ffn",
    )(x, g, w1, w3, w2)


def _router_kernel(x_ref, g_ref, wr_ref, h_ref, e_ref, gate_ref):
    h = _rms(x_ref[...], g_ref[...])
    _to_token_tiles(h_ref, h)
    logits = jnp.dot(h, wr_ref[...], precision=lax.Precision.HIGHEST, preferred_element_type=F32)
    lane = lax.broadcasted_iota(jnp.int32, logits.shape, 1).astype(F32)
    logits = jnp.where(lane < N_EXPERTS, logits, -jnp.inf)
    m1 = jnp.max(logits, axis=-1, keepdims=True)
    i1 = jnp.min(jnp.where(logits == m1, lane, float(LANES)), axis=-1, keepdims=True)
    rest = jnp.where(lane == i1, -jnp.inf, logits)
    m2 = jnp.max(rest, axis=-1, keepdims=True)
    i2 = jnp.min(jnp.where(rest == m2, lane, float(LANES)), axis=-1, keepdims=True)
    e = jnp.exp(m2 - m1)
    den = 1.0 + e
    e_ref[...] = jnp.where(lane == 0.0, i1, jnp.where(lane == 1.0, i2, 0.0)).astype(jnp.int32)
    gate_ref[...] = jnp.where(lane == 0.0, 1.0 / den, jnp.where(lane == 1.0, e / den, 0.0))


def _router_call(x, g, wr_pad):
    n, d = x.shape
    tm = ROW_TILE
    row = lambda w: pl.BlockSpec((tm, w), lambda i: (i, 0))
    full = lambda a: pl.BlockSpec(a.shape, lambda i: (0,) * a.ndim)
    return pl.pallas_call(
        _router_kernel,
        grid=(n // tm,),
        in_specs=[row(d), full(g), full(wr_pad)],
        out_specs=[pl.BlockSpec((tm * TOKEN_ROWS, LANES), lambda i: (i, 0)), row(LANES), row(LANES)],
        out_shape=[jax.ShapeDtypeStruct((n * TOKEN_ROWS, LANES), F32), jax.ShapeDtypeStruct((n, LANES), jnp.int32),
                   jax.ShapeDtypeStruct((n, LANES), F32)],
        compiler_params=_cparams(("parallel",)),
        name="router",
    )(x, g, wr_pad)


TOKEN_ROWS = 8


def _to_token_tiles(dst_ref, x):
    tm, d = x.shape
    assert d == TOKEN_ROWS * LANES
    for j in range(TOKEN_ROWS):
        dst_ref[pl.ds(j, tm, stride=TOKEN_ROWS), :] = x[:, j * LANES:(j + 1) * LANES]


def _from_token_tiles(src_ref, tm):
    return jnp.concatenate([src_ref[pl.ds(j, tm, stride=TOKEN_ROWS), :] for j in range(TOKEN_ROWS)], axis=-1)


def _row_copy(src_hbm, row, dst_ref, r, sem):
    src = src_hbm.at[pl.ds(pl.multiple_of(row * TOKEN_ROWS, TOKEN_ROWS), TOKEN_ROWS)]
    return pltpu.make_async_copy(src, dst_ref.at[pl.ds(pl.multiple_of(r * TOKEN_ROWS, TOKEN_ROWS), TOKEN_ROWS)], sem)


def _start_row_gather(idx_ref, base, src_hbm, dst_ref, sem, count):
    def body(r, c):
        _row_copy(src_hbm, idx_ref[base + r], dst_ref, r, sem).start()
        return c
    lax.fori_loop(0, count, body, 0, unroll=GATHER_UNROLL)


def _wait_row_gather(src_hbm, dst_ref, sem, count):
    pltpu.make_async_copy(src_hbm.at[pl.ds(0, count * TOKEN_ROWS)], dst_ref, sem).wait()


def _moe_ffn_kernel(tile_e_ref, tile_ok_ref, row_tok_ref, h_hbm, w1_ref, w3_ref, w2_ref, out_ref,
                    xg_ref, hb_ref, acc_ref, sem):
    i = pl.program_id(0)
    f = pl.program_id(1)
    n_tiles = pl.num_programs(0)
    tm = MOE_TM
    slot = i % 2

    @pl.when(f == 0)
    def _():
        @pl.when(i == 0)
        def _():
            _start_row_gather(row_tok_ref, 0, h_hbm, xg_ref.at[0], sem.at[0], tm)

        _wait_row_gather(h_hbm, xg_ref.at[slot], sem.at[slot], tm)

        @pl.when(i + 1 < n_tiles)
        def _():
            _start_row_gather(row_tok_ref, (i + 1) * tm, h_hbm, xg_ref.at[1 - slot], sem.at[1 - slot], tm)

        hb_ref[...] = _from_token_tiles(xg_ref.at[slot], tm).astype(BF16)
        acc_ref[...] = jnp.zeros_like(acc_ref)

    @pl.when(tile_ok_ref[i] != 0)
    def _():
        h = hb_ref[...]
        a = (_silu(_dot(h, w1_ref[...])) * _dot(h, w3_ref[...])).astype(BF16)
        acc_ref[...] += _dot(a, w2_ref[...])

    @pl.when(f == pl.num_programs(1) - 1)
    def _():
        _to_token_tiles(out_ref, acc_ref[...])


def _moe_ffn_call(tile_e, tile_ok, row_tok, h, w1, w3, w2, layer):
    d = w1.shape[2]
    n_tiles = tile_e.shape[0]
    ff = w1.shape[3]
    tm, tf = MOE_TM, MOE_TF
    grid_spec = pltpu.PrefetchScalarGridSpec(
        num_scalar_prefetch=3,
        grid=(n_tiles, ff // tf),
        in_specs=[pl.BlockSpec(memory_space=pl.ANY),
                  pl.BlockSpec((None, None, d, tf), lambda i, f, te, tv, rt: (layer, te[i], 0, f)),
                  pl.BlockSpec((None, None, d, tf), lambda i, f, te, tv, rt: (layer, te[i], 0, f)),
                  pl.BlockSpec((None, None, tf, d), lambda i, f, te, tv, rt: (layer, te[i], f, 0))],
        out_specs=pl.BlockSpec((tm * TOKEN_ROWS, LANES), lambda i, f, te, tv, rt: (i, 0)),
        scratch_shapes=[pltpu.VMEM((2, tm * TOKEN_ROWS, LANES), F32), pltpu.VMEM((tm, d), BF16),
                        pltpu.VMEM((tm, d), F32), pltpu.SemaphoreType.DMA((2,))],
    )
    return pl.pallas_call(
        _moe_ffn_kernel,
        grid_spec=grid_spec,
        out_shape=jax.ShapeDtypeStruct((n_tiles * tm * TOKEN_ROWS, LANES), F32),
        compiler_params=_cparams(("arbitrary", "arbitrary")),
        name="moe_ffn",
    )(tile_e, tile_ok, row_tok, h, w1, w3, w2)


def _combine_kernel(d0_ref, d1_ref, y_hbm, x_ref, gate_ref, out_ref, b0_ref, b1_ref, sem):
    i = pl.program_id(0)
    n_tiles = pl.num_programs(0)
    tm = COMBINE_TM
    slot = i % 2

    def start(tile, s):
        _start_row_gather(d0_ref, tile * tm, y_hbm, b0_ref.at[s], sem.at[0, s], tm)
        _start_row_gather(d1_ref, tile * tm, y_hbm, b1_ref.at[s], sem.at[1, s], tm)

    @pl.when(i == 0)
    def _():
        start(0, 0)

    _wait_row_gather(y_hbm, b0_ref.at[slot], sem.at[0, slot], tm)
    _wait_row_gather(y_hbm, b1_ref.at[slot], sem.at[1, slot], tm)

    @pl.when(i + 1 < n_tiles)
    def _():
        start(i + 1, 1 - slot)

    gates = gate_ref[...]
    y0 = _from_token_tiles(b0_ref.at[slot], tm)
    y1 = _from_token_tiles(b1_ref.at[slot], tm)
    out_ref[...] = x_ref[...] + (gates[:, 0:1] * y0 + gates[:, 1:2] * y1)


def _combine_call(d0, d1, y, x, gates):
    n, d = x.shape
    tm = COMBINE_TM
    grid_spec = pltpu.PrefetchScalarGridSpec(
        num_scalar_prefetch=2,
        grid=(n // tm,),
        in_specs=[pl.BlockSpec(memory_space=pl.ANY),
                  pl.BlockSpec((tm, d), lambda i, a, b: (i, 0)),
                  pl.BlockSpec((tm, LANES), lambda i, a, b: (i, 0))],
        out_specs=pl.BlockSpec((tm, d), lambda i, a, b: (i, 0)),
        scratch_shapes=[pltpu.VMEM((2, tm * TOKEN_ROWS, LANES), F32)] * 2 + [pltpu.SemaphoreType.DMA((2, 2))],
    )
    return pl.pallas_call(
        _combine_kernel,
        grid_spec=grid_spec,
        out_shape=jax.ShapeDtypeStruct((n, d), F32),
        compiler_params=_cparams(("arbitrary",)),
        name="moe_combine",
    )(d0, d1, y, x, gates)


def _moe_plan(top_e, n_tiles):
    n_assign = top_e.shape[0] * TOP_K
    flat_e = top_e.reshape(n_assign)
    onehot = (flat_e[:, None] == jnp.arange(N_EXPERTS, dtype=jnp.int32)[None, :]).astype(jnp.int32)
    csum = jnp.cumsum(onehot, axis=0)
    rank = jnp.take_along_axis(csum, flat_e[:, None], axis=1)[:, 0] - 1
    counts = csum[-1]
    padded = (counts + MOE_TM - 1) // MOE_TM * MOE_TM
    pend = jnp.cumsum(padded)
    dest = (pend - padded)[flat_e] + rank
    row_tok = jnp.zeros((n_tiles * MOE_TM,), jnp.int32).at[dest].set(jnp.arange(n_assign, dtype=jnp.int32) // TOP_K)
    tile_start = jnp.arange(n_tiles, dtype=jnp.int32) * MOE_TM
    tile_e = jnp.minimum(jnp.searchsorted(pend, tile_start, side="right"), N_EXPERTS - 1).astype(jnp.int32)
    tile_ok = (tile_start < pend[-1]).astype(jnp.int32)
    dest = dest.reshape(-1, TOP_K).astype(jnp.int32)
    return tile_e, tile_ok, row_tok, dest[:, 0], dest[:, 1]


def _moe_layer(x, g, wr_pad, w1, w3, w2, layer):
    n = x.shape[0]
    h, top_e, gates = _router_call(x, g, wr_pad)
    n_tiles = -(-(n * TOP_K + N_EXPERTS * (MOE_TM - 1)) // MOE_TM)
    tile_e, tile_ok, row_tok, d0, d1 = _moe_plan(top_e[:, :TOP_K], n_tiles)
    y = _moe_ffn_call(tile_e, tile_ok, row_tok, h, w1, w3, w2, layer)
    return _combine_call(d0, d1, y, x, gates)


def _norm_kernel(x_ref, g_ref, o_ref):
    o_ref[...] = _rms(x_ref[...], g_ref[...])


def _norm_call(x, g):
    n, d = x.shape
    tm = ROW_TILE
    return pl.pallas_call(
        _norm_kernel,
        grid=(n // tm,),
        in_specs=[pl.BlockSpec((tm, d), lambda i: (i, 0)), pl.BlockSpec((1, d), lambda i: (0, 0))],
        out_specs=pl.BlockSpec((tm, d), lambda i: (i, 0)),
        out_shape=jax.ShapeDtypeStruct((n, d), F32),
        compiler_params=_cparams(("parallel",)),
        name="final_norm",
    )(x, g)


def _rope_tables(seq):
    inv = ROPE_BASE ** (-jnp.arange(0, QK_ROPE, 2, dtype=F32) / QK_ROPE)
    ang = jnp.arange(seq)[:, None].astype(F32) * inv[None, :]
    cos, sin = jnp.cos(ang), jnp.sin(ang)
    pad = LANES - QK_NOPE - QK_ROPE
    ctab = jnp.concatenate([jnp.ones((seq, QK_NOPE), F32), cos, cos, jnp.zeros((seq, pad), F32)], axis=1)
    stab = jnp.concatenate([jnp.zeros((seq, QK_NOPE), F32), sin, sin, jnp.zeros((seq, pad), F32)], axis=1)
    return ctab, stab


def _split_w_in(w_in):
    d = w_in.shape[0]
    bounds = np.cumsum([WIDTH_A, WIDTH_A, WIDTH_A, Q_LORA, KV_LORA, QK_ROPE, WIDTH_C, WIDTH_C])
    qa, ka, va, cq, ckv, kr, qc, kc, vc = jnp.split(w_in, bounds.tolist(), axis=1)
    half = QK_ROPE // 2
    z_lo = jnp.zeros((d, QK_NOPE), w_in.dtype)
    z_hi = jnp.zeros((d, LANES - QK_NOPE - QK_ROPE), w_in.dtype)
    rope_blk = jnp.concatenate([z_lo, kr, z_hi], axis=1)
    swap_blk = jnp.concatenate([z_lo, -kr[:, half:], kr[:, :half], z_hi], axis=1)
    score_scale = HEAD_DIM ** -0.5
    wa = jnp.concatenate([qa * score_scale, ka, va], axis=1).astype(BF16)
    wc = jnp.concatenate([qc * score_scale, kc, vc], axis=1).astype(BF16)
    wb = jnp.concatenate([cq, ckv, rope_blk, swap_blk], axis=1).astype(BF16)
    return wa, wc, wb


def _split_w_uq(w_uq):
    r = w_uq.shape[0]
    w = w_uq.reshape(r, N_HEADS_B, QK_NOPE + QK_ROPE)
    nope, rope = w[..., :QK_NOPE], w[..., QK_NOPE:]
    half = QK_ROPE // 2
    z_hi = jnp.zeros((r, N_HEADS_B, LANES - QK_NOPE - QK_ROPE), w_uq.dtype)
    w1 = jnp.concatenate([nope, rope, z_hi], axis=-1)
    w2 = jnp.concatenate([jnp.zeros_like(nope), -rope[..., half:], rope[..., :half], z_hi], axis=-1)
    return w1.reshape(r, -1).astype(BF16), w2.reshape(r, -1).astype(BF16)


def _split_w_ukv(w_ukv):
    r = w_ukv.shape[0]
    w = w_ukv.reshape(r, N_HEADS_B, QK_NOPE + V_HEAD)
    k_nope, v = w[..., :QK_NOPE], w[..., QK_NOPE:]
    wk = jnp.concatenate([k_nope, jnp.zeros((r, N_HEADS_B, LANES - QK_NOPE), w_ukv.dtype)], axis=-1)
    return wk.reshape(r, -1).astype(BF16), v.reshape(r, -1).T.astype(BF16)


def kernel(x, g_mix, w_in, g_q, g_kv, w_uq, w_ukv, rpb, g_out_a, g_out_b, g_out_c, w_o, g_ffn, w1, w3, w2,
           w_router, e_w1, e_w3, e_w2, g_final):
    batch, seq, d = x.shape
    n = batch * seq
    depth = g_mix.shape[0]
    rows = seq // GRID_W
    ctab, stab = _rope_tables(seq)
    ew1, ew3, ew2 = e_w1.astype(BF16), e_w3.astype(BF16), e_w2.astype(BF16)
    xf = x.reshape(n, d)
    for layer in range(depth):
        wa, wc, wb = _split_w_in(w_in[layer])
        wq1, wq2 = _split_w_uq(w_uq[layer])
        wk, wvt = _split_w_ukv(w_ukv[layer])
        pa, pa4, pa16, pc, qm, km, vt = _proj_call(xf, g_mix[layer][None], wa, wc, wb, g_q[layer][None], g_kv[layer][None],
                                        wq1, wq2, wk, wvt, ctab, stab, seq)
        o_parts, lse_parts = [], []
        for dil, view in zip(DILATIONS, (pa, pa4, pa16)):
            o, lse = _band_call(view.reshape(batch, seq // dil, dil * 3 * WIDTH_A), dil)
            o_parts.append(o.reshape(n // dil, dil * WIDTH_A))
            lse_parts.append(lse.reshape(n // dil, dil * WIDTH_A))
        ob = _mla_call(qm, km, vt, batch, seq).reshape(n, WIDTH_B)
        oc = _na_call(pc, _na_tables(rpb[layer], rows), batch, seq).reshape(n, WIDTH_C)
        xf = _mix_out_call(xf, o_parts, lse_parts, ob, oc, g_out_a[layer][None], g_out_b[layer][None],
                           g_out_c[layer][None], w_o[layer].astype(BF16))
        j = layer // 2
        if layer % 2 == 0:
            xf = _ffn_call(xf, g_ffn[layer][None], w1[j].astype(BF16), w3[j].astype(BF16), w2[j].astype(BF16))
        else:
            wr_pad = jnp.pad(w_router[j], ((0, 0), (0, LANES - N_EXPERTS)))
            xf = _moe_layer(xf, g_ffn[layer][None], wr_pad, ew1, ew3, ew2, j)
    return _norm_call(xf, g_final[None]).reshape(batch, seq, d)
```

```python
import functools
import math

import numpy as np
import jax
import jax.numpy as jnp
from jax import lax
from jax.experimental import pallas as pl
from jax.experimental.pallas import tpu as pltpu

F32 = jnp.float32
BF16 = jnp.bfloat16

LANES = 128
V7X_VMEM_LIMIT_BYTES = 52 * 1024 * 1024

HEAD_DIM = 64
N_HEADS_A = 6
DILATIONS = (1, 4, 16)
BAND_HALF = 64
N_HEADS_B = 6
Q_LORA = 384
KV_LORA = 256
QK_NOPE = 64
QK_ROPE = 32
V_HEAD = 64
ROPE_BASE = 10000.0
N_HEADS_C = 4
GRID_W = 64
NA_ROWS = 8
NA_COLS = 16
WIDTH_A = N_HEADS_A * HEAD_DIM
WIDTH_B = N_HEADS_B * V_HEAD
WIDTH_C = N_HEADS_C * HEAD_DIM
N_EXPERTS = 8
TOP_K = 2
RMS_EPS = 1e-6
NEG_INF = -1e30

ROW_TILE = 512
MLA_TQ = 256
MLA_TK = 256
MLA_UNROLL = 32
MLA_LOOKAHEAD = 4
MLA_HEADS_PER_STEP = 6
MLA_DEN_ROWS = 16
MLA_Q_PRESCALE = (QK_NOPE + QK_ROPE) ** -0.5 * math.log2(math.e)
BAND_TQ = 128
BAND_TILES_PER_STEP = 4
BAND_LOOKAHEAD = 6
NA_TILE_ROWS = 2
NA_KEY_ROWS = 10
NA_TILES_PER_STEP = 4
NA_LOOKAHEAD = 16
MOE_TM = 512
MOE_TF = 1792
COMBINE_TM = 256
GATHER_UNROLL = 8


def _cparams(semantics):
    return pltpu.CompilerParams(dimension_semantics=semantics, vmem_limit_bytes=V7X_VMEM_LIMIT_BYTES)


def _rms(x, g):
    return x * lax.rsqrt(jnp.mean(x * x, axis=-1, keepdims=True) + RMS_EPS) * g


def _dot(a, b):
    return jnp.dot(a, b, preferred_element_type=F32)


def _dot_nt(a, b):
    return lax.dot_general(a, b, (((1,), (1,)), ((), ())), preferred_element_type=F32)


def _proj_kernel(x_ref, g_ref, wa_ref, wc_ref, wb_ref, gq_ref, gkv_ref, wq1_ref, wq2_ref, wk_ref, wvt_ref,
                 ct_ref, st_ref, pa_ref, pa4_ref, pa16_ref, pc_ref, qm_ref, km_ref, vt_ref, pa_scr):
    tm = x_ref.shape[0]
    h = _rms(x_ref[...], g_ref[...]).astype(BF16)
    pa = _dot(h, wa_ref[...])
    pa_ref[...] = pa.astype(BF16)
    n_chunks = pa.shape[1] // LANES
    for c in range(n_chunks):
        pa_scr[c] = pa[:, c * LANES:(c + 1) * LANES]
    for dil, view_ref in ((DILATIONS[1], pa4_ref), (DILATIONS[2], pa16_ref)):
        for r in range(dil):
            for c in range(n_chunks):
                col = r * 3 * WIDTH_A + c * LANES
                view_ref[:, col:col + LANES] = pa_scr[c, pl.ds(r, tm // dil, stride=dil), :].astype(BF16)
    pc_ref[...] = _dot(h, wc_ref[...]).astype(BF16)
    pb = _dot(h, wb_ref[...])
    hq = _rms(pb[:, :Q_LORA], gq_ref[...]).astype(BF16)
    hkv = _rms(pb[:, Q_LORA:Q_LORA + KV_LORA], gkv_ref[...]).astype(BF16)
    r1 = pb[:, Q_LORA + KV_LORA:Q_LORA + KV_LORA + LANES]
    r2 = pb[:, Q_LORA + KV_LORA + LANES:]
    ct = ct_ref[...]
    st = st_ref[...]
    qa = _dot(hq, wq1_ref[...])
    qb = _dot(hq, wq2_ref[...])
    kn = _dot(hkv, wk_ref[...])
    kr = r1 * ct + r2 * st
    for hd in range(N_HEADS_B):
        sl = slice(hd * LANES, (hd + 1) * LANES)
        qm_ref[:, sl] = ((qa[:, sl] * ct + qb[:, sl] * st) * MLA_Q_PRESCALE).astype(BF16)
        km_ref[:, sl] = (kn[:, sl] + kr).astype(BF16)
    vt_ref[...] = _dot_nt(wvt_ref[...], hkv).astype(BF16)


def _proj_call(x, g, wa, wc, wb, gq, gkv, wq1, wq2, wk, wvt, ctab, stab, seq):
    n, d = x.shape
    tm = ROW_TILE
    tiles_per_seq = seq // tm
    full = lambda a: pl.BlockSpec(a.shape, lambda i: (0,) * a.ndim)
    row = lambda w: pl.BlockSpec((tm, w), lambda i: (i, 0))
    tab = pl.BlockSpec((tm, LANES), lambda i: (i % tiles_per_seq, 0))
    view = lambda dil: pl.BlockSpec((tm // dil, dil * 3 * WIDTH_A), lambda i: (i, 0))
    hb = N_HEADS_B * LANES
    return pl.pallas_call(
        _proj_kernel,
        grid=(n // tm,),
        in_specs=[row(d), full(g), full(wa), full(wc), full(wb), full(gq), full(gkv), full(wq1), full(wq2),
                  full(wk), full(wvt), tab, tab],
        out_specs=[row(3 * WIDTH_A)] + [view(dil) for dil in DILATIONS[1:]] + [row(3 * WIDTH_C), row(hb), row(hb),
                   pl.BlockSpec((WIDTH_B, tm), lambda i: (0, i))],
        out_shape=[jax.ShapeDtypeStruct((n, 3 * WIDTH_A), BF16)]
                  + [jax.ShapeDtypeStruct((n // dil, dil * 3 * WIDTH_A), BF16) for dil in DILATIONS[1:]]
                  + [jax.ShapeDtypeStruct((n, 3 * WIDTH_C), BF16),
                   jax.ShapeDtypeStruct((n, hb), BF16), jax.ShapeDtypeStruct((n, hb), BF16),
                   jax.ShapeDtypeStruct((WIDTH_B, n), BF16)],
        scratch_shapes=[pltpu.VMEM((3 * WIDTH_A // LANES, tm, LANES), F32)],
        compiler_params=_cparams(("parallel",)),
        name="proj",
    )(x, g, wa, wc, wb, gq, gkv, wq1, wq2, wk, wvt, ctab, stab)


def _run_pipeline(n_items, look, issue, consume, finish):
    if look >= n_items:
        results = [consume(i, s) for i, s in enumerate([issue(i) for i in range(n_items)])]
        for i, res in enumerate(results):
            finish(i, res)
        return
    inflight = {i: issue(i) for i in range(min(look, n_items))}
    pending = None
    for i in range(n_items):
        if i + look < n_items:
            inflight[i + look] = issue(i + look)
        if pending is not None:
            finish(*pending)
        pending = (i, consume(i, inflight.pop(i)))
    finish(*pending)


class _HeadPairs:
    def __init__(self, q_ref, k_ref, v_ref, tq, tkw):
        self.q_ref, self.k_ref, self.v_ref, self.tq, self.tkw = q_ref, k_ref, v_ref, tq, tkw
        lane = lax.broadcasted_iota(jnp.int32, (tq, LANES), 1)
        self.low = lane < HEAD_DIM
        self.ones = jnp.ones((tkw, LANES), BF16)

    @staticmethod
    def cols(h):
        return slice((h // 2) * LANES, (h // 2 + 1) * LANES)

    def scores(self, q_start, k_start, h):
        q = self.q_ref[q_start:q_start + self.tq, self.cols(h)]
        q = jnp.where(self.low if h % 2 == 0 else ~self.low, q, jnp.zeros_like(q))
        return _dot_nt(q, self.k_ref[pl.ds(k_start, self.tkw), self.cols(h)])

    def values(self, k_start, h, p):
        v = jnp.concatenate([self.v_ref[pl.ds(k_start, self.tkw), self.cols(h)], self.ones], axis=-1)
        o = _dot(p, v)
        return o[:, :LANES], o[:, LANES:]

    def merge(self, even, odd):
        return jnp.where(self.low, even, odd)


def _band_tables(dilation):
    tq, tkw = BAND_TQ, BAND_TQ + 2 * BAND_HALF
    shift = np.array([0, -BAND_HALF, -2 * BAND_HALF])
    rel = shift[:, None, None] + np.arange(tkw)[None, None, :] - np.arange(tq)[None, :, None]
    dist = np.abs(rel)
    slopes = 2.0 ** (-8.0 * np.arange(1, N_HEADS_A + 1) / N_HEADS_A)
    bias = -(slopes[None, :, None, None] * dilation) * dist[:, None].astype(np.float64)
    tab = np.where(dist[:, None] <= BAND_HALF, bias, NEG_INF)
    return jnp.asarray(tab, F32)


def _band_kernel(q_ref, k_ref, v_ref, tab_ref, o_ref, lse_ref, *, n):
    tq = BAND_TQ
    tkw = tq + 2 * BAND_HALF
    tiles = q_ref.shape[0] // tq
    last_tile = n // tq - 1
    items = [(t, h) for t in range(tiles) for h in range(N_HEADS_A)]
    info = []
    for t in range(tiles):
        i = pl.program_id(2) * tiles + t
        start = pl.multiple_of(jnp.clip(i * tq - BAND_HALF, 0, n - tkw), BAND_HALF)
        variant = jnp.minimum(i, 1) + (i == last_tile).astype(jnp.int32)
        info.append((start, variant))
    pairs = _HeadPairs(q_ref, k_ref, v_ref, tq, tkw)
    held = {}

    def issue(idx):
        t, h = items[idx]
        return pairs.scores(t * tq, info[t][0], h)

    def consume(idx, s):
        t, h = items[idx]
        s = s + tab_ref[info[t][1], h]
        m = jnp.max(s, axis=-1, keepdims=True)
        return jnp.exp(s - m).astype(BF16), m

    def finish(idx, res):
        t, h = items[idx]
        p, m = res
        out, den = pairs.values(info[t][0], h, p)
        held[h % 2] = (out / den, m + jnp.log(den))
        if h % 2 == 1:
            cols = pairs.cols(h)
            o_ref[t * tq:(t + 1) * tq, cols] = pairs.merge(held[0][0], held[1][0])
            lse_ref[t * tq:(t + 1) * tq, cols] = pairs.merge(held[0][1], held[1][1])

    _run_pipeline(len(items), BAND_LOOKAHEAD, issue, consume, finish)


def _band_call(pa_view, dilation):
    b, n, _ = pa_view.shape
    rows = BAND_TQ * BAND_TILES_PER_STEP
    w = WIDTH_A
    tab = _band_tables(dilation)
    qspec = pl.BlockSpec((None, rows, w), lambda bb, r, i: (bb, i, 3 * r))
    kspec = pl.BlockSpec((None, n, w), lambda bb, r, i: (bb, 0, 3 * r + 1))
    vspec = pl.BlockSpec((None, n, w), lambda bb, r, i: (bb, 0, 3 * r + 2))
    tspec = pl.BlockSpec(tab.shape, lambda bb, r, i: (0, 0, 0, 0))
    ospec = pl.BlockSpec((None, rows, w), lambda bb, r, i: (bb, i, r))
    shape = jax.ShapeDtypeStruct((b, n, dilation * w), F32)
    return pl.pallas_call(
        functools.partial(_band_kernel, n=n),
        grid=(b, dilation, n // rows),
        in_specs=[qspec, kspec, vspec, tspec],
        out_specs=[ospec, ospec],
        out_shape=[shape, shape],
        compiler_params=_cparams(("parallel", "parallel", "parallel")),
        name=f"band_d{dilation}",
    )(pa_view, pa_view, pa_view, tab)


def _mla_kernel(q_ref, k_ref, vt_ref, o_ref):
    tq = q_ref.shape[0]
    seq = k_ref.shape[0]
    tk = MLA_TK
    nh = MLA_HEADS_PER_STEP
    n_chunks = seq // tk
    items = [(u, h) for u in range(MLA_UNROLL) for h in range(nh)]
    look = MLA_LOOKAHEAD
    ones_rows = jnp.ones((MLA_DEN_ROWS, tk), BF16)

    def key_slice(chunk):
        return pl.ds(pl.multiple_of(chunk * tk, tk), tk)

    def score_matmul(chunk, h):
        k = k_ref[key_slice(chunk), h * LANES:(h + 1) * LANES]
        return _dot_nt(k, q_ref[:, h * LANES:(h + 1) * LANES])

    def value_matmul(chunk, h, p):
        vt = jnp.concatenate([vt_ref[h * V_HEAD:(h + 1) * V_HEAD, key_slice(chunk)], ones_rows], axis=0)
        return _dot(vt, p)

    def body(j, carry):
        state = list(carry[:2 * nh])
        scores = dict(zip(items[:look], carry[2 * nh:2 * nh + look]))
        pend_p, pend_alpha = carry[2 * nh + look:]
        pending = (jnp.maximum(j * MLA_UNROLL - 1, 0), nh - 1, pend_p, pend_alpha)
        ahead = []
        for idx, (u, h) in enumerate(items):
            la = idx + look
            if la < len(items):
                lu, lh = items[la]
                scores[lu, lh] = score_matmul(j * MLA_UNROLL + lu, lh)
            else:
                lu, lh = items[la - len(items)]
                ahead.append(score_matmul(jnp.minimum((j + 1) * MLA_UNROLL + lu, n_chunks - 1), lh))
            pc, ph, pp, pa = pending
            state[2 * ph + 1] = pa * state[2 * ph + 1] + value_matmul(pc, ph, pp)
            s = scores.pop((u, h))
            m_new = jnp.maximum(state[2 * h], jnp.max(s, axis=0, keepdims=True))
            alpha = jnp.exp2(state[2 * h] - m_new)
            state[2 * h] = m_new
            pending = (j * MLA_UNROLL + u, h, jnp.exp2(s - m_new).astype(BF16), alpha)
        return tuple(state) + tuple(ahead) + (pending[2], pending[3])

    init = (jnp.full((1, tq), NEG_INF, F32), jnp.zeros((V_HEAD + MLA_DEN_ROWS, tq), F32)) * nh
    init += tuple(score_matmul(u, h) for u, h in items[:look])
    init += (jnp.zeros((tk, tq), BF16), jnp.ones((1, tq), F32))
    res = lax.fori_loop(0, n_chunks // MLA_UNROLL, body, init)
    accs = [res[2 * h + 1] for h in range(nh)]
    accs[nh - 1] = res[-1] * accs[nh - 1] + value_matmul(n_chunks - 1, nh - 1, res[-2])
    out_t = jnp.concatenate([a[:V_HEAD] / a[V_HEAD:V_HEAD + 1] for a in accs], axis=0)
    o_ref[...] = out_t.T


def _mla_call(qm, km, vt, batch, seq):
    tq = MLA_TQ
    nh = MLA_HEADS_PER_STEP
    qspec = pl.BlockSpec((None, tq, nh * LANES), lambda b, g, i: (b, i, g))
    kspec = pl.BlockSpec((None, seq, nh * LANES), lambda b, g, i: (b, 0, g))
    vspec = pl.BlockSpec((nh * V_HEAD, seq), lambda b, g, i: (g, b))
    ospec = pl.BlockSpec((None, tq, nh * V_HEAD), lambda b, g, i: (b, i, g))
    return pl.pallas_call(
        _mla_kernel,
        grid=(batch, N_HEADS_B // nh, seq // tq),
        in_specs=[qspec, kspec, vspec],
        out_specs=ospec,
        out_shape=jax.ShapeDtypeStruct((batch, seq, WIDTH_B), F32),
        compiler_params=_cparams(("parallel", "parallel", "parallel")),
        name="mla",
    )(qm.reshape(batch, seq, -1), km.reshape(batch, seq, -1), vt)


def _na_variant(i, n_tiles):
    return jnp.minimum(i, 2) + jnp.maximum(i - (n_tiles - 3), 0)


def _na_kernel(q_ref, k_ref, v_ref, tab_ref, o_ref, *, rows):
    tq = NA_TILE_ROWS * GRID_W
    tkw = NA_KEY_ROWS * GRID_W
    tiles = q_ref.shape[0] // tq
    n_tiles = rows // NA_TILE_ROWS
    items = [(t, h) for t in range(tiles) for h in range(N_HEADS_C)]
    info = []
    for t in range(tiles):
        i = pl.program_id(1) * tiles + t
        base = jnp.clip(i * NA_TILE_ROWS - NA_ROWS // 2, 0, rows - NA_KEY_ROWS)
        info.append((pl.multiple_of(base * GRID_W, GRID_W), _na_variant(i, n_tiles)))
    pairs = _HeadPairs(q_ref, k_ref, v_ref, tq, tkw)
    held = {}

    def issue(idx):
        t, h = items[idx]
        return pairs.scores(t * tq, info[t][0], h)

    def consume(idx, s):
        t, h = items[idx]
        s = s + tab_ref[info[t][1], h]
        return jnp.exp(s - jnp.max(s, axis=-1, keepdims=True)).astype(BF16)

    def finish(idx, p):
        t, h = items[idx]
        out, den = pairs.values(info[t][0], h, p)
        held[h % 2] = out / den
        if h % 2 == 1:
            o_ref[t * tq:(t + 1) * tq, pairs.cols(h)] = pairs.merge(held[0], held[1])

    _run_pipeline(len(items), NA_LOOKAHEAD, issue, consume, finish)


def _na_tables(rpb, rows):
    r0 = np.array([0, 2, 4, rows - 4, rows - 2])
    base = np.clip(r0 - NA_ROWS // 2, 0, rows - NA_KEY_ROWS)
    r = r0[:, None] + np.arange(NA_TILE_ROWS)[None, :]
    row_start = np.clip(r - NA_ROWS // 2, 0, rows - NA_ROWS)
    krow = base[:, None] + np.arange(NA_KEY_ROWS)[None, :]
    drow = krow[:, None, :] - r[:, :, None]
    row_ok = (krow[:, None, :] >= row_start[:, :, None]) & (krow[:, None, :] < row_start[:, :, None] + NA_ROWS)
    c = np.arange(GRID_W)
    win_start = np.clip(c - NA_COLS // 2, 0, GRID_W - NA_COLS)
    col_ok = (c[None, :] >= win_start[:, None]) & (c[None, :] < win_start[:, None] + NA_COLS)
    dcol = np.clip(c[None, :] - c[:, None], -(NA_COLS - 1), NA_COLS - 1)
    ok = row_ok[:, :, None, :, None] & col_ok[None, None, :, None, :]
    di = np.clip(drow, -(NA_ROWS - 1), NA_ROWS - 1) + (NA_ROWS - 1)
    pick_col = (dcol[:, :, None] + NA_COLS - 1 == np.arange(2 * NA_COLS - 1)).astype(np.float32)
    pick_row = (di[..., None] == np.arange(2 * NA_ROWS - 1)).astype(np.float32)
    hi = lax.Precision.HIGHEST
    toeplitz = jnp.einsum("hab,cjb->hacj", rpb.astype(F32), pick_col, precision=hi)
    bias = jnp.einsum("vqka,hacj->vhqckj", pick_row, toeplitz, precision=hi)
    tab = jnp.where(jnp.asarray(ok)[:, None], bias, NEG_INF)
    return tab.reshape(5, N_HEADS_C, NA_TILE_ROWS * GRID_W, NA_KEY_ROWS * GRID_W)


def _na_call(pc, tab, batch, seq):
    rows = seq // GRID_W
    tq = NA_TILE_ROWS * GRID_W * NA_TILES_PER_STEP
    w = WIDTH_C
    qspec = pl.BlockSpec((None, tq, w), lambda b, i: (b, i, 0))
    kspec = pl.BlockSpec((None, seq, w), lambda b, i: (b, 0, 1))
    vspec = pl.BlockSpec((None, seq, w), lambda b, i: (b, 0, 2))
    tspec = pl.BlockSpec(tab.shape, lambda b, i: (0, 0, 0, 0))
    ospec = pl.BlockSpec((None, tq, w), lambda b, i: (b, i, 0))
    pc3 = pc.reshape(batch, seq, 3 * w)
    return pl.pallas_call(
        functools.partial(_na_kernel, rows=rows),
        grid=(batch, seq // tq),
        in_specs=[qspec, kspec, vspec, tspec],
        out_specs=ospec,
        out_shape=jax.ShapeDtypeStruct((batch, seq, w), F32),
        compiler_params=_cparams(("parallel", "parallel")),
        name="natten",
    )(pc3, pc3, pc3, tab)


def _mix_out_kernel(x_ref, o1_ref, o2_ref, o3_ref, l1_ref, l2_ref, l3_ref, ob_ref, oc_ref,
                    ga_ref, gb_ref, gc_ref, wo_ref, out_ref, o2_scr, o3_scr, l2_scr, l3_scr):
    tm = x_ref.shape[0]
    n_chunks = WIDTH_A // LANES

    def natural(view_ref, scr, dil):
        for r in range(dil):
            for c in range(n_chunks):
                col = r * WIDTH_A + c * LANES
                scr[c, pl.ds(r, tm // dil, stride=dil), :] = view_ref[:, col:col + LANES]
        return jnp.concatenate([scr[c] for c in range(n_chunks)], axis=-1)

    l1, o1 = l1_ref[...], o1_ref[...]
    l2, o2 = natural(l2_ref, l2_scr, DILATIONS[1]), natural(o2_ref, o2_scr, DILATIONS[1])
    l3, o3 = natural(l3_ref, l3_scr, DILATIONS[2]), natural(o3_ref, o3_scr, DILATIONS[2])
    mx = jnp.maximum(jnp.maximum(l1, l2), l3)
    w1, w2, w3 = jnp.exp(l1 - mx), jnp.exp(l2 - mx), jnp.exp(l3 - mx)
    oa = (w1 * o1 + w2 * o2 + w3 * o3) / (w1 + w2 + w3)
    ya = _rms(oa, ga_ref[...]).astype(BF16)
    yb = _rms(ob_ref[...], gb_ref[...]).astype(BF16)
    yc = _rms(oc_ref[...], gc_ref[...]).astype(BF16)
    y = _dot(ya, wo_ref[:WIDTH_A, :])
    y = y + _dot(yb, wo_ref[WIDTH_A:WIDTH_A + WIDTH_B, :])
    y = y + _dot(yc, wo_ref[WIDTH_A + WIDTH_B:, :])
    out_ref[...] = x_ref[...] + y


def _mix_out_call(x, o_parts, lse_parts, ob, oc, ga, gb, gc, wo):
    n, d = x.shape
    tm = ROW_TILE
    full = lambda a: pl.BlockSpec(a.shape, lambda i: (0,) * a.ndim)
    row = lambda w: pl.BlockSpec((tm, w), lambda i: (i, 0))
    view = lambda dil: pl.BlockSpec((tm // dil, dil * WIDTH_A), lambda i: (i, 0))
    return pl.pallas_call(
        _mix_out_kernel,
        grid=(n // tm,),
        in_specs=[row(d)] + [view(dil) for dil in DILATIONS] * 2
                 + [row(WIDTH_B), row(WIDTH_C), full(ga), full(gb), full(gc), full(wo)],
        out_specs=row(d),
        out_shape=jax.ShapeDtypeStruct((n, d), F32),
        scratch_shapes=[pltpu.VMEM((WIDTH_A // LANES, tm, LANES), F32)] * 4,
        compiler_params=_cparams(("parallel",)),
        name="mix_out",
    )(x, *o_parts, *lse_parts, ob, oc, ga, gb, gc, wo)


def _silu(u):
    return u * (1.0 / (1.0 + jnp.exp(-u)))


def _ffn_kernel(x_ref, g_ref, w1_ref, w3_ref, w2_ref, out_ref):
    x = x_ref[...]
    h = _rms(x, g_ref[...]).astype(BF16)
    a = (_silu(_dot(h, w1_ref[...])) * _dot(h, w3_ref[...])).astype(BF16)
    out_ref[...] = x + _dot(a, w2_ref[...])


def _ffn_call(x, g, w1, w3, w2):
    n, d = x.shape
    tm = ROW_TILE
    resident = lambda a: pl.BlockSpec(a.shape, lambda i: (0, 0), pipeline_mode=pl.Buffered(1))
    return pl.pallas_call(
        _ffn_kernel,
        grid=(n // tm,),
        in_specs=[pl.BlockSpec((tm, d), lambda i: (i, 0)), pl.BlockSpec((1, d), lambda i: (0, 0)),
                  resident(w1), resident(w3), resident(w2)],
        out_specs=pl.BlockSpec((tm, d), lambda i: (i, 0)),
        out_shape=jax.ShapeDtypeStruct((n, d), F32),
        compiler_params=_cparams(("parallel",)),
        name="ffn",
    )(x, g, w1, w3, w2)


def _router_kernel(x_ref, g_ref, wr_ref, h_ref, e_ref, gate_ref):
    h = _rms(x_ref[...], g_ref[...])
    _to_token_tiles(h_ref, h)
    logits = jnp.dot(h, wr_ref[...], precision=lax.Precision.HIGHEST, preferred_element_type=F32)
    lane = lax.broadcasted_iota(jnp.int32, logits.shape, 1).astype(F32)
    logits = jnp.where(lane < N_EXPERTS, logits, -jnp.inf)
    m1 = jnp.max(logits, axis=-1, keepdims=True)
    i1 = jnp.min(jnp.where(logits == m1, lane, float(LANES)), axis=-1, keepdims=True)
    rest = jnp.where(lane == i1, -jnp.inf, logits)
    m2 = jnp.max(rest, axis=-1, keepdims=True)
    i2 = jnp.min(jnp.where(rest == m2, lane, float(LANES)), axis=-1, keepdims=True)
    e = jnp.exp(m2 - m1)
    den = 1.0 + e
    e_ref[...] = jnp.where(lane == 0.0, i1, jnp.where(lane == 1.0, i2, 0.0)).astype(jnp.int32)
    gate_ref[...] = jnp.where(lane == 0.0, 1.0 / den, jnp.where(lane == 1.0, e / den, 0.0))


def _router_call(x, g, wr_pad):
    n, d = x.shape
    tm = ROW_TILE
    row = lambda w: pl.BlockSpec((tm, w), lambda i: (i, 0))
    full = lambda a: pl.BlockSpec(a.shape, lambda i: (0,) * a.ndim)
    return pl.pallas_call(
        _router_kernel,
        grid=(n // tm,),
        in_specs=[row(d), full(g), full(wr_pad)],
        out_specs=[pl.BlockSpec((tm * TOKEN_ROWS, LANES), lambda i: (i, 0)), row(LANES), row(LANES)],
        out_shape=[jax.ShapeDtypeStruct((n * TOKEN_ROWS, LANES), F32), jax.ShapeDtypeStruct((n, LANES), jnp.int32),
                   jax.ShapeDtypeStruct((n, LANES), F32)],
        compiler_params=_cparams(("parallel",)),
        name="router",
    )(x, g, wr_pad)


TOKEN_ROWS = 8


def _to_token_tiles(dst_ref, x):
    tm, d = x.shape
    assert d == TOKEN_ROWS * LANES
    for j in range(TOKEN_ROWS):
        dst_ref[pl.ds(j, tm, stride=TOKEN_ROWS), :] = x[:, j * LANES:(j + 1) * LANES]


def _from_token_tiles(src_ref, tm):
    return jnp.concatenate([src_ref[pl.ds(j, tm, stride=TOKEN_ROWS), :] for j in range(TOKEN_ROWS)], axis=-1)


def _row_copy(src_hbm, row, dst_ref, r, sem):
    src = src_hbm.at[pl.ds(pl.multiple_of(row * TOKEN_ROWS, TOKEN_ROWS), TOKEN_ROWS)]
    return pltpu.make_async_copy(src, dst_ref.at[pl.ds(pl.multiple_of(r * TOKEN_ROWS, TOKEN_ROWS), TOKEN_ROWS)], sem)


def _start_row_gather(idx_ref, base, src_hbm, dst_ref, sem, count):
    def body(r, c):
        _row_copy(src_hbm, idx_ref[base + r], dst_ref, r, sem).start()
        return c
    lax.fori_loop(0, count, body, 0, unroll=GATHER_UNROLL)


def _wait_row_gather(src_hbm, dst_ref, sem, count):
    pltpu.make_async_copy(src_hbm.at[pl.ds(0, count * TOKEN_ROWS)], dst_ref, sem).wait()


def _moe_ffn_kernel(tile_e_ref, tile_ok_ref, row_tok_ref, h_hbm, w1_ref, w3_ref, w2_ref, out_ref,
                    xg_ref, hb_ref, acc_ref, sem):
    i = pl.program_id(0)
    f = pl.program_id(1)
    n_tiles = pl.num_programs(0)
    tm = MOE_TM
    slot = i % 2

    @pl.when(f == 0)
    def _():
        @pl.when(i == 0)
        def _():
            _start_row_gather(row_tok_ref, 0, h_hbm, xg_ref.at[0], sem.at[0], tm)

        _wait_row_gather(h_hbm, xg_ref.at[slot], sem.at[slot], tm)

        @pl.when(i + 1 < n_tiles)
        def _():
            _start_row_gather(row_tok_ref, (i + 1) * tm, h_hbm, xg_ref.at[1 - slot], sem.at[1 - slot], tm)

        hb_ref[...] = _from_token_tiles(xg_ref.at[slot], tm).astype(BF16)
        acc_ref[...] = jnp.zeros_like(acc_ref)

    @pl.when(tile_ok_ref[i] != 0)
    def _():
        h = hb_ref[...]
        a = (_silu(_dot(h, w1_ref[...])) * _dot(h, w3_ref[...])).astype(BF16)
        acc_ref[...] += _dot(a, w2_ref[...])

    @pl.when(f == pl.num_programs(1) - 1)
    def _():
        _to_token_tiles(out_ref, acc_ref[...])


def _moe_ffn_call(tile_e, tile_ok, row_tok, h, w1, w3, w2, layer):
    d = w1.shape[2]
    n_tiles = tile_e.shape[0]
    ff = w1.shape[3]
    tm, tf = MOE_TM, MOE_TF
    grid_spec = pltpu.PrefetchScalarGridSpec(
        num_scalar_prefetch=3,
        grid=(n_tiles, ff // tf),
        in_specs=[pl.BlockSpec(memory_space=pl.ANY),
                  pl.BlockSpec((None, None, d, tf), lambda i, f, te, tv, rt: (layer, te[i], 0, f)),
                  pl.BlockSpec((None, None, d, tf), lambda i, f, te, tv, rt: (layer, te[i], 0, f)),
                  pl.BlockSpec((None, None, tf, d), lambda i, f, te, tv, rt: (layer, te[i], f, 0))],
        out_specs=pl.BlockSpec((tm * TOKEN_ROWS, LANES), lambda i, f, te, tv, rt: (i, 0)),
        scratch_shapes=[pltpu.VMEM((2, tm * TOKEN_ROWS, LANES), F32), pltpu.VMEM((tm, d), BF16),
                        pltpu.VMEM((tm, d), F32), pltpu.SemaphoreType.DMA((2,))],
    )
    return pl.pallas_call(
        _moe_ffn_kernel,
        grid_spec=grid_spec,
        out_shape=jax.ShapeDtypeStruct((n_tiles * tm * TOKEN_ROWS, LANES), F32),
        compiler_params=_cparams(("arbitrary", "arbitrary")),
        name="moe_ffn",
    )(tile_e, tile_ok, row_tok, h, w1, w3, w2)


def _combine_kernel(d0_ref, d1_ref, y_hbm, x_ref, gate_ref, gain_ref, out_ref, b0_ref, b1_ref, sem, *, final_norm):
    i = pl.program_id(0)
    n_tiles = pl.num_programs(0)
    tm = COMBINE_TM
    slot = i % 2

    def start(tile, s):
        _start_row_gather(d0_ref, tile * tm, y_hbm, b0_ref.at[s], sem.at[0, s], tm)
        _start_row_gather(d1_ref, tile * tm, y_hbm, b1_ref.at[s], sem.at[1, s], tm)

    @pl.when(i == 0)
    def _():
        start(0, 0)

    _wait_row_gather(y_hbm, b0_ref.at[slot], sem.at[0, slot], tm)
    _wait_row_gather(y_hbm, b1_ref.at[slot], sem.at[1, slot], tm)

    @pl.when(i + 1 < n_tiles)
    def _():
        start(i + 1, 1 - slot)

    gates = gate_ref[...]
    y0 = _from_token_tiles(b0_ref.at[slot], tm)
    y1 = _from_token_tiles(b1_ref.at[slot], tm)
    out = x_ref[...] + (gates[:, 0:1] * y0 + gates[:, 1:2] * y1)
    out_ref[...] = _rms(out, gain_ref[...]) if final_norm else out


def _combine_call(d0, d1, y, x, gates, gain, final_norm):
    n, d = x.shape
    tm = COMBINE_TM
    grid_spec = pltpu.PrefetchScalarGridSpec(
        num_scalar_prefetch=2,
        grid=(n // tm,),
        in_specs=[pl.BlockSpec(memory_space=pl.ANY),
                  pl.BlockSpec((tm, d), lambda i, a, b: (i, 0)),
                  pl.BlockSpec((tm, LANES), lambda i, a, b: (i, 0)),
                  pl.BlockSpec((1, d), lambda i, a, b: (0, 0))],
        out_specs=pl.BlockSpec((tm, d), lambda i, a, b: (i, 0)),
        scratch_shapes=[pltpu.VMEM((2, tm * TOKEN_ROWS, LANES), F32)] * 2 + [pltpu.SemaphoreType.DMA((2, 2))],
    )
    return pl.pallas_call(
        functools.partial(_combine_kernel, final_norm=final_norm),
        grid_spec=grid_spec,
        out_shape=jax.ShapeDtypeStruct((n, d), F32),
        compiler_params=_cparams(("arbitrary",)),
        name="moe_combine",
    )(d0, d1, y, x, gates, gain)


def _moe_plan(top_e, n_tiles):
    n_assign = top_e.shape[0] * TOP_K
    flat_e = top_e.reshape(n_assign)
    onehot = (flat_e[:, None] == jnp.arange(N_EXPERTS, dtype=jnp.int32)[None, :]).astype(jnp.int32)
    csum = jnp.cumsum(onehot, axis=0)
    rank = jnp.take_along_axis(csum, flat_e[:, None], axis=1)[:, 0] - 1
    counts = csum[-1]
    padded = (counts + MOE_TM - 1) // MOE_TM * MOE_TM
    pend = jnp.cumsum(padded)
    dest = (pend - padded)[flat_e] + rank
    row_tok = jnp.zeros((n_tiles * MOE_TM,), jnp.int32).at[dest].set(jnp.arange(n_assign, dtype=jnp.int32) // TOP_K)
    tile_start = jnp.arange(n_tiles, dtype=jnp.int32) * MOE_TM
    tile_e = jnp.minimum(jnp.searchsorted(pend, tile_start, side="right"), N_EXPERTS - 1).astype(jnp.int32)
    tile_ok = (tile_start < pend[-1]).astype(jnp.int32)
    dest = dest.reshape(-1, TOP_K).astype(jnp.int32)
    return tile_e, tile_ok, row_tok, dest[:, 0], dest[:, 1]


def _moe_layer(x, g, wr_pad, w1, w3, w2, layer, gain, final_norm):
    n = x.shape[0]
    h, top_e, gates = _router_call(x, g, wr_pad)
    n_tiles = -(-(n * TOP_K + N_EXPERTS * (MOE_TM - 1)) // MOE_TM)
    tile_e, tile_ok, row_tok, d0, d1 = _moe_plan(top_e[:, :TOP_K], n_tiles)
    y = _moe_ffn_call(tile_e, tile_ok, row_tok, h, w1, w3, w2, layer)
    return _combine_call(d0, d1, y, x, gates, gain, final_norm)


def _norm_kernel(x_ref, g_ref, o_ref):
    o_ref[...] = _rms(x_ref[...], g_ref[...])


def _norm_call(x, g):
    n, d = x.shape
    tm = ROW_TILE
    return pl.pallas_call(
        _norm_kernel,
        grid=(n // tm,),
        in_specs=[pl.BlockSpec((tm, d), lambda i: (i, 0)), pl.BlockSpec((1, d), lambda i: (0, 0))],
        out_specs=pl.BlockSpec((tm, d), lambda i: (i, 0)),
        out_shape=jax.ShapeDtypeStruct((n, d), F32),
        compiler_params=_cparams(("parallel",)),
        name="final_norm",
    )(x, g)


def _rope_tables(seq):
    inv = ROPE_BASE ** (-jnp.arange(0, QK_ROPE, 2, dtype=F32) / QK_ROPE)
    ang = jnp.arange(seq)[:, None].astype(F32) * inv[None, :]
    cos, sin = jnp.cos(ang), jnp.sin(ang)
    pad = LANES - QK_NOPE - QK_ROPE
    ctab = jnp.concatenate([jnp.ones((seq, QK_NOPE), F32), cos, cos, jnp.zeros((seq, pad), F32)], axis=1)
    stab = jnp.concatenate([jnp.zeros((seq, QK_NOPE), F32), sin, sin, jnp.zeros((seq, pad), F32)], axis=1)
    return ctab, stab


def _split_w_in(w_in):
    d = w_in.shape[0]
    bounds = np.cumsum([WIDTH_A, WIDTH_A, WIDTH_A, Q_LORA, KV_LORA, QK_ROPE, WIDTH_C, WIDTH_C])
    qa, ka, va, cq, ckv, kr, qc, kc, vc = jnp.split(w_in, bounds.tolist(), axis=1)
    half = QK_ROPE // 2
    z_lo = jnp.zeros((d, QK_NOPE), w_in.dtype)
    z_hi = jnp.zeros((d, LANES - QK_NOPE - QK_ROPE), w_in.dtype)
    rope_blk = jnp.concatenate([z_lo, kr, z_hi], axis=1)
    swap_blk = jnp.concatenate([z_lo, -kr[:, half:], kr[:, :half], z_hi], axis=1)
    score_scale = HEAD_DIM ** -0.5
    wa = jnp.concatenate([qa * score_scale, ka, va], axis=1).astype(BF16)
    wc = jnp.concatenate([qc * score_scale, kc, vc], axis=1).astype(BF16)
    wb = jnp.concatenate([cq, ckv, rope_blk, swap_blk], axis=1).astype(BF16)
    return wa, wc, wb


def _split_w_uq(w_uq):
    r = w_uq.shape[0]
    w = w_uq.reshape(r, N_HEADS_B, QK_NOPE + QK_ROPE)
    nope, rope = w[..., :QK_NOPE], w[..., QK_NOPE:]
    half = QK_ROPE // 2
    z_hi = jnp.zeros((r, N_HEADS_B, LANES - QK_NOPE - QK_ROPE), w_uq.dtype)
    w1 = jnp.concatenate([nope, rope, z_hi], axis=-1)
    w2 = jnp.concatenate([jnp.zeros_like(nope), -rope[..., half:], rope[..., :half], z_hi], axis=-1)
    return w1.reshape(r, -1).astype(BF16), w2.reshape(r, -1).astype(BF16)


def _split_w_ukv(w_ukv):
    r = w_ukv.shape[0]
    w = w_ukv.reshape(r, N_HEADS_B, QK_NOPE + V_HEAD)
    k_nope, v = w[..., :QK_NOPE], w[..., QK_NOPE:]
    wk = jnp.concatenate([k_nope, jnp.zeros((r, N_HEADS_B, LANES - QK_NOPE), w_ukv.dtype)], axis=-1)
    return wk.reshape(r, -1).astype(BF16), v.reshape(r, -1).T.astype(BF16)


def kernel(x, g_mix, w_in, g_q, g_kv, w_uq, w_ukv, rpb, g_out_a, g_out_b, g_out_c, w_o, g_ffn, w1, w3, w2,
           w_router, e_w1, e_w3, e_w2, g_final):
    batch, seq, d = x.shape
    n = batch * seq
    depth = g_mix.shape[0]
    rows = seq // GRID_W
    ctab, stab = _rope_tables(seq)
    ew1, ew3, ew2 = e_w1.astype(BF16), e_w3.astype(BF16), e_w2.astype(BF16)
    xf = x.reshape(n, d)
    for layer in range(depth):
        wa, wc, wb = _split_w_in(w_in[layer])
        wq1, wq2 = _split_w_uq(w_uq[layer])
        wk, wvt = _split_w_ukv(w_ukv[layer])
        pa, pa4, pa16, pc, qm, km, vt = _proj_call(xf, g_mix[layer][None], wa, wc, wb, g_q[layer][None], g_kv[layer][None],
                                        wq1, wq2, wk, wvt, ctab, stab, seq)
        o_parts, lse_parts = [], []
        for dil, view in zip(DILATIONS, (pa, pa4, pa16)):
            o, lse = _band_call(view.reshape(batch, seq // dil, dil * 3 * WIDTH_A), dil)
            o_parts.append(o.reshape(n // dil, dil * WIDTH_A))
            lse_parts.append(lse.reshape(n // dil, dil * WIDTH_A))
        ob = _mla_call(qm, km, vt, batch, seq).reshape(n, WIDTH_B)
        oc = _na_call(pc, _na_tables(rpb[layer], rows), batch, seq).reshape(n, WIDTH_C)
        xf = _mix_out_call(xf, o_parts, lse_parts, ob, oc, g_out_a[layer][None], g_out_b[layer][None],
                           g_out_c[layer][None], w_o[layer].astype(BF16))
        j = layer // 2
        if layer % 2 == 0:
            xf = _ffn_call(xf, g_ffn[layer][None], w1[j].astype(BF16), w3[j].astype(BF16), w2[j].astype(BF16))
        else:
            wr_pad = jnp.pad(w_router[j], ((0, 0), (0, LANES - N_EXPERTS)))
            xf = _moe_layer(xf, g_ffn[layer][None], wr_pad, ew1, ew3, ew2, j, g_final[None], layer == depth - 1)
    if depth % 2 == 1:
        xf = _norm_call(xf, g_final[None])
    return xf.reshape(batch, seq, d)
```

```python
import functools
import math

import numpy as np
import jax
import jax.numpy as jnp
from jax import lax
from jax.experimental import pallas as pl
from jax.experimental.pallas import tpu as pltpu

F32 = jnp.float32
BF16 = jnp.bfloat16

LANES = 128
V7X_VMEM_LIMIT_BYTES = 52 * 1024 * 1024

HEAD_DIM = 64
N_HEADS_A = 6
DILATIONS = (1, 4, 16)
BAND_HALF = 64
N_HEADS_B = 6
Q_LORA = 384
KV_LORA = 256
QK_NOPE = 64
QK_ROPE = 32
V_HEAD = 64
ROPE_BASE = 10000.0
N_HEADS_C = 4
GRID_W = 64
NA_ROWS = 8
NA_COLS = 16
WIDTH_A = N_HEADS_A * HEAD_DIM
WIDTH_B = N_HEADS_B * V_HEAD
WIDTH_C = N_HEADS_C * HEAD_DIM
N_EXPERTS = 8
TOP_K = 2
RMS_EPS = 1e-6
NEG_INF = -1e30

ROW_TILE = 512
MLA_TQ = 256
MLA_TK = 256
MLA_UNROLL = 32
MLA_LOOKAHEAD = 4
MLA_HEADS_PER_STEP = 6
MLA_DEN_ROWS = 16
MLA_Q_PRESCALE = (QK_NOPE + QK_ROPE) ** -0.5 * math.log2(math.e)
BAND_TQ = 128
BAND_TILES_PER_STEP = 4
BAND_LOOKAHEAD = 6
NA_TILE_ROWS = 2
NA_KEY_ROWS = 10
NA_TILES_PER_STEP = 4
NA_LOOKAHEAD = 16
MOE_TM = 512
MOE_TF = 1792
COMBINE_TM = 256
GATHER_UNROLL = 8


def _cparams(semantics):
    return pltpu.CompilerParams(dimension_semantics=semantics, vmem_limit_bytes=V7X_VMEM_LIMIT_BYTES)


def _rms(x, g):
    return x * lax.rsqrt(jnp.mean(x * x, axis=-1, keepdims=True) + RMS_EPS) * g


def _dot(a, b):
    return jnp.dot(a, b, preferred_element_type=F32)


def _dot_nt(a, b):
    return lax.dot_general(a, b, (((1,), (1,)), ((), ())), preferred_element_type=F32)


def _proj_kernel(x_ref, g_ref, wa_ref, wc_ref, wb_ref, gq_ref, gkv_ref, wq1_ref, wq2_ref, wk_ref, wvt_ref,
                 ct_ref, st_ref, pa_ref, pa4_ref, pa16_ref, pc_ref, qm_ref, km_ref, vt_ref, pa_scr):
    tm = x_ref.shape[0]
    h = _rms(x_ref[...], g_ref[...]).astype(BF16)
    pa = _dot(h, wa_ref[...])
    pa_ref[...] = pa.astype(BF16)
    n_chunks = pa.shape[1] // LANES
    for c in range(n_chunks):
        pa_scr[c] = pa[:, c * LANES:(c + 1) * LANES]
    for dil, view_ref in ((DILATIONS[1], pa4_ref), (DILATIONS[2], pa16_ref)):
        for r in range(dil):
            for c in range(n_chunks):
                col = r * 3 * WIDTH_A + c * LANES
                view_ref[:, col:col + LANES] = pa_scr[c, pl.ds(r, tm // dil, stride=dil), :].astype(BF16)
    pc_ref[...] = _dot(h, wc_ref[...]).astype(BF16)
    pb = _dot(h, wb_ref[...])
    hq = _rms(pb[:, :Q_LORA], gq_ref[...]).astype(BF16)
    hkv = _rms(pb[:, Q_LORA:Q_LORA + KV_LORA], gkv_ref[...]).astype(BF16)
    r1 = pb[:, Q_LORA + KV_LORA:Q_LORA + KV_LORA + LANES]
    r2 = pb[:, Q_LORA + KV_LORA + LANES:]
    ct = ct_ref[...]
    st = st_ref[...]
    qa = _dot(hq, wq1_ref[...])
    qb = _dot(hq, wq2_ref[...])
    kn = _dot(hkv, wk_ref[...])
    kr = r1 * ct + r2 * st
    for hd in range(N_HEADS_B):
        sl = slice(hd * LANES, (hd + 1) * LANES)
        qm_ref[:, sl] = ((qa[:, sl] * ct + qb[:, sl] * st) * MLA_Q_PRESCALE).astype(BF16)
        km_ref[:, sl] = (kn[:, sl] + kr).astype(BF16)
    vt_ref[...] = _dot_nt(wvt_ref[...], hkv).astype(BF16)


def _proj_call(x, g, wa, wc, wb, gq, gkv, wq1, wq2, wk, wvt, ctab, stab, seq):
    n, d = x.shape
    tm = ROW_TILE
    tiles_per_seq = seq // tm
    full = lambda a: pl.BlockSpec(a.shape, lambda i: (0,) * a.ndim)
    row = lambda w: pl.BlockSpec((tm, w), lambda i: (i, 0))
    tab = pl.BlockSpec((tm, LANES), lambda i: (i % tiles_per_seq, 0))
    view = lambda dil: pl.BlockSpec((tm // dil, dil * 3 * WIDTH_A), lambda i: (i, 0))
    hb = N_HEADS_B * LANES
    return pl.pallas_call(
        _proj_kernel,
        grid=(n // tm,),
        in_specs=[row(d), full(g), full(wa), full(wc), full(wb), full(gq), full(gkv), full(wq1), full(wq2),
                  full(wk), full(wvt), tab, tab],
        out_specs=[row(3 * WIDTH_A)] + [view(dil) for dil in DILATIONS[1:]] + [row(3 * WIDTH_C), row(hb), row(hb),
                   pl.BlockSpec((WIDTH_B, tm), lambda i: (0, i))],
        out_shape=[jax.ShapeDtypeStruct((n, 3 * WIDTH_A), BF16)]
                  + [jax.ShapeDtypeStruct((n // dil, dil * 3 * WIDTH_A), BF16) for dil in DILATIONS[1:]]
                  + [jax.ShapeDtypeStruct((n, 3 * WIDTH_C), BF16),
                   jax.ShapeDtypeStruct((n, hb), BF16), jax.ShapeDtypeStruct((n, hb), BF16),
                   jax.ShapeDtypeStruct((WIDTH_B, n), BF16)],
        scratch_shapes=[pltpu.VMEM((3 * WIDTH_A // LANES, tm, LANES), F32)],
        compiler_params=_cparams(("parallel",)),
        name="proj",
    )(x, g, wa, wc, wb, gq, gkv, wq1, wq2, wk, wvt, ctab, stab)


def _run_pipeline(n_items, look, issue, consume, finish):
    if look >= n_items:
        results = [consume(i, s) for i, s in enumerate([issue(i) for i in range(n_items)])]
        for i, res in enumerate(results):
            finish(i, res)
        return
    inflight = {i: issue(i) for i in range(min(look, n_items))}
    pending = None
    for i in range(n_items):
        if i + look < n_items:
            inflight[i + look] = issue(i + look)
        if pending is not None:
            finish(*pending)
        pending = (i, consume(i, inflight.pop(i)))
    finish(*pending)


class _HeadPairs:
    def __init__(self, q_ref, k_ref, v_ref, tq, tkw):
        self.q_ref, self.k_ref, self.v_ref, self.tq, self.tkw = q_ref, k_ref, v_ref, tq, tkw
        lane = lax.broadcasted_iota(jnp.int32, (tq, LANES), 1)
        self.low = lane < HEAD_DIM
        self.ones = jnp.ones((tkw, LANES), BF16)

    @staticmethod
    def cols(h):
        return slice((h // 2) * LANES, (h // 2 + 1) * LANES)

    def scores(self, q_start, k_start, h):
        q = self.q_ref[q_start:q_start + self.tq, self.cols(h)]
        q = jnp.where(self.low if h % 2 == 0 else ~self.low, q, jnp.zeros_like(q))
        return _dot_nt(q, self.k_ref[pl.ds(k_start, self.tkw), self.cols(h)])

    def values(self, k_start, h, p):
        v = jnp.concatenate([self.v_ref[pl.ds(k_start, self.tkw), self.cols(h)], self.ones], axis=-1)
        o = _dot(p, v)
        return o[:, :LANES], o[:, LANES:]

    def merge(self, even, odd):
        return jnp.where(self.low, even, odd)


def _band_tables(dilation):
    tq, tkw = BAND_TQ, BAND_TQ + 2 * BAND_HALF
    shift = np.array([0, -BAND_HALF, -2 * BAND_HALF])
    rel = shift[:, None, None] + np.arange(tkw)[None, None, :] - np.arange(tq)[None, :, None]
    dist = np.abs(rel)
    slopes = 2.0 ** (-8.0 * np.arange(1, N_HEADS_A + 1) / N_HEADS_A)
    bias = -(slopes[None, :, None, None] * dilation) * dist[:, None].astype(np.float64)
    tab = np.where(dist[:, None] <= BAND_HALF, bias, NEG_INF)
    return jnp.asarray(tab, F32)


def _band_kernel(q_ref, k_ref, v_ref, tab_ref, o_ref, lse_ref, *, n):
    tq = BAND_TQ
    tkw = tq + 2 * BAND_HALF
    tiles = q_ref.shape[0] // tq
    last_tile = n // tq - 1
    items = [(t, h) for t in range(tiles) for h in range(N_HEADS_A)]
    info = []
    for t in range(tiles):
        i = pl.program_id(2) * tiles + t
        start = pl.multiple_of(jnp.clip(i * tq - BAND_HALF, 0, n - tkw), BAND_HALF)
        variant = jnp.minimum(i, 1) + (i == last_tile).astype(jnp.int32)
        info.append((start, variant))
    pairs = _HeadPairs(q_ref, k_ref, v_ref, tq, tkw)
    held = {}

    def issue(idx):
        t, h = items[idx]
        return pairs.scores(t * tq, info[t][0], h)

    def consume(idx, s):
        t, h = items[idx]
        s = s + tab_ref[info[t][1], h]
        m = jnp.max(s, axis=-1, keepdims=True)
        return jnp.exp(s - m).astype(BF16), m

    def finish(idx, res):
        t, h = items[idx]
        p, m = res
        out, den = pairs.values(info[t][0], h, p)
        held[h % 2] = (out / den, m + jnp.log(den))
        if h % 2 == 1:
            cols = pairs.cols(h)
            o_ref[t * tq:(t + 1) * tq, cols] = pairs.merge(held[0][0], held[1][0])
            lse_ref[t * tq:(t + 1) * tq, cols] = pairs.merge(held[0][1], held[1][1])

    _run_pipeline(len(items), BAND_LOOKAHEAD, issue, consume, finish)


def _band_call(pa_view, dilation):
    b, n, _ = pa_view.shape
    rows = BAND_TQ * BAND_TILES_PER_STEP
    w = WIDTH_A
    tab = _band_tables(dilation)
    qspec = pl.BlockSpec((None, rows, w), lambda bb, r, i: (bb, i, 3 * r))
    kspec = pl.BlockSpec((None, n, w), lambda bb, r, i: (bb, 0, 3 * r + 1))
    vspec = pl.BlockSpec((None, n, w), lambda bb, r, i: (bb, 0, 3 * r + 2))
    tspec = pl.BlockSpec(tab.shape, lambda bb, r, i: (0, 0, 0, 0))
    ospec = pl.BlockSpec((None, rows, w), lambda bb, r, i: (bb, i, r))
    shape = jax.ShapeDtypeStruct((b, n, dilation * w), F32)
    return pl.pallas_call(
        functools.partial(_band_kernel, n=n),
        grid=(b, dilation, n // rows),
        in_specs=[qspec, kspec, vspec, tspec],
        out_specs=[ospec, ospec],
        out_shape=[shape, shape],
        compiler_params=_cparams(("parallel", "parallel", "parallel")),
        name=f"band_d{dilation}",
    )(pa_view, pa_view, pa_view, tab)


def _mla_kernel(q_ref, k_ref, vt_ref, o_ref):
    tq = q_ref.shape[0]
    seq = k_ref.shape[0]
    tk = MLA_TK
    nh = MLA_HEADS_PER_STEP
    n_chunks = seq // tk
    items = [(u, h) for u in range(MLA_UNROLL) for h in range(nh)]
    look = MLA_LOOKAHEAD
    ones_rows = jnp.ones((MLA_DEN_ROWS, tk), BF16)

    def key_slice(chunk):
        return pl.ds(pl.multiple_of(chunk * tk, tk), tk)

    def score_matmul(chunk, h):
        k = k_ref[key_slice(chunk), h * LANES:(h + 1) * LANES]
        return _dot_nt(k, q_ref[:, h * LANES:(h + 1) * LANES])

    def value_matmul(chunk, h, p):
        vt = jnp.concatenate([vt_ref[h * V_HEAD:(h + 1) * V_HEAD, key_slice(chunk)], ones_rows], axis=0)
        return _dot(vt, p)

    def body(j, carry):
        state = list(carry[:2 * nh])
        scores = dict(zip(items[:look], carry[2 * nh:2 * nh + look]))
        pend_p, pend_alpha = carry[2 * nh + look:]
        pending = (jnp.maximum(j * MLA_UNROLL - 1, 0), nh - 1, pend_p, pend_alpha)
        ahead = []
        for idx, (u, h) in enumerate(items):
            la = idx + look
            if la < len(items):
                lu, lh = items[la]
                scores[lu, lh] = score_matmul(j * MLA_UNROLL + lu, lh)
            else:
                lu, lh = items[la - len(items)]
                ahead.append(score_matmul(jnp.minimum((j + 1) * MLA_UNROLL + lu, n_chunks - 1), lh))
            pc, ph, pp, pa = pending
            state[2 * ph + 1] = pa * state[2 * ph + 1] + value_matmul(pc, ph, pp)
            s = scores.pop((u, h))
            m_new = jnp.maximum(state[2 * h], jnp.max(s, axis=0, keepdims=True))
            alpha = jnp.exp2(state[2 * h] - m_new)
            state[2 * h] = m_new
            pending = (j * MLA_UNROLL + u, h, jnp.exp2(s - m_new).astype(BF16), alpha)
        return tuple(state) + tuple(ahead) + (pending[2], pending[3])

    init = (jnp.full((1, tq), NEG_INF, F32), jnp.zeros((V_HEAD + MLA_DEN_ROWS, tq), F32)) * nh
    init += tuple(score_matmul(u, h) for u, h in items[:look])
    init += (jnp.zeros((tk, tq), BF16), jnp.ones((1, tq), F32))
    res = lax.fori_loop(0, n_chunks // MLA_UNROLL, body, init)
    accs = [res[2 * h + 1] for h in range(nh)]
    accs[nh - 1] = res[-1] * accs[nh - 1] + value_matmul(n_chunks - 1, nh - 1, res[-2])
    out_t = jnp.concatenate([a[:V_HEAD] / a[V_HEAD:V_HEAD + 1] for a in accs], axis=0)
    o_ref[...] = out_t.T


def _mla_call(qm, km, vt, batch, seq):
    tq = MLA_TQ
    nh = MLA_HEADS_PER_STEP
    qspec = pl.BlockSpec((None, tq, nh * LANES), lambda b, g, i: (b, i, g))
    kspec = pl.BlockSpec((None, seq, nh * LANES), lambda b, g, i: (b, 0, g))
    vspec = pl.BlockSpec((nh * V_HEAD, seq), lambda b, g, i: (g, b))
    ospec = pl.BlockSpec((None, tq, nh * V_HEAD), lambda b, g, i: (b, i, g))
    return pl.pallas_call(
        _mla_kernel,
        grid=(batch, N_HEADS_B // nh, seq // tq),
        in_specs=[qspec, kspec, vspec],
        out_specs=ospec,
        out_shape=jax.ShapeDtypeStruct((batch, seq, WIDTH_B), F32),
        compiler_params=_cparams(("parallel", "parallel", "parallel")),
        name="mla",
    )(qm.reshape(batch, seq, -1), km.reshape(batch, seq, -1), vt)


def _na_variant(i, n_tiles):
    return jnp.minimum(i, 2) + jnp.maximum(i - (n_tiles - 3), 0)


def _na_kernel(q_ref, k_ref, v_ref, tab_ref, o_ref, *, rows):
    tq = NA_TILE_ROWS * GRID_W
    tkw = NA_KEY_ROWS * GRID_W
    tiles = q_ref.shape[0] // tq
    n_tiles = rows // NA_TILE_ROWS
    items = [(t, h) for t in range(tiles) for h in range(N_HEADS_C)]
    info = []
    for t in range(tiles):
        i = pl.program_id(1) * tiles + t
        base = jnp.clip(i * NA_TILE_ROWS - NA_ROWS // 2, 0, rows - NA_KEY_ROWS)
        info.append((pl.multiple_of(base * GRID_W, GRID_W), _na_variant(i, n_tiles)))
    pairs = _HeadPairs(q_ref, k_ref, v_ref, tq, tkw)
    held = {}

    def issue(idx):
        t, h = items[idx]
        return pairs.scores(t * tq, info[t][0], h)

    def consume(idx, s):
        t, h = items[idx]
        s = s + tab_ref[info[t][1], h]
        return jnp.exp(s - jnp.max(s, axis=-1, keepdims=True)).astype(BF16)

    def finish(idx, p):
        t, h = items[idx]
        out, den = pairs.values(info[t][0], h, p)
        held[h % 2] = out / den
        if h % 2 == 1:
            o_ref[t * tq:(t + 1) * tq, pairs.cols(h)] = pairs.merge(held[0], held[1])

    _run_pipeline(len(items), NA_LOOKAHEAD, issue, consume, finish)


def _na_tables(rpb, rows):
    r0 = np.array([0, 2, 4, rows - 4, rows - 2])
    base = np.clip(r0 - NA_ROWS // 2, 0, rows - NA_KEY_ROWS)
    r = r0[:, None] + np.arange(NA_TILE_ROWS)[None, :]
    row_start = np.clip(r - NA_ROWS // 2, 0, rows - NA_ROWS)
    krow = base[:, None] + np.arange(NA_KEY_ROWS)[None, :]
    drow = krow[:, None, :] - r[:, :, None]
    row_ok = (krow[:, None, :] >= row_start[:, :, None]) & (krow[:, None, :] < row_start[:, :, None] + NA_ROWS)
    c = np.arange(GRID_W)
    win_start = np.clip(c - NA_COLS // 2, 0, GRID_W - NA_COLS)
    col_ok = (c[None, :] >= win_start[:, None]) & (c[None, :] < win_start[:, None] + NA_COLS)
    dcol = np.clip(c[None, :] - c[:, None], -(NA_COLS - 1), NA_COLS - 1)
    ok = row_ok[:, :, None, :, None] & col_ok[None, None, :, None, :]
    di = np.clip(drow, -(NA_ROWS - 1), NA_ROWS - 1) + (NA_ROWS - 1)
    pick_col = (dcol[:, :, None] + NA_COLS - 1 == np.arange(2 * NA_COLS - 1)).astype(np.float32)
    pick_row = (di[..., None] == np.arange(2 * NA_ROWS - 1)).astype(np.float32)
    hi = lax.Precision.HIGHEST
    toeplitz = jnp.einsum("hab,cjb->hacj", rpb.astype(F32), pick_col, precision=hi)
    bias = jnp.einsum("vqka,hacj->vhqckj", pick_row, toeplitz, precision=hi)
    tab = jnp.where(jnp.asarray(ok)[:, None], bias, NEG_INF)
    return tab.reshape(5, N_HEADS_C, NA_TILE_ROWS * GRID_W, NA_KEY_ROWS * GRID_W)


def _na_call(pc, tab, batch, seq):
    rows = seq // GRID_W
    tq = NA_TILE_ROWS * GRID_W * NA_TILES_PER_STEP
    w = WIDTH_C
    qspec = pl.BlockSpec((None, tq, w), lambda b, i: (b, i, 0))
    kspec = pl.BlockSpec((None, seq, w), lambda b, i: (b, 0, 1))
    vspec = pl.BlockSpec((None, seq, w), lambda b, i: (b, 0, 2))
    tspec = pl.BlockSpec(tab.shape, lambda b, i: (0, 0, 0, 0))
    ospec = pl.BlockSpec((None, tq, w), lambda b, i: (b, i, 0))
    pc3 = pc.reshape(batch, seq, 3 * w)
    return pl.pallas_call(
        functools.partial(_na_kernel, rows=rows),
        grid=(batch, seq // tq),
        in_specs=[qspec, kspec, vspec, tspec],
        out_specs=ospec,
        out_shape=jax.ShapeDtypeStruct((batch, seq, w), F32),
        compiler_params=_cparams(("parallel", "parallel")),
        name="natten",
    )(pc3, pc3, pc3, tab)


def _mix_out_kernel(x_ref, o1_ref, o2_ref, o3_ref, l1_ref, l2_ref, l3_ref, ob_ref, oc_ref,
                    ga_ref, gb_ref, gc_ref, wo_ref, out_ref, o2_scr, o3_scr, l2_scr, l3_scr):
    tm = x_ref.shape[0]
    n_chunks = WIDTH_A // LANES

    def natural(view_ref, scr, dil):
        for r in range(dil):
            for c in range(n_chunks):
                col = r * WIDTH_A + c * LANES
                scr[c, pl.ds(r, tm // dil, stride=dil), :] = view_ref[:, col:col + LANES]
        return jnp.concatenate([scr[c] for c in range(n_chunks)], axis=-1)

    l1, o1 = l1_ref[...], o1_ref[...]
    l2, o2 = natural(l2_ref, l2_scr, DILATIONS[1]), natural(o2_ref, o2_scr, DILATIONS[1])
    l3, o3 = natural(l3_ref, l3_scr, DILATIONS[2]), natural(o3_ref, o3_scr, DILATIONS[2])
    mx = jnp.maximum(jnp.maximum(l1, l2), l3)
    w1, w2, w3 = jnp.exp(l1 - mx), jnp.exp(l2 - mx), jnp.exp(l3 - mx)
    oa = (w1 * o1 + w2 * o2 + w3 * o3) / (w1 + w2 + w3)
    ya = _rms(oa, ga_ref[...]).astype(BF16)
    yb = _rms(ob_ref[...], gb_ref[...]).astype(BF16)
    yc = _rms(oc_ref[...], gc_ref[...]).astype(BF16)
    y = _dot(ya, wo_ref[:WIDTH_A, :])
    y = y + _dot(yb, wo_ref[WIDTH_A:WIDTH_A + WIDTH_B, :])
    y = y + _dot(yc, wo_ref[WIDTH_A + WIDTH_B:, :])
    out_ref[...] = x_ref[...] + y


def _mix_out_call(x, o_parts, lse_parts, ob, oc, ga, gb, gc, wo):
    n, d = x.shape
    tm = ROW_TILE
    full = lambda a: pl.BlockSpec(a.shape, lambda i: (0,) * a.ndim)
    row = lambda w: pl.BlockSpec((tm, w), lambda i: (i, 0))
    view = lambda dil: pl.BlockSpec((tm // dil, dil * WIDTH_A), lambda i: (i, 0))
    return pl.pallas_call(
        _mix_out_kernel,
        grid=(n // tm,),
        in_specs=[row(d)] + [view(dil) for dil in DILATIONS] * 2
                 + [row(WIDTH_B), row(WIDTH_C), full(ga), full(gb), full(gc), full(wo)],
        out_specs=row(d),
        out_shape=jax.ShapeDtypeStruct((n, d), F32),
        scratch_shapes=[pltpu.VMEM((WIDTH_A // LANES, tm, LANES), F32)] * 4,
        compiler_params=_cparams(("parallel",)),
        name="mix_out",
    )(x, *o_parts, *lse_parts, ob, oc, ga, gb, gc, wo)


def _silu(u):
    return u * (1.0 / (1.0 + jnp.exp(-u)))


def _ffn_kernel(x_ref, g_ref, w1_ref, w3_ref, w2_ref, out_ref):
    x = x_ref[...]
    h = _rms(x, g_ref[...]).astype(BF16)
    a = (_silu(_dot(h, w1_ref[...])) * _dot(h, w3_ref[...])).astype(BF16)
    out_ref[...] = x + _dot(a, w2_ref[...])


def _ffn_call(x, g, w1, w3, w2):
    n, d = x.shape
    tm = ROW_TILE
    resident = lambda a: pl.BlockSpec(a.shape, lambda i: (0, 0), pipeline_mode=pl.Buffered(1))
    return pl.pallas_call(
        _ffn_kernel,
        grid=(n // tm,),
        in_specs=[pl.BlockSpec((tm, d), lambda i: (i, 0)), pl.BlockSpec((1, d), lambda i: (0, 0)),
                  resident(w1), resident(w3), resident(w2)],
        out_specs=pl.BlockSpec((tm, d), lambda i: (i, 0)),
        out_shape=jax.ShapeDtypeStruct((n, d), F32),
        compiler_params=_cparams(("parallel",)),
        name="ffn",
    )(x, g, w1, w3, w2)


def _router_kernel(x_ref, g_ref, wr_ref, h_ref, e_ref, gate_ref):
    h = _rms(x_ref[...], g_ref[...])
    _to_token_tiles(h_ref, h)
    logits = jnp.dot(h, wr_ref[...], precision=lax.Precision.HIGHEST, preferred_element_type=F32)
    lane = lax.broadcasted_iota(jnp.int32, logits.shape, 1).astype(F32)
    logits = jnp.where(lane < N_EXPERTS, logits, -jnp.inf)
    m1 = jnp.max(logits, axis=-1, keepdims=True)
    i1 = jnp.min(jnp.where(logits == m1, lane, float(LANES)), axis=-1, keepdims=True)
    rest = jnp.where(lane == i1, -jnp.inf, logits)
    m2 = jnp.max(rest, axis=-1, keepdims=True)
    i2 = jnp.min(jnp.where(rest == m2, lane, float(LANES)), axis=-1, keepdims=True)
    e = jnp.exp(m2 - m1)
    den = 1.0 + e
    e_ref[...] = jnp.where(lane == 0.0, i1, jnp.where(lane == 1.0, i2, 0.0)).astype(jnp.int32)
    gate_ref[...] = jnp.where(lane == 0.0, 1.0 / den, jnp.where(lane == 1.0, e / den, 0.0))


def _router_call(x, g, wr_pad):
    n, d = x.shape
    tm = ROW_TILE
    row = lambda w: pl.BlockSpec((tm, w), lambda i: (i, 0))
    full = lambda a: pl.BlockSpec(a.shape, lambda i: (0,) * a.ndim)
    return pl.pallas_call(
        _router_kernel,
        grid=(n // tm,),
        in_specs=[row(d), full(g), full(wr_pad)],
        out_specs=[pl.BlockSpec((tm * TOKEN_ROWS, LANES), lambda i: (i, 0)), row(LANES), row(LANES)],
        out_shape=[jax.ShapeDtypeStruct((n * TOKEN_ROWS, LANES), F32), jax.ShapeDtypeStruct((n, LANES), jnp.int32),
                   jax.ShapeDtypeStruct((n, LANES), F32)],
        compiler_params=_cparams(("parallel",)),
        name="router",
    )(x, g, wr_pad)


TOKEN_ROWS = 8


def _to_token_tiles(dst_ref, x):
    tm, d = x.shape
    assert d == TOKEN_ROWS * LANES
    for j in range(TOKEN_ROWS):
        dst_ref[pl.ds(j, tm, stride=TOKEN_ROWS), :] = x[:, j * LANES:(j + 1) * LANES]


def _from_token_tiles(src_ref, tm):
    return jnp.concatenate([src_ref[pl.ds(j, tm, stride=TOKEN_ROWS), :] for j in range(TOKEN_ROWS)], axis=-1)


def _row_copy(src_hbm, row, dst_ref, r, sem):
    src = src_hbm.at[pl.ds(pl.multiple_of(row * TOKEN_ROWS, TOKEN_ROWS), TOKEN_ROWS)]
    return pltpu.make_async_copy(src, dst_ref.at[pl.ds(pl.multiple_of(r * TOKEN_ROWS, TOKEN_ROWS), TOKEN_ROWS)], sem)


def _start_row_gather(idx_ref, base, src_hbm, dst_ref, sem, count):
    def body(r, c):
        _row_copy(src_hbm, idx_ref[base + r], dst_ref, r, sem).start()
        return c
    lax.fori_loop(0, count, body, 0, unroll=GATHER_UNROLL)


def _wait_row_gather(src_hbm, dst_ref, sem, count):
    pltpu.make_async_copy(src_hbm.at[pl.ds(0, count * TOKEN_ROWS)], dst_ref, sem).wait()


def _moe_ffn_kernel(tile_e_ref, tile_ok_ref, row_tok_ref, h_hbm, w1_ref, w3_ref, w2_ref, out_ref,
                    xg_ref, hb_ref, acc_ref, sem):
    i = pl.program_id(0)
    f = pl.program_id(1)
    n_tiles = pl.num_programs(0)
    tm = MOE_TM
    slot = i % 2

    @pl.when(f == 0)
    def _():
        @pl.when(i == 0)
        def _():
            _start_row_gather(row_tok_ref, 0, h_hbm, xg_ref.at[0], sem.at[0], tm)

        _wait_row_gather(h_hbm, xg_ref.at[slot], sem.at[slot], tm)

        @pl.when(i + 1 < n_tiles)
        def _():
            _start_row_gather(row_tok_ref, (i + 1) * tm, h_hbm, xg_ref.at[1 - slot], sem.at[1 - slot], tm)

        hb_ref[...] = _from_token_tiles(xg_ref.at[slot], tm).astype(BF16)
        acc_ref[...] = jnp.zeros_like(acc_ref)

    @pl.when(tile_ok_ref[i] != 0)
    def _():
        h = hb_ref[...]
        a = (_silu(_dot(h, w1_ref[...])) * _dot(h, w3_ref[...])).astype(BF16)
        acc_ref[...] += _dot(a, w2_ref[...])

    @pl.when(f == pl.num_programs(1) - 1)
    def _():
        _to_token_tiles(out_ref, acc_ref[...])


def _moe_ffn_call(tile_e, tile_ok, row_tok, h, w1, w3, w2, layer):
    d = w1.shape[2]
    n_tiles = tile_e.shape[0]
    ff = w1.shape[3]
    tm, tf = MOE_TM, MOE_TF
    grid_spec = pltpu.PrefetchScalarGridSpec(
        num_scalar_prefetch=3,
        grid=(n_tiles, ff // tf),
        in_specs=[pl.BlockSpec(memory_space=pl.ANY),
                  pl.BlockSpec((None, None, d, tf), lambda i, f, te, tv, rt: (layer, te[i], 0, f)),
                  pl.BlockSpec((None, None, d, tf), lambda i, f, te, tv, rt: (layer, te[i], 0, f)),
                  pl.BlockSpec((None, None, tf, d), lambda i, f, te, tv, rt: (layer, te[i], f, 0))],
        out_specs=pl.BlockSpec((tm * TOKEN_ROWS, LANES), lambda i, f, te, tv, rt: (i, 0)),
        scratch_shapes=[pltpu.VMEM((2, tm * TOKEN_ROWS, LANES), F32), pltpu.VMEM((tm, d), BF16),
                        pltpu.VMEM((tm, d), F32), pltpu.SemaphoreType.DMA((2,))],
    )
    return pl.pallas_call(
        _moe_ffn_kernel,
        grid_spec=grid_spec,
        out_shape=jax.ShapeDtypeStruct((n_tiles * tm * TOKEN_ROWS, LANES), F32),
        compiler_params=_cparams(("arbitrary", "arbitrary")),
        name="moe_ffn",
    )(tile_e, tile_ok, row_tok, h, w1, w3, w2)


def _combine_kernel(d0_ref, d1_ref, y_hbm, x_ref, gate_ref, gain_ref, out_ref, b0_ref, b1_ref, sem, *, final_norm):
    i = pl.program_id(0)
    n_tiles = pl.num_programs(0)
    tm = COMBINE_TM
    slot = i % 2

    def start(tile, s):
        _start_row_gather(d0_ref, tile * tm, y_hbm, b0_ref.at[s], sem.at[0, s], tm)
        _start_row_gather(d1_ref, tile * tm, y_hbm, b1_ref.at[s], sem.at[1, s], tm)

    @pl.when(i == 0)
    def _():
        start(0, 0)

    _wait_row_gather(y_hbm, b0_ref.at[slot], sem.at[0, slot], tm)
    _wait_row_gather(y_hbm, b1_ref.at[slot], sem.at[1, slot], tm)

    @pl.when(i + 1 < n_tiles)
    def _():
        start(i + 1, 1 - slot)

    gates = gate_ref[...]
    y0 = _from_token_tiles(b0_ref.at[slot], tm)
    y1 = _from_token_tiles(b1_ref.at[slot], tm)
    out = x_ref[...] + (gates[:, 0:1] * y0 + gates[:, 1:2] * y1)
    out_ref[...] = _rms(out, gain_ref[...]) if final_norm else out


def _combine_call(d0, d1, y, x, gates, gain, final_norm):
    n, d = x.shape
    tm = COMBINE_TM
    grid_spec = pltpu.PrefetchScalarGridSpec(
        num_scalar_prefetch=2,
        grid=(n // tm,),
        in_specs=[pl.BlockSpec(memory_space=pl.ANY),
                  pl.BlockSpec((tm, d), lambda i, a, b: (i, 0)),
                  pl.BlockSpec((tm, LANES), lambda i, a, b: (i, 0)),
                  pl.BlockSpec((1, d), lambda i, a, b: (0, 0))],
        out_specs=pl.BlockSpec((tm, d), lambda i, a, b: (i, 0)),
        scratch_shapes=[pltpu.VMEM((2, tm * TOKEN_ROWS, LANES), F32)] * 2 + [pltpu.SemaphoreType.DMA((2, 2))],
    )
    return pl.pallas_call(
        functools.partial(_combine_kernel, final_norm=final_norm),
        grid_spec=grid_spec,
        out_shape=jax.ShapeDtypeStruct((n, d), F32),
        compiler_params=_cparams(("arbitrary",)),
        name="moe_combine",
    )(d0, d1, y, x, gates, gain)


def _moe_plan(top_e, n_tiles):
    n_assign = top_e.shape[0] * TOP_K
    flat_e = top_e.reshape(n_assign)
    onehot = (flat_e[:, None] == jnp.arange(N_EXPERTS, dtype=jnp.int32)[None, :]).astype(jnp.int32)
    csum = jnp.cumsum(onehot, axis=0)
    rank = jnp.take_along_axis(csum, flat_e[:, None], axis=1)[:, 0] - 1
    counts = csum[-1]
    padded = (counts + MOE_TM - 1) // MOE_TM * MOE_TM
    pend = jnp.cumsum(padded)
    pstart = pend - padded
    dest = pstart[flat_e] + rank
    tile_start = jnp.arange(n_tiles, dtype=jnp.int32) * MOE_TM
    tile_e = jnp.minimum(jnp.searchsorted(pend, tile_start, side="right"), N_EXPERTS - 1).astype(jnp.int32)
    tile_ok = (tile_start < pend[-1]).astype(jnp.int32)
    order = jnp.argsort(flat_e, stable=True).astype(jnp.int32)
    row_e = jnp.repeat(tile_e, MOE_TM)
    r = jnp.arange(n_tiles * MOE_TM, dtype=jnp.int32) - pstart[row_e]
    src = jnp.clip((jnp.cumsum(counts) - counts)[row_e] + r, 0, n_assign - 1)
    row_tok = jnp.where(r < counts[row_e], order[src] // TOP_K, 0).astype(jnp.int32)
    dest = dest.reshape(-1, TOP_K).astype(jnp.int32)
    return tile_e, tile_ok, row_tok, dest[:, 0], dest[:, 1]


def _moe_layer(x, g, wr_pad, w1, w3, w2, layer, gain, final_norm):
    n = x.shape[0]
    h, top_e, gates = _router_call(x, g, wr_pad)
    n_tiles = -(-(n * TOP_K + N_EXPERTS * (MOE_TM - 1)) // MOE_TM)
    tile_e, tile_ok, row_tok, d0, d1 = _moe_plan(top_e[:, :TOP_K], n_tiles)
    y = _moe_ffn_call(tile_e, tile_ok, row_tok, h, w1, w3, w2, layer)
    return _combine_call(d0, d1, y, x, gates, gain, final_norm)


def _norm_kernel(x_ref, g_ref, o_ref):
    o_ref[...] = _rms(x_ref[...], g_ref[...])


def _norm_call(x, g):
    n, d = x.shape
    tm = ROW_TILE
    return pl.pallas_call(
        _norm_kernel,
        grid=(n // tm,),
        in_specs=[pl.BlockSpec((tm, d), lambda i: (i, 0)), pl.BlockSpec((1, d), lambda i: (0, 0))],
        out_specs=pl.BlockSpec((tm, d), lambda i: (i, 0)),
        out_shape=jax.ShapeDtypeStruct((n, d), F32),
        compiler_params=_cparams(("parallel",)),
        name="final_norm",
    )(x, g)


def _rope_tables(seq):
    inv = ROPE_BASE ** (-jnp.arange(0, QK_ROPE, 2, dtype=F32) / QK_ROPE)
    ang = jnp.arange(seq)[:, None].astype(F32) * inv[None, :]
    cos, sin = jnp.cos(ang), jnp.sin(ang)
    pad = LANES - QK_NOPE - QK_ROPE
    ctab = jnp.concatenate([jnp.ones((seq, QK_NOPE), F32), cos, cos, jnp.zeros((seq, pad), F32)], axis=1)
    stab = jnp.concatenate([jnp.zeros((seq, QK_NOPE), F32), sin, sin, jnp.zeros((seq, pad), F32)], axis=1)
    return ctab, stab


def _split_w_in(w_in):
    d = w_in.shape[0]
    bounds = np.cumsum([WIDTH_A, WIDTH_A, WIDTH_A, Q_LORA, KV_LORA, QK_ROPE, WIDTH_C, WIDTH_C])
    qa, ka, va, cq, ckv, kr, qc, kc, vc = jnp.split(w_in, bounds.tolist(), axis=1)
    half = QK_ROPE // 2
    z_lo = jnp.zeros((d, QK_NOPE), w_in.dtype)
    z_hi = jnp.zeros((d, LANES - QK_NOPE - QK_ROPE), w_in.dtype)
    rope_blk = jnp.concatenate([z_lo, kr, z_hi], axis=1)
    swap_blk = jnp.concatenate([z_lo, -kr[:, half:], kr[:, :half], z_hi], axis=1)
    score_scale = HEAD_DIM ** -0.5
    wa = jnp.concatenate([qa * score_scale, ka, va], axis=1).astype(BF16)
    wc = jnp.concatenate([qc * score_scale, kc, vc], axis=1).astype(BF16)
    wb = jnp.concatenate([cq, ckv, rope_blk, swap_blk], axis=1).astype(BF16)
    return wa, wc, wb


def _split_w_uq(w_uq):
    r = w_uq.shape[0]
    w = w_uq.reshape(r, N_HEADS_B, QK_NOPE + QK_ROPE)
    nope, rope = w[..., :QK_NOPE], w[..., QK_NOPE:]
    half = QK_ROPE // 2
    z_hi = jnp.zeros((r, N_HEADS_B, LANES - QK_NOPE - QK_ROPE), w_uq.dtype)
    w1 = jnp.concatenate([nope, rope, z_hi], axis=-1)
    w2 = jnp.concatenate([jnp.zeros_like(nope), -rope[..., half:], rope[..., :half], z_hi], axis=-1)
    return w1.reshape(r, -1).astype(BF16), w2.reshape(r, -1).astype(BF16)


def _split_w_ukv(w_ukv):
    r = w_ukv.shape[0]
    w = w_ukv.reshape(r, N_HEADS_B, QK_NOPE + V_HEAD)
    k_nope, v = w[..., :QK_NOPE], w[..., QK_NOPE:]
    wk = jnp.concatenate([k_nope, jnp.zeros((r, N_HEADS_B, LANES - QK_NOPE), w_ukv.dtype)], axis=-1)
    return wk.reshape(r, -1).astype(BF16), v.reshape(r, -1).T.astype(BF16)


def kernel(x, g_mix, w_in, g_q, g_kv, w_uq, w_ukv, rpb, g_out_a, g_out_b, g_out_c, w_o, g_ffn, w1, w3, w2,
           w_router, e_w1, e_w3, e_w2, g_final):
    batch, seq, d = x.shape
    n = batch * seq
    depth = g_mix.shape[0]
    rows = seq // GRID_W
    ctab, stab = _rope_tables(seq)
    ew1, ew3, ew2 = e_w1.astype(BF16), e_w3.astype(BF16), e_w2.astype(BF16)
    xf = x.reshape(n, d)
    for layer in range(depth):
        wa, wc, wb = _split_w_in(w_in[layer])
        wq1, wq2 = _split_w_uq(w_uq[layer])
        wk, wvt = _split_w_ukv(w_ukv[layer])
        pa, pa4, pa16, pc, qm, km, vt = _proj_call(xf, g_mix[layer][None], wa, wc, wb, g_q[layer][None], g_kv[layer][None],
                                        wq1, wq2, wk, wvt, ctab, stab, seq)
        o_parts, lse_parts = [], []
        for dil, view in zip(DILATIONS, (pa, pa4, pa16)):
            o, lse = _band_call(view.reshape(batch, seq // dil, dil * 3 * WIDTH_A), dil)
            o_parts.append(o.reshape(n // dil, dil * WIDTH_A))
            lse_parts.append(lse.reshape(n // dil, dil * WIDTH_A))
        ob = _mla_call(qm, km, vt, batch, seq).reshape(n, WIDTH_B)
        oc = _na_call(pc, _na_tables(rpb[layer], rows), batch, seq).reshape(n, WIDTH_C)
        xf = _mix_out_call(xf, o_parts, lse_parts, ob, oc, g_out_a[layer][None], g_out_b[layer][None],
                           g_out_c[layer][None], w_o[layer].astype(BF16))
        j = layer // 2
        if layer % 2 == 0:
            xf = _ffn_call(xf, g_ffn[layer][None], w1[j].astype(BF16), w3[j].astype(BF16), w2[j].astype(BF16))
        else:
            wr_pad = jnp.pad(w_router[j], ((0, 0), (0, LANES - N_EXPERTS)))
            xf = _moe_layer(xf, g_ffn[layer][None], wr_pad, ew1, ew3, ew2, j, g_final[None], layer == depth - 1)
    if depth % 2 == 1:
        xf = _norm_call(xf, g_final[None])
    return xf.reshape(batch, seq, d)
```

```python
import functools
import math

import numpy as np
import jax
import jax.numpy as jnp
from jax import lax
from jax.experimental import pallas as pl
from jax.experimental.pallas import tpu as pltpu

F32 = jnp.float32
BF16 = jnp.bfloat16

LANES = 128
V7X_VMEM_LIMIT_BYTES = 52 * 1024 * 1024

HEAD_DIM = 64
N_HEADS_A = 6
DILATIONS = (1, 4, 16)
BAND_HALF = 64
N_HEADS_B = 6
Q_LORA = 384
KV_LORA = 256
QK_NOPE = 64
QK_ROPE = 32
V_HEAD = 64
ROPE_BASE = 10000.0
N_HEADS_C = 4
GRID_W = 64
NA_ROWS = 8
NA_COLS = 16
WIDTH_A = N_HEADS_A * HEAD_DIM
WIDTH_B = N_HEADS_B * V_HEAD
WIDTH_C = N_HEADS_C * HEAD_DIM
N_EXPERTS = 8
TOP_K = 2
RMS_EPS = 1e-6
NEG_INF = -1e30

ROW_TILE = 512
MLA_TQ = 256
MLA_TK = 256
MLA_UNROLL = 32
MLA_LOOKAHEAD = 4
MLA_HEADS_PER_STEP = 6
MLA_DEN_ROWS = 16
MLA_Q_PRESCALE = (QK_NOPE + QK_ROPE) ** -0.5 * math.log2(math.e)
BAND_TQ = 128
BAND_TILES_PER_STEP = 4
BAND_LOOKAHEAD = 6
NA_TILE_ROWS = 2
NA_KEY_ROWS = 10
NA_TILES_PER_STEP = 4
NA_LOOKAHEAD = 16
MOE_TM = 512
MOE_TF = 1792
COMBINE_TM = 256
GATHER_UNROLL = 8


def _cparams(semantics):
    return pltpu.CompilerParams(dimension_semantics=semantics, vmem_limit_bytes=V7X_VMEM_LIMIT_BYTES)


def _rms(x, g):
    return x * lax.rsqrt(jnp.mean(x * x, axis=-1, keepdims=True) + RMS_EPS) * g


def _dot(a, b):
    return jnp.dot(a, b, preferred_element_type=F32)


def _dot_nt(a, b):
    return lax.dot_general(a, b, (((1,), (1,)), ((), ())), preferred_element_type=F32)


def _proj_kernel(x_ref, g_ref, wa_ref, wc_ref, wb_ref, gq_ref, gkv_ref, wq1_ref, wq2_ref, wk_ref, wvt_ref,
                 ct_ref, st_ref, pa_ref, pa4_ref, pa16_ref, pc_ref, qm_ref, km_ref, vt_ref, pa_scr):
    tm = x_ref.shape[0]
    h = _rms(x_ref[...], g_ref[...]).astype(BF16)
    pa = _dot(h, wa_ref[...])
    pa_ref[...] = pa.astype(BF16)
    n_chunks = pa.shape[1] // LANES
    for c in range(n_chunks):
        pa_scr[c] = pa[:, c * LANES:(c + 1) * LANES]
    for dil, view_ref in ((DILATIONS[1], pa4_ref), (DILATIONS[2], pa16_ref)):
        for r in range(dil):
            for c in range(n_chunks):
                col = r * 3 * WIDTH_A + c * LANES
                view_ref[:, col:col + LANES] = pa_scr[c, pl.ds(r, tm // dil, stride=dil), :].astype(BF16)
    pc_ref[...] = _dot(h, wc_ref[...]).astype(BF16)
    pb = _dot(h, wb_ref[...])
    hq = _rms(pb[:, :Q_LORA], gq_ref[...]).astype(BF16)
    hkv = _rms(pb[:, Q_LORA:Q_LORA + KV_LORA], gkv_ref[...]).astype(BF16)
    r1 = pb[:, Q_LORA + KV_LORA:Q_LORA + KV_LORA + LANES]
    r2 = pb[:, Q_LORA + KV_LORA + LANES:]
    ct = ct_ref[...]
    st = st_ref[...]
    qa = _dot(hq, wq1_ref[...])
    qb = _dot(hq, wq2_ref[...])
    kn = _dot(hkv, wk_ref[...])
    kr = r1 * ct + r2 * st
    for hd in range(N_HEADS_B):
        sl = slice(hd * LANES, (hd + 1) * LANES)
        qm_ref[:, sl] = ((qa[:, sl] * ct + qb[:, sl] * st) * MLA_Q_PRESCALE).astype(BF16)
        km_ref[:, sl] = (kn[:, sl] + kr).astype(BF16)
    vt_ref[...] = _dot_nt(wvt_ref[...], hkv).astype(BF16)


def _proj_call(x, g, wa, wc, wb, gq, gkv, wq1, wq2, wk, wvt, ctab, stab, seq):
    n, d = x.shape
    tm = ROW_TILE
    tiles_per_seq = seq // tm
    full = lambda a: pl.BlockSpec(a.shape, lambda i: (0,) * a.ndim)
    row = lambda w: pl.BlockSpec((tm, w), lambda i: (i, 0))
    tab = pl.BlockSpec((tm, LANES), lambda i: (i % tiles_per_seq, 0))
    view = lambda dil: pl.BlockSpec((tm // dil, dil * 3 * WIDTH_A), lambda i: (i, 0))
    hb = N_HEADS_B * LANES
    return pl.pallas_call(
        _proj_kernel,
        grid=(n // tm,),
        in_specs=[row(d), full(g), full(wa), full(wc), full(wb), full(gq), full(gkv), full(wq1), full(wq2),
                  full(wk), full(wvt), tab, tab],
        out_specs=[row(3 * WIDTH_A)] + [view(dil) for dil in DILATIONS[1:]] + [row(3 * WIDTH_C), row(hb), row(hb),
                   pl.BlockSpec((WIDTH_B, tm), lambda i: (0, i))],
        out_shape=[jax.ShapeDtypeStruct((n, 3 * WIDTH_A), BF16)]
                  + [jax.ShapeDtypeStruct((n // dil, dil * 3 * WIDTH_A), BF16) for dil in DILATIONS[1:]]
                  + [jax.ShapeDtypeStruct((n, 3 * WIDTH_C), BF16),
                   jax.ShapeDtypeStruct((n, hb), BF16), jax.ShapeDtypeStruct((n, hb), BF16),
                   jax.ShapeDtypeStruct((WIDTH_B, n), BF16)],
        scratch_shapes=[pltpu.VMEM((3 * WIDTH_A // LANES, tm, LANES), F32)],
        compiler_params=_cparams(("parallel",)),
        name="proj",
    )(x, g, wa, wc, wb, gq, gkv, wq1, wq2, wk, wvt, ctab, stab)


def _run_pipeline(n_items, look, issue, consume, finish):
    if look >= n_items:
        results = [consume(i, s) for i, s in enumerate([issue(i) for i in range(n_items)])]
        for i, res in enumerate(results):
            finish(i, res)
        return
    inflight = {i: issue(i) for i in range(min(look, n_items))}
    pending = None
    for i in range(n_items):
        if i + look < n_items:
            inflight[i + look] = issue(i + look)
        if pending is not None:
            finish(*pending)
        pending = (i, consume(i, inflight.pop(i)))
    finish(*pending)


class _HeadPairs:
    def __init__(self, q_ref, k_ref, v_ref, tq, tkw):
        self.q_ref, self.k_ref, self.v_ref, self.tq, self.tkw = q_ref, k_ref, v_ref, tq, tkw
        lane = lax.broadcasted_iota(jnp.int32, (tq, LANES), 1)
        self.low = lane < HEAD_DIM
        self.ones = jnp.ones((tkw, LANES), BF16)

    @staticmethod
    def cols(h):
        return slice((h // 2) * LANES, (h // 2 + 1) * LANES)

    def scores(self, q_start, k_start, h):
        q = self.q_ref[q_start:q_start + self.tq, self.cols(h)]
        q = jnp.where(self.low if h % 2 == 0 else ~self.low, q, jnp.zeros_like(q))
        return _dot_nt(q, self.k_ref[pl.ds(k_start, self.tkw), self.cols(h)])

    def values(self, k_start, h, p):
        v = jnp.concatenate([self.v_ref[pl.ds(k_start, self.tkw), self.cols(h)], self.ones], axis=-1)
        o = _dot(p, v)
        return o[:, :LANES], o[:, LANES:]

    def merge(self, even, odd):
        return jnp.where(self.low, even, odd)


def _band_tables(dilation):
    tq, tkw = BAND_TQ, BAND_TQ + 2 * BAND_HALF
    shift = np.array([0, -BAND_HALF, -2 * BAND_HALF])
    rel = shift[:, None, None] + np.arange(tkw)[None, None, :] - np.arange(tq)[None, :, None]
    dist = np.abs(rel)
    slopes = 2.0 ** (-8.0 * np.arange(1, N_HEADS_A + 1) / N_HEADS_A)
    bias = -(slopes[None, :, None, None] * dilation) * dist[:, None].astype(np.float64)
    tab = np.where(dist[:, None] <= BAND_HALF, bias, NEG_INF)
    return jnp.asarray(tab, F32)


def _band_kernel(q_ref, k_ref, v_ref, tab_ref, o_ref, lse_ref, *, n):
    tq = BAND_TQ
    tkw = tq + 2 * BAND_HALF
    tiles = q_ref.shape[0] // tq
    last_tile = n // tq - 1
    items = [(t, h) for t in range(tiles) for h in range(N_HEADS_A)]
    info = []
    for t in range(tiles):
        i = pl.program_id(2) * tiles + t
        start = pl.multiple_of(jnp.clip(i * tq - BAND_HALF, 0, n - tkw), BAND_HALF)
        variant = jnp.minimum(i, 1) + (i == last_tile).astype(jnp.int32)
        info.append((start, variant))
    pairs = _HeadPairs(q_ref, k_ref, v_ref, tq, tkw)
    held = {}

    def issue(idx):
        t, h = items[idx]
        return pairs.scores(t * tq, info[t][0], h)

    def consume(idx, s):
        t, h = items[idx]
        s = s + tab_ref[info[t][1], h]
        m = jnp.max(s, axis=-1, keepdims=True)
        return jnp.exp(s - m).astype(BF16), m

    def finish(idx, res):
        t, h = items[idx]
        p, m = res
        out, den = pairs.values(info[t][0], h, p)
        held[h % 2] = (out / den, m + jnp.log(den))
        if h % 2 == 1:
            cols = pairs.cols(h)
            o_ref[t * tq:(t + 1) * tq, cols] = pairs.merge(held[0][0], held[1][0])
            lse_ref[t * tq:(t + 1) * tq, cols] = pairs.merge(held[0][1], held[1][1])

    _run_pipeline(len(items), BAND_LOOKAHEAD, issue, consume, finish)


def _band_call(pa_view, dilation):
    b, n, _ = pa_view.shape
    rows = BAND_TQ * BAND_TILES_PER_STEP
    w = WIDTH_A
    tab = _band_tables(dilation)
    qspec = pl.BlockSpec((None, rows, w), lambda bb, r, i: (bb, i, 3 * r))
    kspec = pl.BlockSpec((None, n, w), lambda bb, r, i: (bb, 0, 3 * r + 1))
    vspec = pl.BlockSpec((None, n, w), lambda bb, r, i: (bb, 0, 3 * r + 2))
    tspec = pl.BlockSpec(tab.shape, lambda bb, r, i: (0, 0, 0, 0))
    ospec = pl.BlockSpec((None, rows, w), lambda bb, r, i: (bb, i, r))
    shape = jax.ShapeDtypeStruct((b, n, dilation * w), F32)
    return pl.pallas_call(
        functools.partial(_band_kernel, n=n),
        grid=(b, dilation, n // rows),
        in_specs=[qspec, kspec, vspec, tspec],
        out_specs=[ospec, ospec],
        out_shape=[shape, shape],
        compiler_params=_cparams(("parallel", "parallel", "parallel")),
        name=f"band_d{dilation}",
    )(pa_view, pa_view, pa_view, tab)


def _mla_kernel(q_ref, k_ref, vt_ref, o_ref):
    tq = q_ref.shape[0]
    seq = k_ref.shape[0]
    tk = MLA_TK
    nh = MLA_HEADS_PER_STEP
    n_chunks = seq // tk
    items = [(u, h) for u in range(MLA_UNROLL) for h in range(nh)]
    look = MLA_LOOKAHEAD
    ones_rows = jnp.ones((MLA_DEN_ROWS, tk), BF16)

    def key_slice(chunk):
        return pl.ds(pl.multiple_of(chunk * tk, tk), tk)

    def score_matmul(chunk, h):
        k = k_ref[key_slice(chunk), h * LANES:(h + 1) * LANES]
        return _dot_nt(k, q_ref[:, h * LANES:(h + 1) * LANES])

    def value_matmul(chunk, h, p):
        vt = jnp.concatenate([vt_ref[h * V_HEAD:(h + 1) * V_HEAD, key_slice(chunk)], ones_rows], axis=0)
        return _dot(vt, p)

    def body(j, carry):
        state = list(carry[:2 * nh])
        scores = dict(zip(items[:look], carry[2 * nh:2 * nh + look]))
        pend_p, pend_alpha = carry[2 * nh + look:]
        pending = (jnp.maximum(j * MLA_UNROLL - 1, 0), nh - 1, pend_p, pend_alpha)
        ahead = []
        for idx, (u, h) in enumerate(items):
            la = idx + look
            if la < len(items):
                lu, lh = items[la]
                scores[lu, lh] = score_matmul(j * MLA_UNROLL + lu, lh)
            else:
                lu, lh = items[la - len(items)]
                ahead.append(score_matmul(jnp.minimum((j + 1) * MLA_UNROLL + lu, n_chunks - 1), lh))
            pc, ph, pp, pa = pending
            state[2 * ph + 1] = pa * state[2 * ph + 1] + value_matmul(pc, ph, pp)
            s = scores.pop((u, h))
            m_new = jnp.maximum(state[2 * h], jnp.max(s, axis=0, keepdims=True))
            alpha = jnp.exp2(state[2 * h] - m_new)
            state[2 * h] = m_new
            pending = (j * MLA_UNROLL + u, h, jnp.exp2(s - m_new).astype(BF16), alpha)
        return tuple(state) + tuple(ahead) + (pending[2], pending[3])

    init = (jnp.full((1, tq), NEG_INF, F32), jnp.zeros((V_HEAD + MLA_DEN_ROWS, tq), F32)) * nh
    init += tuple(score_matmul(u, h) for u, h in items[:look])
    init += (jnp.zeros((tk, tq), BF16), jnp.ones((1, tq), F32))
    res = lax.fori_loop(0, n_chunks // MLA_UNROLL, body, init)
    accs = [res[2 * h + 1] for h in range(nh)]
    accs[nh - 1] = res[-1] * accs[nh - 1] + value_matmul(n_chunks - 1, nh - 1, res[-2])
    out_t = jnp.concatenate([a[:V_HEAD] / a[V_HEAD:V_HEAD + 1] for a in accs], axis=0)
    o_ref[...] = out_t.T


def _mla_call(qm, km, vt, batch, seq):
    tq = MLA_TQ
    nh = MLA_HEADS_PER_STEP
    qspec = pl.BlockSpec((None, tq, nh * LANES), lambda b, g, i: (b, i, g))
    kspec = pl.BlockSpec((None, seq, nh * LANES), lambda b, g, i: (b, 0, g))
    vspec = pl.BlockSpec((nh * V_HEAD, seq), lambda b, g, i: (g, b))
    ospec = pl.BlockSpec((None, tq, nh * V_HEAD), lambda b, g, i: (b, i, g))
    return pl.pallas_call(
        _mla_kernel,
        grid=(batch, N_HEADS_B // nh, seq // tq),
        in_specs=[qspec, kspec, vspec],
        out_specs=ospec,
        out_shape=jax.ShapeDtypeStruct((batch, seq, WIDTH_B), F32),
        compiler_params=_cparams(("parallel", "parallel", "parallel")),
        name="mla",
    )(qm.reshape(batch, seq, -1), km.reshape(batch, seq, -1), vt)


def _na_variant(i, n_tiles):
    return jnp.minimum(i, 2) + jnp.maximum(i - (n_tiles - 3), 0)


def _na_kernel(q_ref, k_ref, v_ref, tab_ref, o_ref, *, rows):
    tq = NA_TILE_ROWS * GRID_W
    tkw = NA_KEY_ROWS * GRID_W
    tiles = q_ref.shape[0] // tq
    n_tiles = rows // NA_TILE_ROWS
    items = [(t, h) for t in range(tiles) for h in range(N_HEADS_C)]
    info = []
    for t in range(tiles):
        i = pl.program_id(1) * tiles + t
        base = jnp.clip(i * NA_TILE_ROWS - NA_ROWS // 2, 0, rows - NA_KEY_ROWS)
        info.append((pl.multiple_of(base * GRID_W, GRID_W), _na_variant(i, n_tiles)))
    pairs = _HeadPairs(q_ref, k_ref, v_ref, tq, tkw)
    held = {}

    def issue(idx):
        t, h = items[idx]
        return pairs.scores(t * tq, info[t][0], h)

    def consume(idx, s):
        t, h = items[idx]
        s = s + tab_ref[info[t][1], h]
        return jnp.exp(s - jnp.max(s, axis=-1, keepdims=True)).astype(BF16)

    def finish(idx, p):
        t, h = items[idx]
        out, den = pairs.values(info[t][0], h, p)
        held[h % 2] = out / den
        if h % 2 == 1:
            o_ref[t * tq:(t + 1) * tq, pairs.cols(h)] = pairs.merge(held[0], held[1])

    _run_pipeline(len(items), NA_LOOKAHEAD, issue, consume, finish)


def _na_tables(rpb, rows):
    r0 = np.array([0, 2, 4, rows - 4, rows - 2])
    base = np.clip(r0 - NA_ROWS // 2, 0, rows - NA_KEY_ROWS)
    r = r0[:, None] + np.arange(NA_TILE_ROWS)[None, :]
    row_start = np.clip(r - NA_ROWS // 2, 0, rows - NA_ROWS)
    krow = base[:, None] + np.arange(NA_KEY_ROWS)[None, :]
    drow = krow[:, None, :] - r[:, :, None]
    row_ok = (krow[:, None, :] >= row_start[:, :, None]) & (krow[:, None, :] < row_start[:, :, None] + NA_ROWS)
    c = np.arange(GRID_W)
    win_start = np.clip(c - NA_COLS // 2, 0, GRID_W - NA_COLS)
    col_ok = (c[None, :] >= win_start[:, None]) & (c[None, :] < win_start[:, None] + NA_COLS)
    dcol = np.clip(c[None, :] - c[:, None], -(NA_COLS - 1), NA_COLS - 1)
    ok = row_ok[:, :, None, :, None] & col_ok[None, None, :, None, :]
    di = np.clip(drow, -(NA_ROWS - 1), NA_ROWS - 1) + (NA_ROWS - 1)
    pick_col = (dcol[:, :, None] + NA_COLS - 1 == np.arange(2 * NA_COLS - 1)).astype(np.float32)
    pick_row = (di[..., None] == np.arange(2 * NA_ROWS - 1)).astype(np.float32)
    hi = lax.Precision.HIGHEST
    toeplitz = jnp.einsum("hab,cjb->hacj", rpb.astype(F32), pick_col, precision=hi)
    bias = jnp.einsum("vqka,hacj->vhqckj", pick_row, toeplitz, precision=hi)
    tab = jnp.where(jnp.asarray(ok)[:, None], bias, NEG_INF)
    return tab.reshape(5, N_HEADS_C, NA_TILE_ROWS * GRID_W, NA_KEY_ROWS * GRID_W)


def _na_call(pc, tab, batch, seq):
    rows = seq // GRID_W
    tq = NA_TILE_ROWS * GRID_W * NA_TILES_PER_STEP
    w = WIDTH_C
    qspec = pl.BlockSpec((None, tq, w), lambda b, i: (b, i, 0))
    kspec = pl.BlockSpec((None, seq, w), lambda b, i: (b, 0, 1))
    vspec = pl.BlockSpec((None, seq, w), lambda b, i: (b, 0, 2))
    tspec = pl.BlockSpec(tab.shape, lambda b, i: (0, 0, 0, 0))
    ospec = pl.BlockSpec((None, tq, w), lambda b, i: (b, i, 0))
    pc3 = pc.reshape(batch, seq, 3 * w)
    return pl.pallas_call(
        functools.partial(_na_kernel, rows=rows),
        grid=(batch, seq // tq),
        in_specs=[qspec, kspec, vspec, tspec],
        out_specs=ospec,
        out_shape=jax.ShapeDtypeStruct((batch, seq, w), F32),
        compiler_params=_cparams(("parallel", "parallel")),
        name="natten",
    )(pc3, pc3, pc3, tab)


def _mix_out_kernel(x_ref, o1_ref, o2_ref, o3_ref, l1_ref, l2_ref, l3_ref, ob_ref, oc_ref,
                    ga_ref, gb_ref, gc_ref, wo_ref, out_ref, o2_scr, o3_scr, l2_scr, l3_scr):
    tm = x_ref.shape[0]
    n_chunks = WIDTH_A // LANES

    def natural(view_ref, scr, dil):
        for r in range(dil):
            for c in range(n_chunks):
                col = r * WIDTH_A + c * LANES
                scr[c, pl.ds(r, tm // dil, stride=dil), :] = view_ref[:, col:col + LANES]
        return jnp.concatenate([scr[c] for c in range(n_chunks)], axis=-1)

    l1, o1 = l1_ref[...], o1_ref[...]
    l2, o2 = natural(l2_ref, l2_scr, DILATIONS[1]), natural(o2_ref, o2_scr, DILATIONS[1])
    l3, o3 = natural(l3_ref, l3_scr, DILATIONS[2]), natural(o3_ref, o3_scr, DILATIONS[2])
    mx = jnp.maximum(jnp.maximum(l1, l2), l3)
    w1, w2, w3 = jnp.exp(l1 - mx), jnp.exp(l2 - mx), jnp.exp(l3 - mx)
    oa = (w1 * o1 + w2 * o2 + w3 * o3) / (w1 + w2 + w3)
    ya = _rms(oa, ga_ref[...]).astype(BF16)
    yb = _rms(ob_ref[...], gb_ref[...]).astype(BF16)
    yc = _rms(oc_ref[...], gc_ref[...]).astype(BF16)
    y = _dot(ya, wo_ref[:WIDTH_A, :])
    y = y + _dot(yb, wo_ref[WIDTH_A:WIDTH_A + WIDTH_B, :])
    y = y + _dot(yc, wo_ref[WIDTH_A + WIDTH_B:, :])
    out_ref[...] = x_ref[...] + y


def _mix_out_call(x, o_parts, lse_parts, ob, oc, ga, gb, gc, wo):
    n, d = x.shape
    tm = ROW_TILE
    full = lambda a: pl.BlockSpec(a.shape, lambda i: (0,) * a.ndim)
    row = lambda w: pl.BlockSpec((tm, w), lambda i: (i, 0))
    view = lambda dil: pl.BlockSpec((tm // dil, dil * WIDTH_A), lambda i: (i, 0))
    return pl.pallas_call(
        _mix_out_kernel,
        grid=(n // tm,),
        in_specs=[row(d)] + [view(dil) for dil in DILATIONS] * 2
                 + [row(WIDTH_B), row(WIDTH_C), full(ga), full(gb), full(gc), full(wo)],
        out_specs=row(d),
        out_shape=jax.ShapeDtypeStruct((n, d), F32),
        scratch_shapes=[pltpu.VMEM((WIDTH_A // LANES, tm, LANES), F32)] * 4,
        compiler_params=_cparams(("parallel",)),
        name="mix_out",
    )(x, *o_parts, *lse_parts, ob, oc, ga, gb, gc, wo)


def _silu(u):
    return u * (1.0 / (1.0 + jnp.exp(-u)))


def _ffn_kernel(x_ref, g_ref, w1_ref, w3_ref, w2_ref, out_ref):
    x = x_ref[...]
    h = _rms(x, g_ref[...]).astype(BF16)
    a = (_silu(_dot(h, w1_ref[...])) * _dot(h, w3_ref[...])).astype(BF16)
    out_ref[...] = x + _dot(a, w2_ref[...])


def _ffn_call(x, g, w1, w3, w2):
    n, d = x.shape
    tm = ROW_TILE
    resident = lambda a: pl.BlockSpec(a.shape, lambda i: (0, 0), pipeline_mode=pl.Buffered(1))
    return pl.pallas_call(
        _ffn_kernel,
        grid=(n // tm,),
        in_specs=[pl.BlockSpec((tm, d), lambda i: (i, 0)), pl.BlockSpec((1, d), lambda i: (0, 0)),
                  resident(w1), resident(w3), resident(w2)],
        out_specs=pl.BlockSpec((tm, d), lambda i: (i, 0)),
        out_shape=jax.ShapeDtypeStruct((n, d), F32),
        compiler_params=_cparams(("parallel",)),
        name="ffn",
    )(x, g, w1, w3, w2)


def _router_kernel(x_ref, g_ref, wr_ref, h_ref, e_ref, gate_ref):
    h = _rms(x_ref[...], g_ref[...])
    _to_token_tiles(h_ref, h)
    logits = jnp.dot(h, wr_ref[...], precision=lax.Precision.HIGHEST, preferred_element_type=F32)
    lane = lax.broadcasted_iota(jnp.int32, logits.shape, 1).astype(F32)
    logits = jnp.where(lane < N_EXPERTS, logits, -jnp.inf)
    m1 = jnp.max(logits, axis=-1, keepdims=True)
    i1 = jnp.min(jnp.where(logits == m1, lane, float(LANES)), axis=-1, keepdims=True)
    rest = jnp.where(lane == i1, -jnp.inf, logits)
    m2 = jnp.max(rest, axis=-1, keepdims=True)
    i2 = jnp.min(jnp.where(rest == m2, lane, float(LANES)), axis=-1, keepdims=True)
    e = jnp.exp(m2 - m1)
    den = 1.0 + e
    e_ref[...] = jnp.where(lane == 0.0, i1, jnp.where(lane == 1.0, i2, 0.0)).astype(jnp.int32)
    gate_ref[...] = jnp.where(lane == 0.0, 1.0 / den, jnp.where(lane == 1.0, e / den, 0.0))


def _router_call(x, g, wr_pad):
    n, d = x.shape
    tm = ROW_TILE
    row = lambda w: pl.BlockSpec((tm, w), lambda i: (i, 0))
    full = lambda a: pl.BlockSpec(a.shape, lambda i: (0,) * a.ndim)
    return pl.pallas_call(
        _router_kernel,
        grid=(n // tm,),
        in_specs=[row(d), full(g), full(wr_pad)],
        out_specs=[pl.BlockSpec((tm * TOKEN_ROWS, LANES), lambda i: (i, 0)), row(LANES), row(LANES)],
        out_shape=[jax.ShapeDtypeStruct((n * TOKEN_ROWS, LANES), F32), jax.ShapeDtypeStruct((n, LANES), jnp.int32),
                   jax.ShapeDtypeStruct((n, LANES), F32)],
        compiler_params=_cparams(("parallel",)),
        name="router",
    )(x, g, wr_pad)


TOKEN_ROWS = 8


def _to_token_tiles(dst_ref, x):
    tm, d = x.shape
    assert d == TOKEN_ROWS * LANES
    for j in range(TOKEN_ROWS):
        dst_ref[pl.ds(j, tm, stride=TOKEN_ROWS), :] = x[:, j * LANES:(j + 1) * LANES]


def _from_token_tiles(src_ref, tm):
    return jnp.concatenate([src_ref[pl.ds(j, tm, stride=TOKEN_ROWS), :] for j in range(TOKEN_ROWS)], axis=-1)


def _row_copy(src_hbm, row, dst_ref, r, sem):
    src = src_hbm.at[pl.ds(pl.multiple_of(row * TOKEN_ROWS, TOKEN_ROWS), TOKEN_ROWS)]
    return pltpu.make_async_copy(src, dst_ref.at[pl.ds(pl.multiple_of(r * TOKEN_ROWS, TOKEN_ROWS), TOKEN_ROWS)], sem)


def _start_row_gather(idx_ref, base, src_hbm, dst_ref, sem, count):
    def body(r, c):
        _row_copy(src_hbm, idx_ref[base + r], dst_ref, r, sem).start()
        return c
    lax.fori_loop(0, count, body, 0, unroll=GATHER_UNROLL)


def _wait_row_gather(src_hbm, dst_ref, sem, count):
    pltpu.make_async_copy(src_hbm.at[pl.ds(0, count * TOKEN_ROWS)], dst_ref, sem).wait()


def _moe_ffn_kernel(tile_e_ref, tile_ok_ref, row_tok_ref, h_hbm, w1_ref, w3_ref, w2_ref, out_ref,
                    xg_ref, hb_ref, acc_ref, sem):
    i = pl.program_id(0)
    f = pl.program_id(1)
    n_tiles = pl.num_programs(0)
    tm = MOE_TM
    slot = i % 2

    @pl.when(f == 0)
    def _():
        @pl.when(i == 0)
        def _():
            _start_row_gather(row_tok_ref, 0, h_hbm, xg_ref.at[0], sem.at[0], tm)

        _wait_row_gather(h_hbm, xg_ref.at[slot], sem.at[slot], tm)

        @pl.when(i + 1 < n_tiles)
        def _():
            _start_row_gather(row_tok_ref, (i + 1) * tm, h_hbm, xg_ref.at[1 - slot], sem.at[1 - slot], tm)

        hb_ref[...] = _from_token_tiles(xg_ref.at[slot], tm).astype(BF16)
        acc_ref[...] = jnp.zeros_like(acc_ref)

    @pl.when(tile_ok_ref[i] != 0)
    def _():
        h = hb_ref[...]
        a = (_silu(_dot(h, w1_ref[...])) * _dot(h, w3_ref[...])).astype(BF16)
        acc_ref[...] += _dot(a, w2_ref[...])

    @pl.when(f == pl.num_programs(1) - 1)
    def _():
        _to_token_tiles(out_ref, acc_ref[...])


def _moe_ffn_call(tile_e, tile_ok, row_tok, h, w1, w3, w2, layer):
    d = w1.shape[2]
    n_tiles = tile_e.shape[0]
    ff = w1.shape[3]
    tm, tf = MOE_TM, MOE_TF
    n_ff = ff // tf

    def ff_block(i, f):
        return jnp.where(i % 2 == 0, f, n_ff - 1 - f)

    grid_spec = pltpu.PrefetchScalarGridSpec(
        num_scalar_prefetch=3,
        grid=(n_tiles, ff // tf),
        in_specs=[pl.BlockSpec(memory_space=pl.ANY),
                  pl.BlockSpec((None, None, d, tf), lambda i, f, te, tv, rt: (layer, te[i], 0, ff_block(i, f))),
                  pl.BlockSpec((None, None, d, tf), lambda i, f, te, tv, rt: (layer, te[i], 0, ff_block(i, f))),
                  pl.BlockSpec((None, None, tf, d), lambda i, f, te, tv, rt: (layer, te[i], ff_block(i, f), 0))],
        out_specs=pl.BlockSpec((tm * TOKEN_ROWS, LANES), lambda i, f, te, tv, rt: (i, 0)),
        scratch_shapes=[pltpu.VMEM((2, tm * TOKEN_ROWS, LANES), F32), pltpu.VMEM((tm, d), BF16),
                        pltpu.VMEM((tm, d), F32), pltpu.SemaphoreType.DMA((2,))],
    )
    return pl.pallas_call(
        _moe_ffn_kernel,
        grid_spec=grid_spec,
        out_shape=jax.ShapeDtypeStruct((n_tiles * tm * TOKEN_ROWS, LANES), F32),
        compiler_params=_cparams(("arbitrary", "arbitrary")),
        name="moe_ffn",
    )(tile_e, tile_ok, row_tok, h, w1, w3, w2)


def _combine_kernel(d0_ref, d1_ref, y_hbm, x_ref, gate_ref, gain_ref, out_ref, b0_ref, b1_ref, sem, *, final_norm):
    i = pl.program_id(0)
    n_tiles = pl.num_programs(0)
    tm = COMBINE_TM
    slot = i % 2

    def start(tile, s):
        _start_row_gather(d0_ref, tile * tm, y_hbm, b0_ref.at[s], sem.at[0, s], tm)
        _start_row_gather(d1_ref, tile * tm, y_hbm, b1_ref.at[s], sem.at[1, s], tm)

    @pl.when(i == 0)
    def _():
        start(0, 0)

    _wait_row_gather(y_hbm, b0_ref.at[slot], sem.at[0, slot], tm)
    _wait_row_gather(y_hbm, b1_ref.at[slot], sem.at[1, slot], tm)

    @pl.when(i + 1 < n_tiles)
    def _():
        start(i + 1, 1 - slot)

    gates = gate_ref[...]
    y0 = _from_token_tiles(b0_ref.at[slot], tm)
    y1 = _from_token_tiles(b1_ref.at[slot], tm)
    out = x_ref[...] + (gates[:, 0:1] * y0 + gates[:, 1:2] * y1)
    out_ref[...] = _rms(out, gain_ref[...]) if final_norm else out


def _combine_call(d0, d1, y, x, gates, gain, final_norm):
    n, d = x.shape
    tm = COMBINE_TM
    grid_spec = pltpu.PrefetchScalarGridSpec(
        num_scalar_prefetch=2,
        grid=(n // tm,),
        in_specs=[pl.BlockSpec(memory_space=pl.ANY),
                  pl.BlockSpec((tm, d), lambda i, a, b: (i, 0)),
                  pl.BlockSpec((tm, LANES), lambda i, a, b: (i, 0)),
                  pl.BlockSpec((1, d), lambda i, a, b: (0, 0))],
        out_specs=pl.BlockSpec((tm, d), lambda i, a, b: (i, 0)),
        scratch_shapes=[pltpu.VMEM((2, tm * TOKEN_ROWS, LANES), F32)] * 2 + [pltpu.SemaphoreType.DMA((2, 2))],
    )
    return pl.pallas_call(
        functools.partial(_combine_kernel, final_norm=final_norm),
        grid_spec=grid_spec,
        out_shape=jax.ShapeDtypeStruct((n, d), F32),
        compiler_params=_cparams(("arbitrary",)),
        name="moe_combine",
    )(d0, d1, y, x, gates, gain)


def _moe_plan(top_e, n_tiles):
    n_assign = top_e.shape[0] * TOP_K
    flat_e = top_e.reshape(n_assign)
    onehot = (flat_e[:, None] == jnp.arange(N_EXPERTS, dtype=jnp.int32)[None, :]).astype(jnp.int32)
    csum = jnp.cumsum(onehot, axis=0)
    rank = jnp.take_along_axis(csum, flat_e[:, None], axis=1)[:, 0] - 1
    counts = csum[-1]
    padded = (counts + MOE_TM - 1) // MOE_TM * MOE_TM
    pend = jnp.cumsum(padded)
    pstart = pend - padded
    dest = pstart[flat_e] + rank
    tile_start = jnp.arange(n_tiles, dtype=jnp.int32) * MOE_TM
    tile_e = jnp.minimum(jnp.searchsorted(pend, tile_start, side="right"), N_EXPERTS - 1).astype(jnp.int32)
    tile_ok = (tile_start < pend[-1]).astype(jnp.int32)
    order = jnp.argsort(flat_e, stable=True).astype(jnp.int32)
    row_e = jnp.repeat(tile_e, MOE_TM)
    r = jnp.arange(n_tiles * MOE_TM, dtype=jnp.int32) - pstart[row_e]
    src = jnp.clip((jnp.cumsum(counts) - counts)[row_e] + r, 0, n_assign - 1)
    row_tok = jnp.where(r < counts[row_e], order[src] // TOP_K, 0).astype(jnp.int32)
    dest = dest.reshape(-1, TOP_K).astype(jnp.int32)
    return tile_e, tile_ok, row_tok, dest[:, 0], dest[:, 1]


def _moe_layer(x, g, wr_pad, w1, w3, w2, layer, gain, final_norm):
    n = x.shape[0]
    h, top_e, gates = _router_call(x, g, wr_pad)
    n_tiles = -(-(n * TOP_K + N_EXPERTS * (MOE_TM - 1)) // MOE_TM)
    tile_e, tile_ok, row_tok, d0, d1 = _moe_plan(top_e[:, :TOP_K], n_tiles)
    y = _moe_ffn_call(tile_e, tile_ok, row_tok, h, w1, w3, w2, layer)
    return _combine_call(d0, d1, y, x, gates, gain, final_norm)


def _norm_kernel(x_ref, g_ref, o_ref):
    o_ref[...] = _rms(x_ref[...], g_ref[...])


def _norm_call(x, g):
    n, d = x.shape
    tm = ROW_TILE
    return pl.pallas_call(
        _norm_kernel,
        grid=(n // tm,),
        in_specs=[pl.BlockSpec((tm, d), lambda i: (i, 0)), pl.BlockSpec((1, d), lambda i: (0, 0))],
        out_specs=pl.BlockSpec((tm, d), lambda i: (i, 0)),
        out_shape=jax.ShapeDtypeStruct((n, d), F32),
        compiler_params=_cparams(("parallel",)),
        name="final_norm",
    )(x, g)


def _rope_tables(seq):
    inv = ROPE_BASE ** (-jnp.arange(0, QK_ROPE, 2, dtype=F32) / QK_ROPE)
    ang = jnp.arange(seq)[:, None].astype(F32) * inv[None, :]
    cos, sin = jnp.cos(ang), jnp.sin(ang)
    pad = LANES - QK_NOPE - QK_ROPE
    ctab = jnp.concatenate([jnp.ones((seq, QK_NOPE), F32), cos, cos, jnp.zeros((seq, pad), F32)], axis=1)
    stab = jnp.concatenate([jnp.zeros((seq, QK_NOPE), F32), sin, sin, jnp.zeros((seq, pad), F32)], axis=1)
    return ctab, stab


def _split_w_in(w_in):
    d = w_in.shape[0]
    bounds = np.cumsum([WIDTH_A, WIDTH_A, WIDTH_A, Q_LORA, KV_LORA, QK_ROPE, WIDTH_C, WIDTH_C])
    qa, ka, va, cq, ckv, kr, qc, kc, vc = jnp.split(w_in, bounds.tolist(), axis=1)
    half = QK_ROPE // 2
    z_lo = jnp.zeros((d, QK_NOPE), w_in.dtype)
    z_hi = jnp.zeros((d, LANES - QK_NOPE - QK_ROPE), w_in.dtype)
    rope_blk = jnp.concatenate([z_lo, kr, z_hi], axis=1)
    swap_blk = jnp.concatenate([z_lo, -kr[:, half:], kr[:, :half], z_hi], axis=1)
    score_scale = HEAD_DIM ** -0.5
    wa = jnp.concatenate([qa * score_scale, ka, va], axis=1).astype(BF16)
    wc = jnp.concatenate([qc * score_scale, kc, vc], axis=1).astype(BF16)
    wb = jnp.concatenate([cq, ckv, rope_blk, swap_blk], axis=1).astype(BF16)
    return wa, wc, wb


def _split_w_uq(w_uq):
    r = w_uq.shape[0]
    w = w_uq.reshape(r, N_HEADS_B, QK_NOPE + QK_ROPE)
    nope, rope = w[..., :QK_NOPE], w[..., QK_NOPE:]
    half = QK_ROPE // 2
    z_hi = jnp.zeros((r, N_HEADS_B, LANES - QK_NOPE - QK_ROPE), w_uq.dtype)
    w1 = jnp.concatenate([nope, rope, z_hi], axis=-1)
    w2 = jnp.concatenate([jnp.zeros_like(nope), -rope[..., half:], rope[..., :half], z_hi], axis=-1)
    return w1.reshape(r, -1).astype(BF16), w2.reshape(r, -1).astype(BF16)


def _split_w_ukv(w_ukv):
    r = w_ukv.shape[0]
    w = w_ukv.reshape(r, N_HEADS_B, QK_NOPE + V_HEAD)
    k_nope, v = w[..., :QK_NOPE], w[..., QK_NOPE:]
    wk = jnp.concatenate([k_nope, jnp.zeros((r, N_HEADS_B, LANES - QK_NOPE), w_ukv.dtype)], axis=-1)
    return wk.reshape(r, -1).astype(BF16), v.reshape(r, -1).T.astype(BF16)


def kernel(x, g_mix, w_in, g_q, g_kv, w_uq, w_ukv, rpb, g_out_a, g_out_b, g_out_c, w_o, g_ffn, w1, w3, w2,
           w_router, e_w1, e_w3, e_w2, g_final):
    batch, seq, d = x.shape
    n = batch * seq
    depth = g_mix.shape[0]
    rows = seq // GRID_W
    ctab, stab = _rope_tables(seq)
    ew1, ew3, ew2 = e_w1.astype(BF16), e_w3.astype(BF16), e_w2.astype(BF16)
    xf = x.reshape(n, d)
    for layer in range(depth):
        wa, wc, wb = _split_w_in(w_in[layer])
        wq1, wq2 = _split_w_uq(w_uq[layer])
        wk, wvt = _split_w_ukv(w_ukv[layer])
        pa, pa4, pa16, pc, qm, km, vt = _proj_call(xf, g_mix[layer][None], wa, wc, wb, g_q[layer][None], g_kv[layer][None],
                                        wq1, wq2, wk, wvt, ctab, stab, seq)
        o_parts, lse_parts = [], []
        for dil, view in zip(DILATIONS, (pa, pa4, pa16)):
            o, lse = _band_call(view.reshape(batch, seq // dil, dil * 3 * WIDTH_A), dil)
            o_parts.append(o.reshape(n // dil, dil * WIDTH_A))
            lse_parts.append(lse.reshape(n // dil, dil * WIDTH_A))
        ob = _mla_call(qm, km, vt, batch, seq).reshape(n, WIDTH_B)
        oc = _na_call(pc, _na_tables(rpb[layer], rows), batch, seq).reshape(n, WIDTH_C)
        xf = _mix_out_call(xf, o_parts, lse_parts, ob, oc, g_out_a[layer][None], g_out_b[layer][None],
                           g_out_c[layer][None], w_o[layer].astype(BF16))
        j = layer // 2
        if layer % 2 == 0:
            xf = _ffn_call(xf, g_ffn[layer][None], w1[j].astype(BF16), w3[j].astype(BF16), w2[j].astype(BF16))
        else:
            wr_pad = jnp.pad(w_router[j], ((0, 0), (0, LANES - N_EXPERTS)))
            xf = _moe_layer(xf, g_ffn[layer][None], wr_pad, ew1, ew3, ew2, j, g_final[None], layer == depth - 1)
    if depth % 2 == 1:
        xf = _norm_call(xf, g_final[None])
    return xf.reshape(batch, seq, d)
```

```python
import functools
import math

import numpy as np
import jax
import jax.numpy as jnp
from jax import lax
from jax.experimental import pallas as pl
from jax.experimental.pallas import tpu as pltpu

F32 = jnp.float32
BF16 = jnp.bfloat16

LANES = 128
V7X_VMEM_LIMIT_BYTES = 52 * 1024 * 1024

HEAD_DIM = 64
N_HEADS_A = 6
DILATIONS = (1, 4, 16)
BAND_HALF = 64
N_HEADS_B = 6
Q_LORA = 384
KV_LORA = 256
QK_NOPE = 64
QK_ROPE = 32
V_HEAD = 64
ROPE_BASE = 10000.0
N_HEADS_C = 4
GRID_W = 64
NA_ROWS = 8
NA_COLS = 16
WIDTH_A = N_HEADS_A * HEAD_DIM
WIDTH_B = N_HEADS_B * V_HEAD
WIDTH_C = N_HEADS_C * HEAD_DIM
N_EXPERTS = 8
TOP_K = 2
RMS_EPS = 1e-6
NEG_INF = -1e30

ROW_TILE = 512
MLA_TQ = 256
MLA_TK = 256
MLA_UNROLL = 32
MLA_LOOKAHEAD = 4
MLA_HEADS_PER_STEP = 6
MLA_DEN_ROWS = 16
MLA_Q_PRESCALE = (QK_NOPE + QK_ROPE) ** -0.5 * math.log2(math.e)
BAND_TQ = 128
BAND_TILES_PER_STEP = 4
BAND_LOOKAHEAD = 6
NA_TILE_ROWS = 2
NA_KEY_ROWS = 10
NA_TILES_PER_STEP = 4
NA_LOOKAHEAD = 16
MOE_TM = 512
MOE_TF = 1792
COMBINE_TM = 256
GATHER_UNROLL = 8


def _cparams(semantics):
    return pltpu.CompilerParams(dimension_semantics=semantics, vmem_limit_bytes=V7X_VMEM_LIMIT_BYTES)


def _rms(x, g):
    return x * lax.rsqrt(jnp.mean(x * x, axis=-1, keepdims=True) + RMS_EPS) * g


def _dot(a, b):
    return jnp.dot(a, b, preferred_element_type=F32)


def _dot_nt(a, b):
    return lax.dot_general(a, b, (((1,), (1,)), ((), ())), preferred_element_type=F32)


def _proj_kernel(x_ref, g_ref, win_ref, gq_ref, gkv_ref, wq_ref, wk_ref, wvt_ref,
                 ct_ref, st_ref, sa_ref, sb_ref, pa_ref, pa4_ref, pa16_ref, pc_ref, qm_ref, km_ref, vt_ref, pa_scr):
    tm = x_ref.shape[0]
    h = _rms(x_ref[...], g_ref[...]).astype(BF16)
    proj = _dot(h, win_ref[...])
    pa = proj[:, :3 * WIDTH_A]
    pb = proj[:, 3 * (WIDTH_A + WIDTH_C):]
    pa_ref[...] = pa.astype(BF16)
    n_chunks = pa.shape[1] // LANES
    for c in range(n_chunks):
        pa_scr[c] = pa[:, c * LANES:(c + 1) * LANES]
    for dil, view_ref in ((DILATIONS[1], pa4_ref), (DILATIONS[2], pa16_ref)):
        for r in range(dil):
            for c in range(n_chunks):
                col = r * 3 * WIDTH_A + c * LANES
                view_ref[:, col:col + LANES] = pa_scr[c, pl.ds(r, tm // dil, stride=dil), :].astype(BF16)
    pc_ref[...] = proj[:, 3 * WIDTH_A:3 * (WIDTH_A + WIDTH_C)].astype(BF16)
    hq = _rms(pb[:, :Q_LORA], gq_ref[...]).astype(BF16)
    hkv = _rms(pb[:, Q_LORA:Q_LORA + KV_LORA], gkv_ref[...]).astype(BF16)
    r1 = pb[:, Q_LORA + KV_LORA:Q_LORA + KV_LORA + LANES]
    r2 = pb[:, Q_LORA + KV_LORA + LANES:]
    ct = ct_ref[...]
    st = st_ref[...]
    sa = sa_ref[...]
    sb = sb_ref[...]
    half = QK_ROPE // 2
    qa = _dot(hq, wq_ref[...])
    kn = _dot(hkv, wk_ref[...])
    kr = r1 * ct + r2 * st
    for hd in range(N_HEADS_B):
        sl = slice(hd * LANES, (hd + 1) * LANES)
        q = qa[:, sl]
        q = q * ct + pltpu.roll(q, LANES - half, 1) * sa + pltpu.roll(q, half, 1) * sb
        qm_ref[:, sl] = (q * MLA_Q_PRESCALE).astype(BF16)
        km_ref[:, sl] = (kn[:, sl] + kr).astype(BF16)
    vt_ref[...] = _dot_nt(wvt_ref[...], hkv).astype(BF16)


def _proj_call(x, g, win, gq, gkv, wq, wk, wvt, tabs, seq):
    n, d = x.shape
    tm = ROW_TILE
    tiles_per_seq = seq // tm
    full = lambda a: pl.BlockSpec(a.shape, lambda i: (0,) * a.ndim)
    row = lambda w: pl.BlockSpec((tm, w), lambda i: (i, 0))
    tab = pl.BlockSpec((tm, LANES), lambda i: (i % tiles_per_seq, 0))
    view = lambda dil: pl.BlockSpec((tm // dil, dil * 3 * WIDTH_A), lambda i: (i, 0))
    hb = N_HEADS_B * LANES
    return pl.pallas_call(
        _proj_kernel,
        grid=(n // tm,),
        in_specs=[row(d), full(g), full(win), full(gq), full(gkv), full(wq), full(wk), full(wvt)] + [tab] * 4,
        out_specs=[row(3 * WIDTH_A)] + [view(dil) for dil in DILATIONS[1:]] + [row(3 * WIDTH_C), row(hb), row(hb),
                   pl.BlockSpec((WIDTH_B, tm), lambda i: (0, i))],
        out_shape=[jax.ShapeDtypeStruct((n, 3 * WIDTH_A), BF16)]
                  + [jax.ShapeDtypeStruct((n // dil, dil * 3 * WIDTH_A), BF16) for dil in DILATIONS[1:]]
                  + [jax.ShapeDtypeStruct((n, 3 * WIDTH_C), BF16),
                   jax.ShapeDtypeStruct((n, hb), BF16), jax.ShapeDtypeStruct((n, hb), BF16),
                   jax.ShapeDtypeStruct((WIDTH_B, n), BF16)],
        scratch_shapes=[pltpu.VMEM((3 * WIDTH_A // LANES, tm, LANES), F32)],
        compiler_params=_cparams(("parallel",)),
        name="proj",
    )(x, g, win, gq, gkv, wq, wk, wvt, *tabs)


def _run_pipeline(n_items, look, issue, consume, finish):
    if look >= n_items:
        results = [consume(i, s) for i, s in enumerate([issue(i) for i in range(n_items)])]
        for i, res in enumerate(results):
            finish(i, res)
        return
    inflight = {i: issue(i) for i in range(min(look, n_items))}
    pending = None
    for i in range(n_items):
        if i + look < n_items:
            inflight[i + look] = issue(i + look)
        if pending is not None:
            finish(*pending)
        pending = (i, consume(i, inflight.pop(i)))
    finish(*pending)


class _HeadPairs:
    def __init__(self, q_ref, k_ref, v_ref, tq, tkw):
        self.q_ref, self.k_ref, self.v_ref, self.tq, self.tkw = q_ref, k_ref, v_ref, tq, tkw
        lane = lax.broadcasted_iota(jnp.int32, (tq, LANES), 1)
        self.low = lane < HEAD_DIM
        self.ones = jnp.ones((tkw, LANES), BF16)

    @staticmethod
    def cols(h):
        return slice((h // 2) * LANES, (h // 2 + 1) * LANES)

    def scores(self, q_start, k_start, h):
        q = self.q_ref[q_start:q_start + self.tq, self.cols(h)]
        q = jnp.where(self.low if h % 2 == 0 else ~self.low, q, jnp.zeros_like(q))
        return _dot_nt(q, self.k_ref[pl.ds(k_start, self.tkw), self.cols(h)])

    def values(self, k_start, h, p):
        v = jnp.concatenate([self.v_ref[pl.ds(k_start, self.tkw), self.cols(h)], self.ones], axis=-1)
        o = _dot(p, v)
        return o[:, :LANES], o[:, LANES:]

    def merge(self, even, odd):
        return jnp.where(self.low, even, odd)


def _band_tables(dilation):
    tq, tkw = BAND_TQ, BAND_TQ + 2 * BAND_HALF
    shift = np.array([0, -BAND_HALF, -2 * BAND_HALF])
    rel = shift[:, None, None] + np.arange(tkw)[None, None, :] - np.arange(tq)[None, :, None]
    dist = np.abs(rel)
    slopes = 2.0 ** (-8.0 * np.arange(1, N_HEADS_A + 1) / N_HEADS_A)
    bias = -(slopes[None, :, None, None] * dilation) * dist[:, None].astype(np.float64)
    tab = np.where(dist[:, None] <= BAND_HALF, bias, NEG_INF)
    return jnp.asarray(tab, F32)


def _band_kernel(q_ref, k_ref, v_ref, tab_ref, o_ref, lse_ref, *, n):
    tq = BAND_TQ
    tkw = tq + 2 * BAND_HALF
    tiles = q_ref.shape[0] // tq
    last_tile = n // tq - 1
    items = [(t, h) for t in range(tiles) for h in range(N_HEADS_A)]
    info = []
    for t in range(tiles):
        i = pl.program_id(2) * tiles + t
        start = pl.multiple_of(jnp.clip(i * tq - BAND_HALF, 0, n - tkw), BAND_HALF)
        variant = jnp.minimum(i, 1) + (i == last_tile).astype(jnp.int32)
        info.append((start, variant))
    pairs = _HeadPairs(q_ref, k_ref, v_ref, tq, tkw)
    held = {}

    def issue(idx):
        t, h = items[idx]
        return pairs.scores(t * tq, info[t][0], h)

    def consume(idx, s):
        t, h = items[idx]
        s = s + tab_ref[info[t][1], h]
        m = jnp.max(s, axis=-1, keepdims=True)
        return jnp.exp(s - m).astype(BF16), m

    def finish(idx, res):
        t, h = items[idx]
        p, m = res
        out, den = pairs.values(info[t][0], h, p)
        held[h % 2] = (out / den, m + jnp.log(den))
        if h % 2 == 1:
            cols = pairs.cols(h)
            o_ref[t * tq:(t + 1) * tq, cols] = pairs.merge(held[0][0], held[1][0])
            lse_ref[t * tq:(t + 1) * tq, cols] = pairs.merge(held[0][1], held[1][1])

    _run_pipeline(len(items), BAND_LOOKAHEAD, issue, consume, finish)


def _band_call(pa_view, dilation):
    b, n, _ = pa_view.shape
    rows = BAND_TQ * BAND_TILES_PER_STEP
    w = WIDTH_A
    tab = _band_tables(dilation)
    qspec = pl.BlockSpec((None, rows, w), lambda bb, r, i: (bb, i, 3 * r))
    kspec = pl.BlockSpec((None, n, w), lambda bb, r, i: (bb, 0, 3 * r + 1))
    vspec = pl.BlockSpec((None, n, w), lambda bb, r, i: (bb, 0, 3 * r + 2))
    tspec = pl.BlockSpec(tab.shape, lambda bb, r, i: (0, 0, 0, 0))
    ospec = pl.BlockSpec((None, rows, w), lambda bb, r, i: (bb, i, r))
    shape = jax.ShapeDtypeStruct((b, n, dilation * w), F32)
    return pl.pallas_call(
        functools.partial(_band_kernel, n=n),
        grid=(b, dilation, n // rows),
        in_specs=[qspec, kspec, vspec, tspec],
        out_specs=[ospec, ospec],
        out_shape=[shape, shape],
        compiler_params=_cparams(("parallel", "parallel", "parallel")),
        name=f"band_d{dilation}",
    )(pa_view, pa_view, pa_view, tab)


def _mla_kernel(q_ref, k_ref, vt_ref, o_ref):
    tq = q_ref.shape[0]
    seq = k_ref.shape[0]
    tk = MLA_TK
    nh = MLA_HEADS_PER_STEP
    n_chunks = seq // tk
    items = [(u, h) for u in range(MLA_UNROLL) for h in range(nh)]
    look = MLA_LOOKAHEAD
    ones_rows = jnp.ones((MLA_DEN_ROWS, tk), BF16)

    def key_slice(chunk):
        return pl.ds(pl.multiple_of(chunk * tk, tk), tk)

    def score_matmul(chunk, h):
        k = k_ref[key_slice(chunk), h * LANES:(h + 1) * LANES]
        return _dot_nt(k, q_ref[:, h * LANES:(h + 1) * LANES])

    def value_matmul(chunk, h, p):
        vt = jnp.concatenate([vt_ref[h * V_HEAD:(h + 1) * V_HEAD, key_slice(chunk)], ones_rows], axis=0)
        return _dot(vt, p)

    def body(j, carry):
        state = list(carry[:2 * nh])
        scores = dict(zip(items[:look], carry[2 * nh:2 * nh + look]))
        pend_p, pend_alpha = carry[2 * nh + look:]
        pending = (jnp.maximum(j * MLA_UNROLL - 1, 0), nh - 1, pend_p, pend_alpha)
        ahead = []
        for idx, (u, h) in enumerate(items):
            la = idx + look
            if la < len(items):
                lu, lh = items[la]
                scores[lu, lh] = score_matmul(j * MLA_UNROLL + lu, lh)
            else:
                lu, lh = items[la - len(items)]
                ahead.append(score_matmul(jnp.minimum((j + 1) * MLA_UNROLL + lu, n_chunks - 1), lh))
            pc, ph, pp, pa = pending
            state[2 * ph + 1] = pa * state[2 * ph + 1] + value_matmul(pc, ph, pp)
            s = scores.pop((u, h))
            m_new = jnp.maximum(state[2 * h], jnp.max(s, axis=0, keepdims=True))
            alpha = jnp.exp2(state[2 * h] - m_new)
            state[2 * h] = m_new
            pending = (j * MLA_UNROLL + u, h, jnp.exp2(s - m_new).astype(BF16), alpha)
        return tuple(state) + tuple(ahead) + (pending[2], pending[3])

    init = (jnp.full((1, tq), NEG_INF, F32), jnp.zeros((V_HEAD + MLA_DEN_ROWS, tq), F32)) * nh
    init += tuple(score_matmul(u, h) for u, h in items[:look])
    init += (jnp.zeros((tk, tq), BF16), jnp.ones((1, tq), F32))
    res = lax.fori_loop(0, n_chunks // MLA_UNROLL, body, init)
    accs = [res[2 * h + 1] for h in range(nh)]
    accs[nh - 1] = res[-1] * accs[nh - 1] + value_matmul(n_chunks - 1, nh - 1, res[-2])
    out_t = jnp.concatenate([a[:V_HEAD] / a[V_HEAD:V_HEAD + 1] for a in accs], axis=0)
    o_ref[...] = out_t.T


def _mla_call(qm, km, vt, batch, seq):
    tq = MLA_TQ
    nh = MLA_HEADS_PER_STEP
    qspec = pl.BlockSpec((None, tq, nh * LANES), lambda b, g, i: (b, i, g))
    kspec = pl.BlockSpec((None, seq, nh * LANES), lambda b, g, i: (b, 0, g))
    vspec = pl.BlockSpec((nh * V_HEAD, seq), lambda b, g, i: (g, b))
    ospec = pl.BlockSpec((None, tq, nh * V_HEAD), lambda b, g, i: (b, i, g))
    return pl.pallas_call(
        _mla_kernel,
        grid=(batch, N_HEADS_B // nh, seq // tq),
        in_specs=[qspec, kspec, vspec],
        out_specs=ospec,
        out_shape=jax.ShapeDtypeStruct((batch, seq, WIDTH_B), F32),
        compiler_params=_cparams(("parallel", "parallel", "parallel")),
        name="mla",
    )(qm.reshape(batch, seq, -1), km.reshape(batch, seq, -1), vt)


def _na_variant(i, n_tiles):
    return jnp.minimum(i, 2) + jnp.maximum(i - (n_tiles - 3), 0)


def _na_kernel(q_ref, k_ref, v_ref, tab_ref, o_ref, *, rows):
    tq = NA_TILE_ROWS * GRID_W
    tkw = NA_KEY_ROWS * GRID_W
    tiles = q_ref.shape[0] // tq
    n_tiles = rows // NA_TILE_ROWS
    items = [(t, h) for t in range(tiles) for h in range(N_HEADS_C)]
    info = []
    for t in range(tiles):
        i = pl.program_id(1) * tiles + t
        base = jnp.clip(i * NA_TILE_ROWS - NA_ROWS // 2, 0, rows - NA_KEY_ROWS)
        info.append((pl.multiple_of(base * GRID_W, GRID_W), _na_variant(i, n_tiles)))
    pairs = _HeadPairs(q_ref, k_ref, v_ref, tq, tkw)
    held = {}

    def issue(idx):
        t, h = items[idx]
        return pairs.scores(t * tq, info[t][0], h)

    def consume(idx, s):
        t, h = items[idx]
        s = s + tab_ref[info[t][1], h]
        return jnp.exp(s - jnp.max(s, axis=-1, keepdims=True)).astype(BF16)

    def finish(idx, p):
        t, h = items[idx]
        out, den = pairs.values(info[t][0], h, p)
        held[h % 2] = out / den
        if h % 2 == 1:
            o_ref[t * tq:(t + 1) * tq, pairs.cols(h)] = pairs.merge(held[0], held[1])

    _run_pipeline(len(items), NA_LOOKAHEAD, issue, consume, finish)


def _na_tables(rpb, rows):
    r0 = np.array([0, 2, 4, rows - 4, rows - 2])
    base = np.clip(r0 - NA_ROWS // 2, 0, rows - NA_KEY_ROWS)
    r = r0[:, None] + np.arange(NA_TILE_ROWS)[None, :]
    row_start = np.clip(r - NA_ROWS // 2, 0, rows - NA_ROWS)
    krow = base[:, None] + np.arange(NA_KEY_ROWS)[None, :]
    drow = krow[:, None, :] - r[:, :, None]
    row_ok = (krow[:, None, :] >= row_start[:, :, None]) & (krow[:, None, :] < row_start[:, :, None] + NA_ROWS)
    c = np.arange(GRID_W)
    win_start = np.clip(c - NA_COLS // 2, 0, GRID_W - NA_COLS)
    col_ok = (c[None, :] >= win_start[:, None]) & (c[None, :] < win_start[:, None] + NA_COLS)
    dcol = np.clip(c[None, :] - c[:, None], -(NA_COLS - 1), NA_COLS - 1)
    ok = row_ok[:, :, None, :, None] & col_ok[None, None, :, None, :]
    di = np.clip(drow, -(NA_ROWS - 1), NA_ROWS - 1) + (NA_ROWS - 1)
    pick_col = (dcol[:, :, None] + NA_COLS - 1 == np.arange(2 * NA_COLS - 1)).astype(np.float32)
    pick_row = (di[..., None] == np.arange(2 * NA_ROWS - 1)).astype(np.float32)
    hi = lax.Precision.HIGHEST
    toeplitz = jnp.einsum("hab,cjb->hacj", rpb.astype(F32), pick_col, precision=hi)
    bias = jnp.einsum("vqka,hacj->vhqckj", pick_row, toeplitz, precision=hi)
    tab = jnp.where(jnp.asarray(ok)[:, None], bias, NEG_INF)
    return tab.reshape(5, N_HEADS_C, NA_TILE_ROWS * GRID_W, NA_KEY_ROWS * GRID_W)


def _na_call(pc, tab, batch, seq):
    rows = seq // GRID_W
    tq = NA_TILE_ROWS * GRID_W * NA_TILES_PER_STEP
    w = WIDTH_C
    qspec = pl.BlockSpec((None, tq, w), lambda b, i: (b, i, 0))
    kspec = pl.BlockSpec((None, seq, w), lambda b, i: (b, 0, 1))
    vspec = pl.BlockSpec((None, seq, w), lambda b, i: (b, 0, 2))
    tspec = pl.BlockSpec(tab.shape, lambda b, i: (0, 0, 0, 0))
    ospec = pl.BlockSpec((None, tq, w), lambda b, i: (b, i, 0))
    pc3 = pc.reshape(batch, seq, 3 * w)
    return pl.pallas_call(
        functools.partial(_na_kernel, rows=rows),
        grid=(batch, seq // tq),
        in_specs=[qspec, kspec, vspec, tspec],
        out_specs=ospec,
        out_shape=jax.ShapeDtypeStruct((batch, seq, w), F32),
        compiler_params=_cparams(("parallel", "parallel")),
        name="natten",
    )(pc3, pc3, pc3, tab)


def _mix_out_kernel(x_ref, o1_ref, o2_ref, o3_ref, l1_ref, l2_ref, l3_ref, ob_ref, oc_ref,
                    ga_ref, gb_ref, gc_ref, wo_ref, out_ref, o2_scr, o3_scr, l2_scr, l3_scr):
    tm = x_ref.shape[0]
    n_chunks = WIDTH_A // LANES

    def natural(view_ref, scr, dil):
        for r in range(dil):
            for c in range(n_chunks):
                col = r * WIDTH_A + c * LANES
                scr[c, pl.ds(r, tm // dil, stride=dil), :] = view_ref[:, col:col + LANES]
        return jnp.concatenate([scr[c] for c in range(n_chunks)], axis=-1)

    l1, o1 = l1_ref[...], o1_ref[...]
    l2, o2 = natural(l2_ref, l2_scr, DILATIONS[1]), natural(o2_ref, o2_scr, DILATIONS[1])
    l3, o3 = natural(l3_ref, l3_scr, DILATIONS[2]), natural(o3_ref, o3_scr, DILATIONS[2])
    mx = jnp.maximum(jnp.maximum(l1, l2), l3)
    w1, w2, w3 = jnp.exp(l1 - mx), jnp.exp(l2 - mx), jnp.exp(l3 - mx)
    oa = (w1 * o1 + w2 * o2 + w3 * o3) / (w1 + w2 + w3)
    ya = _rms(oa, ga_ref[...]).astype(BF16)
    yb = _rms(ob_ref[...], gb_ref[...]).astype(BF16)
    yc = _rms(oc_ref[...], gc_ref[...]).astype(BF16)
    y = _dot(ya, wo_ref[:WIDTH_A, :])
    y = y + _dot(yb, wo_ref[WIDTH_A:WIDTH_A + WIDTH_B, :])
    y = y + _dot(yc, wo_ref[WIDTH_A + WIDTH_B:, :])
    out_ref[...] = x_ref[...] + y


def _mix_out_call(x, o_parts, lse_parts, ob, oc, ga, gb, gc, wo):
    n, d = x.shape
    tm = ROW_TILE
    full = lambda a: pl.BlockSpec(a.shape, lambda i: (0,) * a.ndim)
    row = lambda w: pl.BlockSpec((tm, w), lambda i: (i, 0))
    view = lambda dil: pl.BlockSpec((tm // dil, dil * WIDTH_A), lambda i: (i, 0))
    return pl.pallas_call(
        _mix_out_kernel,
        grid=(n // tm,),
        in_specs=[row(d)] + [view(dil) for dil in DILATIONS] * 2
                 + [row(WIDTH_B), row(WIDTH_C), full(ga), full(gb), full(gc), full(wo)],
        out_specs=row(d),
        out_shape=jax.ShapeDtypeStruct((n, d), F32),
        scratch_shapes=[pltpu.VMEM((WIDTH_A // LANES, tm, LANES), F32)] * 4,
        compiler_params=_cparams(("parallel",)),
        name="mix_out",
    )(x, *o_parts, *lse_parts, ob, oc, ga, gb, gc, wo)


def _silu(u):
    return u * (1.0 / (1.0 + jnp.exp(-u)))


def _ffn_kernel(x_ref, g_ref, w1_ref, w3_ref, w2_ref, out_ref):
    x = x_ref[...]
    h = _rms(x, g_ref[...]).astype(BF16)
    a = (_silu(_dot(h, w1_ref[...])) * _dot(h, w3_ref[...])).astype(BF16)
    out_ref[...] = x + _dot(a, w2_ref[...])


def _ffn_call(x, g, w1, w3, w2):
    n, d = x.shape
    tm = ROW_TILE
    resident = lambda a: pl.BlockSpec(a.shape, lambda i: (0, 0), pipeline_mode=pl.Buffered(1))
    return pl.pallas_call(
        _ffn_kernel,
        grid=(n // tm,),
        in_specs=[pl.BlockSpec((tm, d), lambda i: (i, 0)), pl.BlockSpec((1, d), lambda i: (0, 0)),
                  resident(w1), resident(w3), resident(w2)],
        out_specs=pl.BlockSpec((tm, d), lambda i: (i, 0)),
        out_shape=jax.ShapeDtypeStruct((n, d), F32),
        compiler_params=_cparams(("parallel",)),
        name="ffn",
    )(x, g, w1, w3, w2)


def _router_kernel(x_ref, g_ref, wr_ref, h_ref, e_ref, gate_ref):
    h = _rms(x_ref[...], g_ref[...])
    _to_token_tiles(h_ref, h)
    logits = jnp.dot(h, wr_ref[...], precision=lax.Precision.HIGHEST, preferred_element_type=F32)
    lane = lax.broadcasted_iota(jnp.int32, logits.shape, 1).astype(F32)
    logits = jnp.where(lane < N_EXPERTS, logits, -jnp.inf)
    m1 = jnp.max(logits, axis=-1, keepdims=True)
    i1 = jnp.min(jnp.where(logits == m1, lane, float(LANES)), axis=-1, keepdims=True)
    rest = jnp.where(lane == i1, -jnp.inf, logits)
    m2 = jnp.max(rest, axis=-1, keepdims=True)
    i2 = jnp.min(jnp.where(rest == m2, lane, float(LANES)), axis=-1, keepdims=True)
    e = jnp.exp(m2 - m1)
    den = 1.0 + e
    e_ref[...] = jnp.where(lane == 0.0, i1, jnp.where(lane == 1.0, i2, 0.0)).astype(jnp.int32)
    gate_ref[...] = jnp.where(lane == 0.0, 1.0 / den, jnp.where(lane == 1.0, e / den, 0.0))


def _router_call(x, g, wr_pad):
    n, d = x.shape
    tm = ROW_TILE
    row = lambda w: pl.BlockSpec((tm, w), lambda i: (i, 0))
    full = lambda a: pl.BlockSpec(a.shape, lambda i: (0,) * a.ndim)
    return pl.pallas_call(
        _router_kernel,
        grid=(n // tm,),
        in_specs=[row(d), full(g), full(wr_pad)],
        out_specs=[pl.BlockSpec((tm * TOKEN_ROWS, LANES), lambda i: (i, 0)), row(LANES), row(LANES)],
        out_shape=[jax.ShapeDtypeStruct((n * TOKEN_ROWS, LANES), F32), jax.ShapeDtypeStruct((n, LANES), jnp.int32),
                   jax.ShapeDtypeStruct((n, LANES), F32)],
        compiler_params=_cparams(("parallel",)),
        name="router",
    )(x, g, wr_pad)


TOKEN_ROWS = 8


def _to_token_tiles(dst_ref, x):
    tm, d = x.shape
    assert d == TOKEN_ROWS * LANES
    for j in range(TOKEN_ROWS):
        dst_ref[pl.ds(j, tm, stride=TOKEN_ROWS), :] = x[:, j * LANES:(j + 1) * LANES]


def _from_token_tiles(src_ref, tm):
    return jnp.concatenate([src_ref[pl.ds(j, tm, stride=TOKEN_ROWS), :] for j in range(TOKEN_ROWS)], axis=-1)


def _row_copy(src_hbm, row, dst_ref, r, sem):
    src = src_hbm.at[pl.ds(pl.multiple_of(row * TOKEN_ROWS, TOKEN_ROWS), TOKEN_ROWS)]
    return pltpu.make_async_copy(src, dst_ref.at[pl.ds(pl.multiple_of(r * TOKEN_ROWS, TOKEN_ROWS), TOKEN_ROWS)], sem)


def _start_row_gather(idx_ref, base, src_hbm, dst_ref, sem, count):
    def body(r, c):
        _row_copy(src_hbm, idx_ref[base + r], dst_ref, r, sem).start()
        return c
    lax.fori_loop(0, count, body, 0, unroll=GATHER_UNROLL)


def _wait_row_gather(src_hbm, dst_ref, sem, count):
    pltpu.make_async_copy(src_hbm.at[pl.ds(0, count * TOKEN_ROWS)], dst_ref, sem).wait()


def _moe_ffn_kernel(tile_e_ref, tile_ok_ref, row_tok_ref, h_hbm, w1_ref, w3_ref, w2_ref, out_ref,
                    xg_ref, hb_ref, acc_ref, sem):
    i = pl.program_id(0)
    f = pl.program_id(1)
    n_tiles = pl.num_programs(0)
    tm = MOE_TM
    slot = i % 2

    @pl.when(f == 0)
    def _():
        @pl.when(i == 0)
        def _():
            _start_row_gather(row_tok_ref, 0, h_hbm, xg_ref.at[0], sem.at[0], tm)

        _wait_row_gather(h_hbm, xg_ref.at[slot], sem.at[slot], tm)

        @pl.when(i + 1 < n_tiles)
        def _():
            _start_row_gather(row_tok_ref, (i + 1) * tm, h_hbm, xg_ref.at[1 - slot], sem.at[1 - slot], tm)

        hb_ref[...] = _from_token_tiles(xg_ref.at[slot], tm).astype(BF16)
        acc_ref[...] = jnp.zeros_like(acc_ref)

    @pl.when(tile_ok_ref[i] != 0)
    def _():
        h = hb_ref[...]
        a = (_silu(_dot(h, w1_ref[...])) * _dot(h, w3_ref[...])).astype(BF16)
        acc_ref[...] += _dot(a, w2_ref[...])

    @pl.when(f == pl.num_programs(1) - 1)
    def _():
        _to_token_tiles(out_ref, acc_ref[...])


def _moe_ffn_call(tile_e, tile_ok, row_tok, h, w1, w3, w2, layer):
    d = w1.shape[2]
    n_tiles = tile_e.shape[0]
    ff = w1.shape[3]
    tm, tf = MOE_TM, MOE_TF
    grid_spec = pltpu.PrefetchScalarGridSpec(
        num_scalar_prefetch=3,
        grid=(n_tiles, ff // tf),
        in_specs=[pl.BlockSpec(memory_space=pl.ANY),
                  pl.BlockSpec((None, None, d, tf), lambda i, f, te, tv, rt: (layer, te[i], 0, f)),
                  pl.BlockSpec((None, None, d, tf), lambda i, f, te, tv, rt: (layer, te[i], 0, f)),
                  pl.BlockSpec((None, None, tf, d), lambda i, f, te, tv, rt: (layer, te[i], f, 0))],
        out_specs=pl.BlockSpec((tm * TOKEN_ROWS, LANES), lambda i, f, te, tv, rt: (i, 0)),
        scratch_shapes=[pltpu.VMEM((2, tm * TOKEN_ROWS, LANES), F32), pltpu.VMEM((tm, d), BF16),
                        pltpu.VMEM((tm, d), F32), pltpu.SemaphoreType.DMA((2,))],
    )
    return pl.pallas_call(
        _moe_ffn_kernel,
        grid_spec=grid_spec,
        out_shape=jax.ShapeDtypeStruct((n_tiles * tm * TOKEN_ROWS, LANES), F32),
        compiler_params=_cparams(("arbitrary", "arbitrary")),
        name="moe_ffn",
    )(tile_e, tile_ok, row_tok, h, w1, w3, w2)


def _combine_kernel(d0_ref, d1_ref, y_hbm, x_ref, gate_ref, gain_ref, out_ref, b0_ref, b1_ref, sem, *, final_norm):
    i = pl.program_id(0)
    n_tiles = pl.num_programs(0)
    tm = COMBINE_TM
    slot = i % 2

    def start(tile, s):
        _start_row_gather(d0_ref, tile * tm, y_hbm, b0_ref.at[s], sem.at[0, s], tm)
        _start_row_gather(d1_ref, tile * tm, y_hbm, b1_ref.at[s], sem.at[1, s], tm)

    @pl.when(i == 0)
    def _():
        start(0, 0)

    _wait_row_gather(y_hbm, b0_ref.at[slot], sem.at[0, slot], tm)
    _wait_row_gather(y_hbm, b1_ref.at[slot], sem.at[1, slot], tm)

    @pl.when(i + 1 < n_tiles)
    def _():
        start(i + 1, 1 - slot)

    gates = gate_ref[...]
    y0 = _from_token_tiles(b0_ref.at[slot], tm)
    y1 = _from_token_tiles(b1_ref.at[slot], tm)
    out = x_ref[...] + (gates[:, 0:1] * y0 + gates[:, 1:2] * y1)
    out_ref[...] = _rms(out, gain_ref[...]) if final_norm else out


def _combine_call(d0, d1, y, x, gates, gain, final_norm):
    n, d = x.shape
    tm = COMBINE_TM
    grid_spec = pltpu.PrefetchScalarGridSpec(
        num_scalar_prefetch=2,
        grid=(n // tm,),
        in_specs=[pl.BlockSpec(memory_space=pl.ANY),
                  pl.BlockSpec((tm, d), lambda i, a, b: (i, 0)),
                  pl.BlockSpec((tm, LANES), lambda i, a, b: (i, 0)),
                  pl.BlockSpec((1, d), lambda i, a, b: (0, 0))],
        out_specs=pl.BlockSpec((tm, d), lambda i, a, b: (i, 0)),
        scratch_shapes=[pltpu.VMEM((2, tm * TOKEN_ROWS, LANES), F32)] * 2 + [pltpu.SemaphoreType.DMA((2, 2))],
    )
    return pl.pallas_call(
        functools.partial(_combine_kernel, final_norm=final_norm),
        grid_spec=grid_spec,
        out_shape=jax.ShapeDtypeStruct((n, d), F32),
        compiler_params=_cparams(("arbitrary",)),
        name="moe_combine",
    )(d0, d1, y, x, gates, gain)


def _moe_plan(top_e, n_tiles):
    n_assign = top_e.shape[0] * TOP_K
    flat_e = top_e.reshape(n_assign)
    onehot = (flat_e[:, None] == jnp.arange(N_EXPERTS, dtype=jnp.int32)[None, :]).astype(jnp.int32)
    csum = jnp.cumsum(onehot, axis=0)
    rank = jnp.take_along_axis(csum, flat_e[:, None], axis=1)[:, 0] - 1
    counts = csum[-1]
    padded = (counts + MOE_TM - 1) // MOE_TM * MOE_TM
    pend = jnp.cumsum(padded)
    pstart = pend - padded
    dest = pstart[flat_e] + rank
    tile_start = jnp.arange(n_tiles, dtype=jnp.int32) * MOE_TM
    tile_e = jnp.minimum(jnp.searchsorted(pend, tile_start, side="right"), N_EXPERTS - 1).astype(jnp.int32)
    tile_ok = (tile_start < pend[-1]).astype(jnp.int32)
    order = jnp.argsort(flat_e, stable=True).astype(jnp.int32)
    row_e = jnp.repeat(tile_e, MOE_TM)
    r = jnp.arange(n_tiles * MOE_TM, dtype=jnp.int32) - pstart[row_e]
    src = jnp.clip((jnp.cumsum(counts) - counts)[row_e] + r, 0, n_assign - 1)
    row_tok = jnp.where(r < counts[row_e], order[src] // TOP_K, 0).astype(jnp.int32)
    dest = dest.reshape(-1, TOP_K).astype(jnp.int32)
    return tile_e, tile_ok, row_tok, dest[:, 0], dest[:, 1]


def _moe_layer(x, g, wr_pad, w1, w3, w2, layer, gain, final_norm):
    n = x.shape[0]
    h, top_e, gates = _router_call(x, g, wr_pad)
    n_tiles = -(-(n * TOP_K + N_EXPERTS * (MOE_TM - 1)) // MOE_TM)
    tile_e, tile_ok, row_tok, d0, d1 = _moe_plan(top_e[:, :TOP_K], n_tiles)
    y = _moe_ffn_call(tile_e, tile_ok, row_tok, h, w1, w3, w2, layer)
    return _combine_call(d0, d1, y, x, gates, gain, final_norm)


def _norm_kernel(x_ref, g_ref, o_ref):
    o_ref[...] = _rms(x_ref[...], g_ref[...])


def _norm_call(x, g):
    n, d = x.shape
    tm = ROW_TILE
    return pl.pallas_call(
        _norm_kernel,
        grid=(n // tm,),
        in_specs=[pl.BlockSpec((tm, d), lambda i: (i, 0)), pl.BlockSpec((1, d), lambda i: (0, 0))],
        out_specs=pl.BlockSpec((tm, d), lambda i: (i, 0)),
        out_shape=jax.ShapeDtypeStruct((n, d), F32),
        compiler_params=_cparams(("parallel",)),
        name="final_norm",
    )(x, g)


def _rope_tables(seq):
    inv = ROPE_BASE ** (-jnp.arange(0, QK_ROPE, 2, dtype=F32) / QK_ROPE)
    ang = jnp.arange(seq)[:, None].astype(F32) * inv[None, :]
    cos, sin = jnp.cos(ang), jnp.sin(ang)
    pad = LANES - QK_NOPE - QK_ROPE
    zeros = lambda w: jnp.zeros((seq, w), F32)
    ctab = jnp.concatenate([jnp.ones((seq, QK_NOPE), F32), cos, cos, zeros(pad)], axis=1)
    stab = jnp.concatenate([zeros(QK_NOPE), sin, sin, zeros(pad)], axis=1)
    sa = jnp.concatenate([zeros(QK_NOPE), -sin, zeros(QK_ROPE // 2 + pad)], axis=1)
    sb = jnp.concatenate([zeros(QK_NOPE + QK_ROPE // 2), sin, zeros(pad)], axis=1)
    return ctab, stab, sa, sb


def _split_w_in(w_in):
    d = w_in.shape[0]
    bounds = np.cumsum([WIDTH_A, WIDTH_A, WIDTH_A, Q_LORA, KV_LORA, QK_ROPE, WIDTH_C, WIDTH_C])
    qa, ka, va, cq, ckv, kr, qc, kc, vc = jnp.split(w_in, bounds.tolist(), axis=1)
    half = QK_ROPE // 2
    z_lo = jnp.zeros((d, QK_NOPE), w_in.dtype)
    z_hi = jnp.zeros((d, LANES - QK_NOPE - QK_ROPE), w_in.dtype)
    rope_blk = jnp.concatenate([z_lo, kr, z_hi], axis=1)
    swap_blk = jnp.concatenate([z_lo, -kr[:, half:], kr[:, :half], z_hi], axis=1)
    score_scale = HEAD_DIM ** -0.5
    wa = jnp.concatenate([qa * score_scale, ka, va], axis=1).astype(BF16)
    wc = jnp.concatenate([qc * score_scale, kc, vc], axis=1).astype(BF16)
    wb = jnp.concatenate([cq, ckv, rope_blk, swap_blk], axis=1).astype(BF16)
    return jnp.concatenate([wa, wc, wb], axis=1)


def _split_w_uq(w_uq):
    r = w_uq.shape[0]
    w = w_uq.reshape(r, N_HEADS_B, QK_NOPE + QK_ROPE)
    nope, rope = w[..., :QK_NOPE], w[..., QK_NOPE:]
    z_hi = jnp.zeros((r, N_HEADS_B, LANES - QK_NOPE - QK_ROPE), w_uq.dtype)
    return jnp.concatenate([nope, rope, z_hi], axis=-1).reshape(r, -1).astype(BF16)


def _split_w_ukv(w_ukv):
    r = w_ukv.shape[0]
    w = w_ukv.reshape(r, N_HEADS_B, QK_NOPE + V_HEAD)
    k_nope, v = w[..., :QK_NOPE], w[..., QK_NOPE:]
    wk = jnp.concatenate([k_nope, jnp.zeros((r, N_HEADS_B, LANES - QK_NOPE), w_ukv.dtype)], axis=-1)
    return wk.reshape(r, -1).astype(BF16), v.reshape(r, -1).T.astype(BF16)


def kernel(x, g_mix, w_in, g_q, g_kv, w_uq, w_ukv, rpb, g_out_a, g_out_b, g_out_c, w_o, g_ffn, w1, w3, w2,
           w_router, e_w1, e_w3, e_w2, g_final):
    batch, seq, d = x.shape
    n = batch * seq
    depth = g_mix.shape[0]
    rows = seq // GRID_W
    rope_tabs = _rope_tables(seq)
    ew1, ew3, ew2 = e_w1.astype(BF16), e_w3.astype(BF16), e_w2.astype(BF16)
    xf = x.reshape(n, d)
    for layer in range(depth):
        win = _split_w_in(w_in[layer])
        wq = _split_w_uq(w_uq[layer])
        wk, wvt = _split_w_ukv(w_ukv[layer])
        pa, pa4, pa16, pc, qm, km, vt = _proj_call(xf, g_mix[layer][None], win, g_q[layer][None], g_kv[layer][None],
                                                   wq, wk, wvt, rope_tabs, seq)
        o_parts, lse_parts = [], []
        for dil, view in zip(DILATIONS, (pa, pa4, pa16)):
            o, lse = _band_call(view.reshape(batch, seq // dil, dil * 3 * WIDTH_A), dil)
            o_parts.append(o.reshape(n // dil, dil * WIDTH_A))
            lse_parts.append(lse.reshape(n // dil, dil * WIDTH_A))
        ob = _mla_call(qm, km, vt, batch, seq).reshape(n, WIDTH_B)
        oc = _na_call(pc, _na_tables(rpb[layer], rows), batch, seq).reshape(n, WIDTH_C)
        xf = _mix_out_call(xf, o_parts, lse_parts, ob, oc, g_out_a[layer][None], g_out_b[layer][None],
                           g_out_c[layer][None], w_o[layer].astype(BF16))
        j = layer // 2
        if layer % 2 == 0:
            xf = _ffn_call(xf, g_ffn[layer][None], w1[j].astype(BF16), w3[j].astype(BF16), w2[j].astype(BF16))
        else:
            wr_pad = jnp.pad(w_router[j], ((0, 0), (0, LANES - N_EXPERTS)))
            xf = _moe_layer(xf, g_ffn[layer][None], wr_pad, ew1, ew3, ew2, j, g_final[None], layer == depth - 1)
    if depth % 2 == 1:
        xf = _norm_call(xf, g_final[None])
    return xf.reshape(batch, seq, d)
```

```python
import functools
import math

import numpy as np
import jax
import jax.numpy as jnp
from jax import lax
from jax.experimental import pallas as pl
from jax.experimental.pallas import tpu as pltpu

F32 = jnp.float32
BF16 = jnp.bfloat16

LANES = 128
V7X_VMEM_LIMIT_BYTES = 52 * 1024 * 1024

HEAD_DIM = 64
N_HEADS_A = 6
DILATIONS = (1, 4, 16)
BAND_HALF = 64
N_HEADS_B = 6
Q_LORA = 384
KV_LORA = 256
QK_NOPE = 64
QK_ROPE = 32
V_HEAD = 64
ROPE_BASE = 10000.0
N_HEADS_C = 4
GRID_W = 64
NA_ROWS = 8
NA_COLS = 16
WIDTH_A = N_HEADS_A * HEAD_DIM
WIDTH_B = N_HEADS_B * V_HEAD
WIDTH_C = N_HEADS_C * HEAD_DIM
N_EXPERTS = 8
TOP_K = 2
RMS_EPS = 1e-6
NEG_INF = -1e30

ROW_TILE = 512
MLA_TQ = 256
MLA_TK = 256
MLA_UNROLL = 32
MLA_LOOKAHEAD = 4
MLA_HEADS_PER_STEP = 6
MLA_DEN_ROWS = 16
MLA_Q_PRESCALE = (QK_NOPE + QK_ROPE) ** -0.5 * math.log2(math.e)
BAND_TQ = 128
BAND_TILES_PER_STEP = 4
BAND_LOOKAHEAD = 6
NA_TILE_ROWS = 2
NA_KEY_ROWS = 10
NA_TILES_PER_STEP = 4
NA_LOOKAHEAD = 16
MOE_TM = 512
MOE_TF = 1792
COMBINE_TM = 256
GATHER_UNROLL = 8


def _cparams(semantics):
    return pltpu.CompilerParams(dimension_semantics=semantics, vmem_limit_bytes=V7X_VMEM_LIMIT_BYTES)


def _rms(x, g):
    return x * lax.rsqrt(jnp.mean(x * x, axis=-1, keepdims=True) + RMS_EPS) * g


def _dot(a, b):
    return jnp.dot(a, b, preferred_element_type=F32)


def _dot_nt(a, b):
    return lax.dot_general(a, b, (((1,), (1,)), ((), ())), preferred_element_type=F32)


def _proj_kernel(x_ref, g_ref, win_ref, gq_ref, gkv_ref, wq_ref, wk_ref, wvt_ref,
                 ct_ref, st_ref, sa_ref, sb_ref, pa_ref, pa4_ref, pa16_ref, pc_ref, qm_ref, km_ref, vt_ref, pa_scr):
    tm = x_ref.shape[0]
    h = _rms(x_ref[...], g_ref[...]).astype(BF16)
    proj = _dot(h, win_ref[...])
    pa = proj[:, :3 * WIDTH_A]
    pb = proj[:, 3 * (WIDTH_A + WIDTH_C):]
    pa_ref[...] = pa.astype(BF16)
    n_chunks = pa.shape[1] // LANES
    for c in range(n_chunks):
        pa_scr[c] = pa[:, c * LANES:(c + 1) * LANES]
    for dil, view_ref in ((DILATIONS[1], pa4_ref), (DILATIONS[2], pa16_ref)):
        for r in range(dil):
            for c in range(n_chunks):
                col = r * 3 * WIDTH_A + c * LANES
                view_ref[:, col:col + LANES] = pa_scr[c, pl.ds(r, tm // dil, stride=dil), :].astype(BF16)
    pc_ref[...] = proj[:, 3 * WIDTH_A:3 * (WIDTH_A + WIDTH_C)].astype(BF16)
    hq = _rms(pb[:, :Q_LORA], gq_ref[...]).astype(BF16)
    hkv = _rms(pb[:, Q_LORA:Q_LORA + KV_LORA], gkv_ref[...]).astype(BF16)
    r1 = pb[:, Q_LORA + KV_LORA:Q_LORA + KV_LORA + LANES]
    r2 = pb[:, Q_LORA + KV_LORA + LANES:]
    ct = ct_ref[...]
    st = st_ref[...]
    sa = sa_ref[...]
    sb = sb_ref[...]
    half = QK_ROPE // 2
    qa = _dot(hq, wq_ref[...])
    kn = _dot(hkv, wk_ref[...])
    kr = r1 * ct + r2 * st
    for hd in range(N_HEADS_B):
        sl = slice(hd * LANES, (hd + 1) * LANES)
        q = qa[:, sl]
        q = q * ct + pltpu.roll(q, LANES - half, 1) * sa + pltpu.roll(q, half, 1) * sb
        qm_ref[:, sl] = (q * MLA_Q_PRESCALE).astype(BF16)
        km_ref[:, sl] = (kn[:, sl] + kr).astype(BF16)
    vt_ref[...] = _dot_nt(wvt_ref[...], hkv).astype(BF16)


def _proj_call(x, g, win, gq, gkv, wq, wk, wvt, tabs, seq):
    n, d = x.shape
    tm = ROW_TILE
    tiles_per_seq = seq // tm
    full = lambda a: pl.BlockSpec(a.shape, lambda i: (0,) * a.ndim)
    row = lambda w: pl.BlockSpec((tm, w), lambda i: (i, 0))
    tab = pl.BlockSpec((tm, LANES), lambda i: (i % tiles_per_seq, 0))
    view = lambda dil: pl.BlockSpec((tm // dil, dil * 3 * WIDTH_A), lambda i: (i, 0))
    hb = N_HEADS_B * LANES
    return pl.pallas_call(
        _proj_kernel,
        grid=(n // tm,),
        in_specs=[row(d), full(g), full(win), full(gq), full(gkv), full(wq), full(wk), full(wvt)] + [tab] * 4,
        out_specs=[row(3 * WIDTH_A)] + [view(dil) for dil in DILATIONS[1:]] + [row(3 * WIDTH_C), row(hb), row(hb),
                   pl.BlockSpec((WIDTH_B, tm), lambda i: (0, i))],
        out_shape=[jax.ShapeDtypeStruct((n, 3 * WIDTH_A), BF16)]
                  + [jax.ShapeDtypeStruct((n // dil, dil * 3 * WIDTH_A), BF16) for dil in DILATIONS[1:]]
                  + [jax.ShapeDtypeStruct((n, 3 * WIDTH_C), BF16),
                   jax.ShapeDtypeStruct((n, hb), BF16), jax.ShapeDtypeStruct((n, hb), BF16),
                   jax.ShapeDtypeStruct((WIDTH_B, n), BF16)],
        scratch_shapes=[pltpu.VMEM((3 * WIDTH_A // LANES, tm, LANES), F32)],
        compiler_params=_cparams(("parallel",)),
        name="proj",
    )(x, g, win, gq, gkv, wq, wk, wvt, *tabs)


def _run_pipeline(n_items, look, issue, consume, finish):
    if look >= n_items:
        results = [consume(i, s) for i, s in enumerate([issue(i) for i in range(n_items)])]
        for i, res in enumerate(results):
            finish(i, res)
        return
    inflight = {i: issue(i) for i in range(min(look, n_items))}
    pending = None
    for i in range(n_items):
        if i + look < n_items:
            inflight[i + look] = issue(i + look)
        if pending is not None:
            finish(*pending)
        pending = (i, consume(i, inflight.pop(i)))
    finish(*pending)


class _HeadPairs:
    def __init__(self, q_ref, k_ref, v_ref, tq, tkw):
        self.q_ref, self.k_ref, self.v_ref, self.tq, self.tkw = q_ref, k_ref, v_ref, tq, tkw
        lane = lax.broadcasted_iota(jnp.int32, (tq, LANES), 1)
        self.low = lane < HEAD_DIM
        self.ones = jnp.ones((tkw, LANES), BF16)

    @staticmethod
    def cols(h):
        return slice((h // 2) * LANES, (h // 2 + 1) * LANES)

    def scores(self, q_start, k_start, h):
        q = self.q_ref[q_start:q_start + self.tq, self.cols(h)]
        q = jnp.where(self.low if h % 2 == 0 else ~self.low, q, jnp.zeros_like(q))
        return _dot_nt(q, self.k_ref[pl.ds(k_start, self.tkw), self.cols(h)])

    def values(self, k_start, h, p):
        v = jnp.concatenate([self.v_ref[pl.ds(k_start, self.tkw), self.cols(h)], self.ones], axis=-1)
        o = _dot(p, v)
        return o[:, :LANES], o[:, LANES:]

    def merge(self, even, odd):
        return jnp.where(self.low, even, odd)


def _band_tables(dilation):
    tq, tkw = BAND_TQ, BAND_TQ + 2 * BAND_HALF
    shift = np.array([0, -BAND_HALF, -2 * BAND_HALF])
    rel = shift[:, None, None] + np.arange(tkw)[None, None, :] - np.arange(tq)[None, :, None]
    dist = np.abs(rel)
    slopes = 2.0 ** (-8.0 * np.arange(1, N_HEADS_A + 1) / N_HEADS_A)
    bias = -(slopes[None, :, None, None] * dilation) * dist[:, None].astype(np.float64)
    tab = np.where(dist[:, None] <= BAND_HALF, bias, NEG_INF)
    return jnp.asarray(tab, F32)


def _band_kernel(q_ref, k_ref, v_ref, tab_ref, o_ref, lse_ref, *, n):
    tq = BAND_TQ
    tkw = tq + 2 * BAND_HALF
    tiles = q_ref.shape[0] // tq
    last_tile = n // tq - 1
    items = [(t, h) for t in range(tiles) for h in range(N_HEADS_A)]
    info = []
    for t in range(tiles):
        i = pl.program_id(2) * tiles + t
        start = pl.multiple_of(jnp.clip(i * tq - BAND_HALF, 0, n - tkw), BAND_HALF)
        variant = jnp.minimum(i, 1) + (i == last_tile).astype(jnp.int32)
        info.append((start, variant))
    pairs = _HeadPairs(q_ref, k_ref, v_ref, tq, tkw)
    held = {}

    def issue(idx):
        t, h = items[idx]
        return pairs.scores(t * tq, info[t][0], h)

    def consume(idx, s):
        t, h = items[idx]
        s = s + tab_ref[info[t][1], h]
        m = jnp.max(s, axis=-1, keepdims=True)
        return jnp.exp(s - m).astype(BF16), m

    def finish(idx, res):
        t, h = items[idx]
        p, m = res
        out, den = pairs.values(info[t][0], h, p)
        held[h % 2] = (out / den, m + jnp.log(den))
        if h % 2 == 1:
            cols = pairs.cols(h)
            o_ref[t * tq:(t + 1) * tq, cols] = pairs.merge(held[0][0], held[1][0])
            lse_ref[t * tq:(t + 1) * tq, cols] = pairs.merge(held[0][1], held[1][1])

    _run_pipeline(len(items), BAND_LOOKAHEAD, issue, consume, finish)


def _band_call(pa_view, dilation):
    b, n, _ = pa_view.shape
    rows = BAND_TQ * BAND_TILES_PER_STEP
    w = WIDTH_A
    tab = _band_tables(dilation)
    qspec = pl.BlockSpec((None, rows, w), lambda bb, r, i: (bb, i, 3 * r))
    kspec = pl.BlockSpec((None, n, w), lambda bb, r, i: (bb, 0, 3 * r + 1))
    vspec = pl.BlockSpec((None, n, w), lambda bb, r, i: (bb, 0, 3 * r + 2))
    tspec = pl.BlockSpec(tab.shape, lambda bb, r, i: (0, 0, 0, 0))
    ospec = pl.BlockSpec((None, rows, w), lambda bb, r, i: (bb, i, r))
    shape = jax.ShapeDtypeStruct((b, n, dilation * w), F32)
    return pl.pallas_call(
        functools.partial(_band_kernel, n=n),
        grid=(b, dilation, n // rows),
        in_specs=[qspec, kspec, vspec, tspec],
        out_specs=[ospec, ospec],
        out_shape=[shape, shape],
        compiler_params=_cparams(("parallel", "parallel", "parallel")),
        name=f"band_d{dilation}",
    )(pa_view, pa_view, pa_view, tab)


def _mla_kernel(q_ref, k_ref, vt_ref, o_ref):
    tq = q_ref.shape[0]
    seq = k_ref.shape[0]
    tk = MLA_TK
    nh = MLA_HEADS_PER_STEP
    n_chunks = seq // tk
    items = [(u, h) for u in range(MLA_UNROLL) for h in range(nh)]
    look = MLA_LOOKAHEAD
    ones_rows = jnp.ones((MLA_DEN_ROWS, tk), BF16)

    def key_slice(chunk):
        return pl.ds(pl.multiple_of(chunk * tk, tk), tk)

    def score_matmul(chunk, h):
        k = k_ref[key_slice(chunk), h * LANES:(h + 1) * LANES]
        return _dot_nt(k, q_ref[:, h * LANES:(h + 1) * LANES])

    def value_matmul(chunk, h, p):
        vt = jnp.concatenate([vt_ref[h * V_HEAD:(h + 1) * V_HEAD, key_slice(chunk)], ones_rows], axis=0)
        return _dot(vt, p)

    def body(j, carry):
        state = list(carry[:2 * nh])
        scores = dict(zip(items[:look], carry[2 * nh:2 * nh + look]))
        pend_p, pend_alpha = carry[2 * nh + look:]
        pending = (jnp.maximum(j * MLA_UNROLL - 1, 0), nh - 1, pend_p, pend_alpha)
        ahead = []
        for idx, (u, h) in enumerate(items):
            la = idx + look
            if la < len(items):
                lu, lh = items[la]
                scores[lu, lh] = score_matmul(j * MLA_UNROLL + lu, lh)
            else:
                lu, lh = items[la - len(items)]
                ahead.append(score_matmul(jnp.minimum((j + 1) * MLA_UNROLL + lu, n_chunks - 1), lh))
            pc, ph, pp, pa = pending
            state[2 * ph + 1] = pa * state[2 * ph + 1] + value_matmul(pc, ph, pp)
            s = scores.pop((u, h))
            m_new = jnp.maximum(state[2 * h], jnp.max(s, axis=0, keepdims=True))
            alpha = jnp.exp2(state[2 * h] - m_new)
            state[2 * h] = m_new
            pending = (j * MLA_UNROLL + u, h, jnp.exp2(s - m_new).astype(BF16), alpha)
        return tuple(state) + tuple(ahead) + (pending[2], pending[3])

    init = (jnp.full((1, tq), NEG_INF, F32), jnp.zeros((V_HEAD + MLA_DEN_ROWS, tq), F32)) * nh
    init += tuple(score_matmul(u, h) for u, h in items[:look])
    init += (jnp.zeros((tk, tq), BF16), jnp.ones((1, tq), F32))
    res = lax.fori_loop(0, n_chunks // MLA_UNROLL, body, init)
    accs = [res[2 * h + 1] for h in range(nh)]
    accs[nh - 1] = res[-1] * accs[nh - 1] + value_matmul(n_chunks - 1, nh - 1, res[-2])
    out_t = jnp.concatenate([a[:V_HEAD] / a[V_HEAD:V_HEAD + 1] for a in accs], axis=0)
    o_ref[...] = out_t.T


def _mla_call(qm, km, vt, batch, seq):
    tq = MLA_TQ
    nh = MLA_HEADS_PER_STEP
    qspec = pl.BlockSpec((None, tq, nh * LANES), lambda b, g, i: (b, i, g))
    kspec = pl.BlockSpec((None, seq, nh * LANES), lambda b, g, i: (b, 0, g))
    vspec = pl.BlockSpec((nh * V_HEAD, seq), lambda b, g, i: (g, b))
    ospec = pl.BlockSpec((None, tq, nh * V_HEAD), lambda b, g, i: (b, i, g))
    return pl.pallas_call(
        _mla_kernel,
        grid=(batch, N_HEADS_B // nh, seq // tq),
        in_specs=[qspec, kspec, vspec],
        out_specs=ospec,
        out_shape=jax.ShapeDtypeStruct((batch, seq, WIDTH_B), F32),
        compiler_params=_cparams(("parallel", "parallel", "parallel")),
        name="mla",
    )(qm.reshape(batch, seq, -1), km.reshape(batch, seq, -1), vt)


def _na_variant(i, n_tiles):
    return jnp.minimum(i, 2) + jnp.maximum(i - (n_tiles - 3), 0)


def _na_kernel(q_ref, k_ref, v_ref, tab_ref, o_ref, *, rows):
    tq = NA_TILE_ROWS * GRID_W
    tkw = NA_KEY_ROWS * GRID_W
    tiles = q_ref.shape[0] // tq
    n_tiles = rows // NA_TILE_ROWS
    items = [(t, h) for t in range(tiles) for h in range(N_HEADS_C)]
    info = []
    for t in range(tiles):
        i = pl.program_id(1) * tiles + t
        base = jnp.clip(i * NA_TILE_ROWS - NA_ROWS // 2, 0, rows - NA_KEY_ROWS)
        info.append((pl.multiple_of(base * GRID_W, GRID_W), _na_variant(i, n_tiles)))
    pairs = _HeadPairs(q_ref, k_ref, v_ref, tq, tkw)
    held = {}

    def issue(idx):
        t, h = items[idx]
        return pairs.scores(t * tq, info[t][0], h)

    def consume(idx, s):
        t, h = items[idx]
        s = s + tab_ref[info[t][1], h]
        return jnp.exp(s - jnp.max(s, axis=-1, keepdims=True)).astype(BF16)

    def finish(idx, p):
        t, h = items[idx]
        out, den = pairs.values(info[t][0], h, p)
        held[h % 2] = out / den
        if h % 2 == 1:
            o_ref[t * tq:(t + 1) * tq, pairs.cols(h)] = pairs.merge(held[0], held[1])

    _run_pipeline(len(items), NA_LOOKAHEAD, issue, consume, finish)


def _na_tables(rpb, rows):
    r0 = np.array([0, 2, 4, rows - 4, rows - 2])
    base = np.clip(r0 - NA_ROWS // 2, 0, rows - NA_KEY_ROWS)
    r = r0[:, None] + np.arange(NA_TILE_ROWS)[None, :]
    row_start = np.clip(r - NA_ROWS // 2, 0, rows - NA_ROWS)
    krow = base[:, None] + np.arange(NA_KEY_ROWS)[None, :]
    drow = krow[:, None, :] - r[:, :, None]
    row_ok = (krow[:, None, :] >= row_start[:, :, None]) & (krow[:, None, :] < row_start[:, :, None] + NA_ROWS)
    c = np.arange(GRID_W)
    win_start = np.clip(c - NA_COLS // 2, 0, GRID_W - NA_COLS)
    col_ok = (c[None, :] >= win_start[:, None]) & (c[None, :] < win_start[:, None] + NA_COLS)
    dcol = np.clip(c[None, :] - c[:, None], -(NA_COLS - 1), NA_COLS - 1)
    ok = row_ok[:, :, None, :, None] & col_ok[None, None, :, None, :]
    di = np.clip(drow, -(NA_ROWS - 1), NA_ROWS - 1) + (NA_ROWS - 1)
    pick_col = (dcol[:, :, None] + NA_COLS - 1 == np.arange(2 * NA_COLS - 1)).astype(np.float32)
    pick_row = (di[..., None] == np.arange(2 * NA_ROWS - 1)).astype(np.float32)
    hi = lax.Precision.HIGHEST
    toeplitz = jnp.einsum("hab,cjb->hacj", rpb.astype(F32), pick_col, precision=hi)
    bias = jnp.einsum("vqka,hacj->vhqckj", pick_row, toeplitz, precision=hi)
    tab = jnp.where(jnp.asarray(ok)[:, None], bias, NEG_INF)
    return tab.reshape(5, N_HEADS_C, NA_TILE_ROWS * GRID_W, NA_KEY_ROWS * GRID_W)


def _na_call(pc, tab, batch, seq):
    rows = seq // GRID_W
    tq = NA_TILE_ROWS * GRID_W * NA_TILES_PER_STEP
    w = WIDTH_C
    qspec = pl.BlockSpec((None, tq, w), lambda b, i: (b, i, 0))
    kspec = pl.BlockSpec((None, seq, w), lambda b, i: (b, 0, 1))
    vspec = pl.BlockSpec((None, seq, w), lambda b, i: (b, 0, 2))
    tspec = pl.BlockSpec(tab.shape, lambda b, i: (0, 0, 0, 0))
    ospec = pl.BlockSpec((None, tq, w), lambda b, i: (b, i, 0))
    pc3 = pc.reshape(batch, seq, 3 * w)
    return pl.pallas_call(
        functools.partial(_na_kernel, rows=rows),
        grid=(batch, seq // tq),
        in_specs=[qspec, kspec, vspec, tspec],
        out_specs=ospec,
        out_shape=jax.ShapeDtypeStruct((batch, seq, w), F32),
        compiler_params=_cparams(("parallel", "parallel")),
        name="natten",
    )(pc3, pc3, pc3, tab)


def _mix_out_kernel(x_ref, o1_ref, o2_ref, o3_ref, l1_ref, l2_ref, l3_ref, ob_ref, oc_ref,
                    ga_ref, gb_ref, gc_ref, wo_ref, out_ref, o2_scr, o3_scr, l2_scr, l3_scr):
    tm = x_ref.shape[0]
    n_chunks = WIDTH_A // LANES

    def natural(view_ref, scr, dil):
        for r in range(dil):
            for c in range(n_chunks):
                col = r * WIDTH_A + c * LANES
                scr[c, pl.ds(r, tm // dil, stride=dil), :] = view_ref[:, col:col + LANES]
        return jnp.concatenate([scr[c] for c in range(n_chunks)], axis=-1)

    l1, o1 = l1_ref[...], o1_ref[...]
    l2, o2 = natural(l2_ref, l2_scr, DILATIONS[1]), natural(o2_ref, o2_scr, DILATIONS[1])
    l3, o3 = natural(l3_ref, l3_scr, DILATIONS[2]), natural(o3_ref, o3_scr, DILATIONS[2])
    mx = jnp.maximum(jnp.maximum(l1, l2), l3)
    w1, w2, w3 = jnp.exp(l1 - mx), jnp.exp(l2 - mx), jnp.exp(l3 - mx)
    oa = (w1 * o1 + w2 * o2 + w3 * o3) / (w1 + w2 + w3)
    ya = _rms(oa, ga_ref[...]).astype(BF16)
    yb = _rms(ob_ref[...], gb_ref[...]).astype(BF16)
    yc = _rms(oc_ref[...], gc_ref[...]).astype(BF16)
    y = _dot(ya, wo_ref[:WIDTH_A, :])
    y = y + _dot(yb, wo_ref[WIDTH_A:WIDTH_A + WIDTH_B, :])
    y = y + _dot(yc, wo_ref[WIDTH_A + WIDTH_B:, :])
    out_ref[...] = x_ref[...] + y


def _mix_out_call(x, o_parts, lse_parts, ob, oc, ga, gb, gc, wo):
    n, d = x.shape
    tm = ROW_TILE
    full = lambda a: pl.BlockSpec(a.shape, lambda i: (0,) * a.ndim)
    row = lambda w: pl.BlockSpec((tm, w), lambda i: (i, 0))
    view = lambda dil: pl.BlockSpec((tm // dil, dil * WIDTH_A), lambda i: (i, 0))
    return pl.pallas_call(
        _mix_out_kernel,
        grid=(n // tm,),
        in_specs=[row(d)] + [view(dil) for dil in DILATIONS] * 2
                 + [row(WIDTH_B), row(WIDTH_C), full(ga), full(gb), full(gc), full(wo)],
        out_specs=row(d),
        out_shape=jax.ShapeDtypeStruct((n, d), F32),
        scratch_shapes=[pltpu.VMEM((WIDTH_A // LANES, tm, LANES), F32)] * 4,
        compiler_params=_cparams(("parallel",)),
        name="mix_out",
    )(x, *o_parts, *lse_parts, ob, oc, ga, gb, gc, wo)


def _silu(u):
    return u * (1.0 / (1.0 + jnp.exp(-u)))


def _ffn_kernel(x_ref, g_ref, w1_ref, w3_ref, w2_ref, out_ref):
    x = x_ref[...]
    h = _rms(x, g_ref[...]).astype(BF16)
    a = (_silu(_dot(h, w1_ref[...])) * _dot(h, w3_ref[...])).astype(BF16)
    out_ref[...] = x + _dot(a, w2_ref[...])


def _ffn_call(x, g, w1, w3, w2):
    n, d = x.shape
    tm = ROW_TILE
    resident = lambda a: pl.BlockSpec(a.shape, lambda i: (0, 0), pipeline_mode=pl.Buffered(1))
    return pl.pallas_call(
        _ffn_kernel,
        grid=(n // tm,),
        in_specs=[pl.BlockSpec((tm, d), lambda i: (i, 0)), pl.BlockSpec((1, d), lambda i: (0, 0)),
                  resident(w1), resident(w3), resident(w2)],
        out_specs=pl.BlockSpec((tm, d), lambda i: (i, 0)),
        out_shape=jax.ShapeDtypeStruct((n, d), F32),
        compiler_params=_cparams(("parallel",)),
        name="ffn",
    )(x, g, w1, w3, w2)


def _router_kernel(x_ref, g_ref, wr_ref, h_ref, e_ref, gate_ref):
    h = _rms(x_ref[...], g_ref[...])
    _to_token_tiles(h_ref, h)
    logits = jnp.dot(h, wr_ref[...], precision=lax.Precision.HIGHEST, preferred_element_type=F32)
    lane = lax.broadcasted_iota(jnp.int32, logits.shape, 1).astype(F32)
    logits = jnp.where(lane < N_EXPERTS, logits, -jnp.inf)
    m1 = jnp.max(logits, axis=-1, keepdims=True)
    i1 = jnp.min(jnp.where(logits == m1, lane, float(LANES)), axis=-1, keepdims=True)
    rest = jnp.where(lane == i1, -jnp.inf, logits)
    m2 = jnp.max(rest, axis=-1, keepdims=True)
    i2 = jnp.min(jnp.where(rest == m2, lane, float(LANES)), axis=-1, keepdims=True)
    e = jnp.exp(m2 - m1)
    den = 1.0 + e
    e_ref[...] = jnp.where(lane == 0.0, i1, jnp.where(lane == 1.0, i2, 0.0)).astype(jnp.int32)
    gate_ref[...] = jnp.where(lane == 0.0, 1.0 / den, jnp.where(lane == 1.0, e / den, 0.0))


def _router_call(x, g, wr_pad):
    n, d = x.shape
    tm = ROW_TILE
    row = lambda w: pl.BlockSpec((tm, w), lambda i: (i, 0))
    full = lambda a: pl.BlockSpec(a.shape, lambda i: (0,) * a.ndim)
    return pl.pallas_call(
        _router_kernel,
        grid=(n // tm,),
        in_specs=[row(d), full(g), full(wr_pad)],
        out_specs=[pl.BlockSpec((tm * TOKEN_ROWS, LANES), lambda i: (i, 0)), row(LANES), row(LANES)],
        out_shape=[jax.ShapeDtypeStruct((n * TOKEN_ROWS, LANES), F32), jax.ShapeDtypeStruct((n, LANES), jnp.int32),
                   jax.ShapeDtypeStruct((n, LANES), F32)],
        compiler_params=_cparams(("parallel",)),
        name="router",
    )(x, g, wr_pad)


TOKEN_ROWS = 8


def _to_token_tiles(dst_ref, x):
    tm, d = x.shape
    assert d == TOKEN_ROWS * LANES
    for j in range(TOKEN_ROWS):
        dst_ref[pl.ds(j, tm, stride=TOKEN_ROWS), :] = x[:, j * LANES:(j + 1) * LANES]


def _from_token_tiles(src_ref, tm):
    return jnp.concatenate([src_ref[pl.ds(j, tm, stride=TOKEN_ROWS), :] for j in range(TOKEN_ROWS)], axis=-1)


def _row_copy(src_hbm, row, dst_ref, r, sem):
    src = src_hbm.at[pl.ds(pl.multiple_of(row * TOKEN_ROWS, TOKEN_ROWS), TOKEN_ROWS)]
    return pltpu.make_async_copy(src, dst_ref.at[pl.ds(pl.multiple_of(r * TOKEN_ROWS, TOKEN_ROWS), TOKEN_ROWS)], sem)


def _start_row_gather(idx_ref, base, src_hbm, dst_ref, sem, count):
    def body(r, c):
        _row_copy(src_hbm, idx_ref[base + r], dst_ref, r, sem).start()
        return c
    lax.fori_loop(0, count, body, 0, unroll=GATHER_UNROLL)


def _wait_row_gather(src_hbm, dst_ref, sem, count):
    pltpu.make_async_copy(src_hbm.at[pl.ds(0, count * TOKEN_ROWS)], dst_ref, sem).wait()


def _moe_ffn_kernel(tile_e_ref, tile_ok_ref, row_tok_ref, h_hbm, w1_ref, w3_ref, w2_ref, out_ref,
                    xg_ref, acc_ref, sem):
    i = pl.program_id(0)
    f = pl.program_id(1)
    n_tiles = pl.num_programs(0)
    tm = MOE_TM
    slot = i % 2

    @pl.when(f == 0)
    def _():
        @pl.when(i == 0)
        def _():
            acc_ref[...] = jnp.zeros_like(acc_ref)
            _start_row_gather(row_tok_ref, 0, h_hbm, xg_ref.at[0], sem.at[0], tm)

        _wait_row_gather(h_hbm, xg_ref.at[slot], sem.at[slot], tm)

        @pl.when(i + 1 < n_tiles)
        def _():
            _start_row_gather(row_tok_ref, (i + 1) * tm, h_hbm, xg_ref.at[1 - slot], sem.at[1 - slot], tm)

    @pl.when(tile_ok_ref[i] != 0)
    def _():
        h = _from_token_tiles(xg_ref.at[slot], tm).astype(BF16)
        a = (_silu(_dot(h, w1_ref[...])) * _dot(h, w3_ref[...])).astype(BF16)
        total = _dot(a, w2_ref[...]) + jnp.where(f > 0, acc_ref[...], 0.0)
        acc_ref[...] = total
        _to_token_tiles(out_ref, total)

    @pl.when(tile_ok_ref[i] == 0)
    def _():
        out_ref[...] = jnp.zeros_like(out_ref)


def _moe_ffn_call(tile_e, tile_ok, row_tok, h, w1, w3, w2, layer):
    d = w1.shape[2]
    n_tiles = tile_e.shape[0]
    ff = w1.shape[3]
    tm, tf = MOE_TM, MOE_TF
    grid_spec = pltpu.PrefetchScalarGridSpec(
        num_scalar_prefetch=3,
        grid=(n_tiles, ff // tf),
        in_specs=[pl.BlockSpec(memory_space=pl.ANY),
                  pl.BlockSpec((None, None, d, tf), lambda i, f, te, tv, rt: (layer, te[i], 0, f)),
                  pl.BlockSpec((None, None, d, tf), lambda i, f, te, tv, rt: (layer, te[i], 0, f)),
                  pl.BlockSpec((None, None, tf, d), lambda i, f, te, tv, rt: (layer, te[i], f, 0))],
        out_specs=pl.BlockSpec((tm * TOKEN_ROWS, LANES), lambda i, f, te, tv, rt: (i, 0)),
        scratch_shapes=[pltpu.VMEM((2, tm * TOKEN_ROWS, LANES), F32), pltpu.VMEM((tm, d), F32),
                        pltpu.SemaphoreType.DMA((2,))],
    )
    return pl.pallas_call(
        _moe_ffn_kernel,
        grid_spec=grid_spec,
        out_shape=jax.ShapeDtypeStruct((n_tiles * tm * TOKEN_ROWS, LANES), F32),
        compiler_params=_cparams(("arbitrary", "arbitrary")),
        name="moe_ffn",
    )(tile_e, tile_ok, row_tok, h, w1, w3, w2)


def _combine_kernel(d0_ref, d1_ref, y_hbm, x_ref, gate_ref, gain_ref, out_ref, b0_ref, b1_ref, sem, *, final_norm):
    i = pl.program_id(0)
    n_tiles = pl.num_programs(0)
    tm = COMBINE_TM
    slot = i % 2

    def start(tile, s):
        _start_row_gather(d0_ref, tile * tm, y_hbm, b0_ref.at[s], sem.at[0, s], tm)
        _start_row_gather(d1_ref, tile * tm, y_hbm, b1_ref.at[s], sem.at[1, s], tm)

    @pl.when(i == 0)
    def _():
        start(0, 0)

    _wait_row_gather(y_hbm, b0_ref.at[slot], sem.at[0, slot], tm)
    _wait_row_gather(y_hbm, b1_ref.at[slot], sem.at[1, slot], tm)

    @pl.when(i + 1 < n_tiles)
    def _():
        start(i + 1, 1 - slot)

    gates = gate_ref[...]
    y0 = _from_token_tiles(b0_ref.at[slot], tm)
    y1 = _from_token_tiles(b1_ref.at[slot], tm)
    out = x_ref[...] + (gates[:, 0:1] * y0 + gates[:, 1:2] * y1)
    out_ref[...] = _rms(out, gain_ref[...]) if final_norm else out


def _combine_call(d0, d1, y, x, gates, gain, final_norm):
    n, d = x.shape
    tm = COMBINE_TM
    grid_spec = pltpu.PrefetchScalarGridSpec(
        num_scalar_prefetch=2,
        grid=(n // tm,),
        in_specs=[pl.BlockSpec(memory_space=pl.ANY),
                  pl.BlockSpec((tm, d), lambda i, a, b: (i, 0)),
                  pl.BlockSpec((tm, LANES), lambda i, a, b: (i, 0)),
                  pl.BlockSpec((1, d), lambda i, a, b: (0, 0))],
        out_specs=pl.BlockSpec((tm, d), lambda i, a, b: (i, 0)),
        scratch_shapes=[pltpu.VMEM((2, tm * TOKEN_ROWS, LANES), F32)] * 2 + [pltpu.SemaphoreType.DMA((2, 2))],
    )
    return pl.pallas_call(
        functools.partial(_combine_kernel, final_norm=final_norm),
        grid_spec=grid_spec,
        out_shape=jax.ShapeDtypeStruct((n, d), F32),
        compiler_params=_cparams(("arbitrary",)),
        name="moe_combine",
    )(d0, d1, y, x, gates, gain)


def _moe_plan(top_e, n_tiles):
    n_assign = top_e.shape[0] * TOP_K
    flat_e = top_e.reshape(n_assign)
    onehot = (flat_e[:, None] == jnp.arange(N_EXPERTS, dtype=jnp.int32)[None, :]).astype(jnp.int32)
    csum = jnp.cumsum(onehot, axis=0)
    rank = jnp.take_along_axis(csum, flat_e[:, None], axis=1)[:, 0] - 1
    counts = csum[-1]
    padded = (counts + MOE_TM - 1) // MOE_TM * MOE_TM
    pend = jnp.cumsum(padded)
    pstart = pend - padded
    dest = pstart[flat_e] + rank
    tile_start = jnp.arange(n_tiles, dtype=jnp.int32) * MOE_TM
    tile_e = jnp.minimum(jnp.searchsorted(pend, tile_start, side="right"), N_EXPERTS - 1).astype(jnp.int32)
    tile_ok = (tile_start < pend[-1]).astype(jnp.int32)
    order = jnp.argsort(flat_e, stable=True).astype(jnp.int32)
    row_e = jnp.repeat(tile_e, MOE_TM)
    r = jnp.arange(n_tiles * MOE_TM, dtype=jnp.int32) - pstart[row_e]
    src = jnp.clip((jnp.cumsum(counts) - counts)[row_e] + r, 0, n_assign - 1)
    row_tok = jnp.where(r < counts[row_e], order[src] // TOP_K, 0).astype(jnp.int32)
    dest = dest.reshape(-1, TOP_K).astype(jnp.int32)
    return tile_e, tile_ok, row_tok, dest[:, 0], dest[:, 1]


def _moe_layer(x, g, wr_pad, w1, w3, w2, layer, gain, final_norm):
    n = x.shape[0]
    h, top_e, gates = _router_call(x, g, wr_pad)
    n_tiles = -(-(n * TOP_K + N_EXPERTS * (MOE_TM - 1)) // MOE_TM)
    tile_e, tile_ok, row_tok, d0, d1 = _moe_plan(top_e[:, :TOP_K], n_tiles)
    y = _moe_ffn_call(tile_e, tile_ok, row_tok, h, w1, w3, w2, layer)
    return _combine_call(d0, d1, y, x, gates, gain, final_norm)


def _norm_kernel(x_ref, g_ref, o_ref):
    o_ref[...] = _rms(x_ref[...], g_ref[...])


def _norm_call(x, g):
    n, d = x.shape
    tm = ROW_TILE
    return pl.pallas_call(
        _norm_kernel,
        grid=(n // tm,),
        in_specs=[pl.BlockSpec((tm, d), lambda i: (i, 0)), pl.BlockSpec((1, d), lambda i: (0, 0))],
        out_specs=pl.BlockSpec((tm, d), lambda i: (i, 0)),
        out_shape=jax.ShapeDtypeStruct((n, d), F32),
        compiler_params=_cparams(("parallel",)),
        name="final_norm",
    )(x, g)


def _rope_tables(seq):
    inv = ROPE_BASE ** (-jnp.arange(0, QK_ROPE, 2, dtype=F32) / QK_ROPE)
    ang = jnp.arange(seq)[:, None].astype(F32) * inv[None, :]
    cos, sin = jnp.cos(ang), jnp.sin(ang)
    pad = LANES - QK_NOPE - QK_ROPE
    zeros = lambda w: jnp.zeros((seq, w), F32)
    ctab = jnp.concatenate([jnp.ones((seq, QK_NOPE), F32), cos, cos, zeros(pad)], axis=1)
    stab = jnp.concatenate([zeros(QK_NOPE), sin, sin, zeros(pad)], axis=1)
    sa = jnp.concatenate([zeros(QK_NOPE), -sin, zeros(QK_ROPE // 2 + pad)], axis=1)
    sb = jnp.concatenate([zeros(QK_NOPE + QK_ROPE // 2), sin, zeros(pad)], axis=1)
    return ctab, stab, sa, sb


def _split_w_in(w_in):
    d = w_in.shape[0]
    bounds = np.cumsum([WIDTH_A, WIDTH_A, WIDTH_A, Q_LORA, KV_LORA, QK_ROPE, WIDTH_C, WIDTH_C])
    qa, ka, va, cq, ckv, kr, qc, kc, vc = jnp.split(w_in, bounds.tolist(), axis=1)
    half = QK_ROPE // 2
    z_lo = jnp.zeros((d, QK_NOPE), w_in.dtype)
    z_hi = jnp.zeros((d, LANES - QK_NOPE - QK_ROPE), w_in.dtype)
    rope_blk = jnp.concatenate([z_lo, kr, z_hi], axis=1)
    swap_blk = jnp.concatenate([z_lo, -kr[:, half:], kr[:, :half], z_hi], axis=1)
    score_scale = HEAD_DIM ** -0.5
    wa = jnp.concatenate([qa * score_scale, ka, va], axis=1).astype(BF16)
    wc = jnp.concatenate([qc * score_scale, kc, vc], axis=1).astype(BF16)
    wb = jnp.concatenate([cq, ckv, rope_blk, swap_blk], axis=1).astype(BF16)
    return jnp.concatenate([wa, wc, wb], axis=1)


def _split_w_uq(w_uq):
    r = w_uq.shape[0]
    w = w_uq.reshape(r, N_HEADS_B, QK_NOPE + QK_ROPE)
    nope, rope = w[..., :QK_NOPE], w[..., QK_NOPE:]
    z_hi = jnp.zeros((r, N_HEADS_B, LANES - QK_NOPE - QK_ROPE), w_uq.dtype)
    return jnp.concatenate([nope, rope, z_hi], axis=-1).reshape(r, -1).astype(BF16)


def _split_w_ukv(w_ukv):
    r = w_ukv.shape[0]
    w = w_ukv.reshape(r, N_HEADS_B, QK_NOPE + V_HEAD)
    k_nope, v = w[..., :QK_NOPE], w[..., QK_NOPE:]
    wk = jnp.concatenate([k_nope, jnp.zeros((r, N_HEADS_B, LANES - QK_NOPE), w_ukv.dtype)], axis=-1)
    return wk.reshape(r, -1).astype(BF16), v.reshape(r, -1).T.astype(BF16)


def kernel(x, g_mix, w_in, g_q, g_kv, w_uq, w_ukv, rpb, g_out_a, g_out_b, g_out_c, w_o, g_ffn, w1, w3, w2,
           w_router, e_w1, e_w3, e_w2, g_final):
    batch, seq, d = x.shape
    n = batch * seq
    depth = g_mix.shape[0]
    rows = seq // GRID_W
    rope_tabs = _rope_tables(seq)
    ew1, ew3, ew2 = e_w1.astype(BF16), e_w3.astype(BF16), e_w2.astype(BF16)
    xf = x.reshape(n, d)
    for layer in range(depth):
        win = _split_w_in(w_in[layer])
        wq = _split_w_uq(w_uq[layer])
        wk, wvt = _split_w_ukv(w_ukv[layer])
        pa, pa4, pa16, pc, qm, km, vt = _proj_call(xf, g_mix[layer][None], win, g_q[layer][None], g_kv[layer][None],
                                                   wq, wk, wvt, rope_tabs, seq)
        o_parts, lse_parts = [], []
        for dil, view in zip(DILATIONS, (pa, pa4, pa16)):
            o, lse = _band_call(view.reshape(batch, seq // dil, dil * 3 * WIDTH_A), dil)
            o_parts.append(o.reshape(n // dil, dil * WIDTH_A))
            lse_parts.append(lse.reshape(n // dil, dil * WIDTH_A))
        ob = _mla_call(qm, km, vt, batch, seq).reshape(n, WIDTH_B)
        oc = _na_call(pc, _na_tables(rpb[layer], rows), batch, seq).reshape(n, WIDTH_C)
        xf = _mix_out_call(xf, o_parts, lse_parts, ob, oc, g_out_a[layer][None], g_out_b[layer][None],
                           g_out_c[layer][None], w_o[layer].astype(BF16))
        j = layer // 2
        if layer % 2 == 0:
            xf = _ffn_call(xf, g_ffn[layer][None], w1[j].astype(BF16), w3[j].astype(BF16), w2[j].astype(BF16))
        else:
            wr_pad = jnp.pad(w_router[j], ((0, 0), (0, LANES - N_EXPERTS)))
            xf = _moe_layer(xf, g_ffn[layer][None], wr_pad, ew1, ew3, ew2, j, g_final[None], layer == depth - 1)
    if depth % 2 == 1:
        xf = _norm_call(xf, g_final[None])
    return xf.reshape(batch, seq, d)
```

```python
import functools
import math

import numpy as np
import jax
import jax.numpy as jnp
from jax import lax
from jax.experimental import pallas as pl
from jax.experimental.pallas import tpu as pltpu

F32 = jnp.float32
BF16 = jnp.bfloat16

LANES = 128
V7X_VMEM_LIMIT_BYTES = 52 * 1024 * 1024

HEAD_DIM = 64
N_HEADS_A = 6
DILATIONS = (1, 4, 16)
BAND_HALF = 64
N_HEADS_B = 6
Q_LORA = 384
KV_LORA = 256
QK_NOPE = 64
QK_ROPE = 32
V_HEAD = 64
ROPE_BASE = 10000.0
N_HEADS_C = 4
GRID_W = 64
NA_ROWS = 8
NA_COLS = 16
WIDTH_A = N_HEADS_A * HEAD_DIM
WIDTH_B = N_HEADS_B * V_HEAD
WIDTH_C = N_HEADS_C * HEAD_DIM
N_EXPERTS = 8
TOP_K = 2
RMS_EPS = 1e-6
NEG_INF = -1e30

ROW_TILE = 512
MLA_TQ = 256
MLA_TK = 256
MLA_UNROLL = 32
MLA_LOOKAHEAD = 4
MLA_HEADS_PER_STEP = 6
MLA_DEN_ROWS = 16
MLA_Q_PRESCALE = (QK_NOPE + QK_ROPE) ** -0.5 * math.log2(math.e)
BAND_TQ = 128
BAND_TILES_PER_STEP = 4
BAND_LOOKAHEAD = 6
NA_TILE_ROWS = 2
NA_KEY_ROWS = 10
NA_TILES_PER_STEP = 4
NA_LOOKAHEAD = 16
MOE_TM = 512
MOE_TF = 1792
COMBINE_TM = 256
GATHER_UNROLL = 32


def _cparams(semantics):
    return pltpu.CompilerParams(dimension_semantics=semantics, vmem_limit_bytes=V7X_VMEM_LIMIT_BYTES)


def _rms(x, g):
    return x * lax.rsqrt(jnp.mean(x * x, axis=-1, keepdims=True) + RMS_EPS) * g


def _dot(a, b):
    return jnp.dot(a, b, preferred_element_type=F32)


def _dot_nt(a, b):
    return lax.dot_general(a, b, (((1,), (1,)), ((), ())), preferred_element_type=F32)


def _proj_kernel(x_ref, g_ref, win_ref, gq_ref, gkv_ref, wq_ref, wk_ref, wvt_ref,
                 ct_ref, st_ref, sa_ref, sb_ref, pa_ref, pa4_ref, pa16_ref, pc_ref, qm_ref, km_ref, vt_ref, pa_scr):
    tm = x_ref.shape[0]
    h = _rms(x_ref[...], g_ref[...]).astype(BF16)
    proj = _dot(h, win_ref[...])
    pa = proj[:, :3 * WIDTH_A]
    pb = proj[:, 3 * (WIDTH_A + WIDTH_C):]
    pa_ref[...] = pa.astype(BF16)
    n_chunks = pa.shape[1] // LANES
    for c in range(n_chunks):
        pa_scr[c] = pa[:, c * LANES:(c + 1) * LANES]
    for dil, view_ref in ((DILATIONS[1], pa4_ref), (DILATIONS[2], pa16_ref)):
        for r in range(dil):
            for c in range(n_chunks):
                col = r * 3 * WIDTH_A + c * LANES
                view_ref[:, col:col + LANES] = pa_scr[c, pl.ds(r, tm // dil, stride=dil), :].astype(BF16)
    pc_ref[...] = proj[:, 3 * WIDTH_A:3 * (WIDTH_A + WIDTH_C)].astype(BF16)
    hq = _rms(pb[:, :Q_LORA], gq_ref[...]).astype(BF16)
    hkv = _rms(pb[:, Q_LORA:Q_LORA + KV_LORA], gkv_ref[...]).astype(BF16)
    r1 = pb[:, Q_LORA + KV_LORA:Q_LORA + KV_LORA + LANES]
    r2 = pb[:, Q_LORA + KV_LORA + LANES:]
    ct = ct_ref[...]
    st = st_ref[...]
    sa = sa_ref[...]
    sb = sb_ref[...]
    half = QK_ROPE // 2
    qa = _dot(hq, wq_ref[...])
    kn = _dot(hkv, wk_ref[...])
    kr = r1 * ct + r2 * st
    for hd in range(N_HEADS_B):
        sl = slice(hd * LANES, (hd + 1) * LANES)
        q = qa[:, sl]
        q = q * ct + pltpu.roll(q, LANES - half, 1) * sa + pltpu.roll(q, half, 1) * sb
        qm_ref[:, sl] = (q * MLA_Q_PRESCALE).astype(BF16)
        km_ref[:, sl] = (kn[:, sl] + kr).astype(BF16)
    vt_ref[...] = _dot_nt(wvt_ref[...], hkv).astype(BF16)


def _proj_call(x, g, win, gq, gkv, wq, wk, wvt, tabs, seq):
    n, d = x.shape
    tm = ROW_TILE
    tiles_per_seq = seq // tm
    full = lambda a: pl.BlockSpec(a.shape, lambda i: (0,) * a.ndim)
    row = lambda w: pl.BlockSpec((tm, w), lambda i: (i, 0))
    tab = pl.BlockSpec((tm, LANES), lambda i: (i % tiles_per_seq, 0))
    view = lambda dil: pl.BlockSpec((tm // dil, dil * 3 * WIDTH_A), lambda i: (i, 0))
    hb = N_HEADS_B * LANES
    return pl.pallas_call(
        _proj_kernel,
        grid=(n // tm,),
        in_specs=[row(d), full(g), full(win), full(gq), full(gkv), full(wq), full(wk), full(wvt)] + [tab] * 4,
        out_specs=[row(3 * WIDTH_A)] + [view(dil) for dil in DILATIONS[1:]] + [row(3 * WIDTH_C), row(hb), row(hb),
                   pl.BlockSpec((WIDTH_B, tm), lambda i: (0, i))],
        out_shape=[jax.ShapeDtypeStruct((n, 3 * WIDTH_A), BF16)]
                  + [jax.ShapeDtypeStruct((n // dil, dil * 3 * WIDTH_A), BF16) for dil in DILATIONS[1:]]
                  + [jax.ShapeDtypeStruct((n, 3 * WIDTH_C), BF16),
                   jax.ShapeDtypeStruct((n, hb), BF16), jax.ShapeDtypeStruct((n, hb), BF16),
                   jax.ShapeDtypeStruct((WIDTH_B, n), BF16)],
        scratch_shapes=[pltpu.VMEM((3 * WIDTH_A // LANES, tm, LANES), F32)],
        compiler_params=_cparams(("parallel",)),
        name="proj",
    )(x, g, win, gq, gkv, wq, wk, wvt, *tabs)


def _run_pipeline(n_items, look, issue, consume, finish):
    if look >= n_items:
        results = [consume(i, s) for i, s in enumerate([issue(i) for i in range(n_items)])]
        for i, res in enumerate(results):
            finish(i, res)
        return
    inflight = {i: issue(i) for i in range(min(look, n_items))}
    pending = None
    for i in range(n_items):
        if i + look < n_items:
            inflight[i + look] = issue(i + look)
        if pending is not None:
            finish(*pending)
        pending = (i, consume(i, inflight.pop(i)))
    finish(*pending)


class _HeadPairs:
    def __init__(self, q_ref, k_ref, v_ref, tq, tkw):
        self.q_ref, self.k_ref, self.v_ref, self.tq, self.tkw = q_ref, k_ref, v_ref, tq, tkw
        lane = lax.broadcasted_iota(jnp.int32, (tq, LANES), 1)
        self.low = lane < HEAD_DIM
        self.ones = jnp.ones((tkw, LANES), BF16)

    @staticmethod
    def cols(h):
        return slice((h // 2) * LANES, (h // 2 + 1) * LANES)

    def scores(self, q_start, k_start, h):
        q = self.q_ref[q_start:q_start + self.tq, self.cols(h)]
        q = jnp.where(self.low if h % 2 == 0 else ~self.low, q, jnp.zeros_like(q))
        return _dot_nt(q, self.k_ref[pl.ds(k_start, self.tkw), self.cols(h)])

    def values(self, k_start, h, p):
        v = jnp.concatenate([self.v_ref[pl.ds(k_start, self.tkw), self.cols(h)], self.ones], axis=-1)
        o = _dot(p, v)
        return o[:, :LANES], o[:, LANES:]

    def merge(self, even, odd):
        return jnp.where(self.low, even, odd)


def _band_tables(dilation):
    tq, tkw = BAND_TQ, BAND_TQ + 2 * BAND_HALF
    shift = np.array([0, -BAND_HALF, -2 * BAND_HALF])
    rel = shift[:, None, None] + np.arange(tkw)[None, None, :] - np.arange(tq)[None, :, None]
    dist = np.abs(rel)
    slopes = 2.0 ** (-8.0 * np.arange(1, N_HEADS_A + 1) / N_HEADS_A)
    bias = -(slopes[None, :, None, None] * dilation) * dist[:, None].astype(np.float64)
    tab = np.where(dist[:, None] <= BAND_HALF, bias, NEG_INF)
    return jnp.asarray(tab, F32)


def _band_kernel(q_ref, k_ref, v_ref, tab_ref, o_ref, lse_ref, *, n):
    tq = BAND_TQ
    tkw = tq + 2 * BAND_HALF
    tiles = q_ref.shape[0] // tq
    last_tile = n // tq - 1
    items = [(t, h) for t in range(tiles) for h in range(N_HEADS_A)]
    info = []
    for t in range(tiles):
        i = pl.program_id(2) * tiles + t
        start = pl.multiple_of(jnp.clip(i * tq - BAND_HALF, 0, n - tkw), BAND_HALF)
        variant = jnp.minimum(i, 1) + (i == last_tile).astype(jnp.int32)
        info.append((start, variant))
    pairs = _HeadPairs(q_ref, k_ref, v_ref, tq, tkw)
    held = {}

    def issue(idx):
        t, h = items[idx]
        return pairs.scores(t * tq, info[t][0], h)

    def consume(idx, s):
        t, h = items[idx]
        s = s + tab_ref[info[t][1], h]
        m = jnp.max(s, axis=-1, keepdims=True)
        return jnp.exp(s - m).astype(BF16), m

    def finish(idx, res):
        t, h = items[idx]
        p, m = res
        out, den = pairs.values(info[t][0], h, p)
        held[h % 2] = (out / den, m + jnp.log(den))
        if h % 2 == 1:
            cols = pairs.cols(h)
            o_ref[t * tq:(t + 1) * tq, cols] = pairs.merge(held[0][0], held[1][0])
            lse_ref[t * tq:(t + 1) * tq, cols] = pairs.merge(held[0][1], held[1][1])

    _run_pipeline(len(items), BAND_LOOKAHEAD, issue, consume, finish)


def _band_call(pa_view, dilation):
    b, n, _ = pa_view.shape
    rows = BAND_TQ * BAND_TILES_PER_STEP
    w = WIDTH_A
    tab = _band_tables(dilation)
    qspec = pl.BlockSpec((None, rows, w), lambda bb, r, i: (bb, i, 3 * r))
    kspec = pl.BlockSpec((None, n, w), lambda bb, r, i: (bb, 0, 3 * r + 1))
    vspec = pl.BlockSpec((None, n, w), lambda bb, r, i: (bb, 0, 3 * r + 2))
    tspec = pl.BlockSpec(tab.shape, lambda bb, r, i: (0, 0, 0, 0))
    ospec = pl.BlockSpec((None, rows, w), lambda bb, r, i: (bb, i, r))
    shape = jax.ShapeDtypeStruct((b, n, dilation * w), F32)
    return pl.pallas_call(
        functools.partial(_band_kernel, n=n),
        grid=(b, dilation, n // rows),
        in_specs=[qspec, kspec, vspec, tspec],
        out_specs=[ospec, ospec],
        out_shape=[shape, shape],
        compiler_params=_cparams(("parallel", "parallel", "parallel")),
        name=f"band_d{dilation}",
    )(pa_view, pa_view, pa_view, tab)


def _mla_kernel(q_ref, k_ref, vt_ref, o_ref):
    tq = q_ref.shape[0]
    seq = k_ref.shape[0]
    tk = MLA_TK
    nh = MLA_HEADS_PER_STEP
    n_chunks = seq // tk
    items = [(u, h) for u in range(MLA_UNROLL) for h in range(nh)]
    look = MLA_LOOKAHEAD
    ones_rows = jnp.ones((MLA_DEN_ROWS, tk), BF16)

    def key_slice(chunk):
        return pl.ds(pl.multiple_of(chunk * tk, tk), tk)

    def score_matmul(chunk, h):
        k = k_ref[key_slice(chunk), h * LANES:(h + 1) * LANES]
        return _dot_nt(k, q_ref[:, h * LANES:(h + 1) * LANES])

    def value_matmul(chunk, h, p):
        vt = jnp.concatenate([vt_ref[h * V_HEAD:(h + 1) * V_HEAD, key_slice(chunk)], ones_rows], axis=0)
        return _dot(vt, p)

    def body(j, carry):
        state = list(carry[:2 * nh])
        scores = dict(zip(items[:look], carry[2 * nh:2 * nh + look]))
        pend_p, pend_alpha = carry[2 * nh + look:]
        pending = (jnp.maximum(j * MLA_UNROLL - 1, 0), nh - 1, pend_p, pend_alpha)
        ahead = []
        for idx, (u, h) in enumerate(items):
            la = idx + look
            if la < len(items):
                lu, lh = items[la]
                scores[lu, lh] = score_matmul(j * MLA_UNROLL + lu, lh)
            else:
                lu, lh = items[la - len(items)]
                ahead.append(score_matmul(jnp.minimum((j + 1) * MLA_UNROLL + lu, n_chunks - 1), lh))
            pc, ph, pp, pa = pending
            state[2 * ph + 1] = pa * state[2 * ph + 1] + value_matmul(pc, ph, pp)
            s = scores.pop((u, h))
            m_new = jnp.maximum(state[2 * h], jnp.max(s, axis=0, keepdims=True))
            alpha = jnp.exp2(state[2 * h] - m_new)
            state[2 * h] = m_new
            pending = (j * MLA_UNROLL + u, h, jnp.exp2(s - m_new).astype(BF16), alpha)
        return tuple(state) + tuple(ahead) + (pending[2], pending[3])

    init = (jnp.full((1, tq), NEG_INF, F32), jnp.zeros((V_HEAD + MLA_DEN_ROWS, tq), F32)) * nh
    init += tuple(score_matmul(u, h) for u, h in items[:look])
    init += (jnp.zeros((tk, tq), BF16), jnp.ones((1, tq), F32))
    res = lax.fori_loop(0, n_chunks // MLA_UNROLL, body, init)
    accs = [res[2 * h + 1] for h in range(nh)]
    accs[nh - 1] = res[-1] * accs[nh - 1] + value_matmul(n_chunks - 1, nh - 1, res[-2])
    out_t = jnp.concatenate([a[:V_HEAD] / a[V_HEAD:V_HEAD + 1] for a in accs], axis=0)
    o_ref[...] = out_t.T


def _mla_call(qm, km, vt, batch, seq):
    tq = MLA_TQ
    nh = MLA_HEADS_PER_STEP
    qspec = pl.BlockSpec((None, tq, nh * LANES), lambda b, g, i: (b, i, g))
    kspec = pl.BlockSpec((None, seq, nh * LANES), lambda b, g, i: (b, 0, g))
    vspec = pl.BlockSpec((nh * V_HEAD, seq), lambda b, g, i: (g, b))
    ospec = pl.BlockSpec((None, tq, nh * V_HEAD), lambda b, g, i: (b, i, g))
    return pl.pallas_call(
        _mla_kernel,
        grid=(batch, N_HEADS_B // nh, seq // tq),
        in_specs=[qspec, kspec, vspec],
        out_specs=ospec,
        out_shape=jax.ShapeDtypeStruct((batch, seq, WIDTH_B), F32),
        compiler_params=_cparams(("parallel", "parallel", "parallel")),
        name="mla",
    )(qm.reshape(batch, seq, -1), km.reshape(batch, seq, -1), vt)


def _na_variant(i, n_tiles):
    return jnp.minimum(i, 2) + jnp.maximum(i - (n_tiles - 3), 0)


def _na_kernel(q_ref, k_ref, v_ref, tab_ref, o_ref, *, rows):
    tq = NA_TILE_ROWS * GRID_W
    tkw = NA_KEY_ROWS * GRID_W
    tiles = q_ref.shape[0] // tq
    n_tiles = rows // NA_TILE_ROWS
    items = [(t, h) for t in range(tiles) for h in range(N_HEADS_C)]
    info = []
    for t in range(tiles):
        i = pl.program_id(1) * tiles + t
        base = jnp.clip(i * NA_TILE_ROWS - NA_ROWS // 2, 0, rows - NA_KEY_ROWS)
        info.append((pl.multiple_of(base * GRID_W, GRID_W), _na_variant(i, n_tiles)))
    pairs = _HeadPairs(q_ref, k_ref, v_ref, tq, tkw)
    held = {}

    def issue(idx):
        t, h = items[idx]
        return pairs.scores(t * tq, info[t][0], h)

    def consume(idx, s):
        t, h = items[idx]
        s = s + tab_ref[info[t][1], h]
        return jnp.exp(s - jnp.max(s, axis=-1, keepdims=True)).astype(BF16)

    def finish(idx, p):
        t, h = items[idx]
        out, den = pairs.values(info[t][0], h, p)
        held[h % 2] = out / den
        if h % 2 == 1:
            o_ref[t * tq:(t + 1) * tq, pairs.cols(h)] = pairs.merge(held[0], held[1])

    _run_pipeline(len(items), NA_LOOKAHEAD, issue, consume, finish)


def _na_tables(rpb, rows):
    r0 = np.array([0, 2, 4, rows - 4, rows - 2])
    base = np.clip(r0 - NA_ROWS // 2, 0, rows - NA_KEY_ROWS)
    r = r0[:, None] + np.arange(NA_TILE_ROWS)[None, :]
    row_start = np.clip(r - NA_ROWS // 2, 0, rows - NA_ROWS)
    krow = base[:, None] + np.arange(NA_KEY_ROWS)[None, :]
    drow = krow[:, None, :] - r[:, :, None]
    row_ok = (krow[:, None, :] >= row_start[:, :, None]) & (krow[:, None, :] < row_start[:, :, None] + NA_ROWS)
    c = np.arange(GRID_W)
    win_start = np.clip(c - NA_COLS // 2, 0, GRID_W - NA_COLS)
    col_ok = (c[None, :] >= win_start[:, None]) & (c[None, :] < win_start[:, None] + NA_COLS)
    dcol = np.clip(c[None, :] - c[:, None], -(NA_COLS - 1), NA_COLS - 1)
    ok = row_ok[:, :, None, :, None] & col_ok[None, None, :, None, :]
    di = np.clip(drow, -(NA_ROWS - 1), NA_ROWS - 1) + (NA_ROWS - 1)
    pick_col = (dcol[:, :, None] + NA_COLS - 1 == np.arange(2 * NA_COLS - 1)).astype(np.float32)
    pick_row = (di[..., None] == np.arange(2 * NA_ROWS - 1)).astype(np.float32)
    hi = lax.Precision.HIGHEST
    toeplitz = jnp.einsum("hab,cjb->hacj", rpb.astype(F32), pick_col, precision=hi)
    bias = jnp.einsum("vqka,hacj->vhqckj", pick_row, toeplitz, precision=hi)
    tab = jnp.where(jnp.asarray(ok)[:, None], bias, NEG_INF)
    return tab.reshape(5, N_HEADS_C, NA_TILE_ROWS * GRID_W, NA_KEY_ROWS * GRID_W)


def _na_call(pc, tab, batch, seq):
    rows = seq // GRID_W
    tq = NA_TILE_ROWS * GRID_W * NA_TILES_PER_STEP
    w = WIDTH_C
    qspec = pl.BlockSpec((None, tq, w), lambda b, i: (b, i, 0))
    kspec = pl.BlockSpec((None, seq, w), lambda b, i: (b, 0, 1))
    vspec = pl.BlockSpec((None, seq, w), lambda b, i: (b, 0, 2))
    tspec = pl.BlockSpec(tab.shape, lambda b, i: (0, 0, 0, 0))
    ospec = pl.BlockSpec((None, tq, w), lambda b, i: (b, i, 0))
    pc3 = pc.reshape(batch, seq, 3 * w)
    return pl.pallas_call(
        functools.partial(_na_kernel, rows=rows),
        grid=(batch, seq // tq),
        in_specs=[qspec, kspec, vspec, tspec],
        out_specs=ospec,
        out_shape=jax.ShapeDtypeStruct((batch, seq, w), F32),
        compiler_params=_cparams(("parallel", "parallel")),
        name="natten",
    )(pc3, pc3, pc3, tab)


def _mix_out_kernel(x_ref, o1_ref, o2_ref, o3_ref, l1_ref, l2_ref, l3_ref, ob_ref, oc_ref,
                    ga_ref, gb_ref, gc_ref, wo_ref, out_ref, o2_scr, o3_scr, l2_scr, l3_scr):
    tm = x_ref.shape[0]
    n_chunks = WIDTH_A // LANES

    def natural(view_ref, scr, dil):
        for r in range(dil):
            for c in range(n_chunks):
                col = r * WIDTH_A + c * LANES
                scr[c, pl.ds(r, tm // dil, stride=dil), :] = view_ref[:, col:col + LANES]
        return jnp.concatenate([scr[c] for c in range(n_chunks)], axis=-1)

    l1, o1 = l1_ref[...], o1_ref[...]
    l2, o2 = natural(l2_ref, l2_scr, DILATIONS[1]), natural(o2_ref, o2_scr, DILATIONS[1])
    l3, o3 = natural(l3_ref, l3_scr, DILATIONS[2]), natural(o3_ref, o3_scr, DILATIONS[2])
    mx = jnp.maximum(jnp.maximum(l1, l2), l3)
    w1, w2, w3 = jnp.exp(l1 - mx), jnp.exp(l2 - mx), jnp.exp(l3 - mx)
    oa = (w1 * o1 + w2 * o2 + w3 * o3) / (w1 + w2 + w3)
    ya = _rms(oa, ga_ref[...]).astype(BF16)
    yb = _rms(ob_ref[...], gb_ref[...]).astype(BF16)
    yc = _rms(oc_ref[...], gc_ref[...]).astype(BF16)
    y = _dot(ya, wo_ref[:WIDTH_A, :])
    y = y + _dot(yb, wo_ref[WIDTH_A:WIDTH_A + WIDTH_B, :])
    y = y + _dot(yc, wo_ref[WIDTH_A + WIDTH_B:, :])
    out_ref[...] = x_ref[...] + y


def _mix_out_call(x, o_parts, lse_parts, ob, oc, ga, gb, gc, wo):
    n, d = x.shape
    tm = ROW_TILE
    full = lambda a: pl.BlockSpec(a.shape, lambda i: (0,) * a.ndim)
    row = lambda w: pl.BlockSpec((tm, w), lambda i: (i, 0))
    view = lambda dil: pl.BlockSpec((tm // dil, dil * WIDTH_A), lambda i: (i, 0))
    return pl.pallas_call(
        _mix_out_kernel,
        grid=(n // tm,),
        in_specs=[row(d)] + [view(dil) for dil in DILATIONS] * 2
                 + [row(WIDTH_B), row(WIDTH_C), full(ga), full(gb), full(gc), full(wo)],
        out_specs=row(d),
        out_shape=jax.ShapeDtypeStruct((n, d), F32),
        scratch_shapes=[pltpu.VMEM((WIDTH_A // LANES, tm, LANES), F32)] * 4,
        compiler_params=_cparams(("parallel",)),
        name="mix_out",
    )(x, *o_parts, *lse_parts, ob, oc, ga, gb, gc, wo)


def _silu(u):
    return u * (1.0 / (1.0 + jnp.exp(-u)))


def _ffn_kernel(x_ref, g_ref, w1_ref, w3_ref, w2_ref, out_ref):
    x = x_ref[...]
    h = _rms(x, g_ref[...]).astype(BF16)
    a = (_silu(_dot(h, w1_ref[...])) * _dot(h, w3_ref[...])).astype(BF16)
    out_ref[...] = x + _dot(a, w2_ref[...])


def _ffn_call(x, g, w1, w3, w2):
    n, d = x.shape
    tm = ROW_TILE
    resident = lambda a: pl.BlockSpec(a.shape, lambda i: (0, 0), pipeline_mode=pl.Buffered(1))
    return pl.pallas_call(
        _ffn_kernel,
        grid=(n // tm,),
        in_specs=[pl.BlockSpec((tm, d), lambda i: (i, 0)), pl.BlockSpec((1, d), lambda i: (0, 0)),
                  resident(w1), resident(w3), resident(w2)],
        out_specs=pl.BlockSpec((tm, d), lambda i: (i, 0)),
        out_shape=jax.ShapeDtypeStruct((n, d), F32),
        compiler_params=_cparams(("parallel",)),
        name="ffn",
    )(x, g, w1, w3, w2)


def _router_kernel(x_ref, g_ref, wr_ref, h_ref, e_ref, gate_ref):
    h = _rms(x_ref[...], g_ref[...])
    _to_token_tiles(h_ref, h)
    logits = jnp.dot(h, wr_ref[...], precision=lax.Precision.HIGHEST, preferred_element_type=F32)
    lane = lax.broadcasted_iota(jnp.int32, logits.shape, 1).astype(F32)
    logits = jnp.where(lane < N_EXPERTS, logits, -jnp.inf)
    m1 = jnp.max(logits, axis=-1, keepdims=True)
    i1 = jnp.min(jnp.where(logits == m1, lane, float(LANES)), axis=-1, keepdims=True)
    rest = jnp.where(lane == i1, -jnp.inf, logits)
    m2 = jnp.max(rest, axis=-1, keepdims=True)
    i2 = jnp.min(jnp.where(rest == m2, lane, float(LANES)), axis=-1, keepdims=True)
    e = jnp.exp(m2 - m1)
    den = 1.0 + e
    e_ref[...] = jnp.where(lane == 0.0, i1, jnp.where(lane == 1.0, i2, 0.0)).astype(jnp.int32)
    gate_ref[...] = jnp.where(lane == 0.0, 1.0 / den, jnp.where(lane == 1.0, e / den, 0.0))


def _router_call(x, g, wr_pad):
    n, d = x.shape
    tm = ROW_TILE
    row = lambda w: pl.BlockSpec((tm, w), lambda i: (i, 0))
    full = lambda a: pl.BlockSpec(a.shape, lambda i: (0,) * a.ndim)
    return pl.pallas_call(
        _router_kernel,
        grid=(n // tm,),
        in_specs=[row(d), full(g), full(wr_pad)],
        out_specs=[pl.BlockSpec((tm * TOKEN_ROWS, LANES), lambda i: (i, 0)), row(LANES), row(LANES)],
        out_shape=[jax.ShapeDtypeStruct((n * TOKEN_ROWS, LANES), F32), jax.ShapeDtypeStruct((n, LANES), jnp.int32),
                   jax.ShapeDtypeStruct((n, LANES), F32)],
        compiler_params=_cparams(("parallel",)),
        name="router",
    )(x, g, wr_pad)


TOKEN_ROWS = 8


def _to_token_tiles(dst_ref, x):
    tm, d = x.shape
    assert d == TOKEN_ROWS * LANES
    for j in range(TOKEN_ROWS):
        dst_ref[pl.ds(j, tm, stride=TOKEN_ROWS), :] = x[:, j * LANES:(j + 1) * LANES]


def _from_token_tiles(src_ref, tm):
    return jnp.concatenate([src_ref[pl.ds(j, tm, stride=TOKEN_ROWS), :] for j in range(TOKEN_ROWS)], axis=-1)


def _row_copy(src_hbm, row, dst_ref, r, sem):
    src = src_hbm.at[pl.ds(pl.multiple_of(row * TOKEN_ROWS, TOKEN_ROWS), TOKEN_ROWS)]
    return pltpu.make_async_copy(src, dst_ref.at[pl.ds(pl.multiple_of(r * TOKEN_ROWS, TOKEN_ROWS), TOKEN_ROWS)], sem)


def _start_row_gather(idx_ref, base, src_hbm, dst_ref, sem, count):
    def body(r, c):
        _row_copy(src_hbm, idx_ref[base + r], dst_ref, r, sem).start()
        return c
    lax.fori_loop(0, count, body, 0, unroll=GATHER_UNROLL)


def _wait_row_gather(src_hbm, dst_ref, sem, count):
    pltpu.make_async_copy(src_hbm.at[pl.ds(0, count * TOKEN_ROWS)], dst_ref, sem).wait()


def _moe_ffn_kernel(tile_e_ref, tile_ok_ref, row_tok_ref, h_hbm, w1_ref, w3_ref, w2_ref, out_ref,
                    xg_ref, acc_ref, sem):
    i = pl.program_id(0)
    f = pl.program_id(1)
    n_tiles = pl.num_programs(0)
    tm = MOE_TM
    slot = i % 2

    @pl.when(f == 0)
    def _():
        @pl.when(i == 0)
        def _():
            acc_ref[...] = jnp.zeros_like(acc_ref)
            _start_row_gather(row_tok_ref, 0, h_hbm, xg_ref.at[0], sem.at[0], tm)

        _wait_row_gather(h_hbm, xg_ref.at[slot], sem.at[slot], tm)

        @pl.when(i + 1 < n_tiles)
        def _():
            _start_row_gather(row_tok_ref, (i + 1) * tm, h_hbm, xg_ref.at[1 - slot], sem.at[1 - slot], tm)

    @pl.when(tile_ok_ref[i] != 0)
    def _():
        h = _from_token_tiles(xg_ref.at[slot], tm).astype(BF16)
        a = (_silu(_dot(h, w1_ref[...])) * _dot(h, w3_ref[...])).astype(BF16)
        total = _dot(a, w2_ref[...]) + jnp.where(f > 0, acc_ref[...], 0.0)
        acc_ref[...] = total
        _to_token_tiles(out_ref, total)

    @pl.when(tile_ok_ref[i] == 0)
    def _():
        out_ref[...] = jnp.zeros_like(out_ref)


def _moe_ffn_call(tile_e, tile_ok, row_tok, h, w1, w3, w2, layer):
    d = w1.shape[2]
    n_tiles = tile_e.shape[0]
    ff = w1.shape[3]
    tm, tf = MOE_TM, MOE_TF
    grid_spec = pltpu.PrefetchScalarGridSpec(
        num_scalar_prefetch=3,
        grid=(n_tiles, ff // tf),
        in_specs=[pl.BlockSpec(memory_space=pl.ANY),
                  pl.BlockSpec((None, None, d, tf), lambda i, f, te, tv, rt: (layer, te[i], 0, f)),
                  pl.BlockSpec((None, None, d, tf), lambda i, f, te, tv, rt: (layer, te[i], 0, f)),
                  pl.BlockSpec((None, None, tf, d), lambda i, f, te, tv, rt: (layer, te[i], f, 0))],
        out_specs=pl.BlockSpec((tm * TOKEN_ROWS, LANES), lambda i, f, te, tv, rt: (i, 0)),
        scratch_shapes=[pltpu.VMEM((2, tm * TOKEN_ROWS, LANES), F32), pltpu.VMEM((tm, d), F32),
                        pltpu.SemaphoreType.DMA((2,))],
    )
    return pl.pallas_call(
        _moe_ffn_kernel,
        grid_spec=grid_spec,
        out_shape=jax.ShapeDtypeStruct((n_tiles * tm * TOKEN_ROWS, LANES), F32),
        compiler_params=_cparams(("arbitrary", "arbitrary")),
        name="moe_ffn",
    )(tile_e, tile_ok, row_tok, h, w1, w3, w2)


def _combine_kernel(d0_ref, d1_ref, y_hbm, x_ref, gate_ref, gain_ref, out_ref, b0_ref, b1_ref, sem, *, final_norm):
    i = pl.program_id(0)
    n_tiles = pl.num_programs(0)
    tm = COMBINE_TM
    slot = i % 2

    def start(tile, s):
        _start_row_gather(d0_ref, tile * tm, y_hbm, b0_ref.at[s], sem.at[0, s], tm)
        _start_row_gather(d1_ref, tile * tm, y_hbm, b1_ref.at[s], sem.at[1, s], tm)

    @pl.when(i == 0)
    def _():
        start(0, 0)

    _wait_row_gather(y_hbm, b0_ref.at[slot], sem.at[0, slot], tm)
    _wait_row_gather(y_hbm, b1_ref.at[slot], sem.at[1, slot], tm)

    @pl.when(i + 1 < n_tiles)
    def _():
        start(i + 1, 1 - slot)

    gates = gate_ref[...]
    y0 = _from_token_tiles(b0_ref.at[slot], tm)
    y1 = _from_token_tiles(b1_ref.at[slot], tm)
    out = x_ref[...] + (gates[:, 0:1] * y0 + gates[:, 1:2] * y1)
    out_ref[...] = _rms(out, gain_ref[...]) if final_norm else out


def _combine_call(d0, d1, y, x, gates, gain, final_norm):
    n, d = x.shape
    tm = COMBINE_TM
    grid_spec = pltpu.PrefetchScalarGridSpec(
        num_scalar_prefetch=2,
        grid=(n // tm,),
        in_specs=[pl.BlockSpec(memory_space=pl.ANY),
                  pl.BlockSpec((tm, d), lambda i, a, b: (i, 0)),
                  pl.BlockSpec((tm, LANES), lambda i, a, b: (i, 0)),
                  pl.BlockSpec((1, d), lambda i, a, b: (0, 0))],
        out_specs=pl.BlockSpec((tm, d), lambda i, a, b: (i, 0)),
        scratch_shapes=[pltpu.VMEM((2, tm * TOKEN_ROWS, LANES), F32)] * 2 + [pltpu.SemaphoreType.DMA((2, 2))],
    )
    return pl.pallas_call(
        functools.partial(_combine_kernel, final_norm=final_norm),
        grid_spec=grid_spec,
        out_shape=jax.ShapeDtypeStruct((n, d), F32),
        compiler_params=_cparams(("arbitrary",)),
        name="moe_combine",
    )(d0, d1, y, x, gates, gain)


def _moe_plan(top_e, n_tiles):
    n_assign = top_e.shape[0] * TOP_K
    flat_e = top_e.reshape(n_assign)
    onehot = (flat_e[:, None] == jnp.arange(N_EXPERTS, dtype=jnp.int32)[None, :]).astype(jnp.int32)
    csum = jnp.cumsum(onehot, axis=0)
    rank = jnp.take_along_axis(csum, flat_e[:, None], axis=1)[:, 0] - 1
    counts = csum[-1]
    padded = (counts + MOE_TM - 1) // MOE_TM * MOE_TM
    pend = jnp.cumsum(padded)
    pstart = pend - padded
    dest = pstart[flat_e] + rank
    tile_start = jnp.arange(n_tiles, dtype=jnp.int32) * MOE_TM
    tile_e = jnp.minimum(jnp.searchsorted(pend, tile_start, side="right"), N_EXPERTS - 1).astype(jnp.int32)
    tile_ok = (tile_start < pend[-1]).astype(jnp.int32)
    order = jnp.argsort(flat_e, stable=True).astype(jnp.int32)
    row_e = jnp.repeat(tile_e, MOE_TM)
    r = jnp.arange(n_tiles * MOE_TM, dtype=jnp.int32) - pstart[row_e]
    src = jnp.clip((jnp.cumsum(counts) - counts)[row_e] + r, 0, n_assign - 1)
    row_tok = jnp.where(r < counts[row_e], order[src] // TOP_K, 0).astype(jnp.int32)
    dest = dest.reshape(-1, TOP_K).astype(jnp.int32)
    return tile_e, tile_ok, row_tok, dest[:, 0], dest[:, 1]


def _moe_layer(x, g, wr_pad, w1, w3, w2, layer, gain, final_norm):
    n = x.shape[0]
    h, top_e, gates = _router_call(x, g, wr_pad)
    n_tiles = -(-(n * TOP_K + N_EXPERTS * (MOE_TM - 1)) // MOE_TM)
    tile_e, tile_ok, row_tok, d0, d1 = _moe_plan(top_e[:, :TOP_K], n_tiles)
    y = _moe_ffn_call(tile_e, tile_ok, row_tok, h, w1, w3, w2, layer)
    return _combine_call(d0, d1, y, x, gates, gain, final_norm)


def _norm_kernel(x_ref, g_ref, o_ref):
    o_ref[...] = _rms(x_ref[...], g_ref[...])


def _norm_call(x, g):
    n, d = x.shape
    tm = ROW_TILE
    return pl.pallas_call(
        _norm_kernel,
        grid=(n // tm,),
        in_specs=[pl.BlockSpec((tm, d), lambda i: (i, 0)), pl.BlockSpec((1, d), lambda i: (0, 0))],
        out_specs=pl.BlockSpec((tm, d), lambda i: (i, 0)),
        out_shape=jax.ShapeDtypeStruct((n, d), F32),
        compiler_params=_cparams(("parallel",)),
        name="final_norm",
    )(x, g)


def _rope_tables(seq):
    inv = ROPE_BASE ** (-jnp.arange(0, QK_ROPE, 2, dtype=F32) / QK_ROPE)
    ang = jnp.arange(seq)[:, None].astype(F32) * inv[None, :]
    cos, sin = jnp.cos(ang), jnp.sin(ang)
    pad = LANES - QK_NOPE - QK_ROPE
    zeros = lambda w: jnp.zeros((seq, w), F32)
    ctab = jnp.concatenate([jnp.ones((seq, QK_NOPE), F32), cos, cos, zeros(pad)], axis=1)
    stab = jnp.concatenate([zeros(QK_NOPE), sin, sin, zeros(pad)], axis=1)
    sa = jnp.concatenate([zeros(QK_NOPE), -sin, zeros(QK_ROPE // 2 + pad)], axis=1)
    sb = jnp.concatenate([zeros(QK_NOPE + QK_ROPE // 2), sin, zeros(pad)], axis=1)
    return ctab, stab, sa, sb


def _split_w_in(w_in):
    d = w_in.shape[0]
    bounds = np.cumsum([WIDTH_A, WIDTH_A, WIDTH_A, Q_LORA, KV_LORA, QK_ROPE, WIDTH_C, WIDTH_C])
    qa, ka, va, cq, ckv, kr, qc, kc, vc = jnp.split(w_in, bounds.tolist(), axis=1)
    half = QK_ROPE // 2
    z_lo = jnp.zeros((d, QK_NOPE), w_in.dtype)
    z_hi = jnp.zeros((d, LANES - QK_NOPE - QK_ROPE), w_in.dtype)
    rope_blk = jnp.concatenate([z_lo, kr, z_hi], axis=1)
    swap_blk = jnp.concatenate([z_lo, -kr[:, half:], kr[:, :half], z_hi], axis=1)
    score_scale = HEAD_DIM ** -0.5
    wa = jnp.concatenate([qa * score_scale, ka, va], axis=1).astype(BF16)
    wc = jnp.concatenate([qc * score_scale, kc, vc], axis=1).astype(BF16)
    wb = jnp.concatenate([cq, ckv, rope_blk, swap_blk], axis=1).astype(BF16)
    return jnp.concatenate([wa, wc, wb], axis=1)


def _split_w_uq(w_uq):
    r = w_uq.shape[0]
    w = w_uq.reshape(r, N_HEADS_B, QK_NOPE + QK_ROPE)
    nope, rope = w[..., :QK_NOPE], w[..., QK_NOPE:]
    z_hi = jnp.zeros((r, N_HEADS_B, LANES - QK_NOPE - QK_ROPE), w_uq.dtype)
    return jnp.concatenate([nope, rope, z_hi], axis=-1).reshape(r, -1).astype(BF16)


def _split_w_ukv(w_ukv):
    r = w_ukv.shape[0]
    w = w_ukv.reshape(r, N_HEADS_B, QK_NOPE + V_HEAD)
    k_nope, v = w[..., :QK_NOPE], w[..., QK_NOPE:]
    wk = jnp.concatenate([k_nope, jnp.zeros((r, N_HEADS_B, LANES - QK_NOPE), w_ukv.dtype)], axis=-1)
    return wk.reshape(r, -1).astype(BF16), v.reshape(r, -1).T.astype(BF16)


def kernel(x, g_mix, w_in, g_q, g_kv, w_uq, w_ukv, rpb, g_out_a, g_out_b, g_out_c, w_o, g_ffn, w1, w3, w2,
           w_router, e_w1, e_w3, e_w2, g_final):
    batch, seq, d = x.shape
    n = batch * seq
    depth = g_mix.shape[0]
    rows = seq // GRID_W
    rope_tabs = _rope_tables(seq)
    ew1, ew3, ew2 = e_w1.astype(BF16), e_w3.astype(BF16), e_w2.astype(BF16)
    xf = x.reshape(n, d)
    for layer in range(depth):
        win = _split_w_in(w_in[layer])
        wq = _split_w_uq(w_uq[layer])
        wk, wvt = _split_w_ukv(w_ukv[layer])
        pa, pa4, pa16, pc, qm, km, vt = _proj_call(xf, g_mix[layer][None], win, g_q[layer][None], g_kv[layer][None],
                                                   wq, wk, wvt, rope_tabs, seq)
        o_parts, lse_parts = [], []
        for dil, view in zip(DILATIONS, (pa, pa4, pa16)):
            o, lse = _band_call(view.reshape(batch, seq // dil, dil * 3 * WIDTH_A), dil)
            o_parts.append(o.reshape(n // dil, dil * WIDTH_A))
            lse_parts.append(lse.reshape(n // dil, dil * WIDTH_A))
        ob = _mla_call(qm, km, vt, batch, seq).reshape(n, WIDTH_B)
        oc = _na_call(pc, _na_tables(rpb[layer], rows), batch, seq).reshape(n, WIDTH_C)
        xf = _mix_out_call(xf, o_parts, lse_parts, ob, oc, g_out_a[layer][None], g_out_b[layer][None],
                           g_out_c[layer][None], w_o[layer].astype(BF16))
        j = layer // 2
        if layer % 2 == 0:
            xf = _ffn_call(xf, g_ffn[layer][None], w1[j].astype(BF16), w3[j].astype(BF16), w2[j].astype(BF16))
        else:
            wr_pad = jnp.pad(w_router[j], ((0, 0), (0, LANES - N_EXPERTS)))
            xf = _moe_layer(xf, g_ffn[layer][None], wr_pad, ew1, ew3, ew2, j, g_final[None], layer == depth - 1)
    if depth % 2 == 1:
        xf = _norm_call(xf, g_final[None])
    return xf.reshape(batch, seq, d)
```

```python
import functools
import math

import numpy as np
import jax
import jax.numpy as jnp
from jax import lax
from jax.experimental import pallas as pl
from jax.experimental.pallas import tpu as pltpu

F32 = jnp.float32
BF16 = jnp.bfloat16

LANES = 128
V7X_VMEM_LIMIT_BYTES = 52 * 1024 * 1024

HEAD_DIM = 64
N_HEADS_A = 6
DILATIONS = (1, 4, 16)
BAND_HALF = 64
N_HEADS_B = 6
Q_LORA = 384
KV_LORA = 256
QK_NOPE = 64
QK_ROPE = 32
V_HEAD = 64
ROPE_BASE = 10000.0
N_HEADS_C = 4
GRID_W = 64
NA_ROWS = 8
NA_COLS = 16
WIDTH_A = N_HEADS_A * HEAD_DIM
WIDTH_B = N_HEADS_B * V_HEAD
WIDTH_C = N_HEADS_C * HEAD_DIM
N_EXPERTS = 8
TOP_K = 2
RMS_EPS = 1e-6
NEG_INF = -1e30

ROW_TILE = 512
MLA_TQ = 256
MLA_TK = 256
MLA_UNROLL = 32
MLA_LOOKAHEAD = 4
MLA_HEADS_PER_STEP = 6
MLA_DEN_ROWS = 16
MLA_Q_PRESCALE = (QK_NOPE + QK_ROPE) ** -0.5 * math.log2(math.e)
BAND_TQ = 128
BAND_TILES_PER_STEP = 4
BAND_LOOKAHEAD = 6
NA_TILE_ROWS = 2
NA_KEY_ROWS = 10
NA_TILES_PER_STEP = 4
NA_LOOKAHEAD = 16
MOE_TM = 512
MOE_TF = 1792
COMBINE_TM = 256
GATHER_UNROLL = 8


def _cparams(semantics):
    return pltpu.CompilerParams(dimension_semantics=semantics, vmem_limit_bytes=V7X_VMEM_LIMIT_BYTES)


def _rms(x, g):
    return x * lax.rsqrt(jnp.mean(x * x, axis=-1, keepdims=True) + RMS_EPS) * g


def _dot(a, b):
    return jnp.dot(a, b, preferred_element_type=F32)


def _dot_nt(a, b):
    return lax.dot_general(a, b, (((1,), (1,)), ((), ())), preferred_element_type=F32)


def _proj_kernel(x_ref, g_ref, win_ref, gq_ref, gkv_ref, wq_ref, wk_ref, wvt_ref,
                 ct_ref, st_ref, sa_ref, sb_ref, pa_ref, pa4_ref, pa16_ref, pc_ref, qm_ref, km_ref, vt_ref, pa_scr):
    tm = x_ref.shape[0]
    h = _rms(x_ref[...], g_ref[...]).astype(BF16)
    proj = _dot(h, win_ref[...])
    pa = proj[:, :3 * WIDTH_A]
    pb = proj[:, 3 * (WIDTH_A + WIDTH_C):]
    pa_ref[...] = pa.astype(BF16)
    n_chunks = pa.shape[1] // LANES
    for c in range(n_chunks):
        pa_scr[c] = pa[:, c * LANES:(c + 1) * LANES]
    for dil, view_ref in ((DILATIONS[1], pa4_ref), (DILATIONS[2], pa16_ref)):
        for r in range(dil):
            for c in range(n_chunks):
                col = r * 3 * WIDTH_A + c * LANES
                view_ref[:, col:col + LANES] = pa_scr[c, pl.ds(r, tm // dil, stride=dil), :].astype(BF16)
    pc_ref[...] = proj[:, 3 * WIDTH_A:3 * (WIDTH_A + WIDTH_C)].astype(BF16)
    hq = _rms(pb[:, :Q_LORA], gq_ref[...]).astype(BF16)
    hkv = _rms(pb[:, Q_LORA:Q_LORA + KV_LORA], gkv_ref[...]).astype(BF16)
    r1 = pb[:, Q_LORA + KV_LORA:Q_LORA + KV_LORA + LANES]
    r2 = pb[:, Q_LORA + KV_LORA + LANES:]
    ct = ct_ref[...]
    st = st_ref[...]
    sa = sa_ref[...]
    sb = sb_ref[...]
    half = QK_ROPE // 2
    qa = _dot(hq, wq_ref[...])
    kn = _dot(hkv, wk_ref[...])
    kr = r1 * ct + r2 * st
    for hd in range(N_HEADS_B):
        sl = slice(hd * LANES, (hd + 1) * LANES)
        q = qa[:, sl]
        q = q * ct + pltpu.roll(q, LANES - half, 1) * sa + pltpu.roll(q, half, 1) * sb
        qm_ref[:, sl] = (q * MLA_Q_PRESCALE).astype(BF16)
        km_ref[:, sl] = (kn[:, sl] + kr).astype(BF16)
    vt_ref[...] = _dot_nt(wvt_ref[...], hkv).astype(BF16)


def _proj_call(x, g, win, gq, gkv, wq, wk, wvt, tabs, seq):
    n, d = x.shape
    tm = ROW_TILE
    tiles_per_seq = seq // tm
    full = lambda a: pl.BlockSpec(a.shape, lambda i: (0,) * a.ndim)
    row = lambda w: pl.BlockSpec((tm, w), lambda i: (i, 0))
    tab = pl.BlockSpec((tm, LANES), lambda i: (i % tiles_per_seq, 0))
    view = lambda dil: pl.BlockSpec((tm // dil, dil * 3 * WIDTH_A), lambda i: (i, 0))
    hb = N_HEADS_B * LANES
    return pl.pallas_call(
        _proj_kernel,
        grid=(n // tm,),
        in_specs=[row(d), full(g), full(win), full(gq), full(gkv), full(wq), full(wk), full(wvt)] + [tab] * 4,
        out_specs=[row(3 * WIDTH_A)] + [view(dil) for dil in DILATIONS[1:]] + [row(3 * WIDTH_C), row(hb), row(hb),
                   pl.BlockSpec((WIDTH_B, tm), lambda i: (0, i))],
        out_shape=[jax.ShapeDtypeStruct((n, 3 * WIDTH_A), BF16)]
                  + [jax.ShapeDtypeStruct((n // dil, dil * 3 * WIDTH_A), BF16) for dil in DILATIONS[1:]]
                  + [jax.ShapeDtypeStruct((n, 3 * WIDTH_C), BF16),
                   jax.ShapeDtypeStruct((n, hb), BF16), jax.ShapeDtypeStruct((n, hb), BF16),
                   jax.ShapeDtypeStruct((WIDTH_B, n), BF16)],
        scratch_shapes=[pltpu.VMEM((3 * WIDTH_A // LANES, tm, LANES), F32)],
        compiler_params=_cparams(("parallel",)),
        name="proj",
    )(x, g, win, gq, gkv, wq, wk, wvt, *tabs)


def _run_pipeline(n_items, look, issue, consume, finish):
    if look >= n_items:
        results = [consume(i, s) for i, s in enumerate([issue(i) for i in range(n_items)])]
        for i, res in enumerate(results):
            finish(i, res)
        return
    inflight = {i: issue(i) for i in range(min(look, n_items))}
    pending = None
    for i in range(n_items):
        if i + look < n_items:
            inflight[i + look] = issue(i + look)
        if pending is not None:
            finish(*pending)
        pending = (i, consume(i, inflight.pop(i)))
    finish(*pending)


class _HeadPairs:
    def __init__(self, q_ref, k_ref, v_ref, tq, tkw):
        self.q_ref, self.k_ref, self.v_ref, self.tq, self.tkw = q_ref, k_ref, v_ref, tq, tkw
        lane = lax.broadcasted_iota(jnp.int32, (tq, LANES), 1)
        self.low = lane < HEAD_DIM
        self.ones = jnp.ones((tkw, LANES), BF16)

    @staticmethod
    def cols(h):
        return slice((h // 2) * LANES, (h // 2 + 1) * LANES)

    def scores(self, q_start, k_start, h):
        q = self.q_ref[q_start:q_start + self.tq, self.cols(h)]
        q = jnp.where(self.low if h % 2 == 0 else ~self.low, q, jnp.zeros_like(q))
        return _dot_nt(q, self.k_ref[pl.ds(k_start, self.tkw), self.cols(h)])

    def values(self, k_start, h, p):
        v = jnp.concatenate([self.v_ref[pl.ds(k_start, self.tkw), self.cols(h)], self.ones], axis=-1)
        o = _dot(p, v)
        return o[:, :LANES], o[:, LANES:]

    def merge(self, even, odd):
        return jnp.where(self.low, even, odd)


def _band_tables(dilation):
    tq, tkw = BAND_TQ, BAND_TQ + 2 * BAND_HALF
    shift = np.array([0, -BAND_HALF, -2 * BAND_HALF])
    rel = shift[:, None, None] + np.arange(tkw)[None, None, :] - np.arange(tq)[None, :, None]
    dist = np.abs(rel)
    slopes = 2.0 ** (-8.0 * np.arange(1, N_HEADS_A + 1) / N_HEADS_A)
    bias = -(slopes[None, :, None, None] * dilation) * dist[:, None].astype(np.float64)
    tab = np.where(dist[:, None] <= BAND_HALF, bias, NEG_INF)
    return jnp.asarray(tab, F32)


def _band_kernel(q_ref, k_ref, v_ref, tab_ref, o_ref, lse_ref, *, n):
    tq = BAND_TQ
    tkw = tq + 2 * BAND_HALF
    tiles = q_ref.shape[0] // tq
    last_tile = n // tq - 1
    items = [(t, h) for t in range(tiles) for h in range(N_HEADS_A)]
    info = []
    for t in range(tiles):
        i = pl.program_id(2) * tiles + t
        start = pl.multiple_of(jnp.clip(i * tq - BAND_HALF, 0, n - tkw), BAND_HALF)
        variant = jnp.minimum(i, 1) + (i == last_tile).astype(jnp.int32)
        info.append((start, variant))
    pairs = _HeadPairs(q_ref, k_ref, v_ref, tq, tkw)
    held = {}

    def issue(idx):
        t, h = items[idx]
        return pairs.scores(t * tq, info[t][0], h)

    def consume(idx, s):
        t, h = items[idx]
        s = s + tab_ref[info[t][1], h]
        m = jnp.max(s, axis=-1, keepdims=True)
        return jnp.exp(s - m).astype(BF16), m

    def finish(idx, res):
        t, h = items[idx]
        p, m = res
        out, den = pairs.values(info[t][0], h, p)
        held[h % 2] = (out / den, m + jnp.log(den))
        if h % 2 == 1:
            cols = pairs.cols(h)
            o_ref[t * tq:(t + 1) * tq, cols] = pairs.merge(held[0][0], held[1][0])
            lse_ref[t * tq:(t + 1) * tq, cols] = pairs.merge(held[0][1], held[1][1])

    _run_pipeline(len(items), BAND_LOOKAHEAD, issue, consume, finish)


def _band_call(pa_view, dilation):
    b, n, _ = pa_view.shape
    rows = BAND_TQ * BAND_TILES_PER_STEP
    w = WIDTH_A
    tab = _band_tables(dilation)
    qspec = pl.BlockSpec((None, rows, w), lambda bb, r, i: (bb, i, 3 * r))
    kspec = pl.BlockSpec((None, n, w), lambda bb, r, i: (bb, 0, 3 * r + 1))
    vspec = pl.BlockSpec((None, n, w), lambda bb, r, i: (bb, 0, 3 * r + 2))
    tspec = pl.BlockSpec(tab.shape, lambda bb, r, i: (0, 0, 0, 0))
    ospec = pl.BlockSpec((None, rows, w), lambda bb, r, i: (bb, i, r))
    shape = jax.ShapeDtypeStruct((b, n, dilation * w), F32)
    return pl.pallas_call(
        functools.partial(_band_kernel, n=n),
        grid=(b, dilation, n // rows),
        in_specs=[qspec, kspec, vspec, tspec],
        out_specs=[ospec, ospec],
        out_shape=[shape, shape],
        compiler_params=_cparams(("parallel", "parallel", "parallel")),
        name=f"band_d{dilation}",
    )(pa_view, pa_view, pa_view, tab)


def _mla_kernel(q_ref, k_ref, vt_ref, o_ref):
    tq = q_ref.shape[0]
    seq = k_ref.shape[0]
    tk = MLA_TK
    nh = MLA_HEADS_PER_STEP
    n_chunks = seq // tk
    items = [(u, h) for u in range(MLA_UNROLL) for h in range(nh)]
    look = MLA_LOOKAHEAD
    ones_rows = jnp.ones((MLA_DEN_ROWS, tk), BF16)

    def key_slice(chunk):
        return pl.ds(pl.multiple_of(chunk * tk, tk), tk)

    def score_matmul(chunk, h):
        k = k_ref[key_slice(chunk), h * LANES:(h + 1) * LANES]
        return _dot_nt(k, q_ref[:, h * LANES:(h + 1) * LANES])

    def value_matmul(chunk, h, p):
        vt = jnp.concatenate([vt_ref[h * V_HEAD:(h + 1) * V_HEAD, key_slice(chunk)], ones_rows], axis=0)
        return _dot(vt, p)

    def body(j, carry):
        state = list(carry[:2 * nh])
        scores = dict(zip(items[:look], carry[2 * nh:2 * nh + look]))
        pend_p, pend_alpha = carry[2 * nh + look:]
        pending = (jnp.maximum(j * MLA_UNROLL - 1, 0), nh - 1, pend_p, pend_alpha)
        ahead = []
        for idx, (u, h) in enumerate(items):
            la = idx + look
            if la < len(items):
                lu, lh = items[la]
                scores[lu, lh] = score_matmul(j * MLA_UNROLL + lu, lh)
            else:
                lu, lh = items[la - len(items)]
                ahead.append(score_matmul(jnp.minimum((j + 1) * MLA_UNROLL + lu, n_chunks - 1), lh))
            pc, ph, pp, pa = pending
            state[2 * ph + 1] = pa * state[2 * ph + 1] + value_matmul(pc, ph, pp)
            s = scores.pop((u, h))
            m_new = jnp.maximum(state[2 * h], jnp.max(s, axis=0, keepdims=True))
            alpha = jnp.exp2(state[2 * h] - m_new)
            state[2 * h] = m_new
            pending = (j * MLA_UNROLL + u, h, jnp.exp2(s - m_new).astype(BF16), alpha)
        return tuple(state) + tuple(ahead) + (pending[2], pending[3])

    init = (jnp.full((1, tq), NEG_INF, F32), jnp.zeros((V_HEAD + MLA_DEN_ROWS, tq), F32)) * nh
    init += tuple(score_matmul(u, h) for u, h in items[:look])
    init += (jnp.zeros((tk, tq), BF16), jnp.ones((1, tq), F32))
    res = lax.fori_loop(0, n_chunks // MLA_UNROLL, body, init)
    accs = [res[2 * h + 1] for h in range(nh)]
    accs[nh - 1] = res[-1] * accs[nh - 1] + value_matmul(n_chunks - 1, nh - 1, res[-2])
    out_t = jnp.concatenate([a[:V_HEAD] / a[V_HEAD:V_HEAD + 1] for a in accs], axis=0)
    o_ref[...] = out_t.T


def _mla_call(qm, km, vt, batch, seq):
    tq = MLA_TQ
    nh = MLA_HEADS_PER_STEP
    qspec = pl.BlockSpec((None, tq, nh * LANES), lambda b, g, i: (b, i, g))
    kspec = pl.BlockSpec((None, seq, nh * LANES), lambda b, g, i: (b, 0, g))
    vspec = pl.BlockSpec((nh * V_HEAD, seq), lambda b, g, i: (g, b))
    ospec = pl.BlockSpec((None, tq, nh * V_HEAD), lambda b, g, i: (b, i, g))
    return pl.pallas_call(
        _mla_kernel,
        grid=(batch, N_HEADS_B // nh, seq // tq),
        in_specs=[qspec, kspec, vspec],
        out_specs=ospec,
        out_shape=jax.ShapeDtypeStruct((batch, seq, WIDTH_B), F32),
        compiler_params=_cparams(("parallel", "parallel", "parallel")),
        name="mla",
    )(qm.reshape(batch, seq, -1), km.reshape(batch, seq, -1), vt)


def _na_variant(i, n_tiles):
    return jnp.minimum(i, 2) + jnp.maximum(i - (n_tiles - 3), 0)


def _na_kernel(q_ref, k_ref, v_ref, tab_ref, o_ref, *, rows):
    tq = NA_TILE_ROWS * GRID_W
    tkw = NA_KEY_ROWS * GRID_W
    tiles = q_ref.shape[0] // tq
    n_tiles = rows // NA_TILE_ROWS
    items = [(t, h) for t in range(tiles) for h in range(N_HEADS_C)]
    info = []
    for t in range(tiles):
        i = pl.program_id(1) * tiles + t
        base = jnp.clip(i * NA_TILE_ROWS - NA_ROWS // 2, 0, rows - NA_KEY_ROWS)
        info.append((pl.multiple_of(base * GRID_W, GRID_W), _na_variant(i, n_tiles)))
    pairs = _HeadPairs(q_ref, k_ref, v_ref, tq, tkw)
    held = {}

    def issue(idx):
        t, h = items[idx]
        return pairs.scores(t * tq, info[t][0], h)

    def consume(idx, s):
        t, h = items[idx]
        s = s + tab_ref[info[t][1], h]
        return jnp.exp(s - jnp.max(s, axis=-1, keepdims=True)).astype(BF16)

    def finish(idx, p):
        t, h = items[idx]
        out, den = pairs.values(info[t][0], h, p)
        held[h % 2] = out / den
        if h % 2 == 1:
            o_ref[t * tq:(t + 1) * tq, pairs.cols(h)] = pairs.merge(held[0], held[1])

    _run_pipeline(len(items), NA_LOOKAHEAD, issue, consume, finish)


def _na_tables(rpb, rows):
    r0 = np.array([0, 2, 4, rows - 4, rows - 2])
    base = np.clip(r0 - NA_ROWS // 2, 0, rows - NA_KEY_ROWS)
    r = r0[:, None] + np.arange(NA_TILE_ROWS)[None, :]
    row_start = np.clip(r - NA_ROWS // 2, 0, rows - NA_ROWS)
    krow = base[:, None] + np.arange(NA_KEY_ROWS)[None, :]
    drow = krow[:, None, :] - r[:, :, None]
    row_ok = (krow[:, None, :] >= row_start[:, :, None]) & (krow[:, None, :] < row_start[:, :, None] + NA_ROWS)
    c = np.arange(GRID_W)
    win_start = np.clip(c - NA_COLS // 2, 0, GRID_W - NA_COLS)
    col_ok = (c[None, :] >= win_start[:, None]) & (c[None, :] < win_start[:, None] + NA_COLS)
    dcol = np.clip(c[None, :] - c[:, None], -(NA_COLS - 1), NA_COLS - 1)
    ok = row_ok[:, :, None, :, None] & col_ok[None, None, :, None, :]
    di = np.clip(drow, -(NA_ROWS - 1), NA_ROWS - 1) + (NA_ROWS - 1)
    pick_col = (dcol[:, :, None] + NA_COLS - 1 == np.arange(2 * NA_COLS - 1)).astype(np.float32)
    pick_row = (di[..., None] == np.arange(2 * NA_ROWS - 1)).astype(np.float32)
    hi = lax.Precision.HIGHEST
    toeplitz = jnp.einsum("hab,cjb->hacj", rpb.astype(F32), pick_col, precision=hi)
    bias = jnp.einsum("vqka,hacj->vhqckj", pick_row, toeplitz, precision=hi)
    tab = jnp.where(jnp.asarray(ok)[:, None], bias, NEG_INF)
    return tab.reshape(5, N_HEADS_C, NA_TILE_ROWS * GRID_W, NA_KEY_ROWS * GRID_W)


def _na_call(pc, tab, batch, seq):
    rows = seq // GRID_W
    tq = NA_TILE_ROWS * GRID_W * NA_TILES_PER_STEP
    w = WIDTH_C
    qspec = pl.BlockSpec((None, tq, w), lambda b, i: (b, i, 0))
    kspec = pl.BlockSpec((None, seq, w), lambda b, i: (b, 0, 1))
    vspec = pl.BlockSpec((None, seq, w), lambda b, i: (b, 0, 2))
    tspec = pl.BlockSpec(tab.shape, lambda b, i: (0, 0, 0, 0))
    ospec = pl.BlockSpec((None, tq, w), lambda b, i: (b, i, 0))
    pc3 = pc.reshape(batch, seq, 3 * w)
    return pl.pallas_call(
        functools.partial(_na_kernel, rows=rows),
        grid=(batch, seq // tq),
        in_specs=[qspec, kspec, vspec, tspec],
        out_specs=ospec,
        out_shape=jax.ShapeDtypeStruct((batch, seq, w), F32),
        compiler_params=_cparams(("parallel", "parallel")),
        name="natten",
    )(pc3, pc3, pc3, tab)


def _mix_out_kernel(x_ref, o1_ref, o2_ref, o3_ref, l1_ref, l2_ref, l3_ref, ob_ref, oc_ref,
                    ga_ref, gb_ref, gc_ref, wo_ref, out_ref, o2_scr, o3_scr, l2_scr, l3_scr):
    tm = x_ref.shape[0]
    n_chunks = WIDTH_A // LANES

    def natural(view_ref, scr, dil):
        for r in range(dil):
            for c in range(n_chunks):
                col = r * WIDTH_A + c * LANES
                scr[c, pl.ds(r, tm // dil, stride=dil), :] = view_ref[:, col:col + LANES]
        return jnp.concatenate([scr[c] for c in range(n_chunks)], axis=-1)

    l1, o1 = l1_ref[...], o1_ref[...]
    l2, o2 = natural(l2_ref, l2_scr, DILATIONS[1]), natural(o2_ref, o2_scr, DILATIONS[1])
    l3, o3 = natural(l3_ref, l3_scr, DILATIONS[2]), natural(o3_ref, o3_scr, DILATIONS[2])
    mx = jnp.maximum(jnp.maximum(l1, l2), l3)
    w1, w2, w3 = jnp.exp(l1 - mx), jnp.exp(l2 - mx), jnp.exp(l3 - mx)
    oa = (w1 * o1 + w2 * o2 + w3 * o3) / (w1 + w2 + w3)
    ya = _rms(oa, ga_ref[...]).astype(BF16)
    yb = _rms(ob_ref[...], gb_ref[...]).astype(BF16)
    yc = _rms(oc_ref[...], gc_ref[...]).astype(BF16)
    y = _dot(ya, wo_ref[:WIDTH_A, :])
    y = y + _dot(yb, wo_ref[WIDTH_A:WIDTH_A + WIDTH_B, :])
    y = y + _dot(yc, wo_ref[WIDTH_A + WIDTH_B:, :])
    out_ref[...] = x_ref[...] + y


def _mix_out_call(x, o_parts, lse_parts, ob, oc, ga, gb, gc, wo):
    n, d = x.shape
    tm = ROW_TILE
    full = lambda a: pl.BlockSpec(a.shape, lambda i: (0,) * a.ndim)
    row = lambda w: pl.BlockSpec((tm, w), lambda i: (i, 0))
    view = lambda dil: pl.BlockSpec((tm // dil, dil * WIDTH_A), lambda i: (i, 0))
    return pl.pallas_call(
        _mix_out_kernel,
        grid=(n // tm,),
        in_specs=[row(d)] + [view(dil) for dil in DILATIONS] * 2
                 + [row(WIDTH_B), row(WIDTH_C), full(ga), full(gb), full(gc), full(wo)],
        out_specs=row(d),
        out_shape=jax.ShapeDtypeStruct((n, d), F32),
        scratch_shapes=[pltpu.VMEM((WIDTH_A // LANES, tm, LANES), F32)] * 4,
        compiler_params=_cparams(("parallel",)),
        name="mix_out",
    )(x, *o_parts, *lse_parts, ob, oc, ga, gb, gc, wo)


def _silu(u):
    return u * (1.0 / (1.0 + jnp.exp(-u)))


def _ffn_kernel(x_ref, g_ref, w1_ref, w3_ref, w2_ref, out_ref):
    x = x_ref[...]
    h = _rms(x, g_ref[...]).astype(BF16)
    a = (_silu(_dot(h, w1_ref[...])) * _dot(h, w3_ref[...])).astype(BF16)
    out_ref[...] = x + _dot(a, w2_ref[...])


def _ffn_call(x, g, w1, w3, w2):
    n, d = x.shape
    tm = ROW_TILE
    resident = lambda a: pl.BlockSpec(a.shape, lambda i: (0, 0), pipeline_mode=pl.Buffered(1))
    return pl.pallas_call(
        _ffn_kernel,
        grid=(n // tm,),
        in_specs=[pl.BlockSpec((tm, d), lambda i: (i, 0)), pl.BlockSpec((1, d), lambda i: (0, 0)),
                  resident(w1), resident(w3), resident(w2)],
        out_specs=pl.BlockSpec((tm, d), lambda i: (i, 0)),
        out_shape=jax.ShapeDtypeStruct((n, d), F32),
        compiler_params=_cparams(("parallel",)),
        name="ffn",
    )(x, g, w1, w3, w2)


def _router_kernel(x_ref, g_ref, wr_ref, h_ref, e_ref, gate_ref):
    h = _rms(x_ref[...], g_ref[...])
    _to_token_tiles(h_ref, h)
    logits = jnp.dot(h, wr_ref[...], precision=lax.Precision.HIGHEST, preferred_element_type=F32)
    lane = lax.broadcasted_iota(jnp.int32, logits.shape, 1).astype(F32)
    logits = jnp.where(lane < N_EXPERTS, logits, -jnp.inf)
    m1 = jnp.max(logits, axis=-1, keepdims=True)
    i1 = jnp.min(jnp.where(logits == m1, lane, float(LANES)), axis=-1, keepdims=True)
    rest = jnp.where(lane == i1, -jnp.inf, logits)
    m2 = jnp.max(rest, axis=-1, keepdims=True)
    i2 = jnp.min(jnp.where(rest == m2, lane, float(LANES)), axis=-1, keepdims=True)
    e = jnp.exp(m2 - m1)
    den = 1.0 + e
    e_ref[...] = jnp.where(lane == 0.0, i1, jnp.where(lane == 1.0, i2, 0.0)).astype(jnp.int32)
    gate_ref[...] = jnp.where(lane == 0.0, 1.0 / den, jnp.where(lane == 1.0, e / den, 0.0))


def _router_call(x, g, wr_pad):
    n, d = x.shape
    tm = ROW_TILE
    row = lambda w: pl.BlockSpec((tm, w), lambda i: (i, 0))
    full = lambda a: pl.BlockSpec(a.shape, lambda i: (0,) * a.ndim)
    return pl.pallas_call(
        _router_kernel,
        grid=(n // tm,),
        in_specs=[row(d), full(g), full(wr_pad)],
        out_specs=[pl.BlockSpec((tm * TOKEN_ROWS, LANES), lambda i: (i, 0)), row(LANES), row(LANES)],
        out_shape=[jax.ShapeDtypeStruct((n * TOKEN_ROWS, LANES), F32), jax.ShapeDtypeStruct((n, LANES), jnp.int32),
                   jax.ShapeDtypeStruct((n, LANES), F32)],
        compiler_params=_cparams(("parallel",)),
        name="router",
    )(x, g, wr_pad)


TOKEN_ROWS = 8


def _to_token_tiles(dst_ref, x):
    tm, d = x.shape
    assert d == TOKEN_ROWS * LANES
    for j in range(TOKEN_ROWS):
        dst_ref[pl.ds(j, tm, stride=TOKEN_ROWS), :] = x[:, j * LANES:(j + 1) * LANES]


def _from_token_tiles(src_ref, tm):
    return jnp.concatenate([src_ref[pl.ds(j, tm, stride=TOKEN_ROWS), :] for j in range(TOKEN_ROWS)], axis=-1)


def _row_copy(src_hbm, row, dst_ref, r, sem):
    src = src_hbm.at[pl.ds(pl.multiple_of(row * TOKEN_ROWS, TOKEN_ROWS), TOKEN_ROWS)]
    return pltpu.make_async_copy(src, dst_ref.at[pl.ds(pl.multiple_of(r * TOKEN_ROWS, TOKEN_ROWS), TOKEN_ROWS)], sem)


def _start_row_gather(idx_ref, base, src_hbm, dst_ref, sem, count):
    def body(r, c):
        _row_copy(src_hbm, idx_ref[base + r], dst_ref, r, sem).start()
        return c
    lax.fori_loop(0, count, body, 0, unroll=GATHER_UNROLL)


def _wait_row_gather(src_hbm, dst_ref, sem, count):
    pltpu.make_async_copy(src_hbm.at[pl.ds(0, count * TOKEN_ROWS)], dst_ref, sem).wait()


def _moe_ffn_kernel(tile_e_ref, tile_ok_ref, row_tok_ref, h_hbm, w1_ref, w3_ref, w2_ref, out_ref,
                    xg_ref, acc_ref, sem):
    i = pl.program_id(0)
    f = pl.program_id(1)
    n_tiles = pl.num_programs(0)
    tm = MOE_TM
    slot = i % 2

    @pl.when(f == 0)
    def _():
        @pl.when(i == 0)
        def _():
            acc_ref[...] = jnp.zeros_like(acc_ref)
            _start_row_gather(row_tok_ref, 0, h_hbm, xg_ref.at[0], sem.at[0], tm)

        @pl.when(tile_ok_ref[i] != 0)
        def _():
            _wait_row_gather(h_hbm, xg_ref.at[slot], sem.at[slot], tm)

        @pl.when(jnp.logical_and(i + 1 < n_tiles, tile_ok_ref[jnp.minimum(i + 1, n_tiles - 1)] != 0))
        def _():
            _start_row_gather(row_tok_ref, (i + 1) * tm, h_hbm, xg_ref.at[1 - slot], sem.at[1 - slot], tm)

    @pl.when(tile_ok_ref[i] != 0)
    def _():
        h = _from_token_tiles(xg_ref.at[slot], tm).astype(BF16)
        a = (_silu(_dot(h, w1_ref[...])) * _dot(h, w3_ref[...])).astype(BF16)
        total = _dot(a, w2_ref[...]) + jnp.where(f > 0, acc_ref[...], 0.0)
        acc_ref[...] = total
        _to_token_tiles(out_ref, total)

    @pl.when(tile_ok_ref[i] == 0)
    def _():
        out_ref[...] = jnp.zeros_like(out_ref)


def _moe_ffn_call(tile_e, tile_ok, row_tok, h, w1, w3, w2, layer):
    d = w1.shape[2]
    n_tiles = tile_e.shape[0]
    ff = w1.shape[3]
    tm, tf = MOE_TM, MOE_TF

    def ff_block(i, f, tile_ok):
        return jnp.where(tile_ok[i] != 0, f, ff // tf - 1)

    grid_spec = pltpu.PrefetchScalarGridSpec(
        num_scalar_prefetch=3,
        grid=(n_tiles, ff // tf),
        in_specs=[pl.BlockSpec(memory_space=pl.ANY),
                  pl.BlockSpec((None, None, d, tf), lambda i, f, te, tv, rt: (layer, te[i], 0, ff_block(i, f, tv))),
                  pl.BlockSpec((None, None, d, tf), lambda i, f, te, tv, rt: (layer, te[i], 0, ff_block(i, f, tv))),
                  pl.BlockSpec((None, None, tf, d), lambda i, f, te, tv, rt: (layer, te[i], ff_block(i, f, tv), 0))],
        out_specs=pl.BlockSpec((tm * TOKEN_ROWS, LANES), lambda i, f, te, tv, rt: (i, 0)),
        scratch_shapes=[pltpu.VMEM((2, tm * TOKEN_ROWS, LANES), F32), pltpu.VMEM((tm, d), F32),
                        pltpu.SemaphoreType.DMA((2,))],
    )
    return pl.pallas_call(
        _moe_ffn_kernel,
        grid_spec=grid_spec,
        out_shape=jax.ShapeDtypeStruct((n_tiles * tm * TOKEN_ROWS, LANES), F32),
        compiler_params=_cparams(("arbitrary", "arbitrary")),
        name="moe_ffn",
    )(tile_e, tile_ok, row_tok, h, w1, w3, w2)


def _combine_kernel(d0_ref, d1_ref, y_hbm, x_ref, gate_ref, gain_ref, out_ref, b0_ref, b1_ref, sem, *, final_norm):
    i = pl.program_id(0)
    n_tiles = pl.num_programs(0)
    tm = COMBINE_TM
    slot = i % 2

    def start(tile, s):
        _start_row_gather(d0_ref, tile * tm, y_hbm, b0_ref.at[s], sem.at[0, s], tm)
        _start_row_gather(d1_ref, tile * tm, y_hbm, b1_ref.at[s], sem.at[1, s], tm)

    @pl.when(i == 0)
    def _():
        start(0, 0)

    _wait_row_gather(y_hbm, b0_ref.at[slot], sem.at[0, slot], tm)
    _wait_row_gather(y_hbm, b1_ref.at[slot], sem.at[1, slot], tm)

    @pl.when(i + 1 < n_tiles)
    def _():
        start(i + 1, 1 - slot)

    gates = gate_ref[...]
    y0 = _from_token_tiles(b0_ref.at[slot], tm)
    y1 = _from_token_tiles(b1_ref.at[slot], tm)
    out = x_ref[...] + (gates[:, 0:1] * y0 + gates[:, 1:2] * y1)
    out_ref[...] = _rms(out, gain_ref[...]) if final_norm else out


def _combine_call(d0, d1, y, x, gates, gain, final_norm):
    n, d = x.shape
    tm = COMBINE_TM
    grid_spec = pltpu.PrefetchScalarGridSpec(
        num_scalar_prefetch=2,
        grid=(n // tm,),
        in_specs=[pl.BlockSpec(memory_space=pl.ANY),
                  pl.BlockSpec((tm, d), lambda i, a, b: (i, 0)),
                  pl.BlockSpec((tm, LANES), lambda i, a, b: (i, 0)),
                  pl.BlockSpec((1, d), lambda i, a, b: (0, 0))],
        out_specs=pl.BlockSpec((tm, d), lambda i, a, b: (i, 0)),
        scratch_shapes=[pltpu.VMEM((2, tm * TOKEN_ROWS, LANES), F32)] * 2 + [pltpu.SemaphoreType.DMA((2, 2))],
    )
    return pl.pallas_call(
        functools.partial(_combine_kernel, final_norm=final_norm),
        grid_spec=grid_spec,
        out_shape=jax.ShapeDtypeStruct((n, d), F32),
        compiler_params=_cparams(("arbitrary",)),
        name="moe_combine",
    )(d0, d1, y, x, gates, gain)


def _moe_plan(top_e, n_tiles):
    n_assign = top_e.shape[0] * TOP_K
    flat_e = top_e.reshape(n_assign)
    onehot = (flat_e[:, None] == jnp.arange(N_EXPERTS, dtype=jnp.int32)[None, :]).astype(jnp.int32)
    csum = jnp.cumsum(onehot, axis=0)
    rank = jnp.take_along_axis(csum, flat_e[:, None], axis=1)[:, 0] - 1
    counts = csum[-1]
    padded = (counts + MOE_TM - 1) // MOE_TM * MOE_TM
    pend = jnp.cumsum(padded)
    pstart = pend - padded
    dest = pstart[flat_e] + rank
    tile_start = jnp.arange(n_tiles, dtype=jnp.int32) * MOE_TM
    tile_e = jnp.minimum(jnp.searchsorted(pend, tile_start, side="right"), N_EXPERTS - 1).astype(jnp.int32)
    tile_ok = (tile_start < pend[-1]).astype(jnp.int32)
    order = jnp.argsort(flat_e, stable=True).astype(jnp.int32)
    row_e = jnp.repeat(tile_e, MOE_TM)
    r = jnp.arange(n_tiles * MOE_TM, dtype=jnp.int32) - pstart[row_e]
    src = jnp.clip((jnp.cumsum(counts) - counts)[row_e] + r, 0, n_assign - 1)
    row_tok = jnp.where(r < counts[row_e], order[src] // TOP_K, 0).astype(jnp.int32)
    dest = dest.reshape(-1, TOP_K).astype(jnp.int32)
    return tile_e, tile_ok, row_tok, dest[:, 0], dest[:, 1]


def _moe_layer(x, g, wr_pad, w1, w3, w2, layer, gain, final_norm):
    n = x.shape[0]
    h, top_e, gates = _router_call(x, g, wr_pad)
    n_tiles = -(-(n * TOP_K + N_EXPERTS * (MOE_TM - 1)) // MOE_TM)
    tile_e, tile_ok, row_tok, d0, d1 = _moe_plan(top_e[:, :TOP_K], n_tiles)
    y = _moe_ffn_call(tile_e, tile_ok, row_tok, h, w1, w3, w2, layer)
    return _combine_call(d0, d1, y, x, gates, gain, final_norm)


def _norm_kernel(x_ref, g_ref, o_ref):
    o_ref[...] = _rms(x_ref[...], g_ref[...])


def _norm_call(x, g):
    n, d = x.shape
    tm = ROW_TILE
    return pl.pallas_call(
        _norm_kernel,
        grid=(n // tm,),
        in_specs=[pl.BlockSpec((tm, d), lambda i: (i, 0)), pl.BlockSpec((1, d), lambda i: (0, 0))],
        out_specs=pl.BlockSpec((tm, d), lambda i: (i, 0)),
        out_shape=jax.ShapeDtypeStruct((n, d), F32),
        compiler_params=_cparams(("parallel",)),
        name="final_norm",
    )(x, g)


def _rope_tables(seq):
    inv = ROPE_BASE ** (-jnp.arange(0, QK_ROPE, 2, dtype=F32) / QK_ROPE)
    ang = jnp.arange(seq)[:, None].astype(F32) * inv[None, :]
    cos, sin = jnp.cos(ang), jnp.sin(ang)
    pad = LANES - QK_NOPE - QK_ROPE
    zeros = lambda w: jnp.zeros((seq, w), F32)
    ctab = jnp.concatenate([jnp.ones((seq, QK_NOPE), F32), cos, cos, zeros(pad)], axis=1)
    stab = jnp.concatenate([zeros(QK_NOPE), sin, sin, zeros(pad)], axis=1)
    sa = jnp.concatenate([zeros(QK_NOPE), -sin, zeros(QK_ROPE // 2 + pad)], axis=1)
    sb = jnp.concatenate([zeros(QK_NOPE + QK_ROPE // 2), sin, zeros(pad)], axis=1)
    return ctab, stab, sa, sb


def _split_w_in(w_in):
    d = w_in.shape[0]
    bounds = np.cumsum([WIDTH_A, WIDTH_A, WIDTH_A, Q_LORA, KV_LORA, QK_ROPE, WIDTH_C, WIDTH_C])
    qa, ka, va, cq, ckv, kr, qc, kc, vc = jnp.split(w_in, bounds.tolist(), axis=1)
    half = QK_ROPE // 2
    z_lo = jnp.zeros((d, QK_NOPE), w_in.dtype)
    z_hi = jnp.zeros((d, LANES - QK_NOPE - QK_ROPE), w_in.dtype)
    rope_blk = jnp.concatenate([z_lo, kr, z_hi], axis=1)
    swap_blk = jnp.concatenate([z_lo, -kr[:, half:], kr[:, :half], z_hi], axis=1)
    score_scale = HEAD_DIM ** -0.5
    wa = jnp.concatenate([qa * score_scale, ka, va], axis=1).astype(BF16)
    wc = jnp.concatenate([qc * score_scale, kc, vc], axis=1).astype(BF16)
    wb = jnp.concatenate([cq, ckv, rope_blk, swap_blk], axis=1).astype(BF16)
    return jnp.concatenate([wa, wc, wb], axis=1)


def _split_w_uq(w_uq):
    r = w_uq.shape[0]
    w = w_uq.reshape(r, N_HEADS_B, QK_NOPE + QK_ROPE)
    nope, rope = w[..., :QK_NOPE], w[..., QK_NOPE:]
    z_hi = jnp.zeros((r, N_HEADS_B, LANES - QK_NOPE - QK_ROPE), w_uq.dtype)
    return jnp.concatenate([nope, rope, z_hi], axis=-1).reshape(r, -1).astype(BF16)


def _split_w_ukv(w_ukv):
    r = w_ukv.shape[0]
    w = w_ukv.reshape(r, N_HEADS_B, QK_NOPE + V_HEAD)
    k_nope, v = w[..., :QK_NOPE], w[..., QK_NOPE:]
    wk = jnp.concatenate([k_nope, jnp.zeros((r, N_HEADS_B, LANES - QK_NOPE), w_ukv.dtype)], axis=-1)
    return wk.reshape(r, -1).astype(BF16), v.reshape(r, -1).T.astype(BF16)


def kernel(x, g_mix, w_in, g_q, g_kv, w_uq, w_ukv, rpb, g_out_a, g_out_b, g_out_c, w_o, g_ffn, w1, w3, w2,
           w_router, e_w1, e_w3, e_w2, g_final):
    batch, seq, d = x.shape
    n = batch * seq
    depth = g_mix.shape[0]
    rows = seq // GRID_W
    rope_tabs = _rope_tables(seq)
    ew1, ew3, ew2 = e_w1.astype(BF16), e_w3.astype(BF16), e_w2.astype(BF16)
    xf = x.reshape(n, d)
    for layer in range(depth):
        win = _split_w_in(w_in[layer])
        wq = _split_w_uq(w_uq[layer])
        wk, wvt = _split_w_ukv(w_ukv[layer])
        pa, pa4, pa16, pc, qm, km, vt = _proj_call(xf, g_mix[layer][None], win, g_q[layer][None], g_kv[layer][None],
                                                   wq, wk, wvt, rope_tabs, seq)
        o_parts, lse_parts = [], []
        for dil, view in zip(DILATIONS, (pa, pa4, pa16)):
            o, lse = _band_call(view.reshape(batch, seq // dil, dil * 3 * WIDTH_A), dil)
            o_parts.append(o.reshape(n // dil, dil * WIDTH_A))
            lse_parts.append(lse.reshape(n // dil, dil * WIDTH_A))
        ob = _mla_call(qm, km, vt, batch, seq).reshape(n, WIDTH_B)
        oc = _na_call(pc, _na_tables(rpb[layer], rows), batch, seq).reshape(n, WIDTH_C)
        xf = _mix_out_call(xf, o_parts, lse_parts, ob, oc, g_out_a[layer][None], g_out_b[layer][None],
                           g_out_c[layer][None], w_o[layer].astype(BF16))
        j = layer // 2
        if layer % 2 == 0:
            xf = _ffn_call(xf, g_ffn[layer][None], w1[j].astype(BF16), w3[j].astype(BF16), w2[j].astype(BF16))
        else:
            wr_pad = jnp.pad(w_router[j], ((0, 0), (0, LANES - N_EXPERTS)))
            xf = _moe_layer(xf, g_ffn[layer][None], wr_pad, ew1, ew3, ew2, j, g_final[None], layer == depth - 1)
    if depth % 2 == 1:
        xf = _norm_call(xf, g_final[None])
    return xf.reshape(batch, seq, d)
```

```python
import functools
import math

import numpy as np
import jax
import jax.numpy as jnp
from jax import lax
from jax.experimental import pallas as pl
from jax.experimental.pallas import tpu as pltpu

F32 = jnp.float32
BF16 = jnp.bfloat16

LANES = 128
V7X_VMEM_LIMIT_BYTES = 52 * 1024 * 1024

HEAD_DIM = 64
N_HEADS_A = 6
DILATIONS = (1, 4, 16)
BAND_HALF = 64
N_HEADS_B = 6
Q_LORA = 384
KV_LORA = 256
QK_NOPE = 64
QK_ROPE = 32
V_HEAD = 64
ROPE_BASE = 10000.0
N_HEADS_C = 4
GRID_W = 64
NA_ROWS = 8
NA_COLS = 16
WIDTH_A = N_HEADS_A * HEAD_DIM
WIDTH_B = N_HEADS_B * V_HEAD
WIDTH_C = N_HEADS_C * HEAD_DIM
N_EXPERTS = 8
TOP_K = 2
RMS_EPS = 1e-6
NEG_INF = -1e30

ROW_TILE = 512
MLA_TQ = 256
MLA_TK = 256
MLA_UNROLL = 32
MLA_LOOKAHEAD = 4
MLA_HEADS_PER_STEP = 6
MLA_DEN_ROWS = 16
MLA_Q_PRESCALE = (QK_NOPE + QK_ROPE) ** -0.5 * math.log2(math.e)
BAND_TQ = 128
BAND_TILES_PER_STEP = 4
BAND_LOOKAHEAD = 6
NA_TILE_ROWS = 2
NA_KEY_ROWS = 10
NA_TILES_PER_STEP = 4
NA_LOOKAHEAD = 16
MOE_TM = 512
MOE_TF = 1792
COMBINE_TM = 256
GATHER_UNROLL = 8


def _cparams(semantics):
    return pltpu.CompilerParams(dimension_semantics=semantics, vmem_limit_bytes=V7X_VMEM_LIMIT_BYTES)


def _rms(x, g):
    return x * lax.rsqrt(jnp.mean(x * x, axis=-1, keepdims=True) + RMS_EPS) * g


def _dot(a, b):
    return jnp.dot(a, b, preferred_element_type=F32)


def _dot_nt(a, b):
    return lax.dot_general(a, b, (((1,), (1,)), ((), ())), preferred_element_type=F32)


def _proj_kernel(x_ref, g_ref, win_ref, gq_ref, gkv_ref, wq_ref, wk_ref, wvt_ref,
                 ct_ref, st_ref, sa_ref, sb_ref, pa_ref, pa4_ref, pa16_ref, pc_ref, qm_ref, km_ref, vt_ref, pa_scr):
    tm = x_ref.shape[0]
    h = _rms(x_ref[...], g_ref[...]).astype(BF16)
    proj = _dot(h, win_ref[...])
    pa = proj[:, :3 * WIDTH_A]
    pb = proj[:, 3 * (WIDTH_A + WIDTH_C):]
    pa_ref[...] = pa.astype(BF16)
    n_chunks = pa.shape[1] // LANES
    for c in range(n_chunks):
        pa_scr[c] = pa[:, c * LANES:(c + 1) * LANES]
    for dil, view_ref in ((DILATIONS[1], pa4_ref), (DILATIONS[2], pa16_ref)):
        for r in range(dil):
            for c in range(n_chunks):
                col = r * 3 * WIDTH_A + c * LANES
                view_ref[:, col:col + LANES] = pa_scr[c, pl.ds(r, tm // dil, stride=dil), :].astype(BF16)
    pc_ref[...] = proj[:, 3 * WIDTH_A:3 * (WIDTH_A + WIDTH_C)].astype(BF16)
    hq = _rms(pb[:, :Q_LORA], gq_ref[...]).astype(BF16)
    hkv = _rms(pb[:, Q_LORA:Q_LORA + KV_LORA], gkv_ref[...]).astype(BF16)
    r1 = pb[:, Q_LORA + KV_LORA:Q_LORA + KV_LORA + LANES]
    r2 = pb[:, Q_LORA + KV_LORA + LANES:]
    ct = ct_ref[...]
    st = st_ref[...]
    sa = sa_ref[...]
    sb = sb_ref[...]
    half = QK_ROPE // 2
    qa = _dot(hq, wq_ref[...])
    kn = _dot(hkv, wk_ref[...])
    kr = r1 * ct + r2 * st
    for hd in range(N_HEADS_B):
        sl = slice(hd * LANES, (hd + 1) * LANES)
        q = qa[:, sl]
        q = q * ct + pltpu.roll(q, LANES - half, 1) * sa + pltpu.roll(q, half, 1) * sb
        qm_ref[:, sl] = (q * MLA_Q_PRESCALE).astype(BF16)
        km_ref[:, sl] = (kn[:, sl] + kr).astype(BF16)
    vt_ref[...] = _dot_nt(wvt_ref[...], hkv).astype(BF16)


def _proj_call(x, g, win, gq, gkv, wq, wk, wvt, tabs, seq):
    n, d = x.shape
    tm = ROW_TILE
    tiles_per_seq = seq // tm
    full = lambda a: pl.BlockSpec(a.shape, lambda i: (0,) * a.ndim)
    row = lambda w: pl.BlockSpec((tm, w), lambda i: (i, 0))
    tab = pl.BlockSpec((tm, LANES), lambda i: (i % tiles_per_seq, 0))
    view = lambda dil: pl.BlockSpec((tm // dil, dil * 3 * WIDTH_A), lambda i: (i, 0))
    hb = N_HEADS_B * LANES
    return pl.pallas_call(
        _proj_kernel,
        grid=(n // tm,),
        in_specs=[row(d), full(g), full(win), full(gq), full(gkv), full(wq), full(wk), full(wvt)] + [tab] * 4,
        out_specs=[row(3 * WIDTH_A)] + [view(dil) for dil in DILATIONS[1:]] + [row(3 * WIDTH_C), row(hb), row(hb),
                   pl.BlockSpec((WIDTH_B, tm), lambda i: (0, i))],
        out_shape=[jax.ShapeDtypeStruct((n, 3 * WIDTH_A), BF16)]
                  + [jax.ShapeDtypeStruct((n // dil, dil * 3 * WIDTH_A), BF16) for dil in DILATIONS[1:]]
                  + [jax.ShapeDtypeStruct((n, 3 * WIDTH_C), BF16),
                   jax.ShapeDtypeStruct((n, hb), BF16), jax.ShapeDtypeStruct((n, hb), BF16),
                   jax.ShapeDtypeStruct((WIDTH_B, n), BF16)],
        scratch_shapes=[pltpu.VMEM((3 * WIDTH_A // LANES, tm, LANES), F32)],
        compiler_params=_cparams(("parallel",)),
        name="proj",
    )(x, g, win, gq, gkv, wq, wk, wvt, *tabs)


def _run_pipeline(n_items, look, issue, consume, finish):
    if look >= n_items:
        results = [consume(i, s) for i, s in enumerate([issue(i) for i in range(n_items)])]
        for i, res in enumerate(results):
            finish(i, res)
        return
    inflight = {i: issue(i) for i in range(min(look, n_items))}
    pending = None
    for i in range(n_items):
        if i + look < n_items:
            inflight[i + look] = issue(i + look)
        if pending is not None:
            finish(*pending)
        pending = (i, consume(i, inflight.pop(i)))
    finish(*pending)


class _HeadPairs:
    def __init__(self, q_ref, k_ref, v_ref, tq, tkw):
        self.q_ref, self.k_ref, self.v_ref, self.tq, self.tkw = q_ref, k_ref, v_ref, tq, tkw
        lane = lax.broadcasted_iota(jnp.int32, (tq, LANES), 1)
        self.low = lane < HEAD_DIM
        self.ones = jnp.ones((tkw, LANES), BF16)

    @staticmethod
    def cols(h):
        return slice((h // 2) * LANES, (h // 2 + 1) * LANES)

    def scores(self, q_start, k_start, h):
        q = self.q_ref[q_start:q_start + self.tq, self.cols(h)]
        q = jnp.where(self.low if h % 2 == 0 else ~self.low, q, jnp.zeros_like(q))
        return _dot_nt(q, self.k_ref[pl.ds(k_start, self.tkw), self.cols(h)])

    def values(self, k_start, h, p):
        v = jnp.concatenate([self.v_ref[pl.ds(k_start, self.tkw), self.cols(h)], self.ones], axis=-1)
        o = _dot(p, v)
        return o[:, :LANES], o[:, LANES:]

    def merge(self, even, odd):
        return jnp.where(self.low, even, odd)


def _band_tables(dilation):
    tq, tkw = BAND_TQ, BAND_TQ + 2 * BAND_HALF
    shift = np.array([0, -BAND_HALF, -2 * BAND_HALF])
    rel = shift[:, None, None] + np.arange(tkw)[None, None, :] - np.arange(tq)[None, :, None]
    dist = np.abs(rel)
    slopes = 2.0 ** (-8.0 * np.arange(1, N_HEADS_A + 1) / N_HEADS_A)
    bias = -(slopes[None, :, None, None] * dilation) * dist[:, None].astype(np.float64)
    tab = np.where(dist[:, None] <= BAND_HALF, bias, NEG_INF)
    return jnp.asarray(tab, F32)


def _band_kernel(q_ref, k_ref, v_ref, tab_ref, o_ref, lse_ref, *, n):
    tq = BAND_TQ
    tkw = tq + 2 * BAND_HALF
    tiles = q_ref.shape[0] // tq
    last_tile = n // tq - 1
    items = [(t, h) for t in range(tiles) for h in range(N_HEADS_A)]
    info = []
    for t in range(tiles):
        i = pl.program_id(2) * tiles + t
        start = pl.multiple_of(jnp.clip(i * tq - BAND_HALF, 0, n - tkw), BAND_HALF)
        variant = jnp.minimum(i, 1) + (i == last_tile).astype(jnp.int32)
        info.append((start, variant))
    pairs = _HeadPairs(q_ref, k_ref, v_ref, tq, tkw)
    held = {}

    def issue(idx):
        t, h = items[idx]
        return pairs.scores(t * tq, info[t][0], h)

    def consume(idx, s):
        t, h = items[idx]
        s = s + tab_ref[info[t][1], h]
        m = jnp.max(s, axis=-1, keepdims=True)
        return jnp.exp(s - m).astype(BF16), m

    def finish(idx, res):
        t, h = items[idx]
        p, m = res
        out, den = pairs.values(info[t][0], h, p)
        held[h % 2] = (out / den, m + jnp.log(den))
        if h % 2 == 1:
            cols = pairs.cols(h)
            o_ref[t * tq:(t + 1) * tq, cols] = pairs.merge(held[0][0], held[1][0])
            lse_ref[t * tq:(t + 1) * tq, cols] = pairs.merge(held[0][1], held[1][1])

    _run_pipeline(len(items), BAND_LOOKAHEAD, issue, consume, finish)


def _band_call(pa_view, dilation):
    b, n, _ = pa_view.shape
    rows = BAND_TQ * BAND_TILES_PER_STEP
    w = WIDTH_A
    tab = _band_tables(dilation)
    qspec = pl.BlockSpec((None, rows, w), lambda bb, r, i: (bb, i, 3 * r))
    kspec = pl.BlockSpec((None, n, w), lambda bb, r, i: (bb, 0, 3 * r + 1))
    vspec = pl.BlockSpec((None, n, w), lambda bb, r, i: (bb, 0, 3 * r + 2))
    tspec = pl.BlockSpec(tab.shape, lambda bb, r, i: (0, 0, 0, 0))
    ospec = pl.BlockSpec((None, rows, w), lambda bb, r, i: (bb, i, r))
    shape = jax.ShapeDtypeStruct((b, n, dilation * w), F32)
    return pl.pallas_call(
        functools.partial(_band_kernel, n=n),
        grid=(b, dilation, n // rows),
        in_specs=[qspec, kspec, vspec, tspec],
        out_specs=[ospec, ospec],
        out_shape=[shape, shape],
        compiler_params=_cparams(("parallel", "parallel", "parallel")),
        name=f"band_d{dilation}",
    )(pa_view, pa_view, pa_view, tab)


def _mla_kernel(q_ref, k_ref, vt_ref, o_ref):
    tq = q_ref.shape[0]
    seq = k_ref.shape[0]
    tk = MLA_TK
    nh = MLA_HEADS_PER_STEP
    n_chunks = seq // tk
    items = [(u, h) for u in range(MLA_UNROLL) for h in range(nh)]
    look = MLA_LOOKAHEAD
    ones_rows = jnp.ones((MLA_DEN_ROWS, tk), BF16)

    def key_slice(chunk):
        return pl.ds(pl.multiple_of(chunk * tk, tk), tk)

    def score_matmul(chunk, h):
        k = k_ref[key_slice(chunk), h * LANES:(h + 1) * LANES]
        return _dot_nt(k, q_ref[:, h * LANES:(h + 1) * LANES])

    def value_matmul(chunk, h, p):
        vt = jnp.concatenate([vt_ref[h * V_HEAD:(h + 1) * V_HEAD, key_slice(chunk)], ones_rows], axis=0)
        return _dot(vt, p)

    def body(j, carry):
        state = list(carry[:2 * nh])
        scores = dict(zip(items[:look], carry[2 * nh:2 * nh + look]))
        pend_p, pend_alpha = carry[2 * nh + look:]
        pending = (jnp.maximum(j * MLA_UNROLL - 1, 0), nh - 1, pend_p, pend_alpha)
        ahead = []
        for idx, (u, h) in enumerate(items):
            la = idx + look
            if la < len(items):
                lu, lh = items[la]
                scores[lu, lh] = score_matmul(j * MLA_UNROLL + lu, lh)
            else:
                lu, lh = items[la - len(items)]
                ahead.append(score_matmul(jnp.minimum((j + 1) * MLA_UNROLL + lu, n_chunks - 1), lh))
            pc, ph, pp, pa = pending
            state[2 * ph + 1] = pa * state[2 * ph + 1] + value_matmul(pc, ph, pp)
            s = scores.pop((u, h))
            m_new = jnp.maximum(state[2 * h], jnp.max(s, axis=0, keepdims=True))
            alpha = jnp.exp2(state[2 * h] - m_new)
            state[2 * h] = m_new
            pending = (j * MLA_UNROLL + u, h, jnp.exp2(s - m_new).astype(BF16), alpha)
        return tuple(state) + tuple(ahead) + (pending[2], pending[3])

    init = (jnp.full((1, tq), NEG_INF, F32), jnp.zeros((V_HEAD + MLA_DEN_ROWS, tq), F32)) * nh
    init += tuple(score_matmul(u, h) for u, h in items[:look])
    init += (jnp.zeros((tk, tq), BF16), jnp.ones((1, tq), F32))
    res = lax.fori_loop(0, n_chunks // MLA_UNROLL, body, init)
    accs = [res[2 * h + 1] for h in range(nh)]
    accs[nh - 1] = res[-1] * accs[nh - 1] + value_matmul(n_chunks - 1, nh - 1, res[-2])
    out_t = jnp.concatenate([a[:V_HEAD] / a[V_HEAD:V_HEAD + 1] for a in accs], axis=0)
    o_ref[...] = out_t.T


def _mla_call(qm, km, vt, batch, seq):
    tq = MLA_TQ
    nh = MLA_HEADS_PER_STEP
    qspec = pl.BlockSpec((None, tq, nh * LANES), lambda b, g, i: (b, i, g))
    kspec = pl.BlockSpec((None, seq, nh * LANES), lambda b, g, i: (b, 0, g))
    vspec = pl.BlockSpec((nh * V_HEAD, seq), lambda b, g, i: (g, b))
    ospec = pl.BlockSpec((None, tq, nh * V_HEAD), lambda b, g, i: (b, i, g))
    return pl.pallas_call(
        _mla_kernel,
        grid=(batch, N_HEADS_B // nh, seq // tq),
        in_specs=[qspec, kspec, vspec],
        out_specs=ospec,
        out_shape=jax.ShapeDtypeStruct((batch, seq, WIDTH_B), F32),
        compiler_params=_cparams(("parallel", "parallel", "parallel")),
        name="mla",
    )(qm.reshape(batch, seq, -1), km.reshape(batch, seq, -1), vt)


def _na_variant(i, n_tiles):
    return jnp.minimum(i, 2) + jnp.maximum(i - (n_tiles - 3), 0)


def _na_kernel(q_ref, k_ref, v_ref, tab_ref, o_ref, *, rows):
    tq = NA_TILE_ROWS * GRID_W
    tkw = NA_KEY_ROWS * GRID_W
    tiles = q_ref.shape[0] // tq
    n_tiles = rows // NA_TILE_ROWS
    items = [(t, h) for t in range(tiles) for h in range(N_HEADS_C)]
    info = []
    for t in range(tiles):
        i = pl.program_id(1) * tiles + t
        base = jnp.clip(i * NA_TILE_ROWS - NA_ROWS // 2, 0, rows - NA_KEY_ROWS)
        info.append((pl.multiple_of(base * GRID_W, GRID_W), _na_variant(i, n_tiles)))
    pairs = _HeadPairs(q_ref, k_ref, v_ref, tq, tkw)
    held = {}

    def issue(idx):
        t, h = items[idx]
        return pairs.scores(t * tq, info[t][0], h)

    def consume(idx, s):
        t, h = items[idx]
        s = s + tab_ref[info[t][1], h]
        return jnp.exp(s - jnp.max(s, axis=-1, keepdims=True)).astype(BF16)

    def finish(idx, p):
        t, h = items[idx]
        out, den = pairs.values(info[t][0], h, p)
        held[h % 2] = out / den
        if h % 2 == 1:
            o_ref[t * tq:(t + 1) * tq, pairs.cols(h)] = pairs.merge(held[0], held[1])

    _run_pipeline(len(items), NA_LOOKAHEAD, issue, consume, finish)


def _na_tables(rpb, rows):
    r0 = np.array([0, 2, 4, rows - 4, rows - 2])
    base = np.clip(r0 - NA_ROWS // 2, 0, rows - NA_KEY_ROWS)
    r = r0[:, None] + np.arange(NA_TILE_ROWS)[None, :]
    row_start = np.clip(r - NA_ROWS // 2, 0, rows - NA_ROWS)
    krow = base[:, None] + np.arange(NA_KEY_ROWS)[None, :]
    drow = krow[:, None, :] - r[:, :, None]
    row_ok = (krow[:, None, :] >= row_start[:, :, None]) & (krow[:, None, :] < row_start[:, :, None] + NA_ROWS)
    c = np.arange(GRID_W)
    win_start = np.clip(c - NA_COLS // 2, 0, GRID_W - NA_COLS)
    col_ok = (c[None, :] >= win_start[:, None]) & (c[None, :] < win_start[:, None] + NA_COLS)
    dcol = np.clip(c[None, :] - c[:, None], -(NA_COLS - 1), NA_COLS - 1)
    ok = row_ok[:, :, None, :, None] & col_ok[None, None, :, None, :]
    di = np.clip(drow, -(NA_ROWS - 1), NA_ROWS - 1) + (NA_ROWS - 1)
    pick_col = (dcol[:, :, None] + NA_COLS - 1 == np.arange(2 * NA_COLS - 1)).astype(np.float32)
    pick_row = (di[..., None] == np.arange(2 * NA_ROWS - 1)).astype(np.float32)
    hi = lax.Precision.HIGHEST
    toeplitz = jnp.einsum("hab,cjb->hacj", rpb.astype(F32), pick_col, precision=hi)
    bias = jnp.einsum("vqka,hacj->vhqckj", pick_row, toeplitz, precision=hi)
    tab = jnp.where(jnp.asarray(ok)[:, None], bias, NEG_INF)
    return tab.reshape(5, N_HEADS_C, NA_TILE_ROWS * GRID_W, NA_KEY_ROWS * GRID_W)


def _na_call(pc, tab, batch, seq):
    rows = seq // GRID_W
    tq = NA_TILE_ROWS * GRID_W * NA_TILES_PER_STEP
    w = WIDTH_C
    qspec = pl.BlockSpec((None, tq, w), lambda b, i: (b, i, 0))
    kspec = pl.BlockSpec((None, seq, w), lambda b, i: (b, 0, 1))
    vspec = pl.BlockSpec((None, seq, w), lambda b, i: (b, 0, 2))
    tspec = pl.BlockSpec(tab.shape, lambda b, i: (0, 0, 0, 0))
    ospec = pl.BlockSpec((None, tq, w), lambda b, i: (b, i, 0))
    pc3 = pc.reshape(batch, seq, 3 * w)
    return pl.pallas_call(
        functools.partial(_na_kernel, rows=rows),
        grid=(batch, seq // tq),
        in_specs=[qspec, kspec, vspec, tspec],
        out_specs=ospec,
        out_shape=jax.ShapeDtypeStruct((batch, seq, w), F32),
        compiler_params=_cparams(("parallel", "parallel")),
        name="natten",
    )(pc3, pc3, pc3, tab)


def _mix_out_kernel(x_ref, o1_ref, o2_ref, o3_ref, l1_ref, l2_ref, l3_ref, ob_ref, oc_ref,
                    ga_ref, gb_ref, gc_ref, wo_ref, out_ref, o2_scr, o3_scr, l2_scr, l3_scr):
    tm = x_ref.shape[0]
    n_chunks = WIDTH_A // LANES

    def natural(view_ref, scr, dil):
        for r in range(dil):
            for c in range(n_chunks):
                col = r * WIDTH_A + c * LANES
                scr[c, pl.ds(r, tm // dil, stride=dil), :] = view_ref[:, col:col + LANES]
        return jnp.concatenate([scr[c] for c in range(n_chunks)], axis=-1)

    l1, o1 = l1_ref[...], o1_ref[...]
    l2, o2 = natural(l2_ref, l2_scr, DILATIONS[1]), natural(o2_ref, o2_scr, DILATIONS[1])
    l3, o3 = natural(l3_ref, l3_scr, DILATIONS[2]), natural(o3_ref, o3_scr, DILATIONS[2])
    mx = jnp.maximum(jnp.maximum(l1, l2), l3)
    w1, w2, w3 = jnp.exp(l1 - mx), jnp.exp(l2 - mx), jnp.exp(l3 - mx)
    oa = (w1 * o1 + w2 * o2 + w3 * o3) / (w1 + w2 + w3)
    ya = _rms(oa, ga_ref[...]).astype(BF16)
    yb = _rms(ob_ref[...], gb_ref[...]).astype(BF16)
    yc = _rms(oc_ref[...], gc_ref[...]).astype(BF16)
    y = _dot(ya, wo_ref[:WIDTH_A, :])
    y = y + _dot(yb, wo_ref[WIDTH_A:WIDTH_A + WIDTH_B, :])
    y = y + _dot(yc, wo_ref[WIDTH_A + WIDTH_B:, :])
    out_ref[...] = x_ref[...] + y


def _mix_out_call(x, o_parts, lse_parts, ob, oc, ga, gb, gc, wo):
    n, d = x.shape
    tm = ROW_TILE
    full = lambda a: pl.BlockSpec(a.shape, lambda i: (0,) * a.ndim)
    row = lambda w: pl.BlockSpec((tm, w), lambda i: (i, 0))
    view = lambda dil: pl.BlockSpec((tm // dil, dil * WIDTH_A), lambda i: (i, 0))
    return pl.pallas_call(
        _mix_out_kernel,
        grid=(n // tm,),
        in_specs=[row(d)] + [view(dil) for dil in DILATIONS] * 2
                 + [row(WIDTH_B), row(WIDTH_C), full(ga), full(gb), full(gc), full(wo)],
        out_specs=row(d),
        out_shape=jax.ShapeDtypeStruct((n, d), F32),
        scratch_shapes=[pltpu.VMEM((WIDTH_A // LANES, tm, LANES), F32)] * 4,
        compiler_params=_cparams(("parallel",)),
        name="mix_out",
    )(x, *o_parts, *lse_parts, ob, oc, ga, gb, gc, wo)


def _silu(u):
    return u * (1.0 / (1.0 + jnp.exp(-u)))


def _ffn_kernel(x_ref, g_ref, w1_ref, w3_ref, w2_ref, out_ref):
    x = x_ref[...]
    h = _rms(x, g_ref[...]).astype(BF16)
    a = (_silu(_dot(h, w1_ref[...])) * _dot(h, w3_ref[...])).astype(BF16)
    out_ref[...] = x + _dot(a, w2_ref[...])


def _ffn_call(x, g, w1, w3, w2):
    n, d = x.shape
    tm = ROW_TILE
    resident = lambda a: pl.BlockSpec(a.shape, lambda i: (0, 0), pipeline_mode=pl.Buffered(1))
    return pl.pallas_call(
        _ffn_kernel,
        grid=(n // tm,),
        in_specs=[pl.BlockSpec((tm, d), lambda i: (i, 0)), pl.BlockSpec((1, d), lambda i: (0, 0)),
                  resident(w1), resident(w3), resident(w2)],
        out_specs=pl.BlockSpec((tm, d), lambda i: (i, 0)),
        out_shape=jax.ShapeDtypeStruct((n, d), F32),
        compiler_params=_cparams(("parallel",)),
        name="ffn",
    )(x, g, w1, w3, w2)


def _router_kernel(x_ref, g_ref, wr_ref, h_ref, e_ref, gate_ref):
    h = _rms(x_ref[...], g_ref[...])
    _to_token_tiles(h_ref, h)
    logits = jnp.dot(h, wr_ref[...], precision=lax.Precision.HIGHEST, preferred_element_type=F32)
    lane = lax.broadcasted_iota(jnp.int32, logits.shape, 1).astype(F32)
    logits = jnp.where(lane < N_EXPERTS, logits, -jnp.inf)
    m1 = jnp.max(logits, axis=-1, keepdims=True)
    i1 = jnp.min(jnp.where(logits == m1, lane, float(LANES)), axis=-1, keepdims=True)
    rest = jnp.where(lane == i1, -jnp.inf, logits)
    m2 = jnp.max(rest, axis=-1, keepdims=True)
    i2 = jnp.min(jnp.where(rest == m2, lane, float(LANES)), axis=-1, keepdims=True)
    e = jnp.exp(m2 - m1)
    den = 1.0 + e
    e_ref[...] = jnp.where(lane == 0.0, i1, jnp.where(lane == 1.0, i2, 0.0)).astype(jnp.int32)
    gate_ref[...] = jnp.where(lane == 0.0, 1.0 / den, jnp.where(lane == 1.0, e / den, 0.0))


def _router_call(x, g, wr_pad):
    n, d = x.shape
    tm = ROW_TILE
    row = lambda w: pl.BlockSpec((tm, w), lambda i: (i, 0))
    full = lambda a: pl.BlockSpec(a.shape, lambda i: (0,) * a.ndim)
    return pl.pallas_call(
        _router_kernel,
        grid=(n // tm,),
        in_specs=[row(d), full(g), full(wr_pad)],
        out_specs=[pl.BlockSpec((tm * TOKEN_ROWS, LANES), lambda i: (i, 0)), row(LANES), row(LANES)],
        out_shape=[jax.ShapeDtypeStruct((n * TOKEN_ROWS, LANES), F32), jax.ShapeDtypeStruct((n, LANES), jnp.int32),
                   jax.ShapeDtypeStruct((n, LANES), F32)],
        compiler_params=_cparams(("parallel",)),
        name="router",
    )(x, g, wr_pad)


TOKEN_ROWS = 8


def _to_token_tiles(dst_ref, x):
    tm, d = x.shape
    assert d == TOKEN_ROWS * LANES
    for j in range(TOKEN_ROWS):
        dst_ref[pl.ds(j, tm, stride=TOKEN_ROWS), :] = x[:, j * LANES:(j + 1) * LANES]


def _from_token_tiles(src_ref, tm):
    return jnp.concatenate([src_ref[pl.ds(j, tm, stride=TOKEN_ROWS), :] for j in range(TOKEN_ROWS)], axis=-1)


def _row_copy(src_hbm, row, dst_ref, r, sem):
    src = src_hbm.at[pl.ds(pl.multiple_of(row * TOKEN_ROWS, TOKEN_ROWS), TOKEN_ROWS)]
    return pltpu.make_async_copy(src, dst_ref.at[pl.ds(pl.multiple_of(r * TOKEN_ROWS, TOKEN_ROWS), TOKEN_ROWS)], sem)


def _start_row_gather(idx_ref, base, src_hbm, dst_ref, sem, count, priorities=1):
    def body(g, c):
        for p in range(priorities):
            r = g * priorities + p
            _row_copy(src_hbm, idx_ref[base + r], dst_ref, r, sem).start(priority=p)
        return c
    lax.fori_loop(0, count // priorities, body, 0, unroll=GATHER_UNROLL // priorities)


def _wait_row_gather(src_hbm, dst_ref, sem, count):
    pltpu.make_async_copy(src_hbm.at[pl.ds(0, count * TOKEN_ROWS)], dst_ref, sem).wait()


def _moe_ffn_kernel(tile_e_ref, tile_ok_ref, row_tok_ref, h_hbm, w1_ref, w3_ref, w2_ref, out_ref,
                    xg_ref, acc_ref, sem):
    i = pl.program_id(0)
    f = pl.program_id(1)
    n_tiles = pl.num_programs(0)
    tm = MOE_TM
    slot = i % 2

    @pl.when(f == 0)
    def _():
        @pl.when(i == 0)
        def _():
            acc_ref[...] = jnp.zeros_like(acc_ref)
            _start_row_gather(row_tok_ref, 0, h_hbm, xg_ref.at[0], sem.at[0], tm)

        @pl.when(tile_ok_ref[i] != 0)
        def _():
            _wait_row_gather(h_hbm, xg_ref.at[slot], sem.at[slot], tm)

        @pl.when(jnp.logical_and(i + 1 < n_tiles, tile_ok_ref[jnp.minimum(i + 1, n_tiles - 1)] != 0))
        def _():
            _start_row_gather(row_tok_ref, (i + 1) * tm, h_hbm, xg_ref.at[1 - slot], sem.at[1 - slot], tm)

    @pl.when(tile_ok_ref[i] != 0)
    def _():
        h = _from_token_tiles(xg_ref.at[slot], tm).astype(BF16)
        a = (_silu(_dot(h, w1_ref[...])) * _dot(h, w3_ref[...])).astype(BF16)
        total = _dot(a, w2_ref[...]) + jnp.where(f > 0, acc_ref[...], 0.0)
        acc_ref[...] = total
        _to_token_tiles(out_ref, total)

    @pl.when(tile_ok_ref[i] == 0)
    def _():
        out_ref[...] = jnp.zeros_like(out_ref)


def _moe_ffn_call(tile_e, tile_ok, row_tok, h, w1, w3, w2, layer):
    d = w1.shape[2]
    n_tiles = tile_e.shape[0]
    ff = w1.shape[3]
    tm, tf = MOE_TM, MOE_TF

    def ff_block(i, f, tile_ok):
        return jnp.where(tile_ok[i] != 0, f, ff // tf - 1)

    grid_spec = pltpu.PrefetchScalarGridSpec(
        num_scalar_prefetch=3,
        grid=(n_tiles, ff // tf),
        in_specs=[pl.BlockSpec(memory_space=pl.ANY),
                  pl.BlockSpec((None, None, d, tf), lambda i, f, te, tv, rt: (layer, te[i], 0, ff_block(i, f, tv))),
                  pl.BlockSpec((None, None, d, tf), lambda i, f, te, tv, rt: (layer, te[i], 0, ff_block(i, f, tv))),
                  pl.BlockSpec((None, None, tf, d), lambda i, f, te, tv, rt: (layer, te[i], ff_block(i, f, tv), 0))],
        out_specs=pl.BlockSpec((tm * TOKEN_ROWS, LANES), lambda i, f, te, tv, rt: (i, 0)),
        scratch_shapes=[pltpu.VMEM((2, tm * TOKEN_ROWS, LANES), F32), pltpu.VMEM((tm, d), F32),
                        pltpu.SemaphoreType.DMA((2,))],
    )
    return pl.pallas_call(
        _moe_ffn_kernel,
        grid_spec=grid_spec,
        out_shape=jax.ShapeDtypeStruct((n_tiles * tm * TOKEN_ROWS, LANES), F32),
        compiler_params=_cparams(("arbitrary", "arbitrary")),
        name="moe_ffn",
    )(tile_e, tile_ok, row_tok, h, w1, w3, w2)


def _combine_kernel(d0_ref, d1_ref, y_hbm, x_ref, gate_ref, gain_ref, out_ref, b0_ref, b1_ref, sem, *, final_norm):
    i = pl.program_id(0)
    n_tiles = pl.num_programs(0)
    tm = COMBINE_TM
    slot = i % 2

    def start(tile, s):
        _start_row_gather(d0_ref, tile * tm, y_hbm, b0_ref.at[s], sem.at[0, s], tm, priorities=2)
        _start_row_gather(d1_ref, tile * tm, y_hbm, b1_ref.at[s], sem.at[1, s], tm, priorities=2)

    @pl.when(i == 0)
    def _():
        start(0, 0)

    _wait_row_gather(y_hbm, b0_ref.at[slot], sem.at[0, slot], tm)
    _wait_row_gather(y_hbm, b1_ref.at[slot], sem.at[1, slot], tm)

    @pl.when(i + 1 < n_tiles)
    def _():
        start(i + 1, 1 - slot)

    gates = gate_ref[...]
    y0 = _from_token_tiles(b0_ref.at[slot], tm)
    y1 = _from_token_tiles(b1_ref.at[slot], tm)
    out = x_ref[...] + (gates[:, 0:1] * y0 + gates[:, 1:2] * y1)
    out_ref[...] = _rms(out, gain_ref[...]) if final_norm else out


def _combine_call(d0, d1, y, x, gates, gain, final_norm):
    n, d = x.shape
    tm = COMBINE_TM
    grid_spec = pltpu.PrefetchScalarGridSpec(
        num_scalar_prefetch=2,
        grid=(n // tm,),
        in_specs=[pl.BlockSpec(memory_space=pl.ANY),
                  pl.BlockSpec((tm, d), lambda i, a, b: (i, 0)),
                  pl.BlockSpec((tm, LANES), lambda i, a, b: (i, 0)),
                  pl.BlockSpec((1, d), lambda i, a, b: (0, 0))],
        out_specs=pl.BlockSpec((tm, d), lambda i, a, b: (i, 0)),
        scratch_shapes=[pltpu.VMEM((2, tm * TOKEN_ROWS, LANES), F32)] * 2 + [pltpu.SemaphoreType.DMA((2, 2))],
    )
    return pl.pallas_call(
        functools.partial(_combine_kernel, final_norm=final_norm),
        grid_spec=grid_spec,
        out_shape=jax.ShapeDtypeStruct((n, d), F32),
        compiler_params=_cparams(("arbitrary",)),
        name="moe_combine",
    )(d0, d1, y, x, gates, gain)


def _moe_plan(top_e, n_tiles):
    n_assign = top_e.shape[0] * TOP_K
    flat_e = top_e.reshape(n_assign)
    onehot = (flat_e[:, None] == jnp.arange(N_EXPERTS, dtype=jnp.int32)[None, :]).astype(jnp.int32)
    csum = jnp.cumsum(onehot, axis=0)
    rank = jnp.take_along_axis(csum, flat_e[:, None], axis=1)[:, 0] - 1
    counts = csum[-1]
    padded = (counts + MOE_TM - 1) // MOE_TM * MOE_TM
    pend = jnp.cumsum(padded)
    pstart = pend - padded
    dest = pstart[flat_e] + rank
    tile_start = jnp.arange(n_tiles, dtype=jnp.int32) * MOE_TM
    tile_e = jnp.minimum(jnp.searchsorted(pend, tile_start, side="right"), N_EXPERTS - 1).astype(jnp.int32)
    tile_ok = (tile_start < pend[-1]).astype(jnp.int32)
    order = jnp.argsort(flat_e, stable=True).astype(jnp.int32)
    row_e = jnp.repeat(tile_e, MOE_TM)
    r = jnp.arange(n_tiles * MOE_TM, dtype=jnp.int32) - pstart[row_e]
    src = jnp.clip((jnp.cumsum(counts) - counts)[row_e] + r, 0, n_assign - 1)
    row_tok = jnp.where(r < counts[row_e], order[src] // TOP_K, 0).astype(jnp.int32)
    dest = dest.reshape(-1, TOP_K).astype(jnp.int32)
    return tile_e, tile_ok, row_tok, dest[:, 0], dest[:, 1]


def _moe_layer(x, g, wr_pad, w1, w3, w2, layer, gain, final_norm):
    n = x.shape[0]
    h, top_e, gates = _router_call(x, g, wr_pad)
    n_tiles = -(-(n * TOP_K + N_EXPERTS * (MOE_TM - 1)) // MOE_TM)
    tile_e, tile_ok, row_tok, d0, d1 = _moe_plan(top_e[:, :TOP_K], n_tiles)
    y = _moe_ffn_call(tile_e, tile_ok, row_tok, h, w1, w3, w2, layer)
    return _combine_call(d0, d1, y, x, gates, gain, final_norm)


def _norm_kernel(x_ref, g_ref, o_ref):
    o_ref[...] = _rms(x_ref[...], g_ref[...])


def _norm_call(x, g):
    n, d = x.shape
    tm = ROW_TILE
    return pl.pallas_call(
        _norm_kernel,
        grid=(n // tm,),
        in_specs=[pl.BlockSpec((tm, d), lambda i: (i, 0)), pl.BlockSpec((1, d), lambda i: (0, 0))],
        out_specs=pl.BlockSpec((tm, d), lambda i: (i, 0)),
        out_shape=jax.ShapeDtypeStruct((n, d), F32),
        compiler_params=_cparams(("parallel",)),
        name="final_norm",
    )(x, g)


def _rope_tables(seq):
    inv = ROPE_BASE ** (-jnp.arange(0, QK_ROPE, 2, dtype=F32) / QK_ROPE)
    ang = jnp.arange(seq)[:, None].astype(F32) * inv[None, :]
    cos, sin = jnp.cos(ang), jnp.sin(ang)
    pad = LANES - QK_NOPE - QK_ROPE
    zeros = lambda w: jnp.zeros((seq, w), F32)
    ctab = jnp.concatenate([jnp.ones((seq, QK_NOPE), F32), cos, cos, zeros(pad)], axis=1)
    stab = jnp.concatenate([zeros(QK_NOPE), sin, sin, zeros(pad)], axis=1)
    sa = jnp.concatenate([zeros(QK_NOPE), -sin, zeros(QK_ROPE // 2 + pad)], axis=1)
    sb = jnp.concatenate([zeros(QK_NOPE + QK_ROPE // 2), sin, zeros(pad)], axis=1)
    return ctab, stab, sa, sb


def _split_w_in(w_in):
    d = w_in.shape[0]
    bounds = np.cumsum([WIDTH_A, WIDTH_A, WIDTH_A, Q_LORA, KV_LORA, QK_ROPE, WIDTH_C, WIDTH_C])
    qa, ka, va, cq, ckv, kr, qc, kc, vc = jnp.split(w_in, bounds.tolist(), axis=1)
    half = QK_ROPE // 2
    z_lo = jnp.zeros((d, QK_NOPE), w_in.dtype)
    z_hi = jnp.zeros((d, LANES - QK_NOPE - QK_ROPE), w_in.dtype)
    rope_blk = jnp.concatenate([z_lo, kr, z_hi], axis=1)
    swap_blk = jnp.concatenate([z_lo, -kr[:, half:], kr[:, :half], z_hi], axis=1)
    score_scale = HEAD_DIM ** -0.5
    wa = jnp.concatenate([qa * score_scale, ka, va], axis=1).astype(BF16)
    wc = jnp.concatenate([qc * score_scale, kc, vc], axis=1).astype(BF16)
    wb = jnp.concatenate([cq, ckv, rope_blk, swap_blk], axis=1).astype(BF16)
    return jnp.concatenate([wa, wc, wb], axis=1)


def _split_w_uq(w_uq):
    r = w_uq.shape[0]
    w = w_uq.reshape(r, N_HEADS_B, QK_NOPE + QK_ROPE)
    nope, rope = w[..., :QK_NOPE], w[..., QK_NOPE:]
    z_hi = jnp.zeros((r, N_HEADS_B, LANES - QK_NOPE - QK_ROPE), w_uq.dtype)
    return jnp.concatenate([nope, rope, z_hi], axis=-1).reshape(r, -1).astype(BF16)


def _split_w_ukv(w_ukv):
    r = w_ukv.shape[0]
    w = w_ukv.reshape(r, N_HEADS_B, QK_NOPE + V_HEAD)
    k_nope, v = w[..., :QK_NOPE], w[..., QK_NOPE:]
    wk = jnp.concatenate([k_nope, jnp.zeros((r, N_HEADS_B, LANES - QK_NOPE), w_ukv.dtype)], axis=-1)
    return wk.reshape(r, -1).astype(BF16), v.reshape(r, -1).T.astype(BF16)


def kernel(x, g_mix, w_in, g_q, g_kv, w_uq, w_ukv, rpb, g_out_a, g_out_b, g_out_c, w_o, g_ffn, w1, w3, w2,
           w_router, e_w1, e_w3, e_w2, g_final):
    batch, seq, d = x.shape
    n = batch * seq
    depth = g_mix.shape[0]
    rows = seq // GRID_W
    rope_tabs = _rope_tables(seq)
    ew1, ew3, ew2 = e_w1.astype(BF16), e_w3.astype(BF16), e_w2.astype(BF16)
    xf = x.reshape(n, d)
    for layer in range(depth):
        win = _split_w_in(w_in[layer])
        wq = _split_w_uq(w_uq[layer])
        wk, wvt = _split_w_ukv(w_ukv[layer])
        pa, pa4, pa16, pc, qm, km, vt = _proj_call(xf, g_mix[layer][None], win, g_q[layer][None], g_kv[layer][None],
                                                   wq, wk, wvt, rope_tabs, seq)
        o_parts, lse_parts = [], []
        for dil, view in zip(DILATIONS, (pa, pa4, pa16)):
            o, lse = _band_call(view.reshape(batch, seq // dil, dil * 3 * WIDTH_A), dil)
            o_parts.append(o.reshape(n // dil, dil * WIDTH_A))
            lse_parts.append(lse.reshape(n // dil, dil * WIDTH_A))
        ob = _mla_call(qm, km, vt, batch, seq).reshape(n, WIDTH_B)
        oc = _na_call(pc, _na_tables(rpb[layer], rows), batch, seq).reshape(n, WIDTH_C)
        xf = _mix_out_call(xf, o_parts, lse_parts, ob, oc, g_out_a[layer][None], g_out_b[layer][None],
                           g_out_c[layer][None], w_o[layer].astype(BF16))
        j = layer // 2
        if layer % 2 == 0:
            xf = _ffn_call(xf, g_ffn[layer][None], w1[j].astype(BF16), w3[j].astype(BF16), w2[j].astype(BF16))
        else:
            wr_pad = jnp.pad(w_router[j], ((0, 0), (0, LANES - N_EXPERTS)))
            xf = _moe_layer(xf, g_ffn[layer][None], wr_pad, ew1, ew3, ew2, j, g_final[None], layer == depth - 1)
    if depth % 2 == 1:
        xf = _norm_call(xf, g_final[None])
    return xf.reshape(batch, seq, d)
```

```python
import functools
import math

import numpy as np
import jax
import jax.numpy as jnp
from jax import lax
from jax.experimental import pallas as pl
from jax.experimental.pallas import tpu as pltpu

F32 = jnp.float32
BF16 = jnp.bfloat16

LANES = 128
V7X_VMEM_LIMIT_BYTES = 52 * 1024 * 1024

HEAD_DIM = 64
N_HEADS_A = 6
DILATIONS = (1, 4, 16)
BAND_HALF = 64
N_HEADS_B = 6
Q_LORA = 384
KV_LORA = 256
QK_NOPE = 64
QK_ROPE = 32
V_HEAD = 64
ROPE_BASE = 10000.0
N_HEADS_C = 4
GRID_W = 64
NA_ROWS = 8
NA_COLS = 16
WIDTH_A = N_HEADS_A * HEAD_DIM
WIDTH_B = N_HEADS_B * V_HEAD
WIDTH_C = N_HEADS_C * HEAD_DIM
N_EXPERTS = 8
TOP_K = 2
RMS_EPS = 1e-6
NEG_INF = -1e30

ROW_TILE = 512
MLA_TQ = 256
MLA_TK = 256
MLA_UNROLL = 32
MLA_LOOKAHEAD = 4
MLA_HEADS_PER_STEP = 6
MLA_DEN_ROWS = 16
MLA_Q_PRESCALE = (QK_NOPE + QK_ROPE) ** -0.5 * math.log2(math.e)
BAND_TQ = 128
BAND_TILES_PER_STEP = 4
BAND_LOOKAHEAD = 6
NA_TILE_ROWS = 2
NA_KEY_ROWS = 10
NA_TILES_PER_STEP = 4
NA_LOOKAHEAD = 16
MOE_TM = 512
MOE_TF = 1792
COMBINE_TM = 256
GATHER_UNROLL = 8


def _cparams(semantics):
    return pltpu.CompilerParams(dimension_semantics=semantics, vmem_limit_bytes=V7X_VMEM_LIMIT_BYTES)


def _rms(x, g):
    return x * lax.rsqrt(jnp.mean(x * x, axis=-1, keepdims=True) + RMS_EPS) * g


def _dot(a, b):
    return jnp.dot(a, b, preferred_element_type=F32)


def _dot_nt(a, b):
    return lax.dot_general(a, b, (((1,), (1,)), ((), ())), preferred_element_type=F32)


def _proj_kernel(x_ref, g_ref, win_ref, gq_ref, gkv_ref, wq_ref, wk_ref, wvt_ref,
                 ct_ref, st_ref, sa_ref, sb_ref, pa_ref, pa4_ref, pa16_ref, pc_ref, qm_ref, km_ref, vt_ref, pa_scr):
    tm = x_ref.shape[0]
    h = _rms(x_ref[...], g_ref[...]).astype(BF16)
    proj = _dot(h, win_ref[...])
    pa = proj[:, :3 * WIDTH_A]
    pb = proj[:, 3 * (WIDTH_A + WIDTH_C):]
    pa_ref[...] = pa.astype(BF16)
    n_chunks = pa.shape[1] // LANES
    for c in range(n_chunks):
        pa_scr[c] = pa[:, c * LANES:(c + 1) * LANES]
    for dil, view_ref in ((DILATIONS[1], pa4_ref), (DILATIONS[2], pa16_ref)):
        for r in range(dil):
            for c in range(n_chunks):
                col = r * 3 * WIDTH_A + c * LANES
                view_ref[:, col:col + LANES] = pa_scr[c, pl.ds(r, tm // dil, stride=dil), :].astype(BF16)
    pc_ref[...] = proj[:, 3 * WIDTH_A:3 * (WIDTH_A + WIDTH_C)].astype(BF16)
    hq = _rms(pb[:, :Q_LORA], gq_ref[...]).astype(BF16)
    hkv = _rms(pb[:, Q_LORA:Q_LORA + KV_LORA], gkv_ref[...]).astype(BF16)
    r1 = pb[:, Q_LORA + KV_LORA:Q_LORA + KV_LORA + LANES]
    r2 = pb[:, Q_LORA + KV_LORA + LANES:]
    ct = ct_ref[...]
    st = st_ref[...]
    sa = sa_ref[...]
    sb = sb_ref[...]
    half = QK_ROPE // 2
    qa = _dot(hq, wq_ref[...])
    kn = _dot(hkv, wk_ref[...])
    kr = r1 * ct + r2 * st
    for hd in range(N_HEADS_B):
        sl = slice(hd * LANES, (hd + 1) * LANES)
        q = qa[:, sl]
        q = q * ct + pltpu.roll(q, LANES - half, 1) * sa + pltpu.roll(q, half, 1) * sb
        qm_ref[:, sl] = (q * MLA_Q_PRESCALE).astype(BF16)
        km_ref[:, sl] = (kn[:, sl] + kr).astype(BF16)
    vt_ref[...] = _dot_nt(wvt_ref[...], hkv).astype(BF16)


def _proj_call(x, g, win, gq, gkv, wq, wk, wvt, tabs, seq):
    n, d = x.shape
    tm = ROW_TILE
    tiles_per_seq = seq // tm
    full = lambda a: pl.BlockSpec(a.shape, lambda i: (0,) * a.ndim)
    row = lambda w: pl.BlockSpec((tm, w), lambda i: (i, 0))
    tab = pl.BlockSpec((tm, LANES), lambda i: (i % tiles_per_seq, 0))
    view = lambda dil: pl.BlockSpec((tm // dil, dil * 3 * WIDTH_A), lambda i: (i, 0))
    hb = N_HEADS_B * LANES
    return pl.pallas_call(
        _proj_kernel,
        grid=(n // tm,),
        in_specs=[row(d), full(g), full(win), full(gq), full(gkv), full(wq), full(wk), full(wvt)] + [tab] * 4,
        out_specs=[row(3 * WIDTH_A)] + [view(dil) for dil in DILATIONS[1:]] + [row(3 * WIDTH_C), row(hb), row(hb),
                   pl.BlockSpec((WIDTH_B, tm), lambda i: (0, i))],
        out_shape=[jax.ShapeDtypeStruct((n, 3 * WIDTH_A), BF16)]
                  + [jax.ShapeDtypeStruct((n // dil, dil * 3 * WIDTH_A), BF16) for dil in DILATIONS[1:]]
                  + [jax.ShapeDtypeStruct((n, 3 * WIDTH_C), BF16),
                   jax.ShapeDtypeStruct((n, hb), BF16), jax.ShapeDtypeStruct((n, hb), BF16),
                   jax.ShapeDtypeStruct((WIDTH_B, n), BF16)],
        scratch_shapes=[pltpu.VMEM((3 * WIDTH_A // LANES, tm, LANES), F32)],
        compiler_params=_cparams(("parallel",)),
        name="proj",
    )(x, g, win, gq, gkv, wq, wk, wvt, *tabs)


def _run_pipeline(n_items, look, issue, consume, finish):
    if look >= n_items:
        results = [consume(i, s) for i, s in enumerate([issue(i) for i in range(n_items)])]
        for i, res in enumerate(results):
            finish(i, res)
        return
    inflight = {i: issue(i) for i in range(min(look, n_items))}
    pending = None
    for i in range(n_items):
        if i + look < n_items:
            inflight[i + look] = issue(i + look)
        if pending is not None:
            finish(*pending)
        pending = (i, consume(i, inflight.pop(i)))
    finish(*pending)


class _HeadPairs:
    def __init__(self, q_ref, k_ref, v_ref, tq, tkw):
        self.q_ref, self.k_ref, self.v_ref, self.tq, self.tkw = q_ref, k_ref, v_ref, tq, tkw
        lane = lax.broadcasted_iota(jnp.int32, (tq, LANES), 1)
        self.low = lane < HEAD_DIM
        self.ones = jnp.ones((tkw, LANES), BF16)

    @staticmethod
    def cols(h):
        return slice((h // 2) * LANES, (h // 2 + 1) * LANES)

    def scores(self, q_start, k_start, h):
        q = self.q_ref[q_start:q_start + self.tq, self.cols(h)]
        q = jnp.where(self.low if h % 2 == 0 else ~self.low, q, jnp.zeros_like(q))
        return _dot_nt(q, self.k_ref[pl.ds(k_start, self.tkw), self.cols(h)])

    def values(self, k_start, h, p):
        v = jnp.concatenate([self.v_ref[pl.ds(k_start, self.tkw), self.cols(h)], self.ones], axis=-1)
        o = _dot(p, v)
        return o[:, :LANES], o[:, LANES:]

    def merge(self, even, odd):
        return jnp.where(self.low, even, odd)


def _band_tables(dilation):
    tq, tkw = BAND_TQ, BAND_TQ + 2 * BAND_HALF
    shift = np.array([0, -BAND_HALF, -2 * BAND_HALF])
    rel = shift[:, None, None] + np.arange(tkw)[None, None, :] - np.arange(tq)[None, :, None]
    dist = np.abs(rel)
    slopes = 2.0 ** (-8.0 * np.arange(1, N_HEADS_A + 1) / N_HEADS_A)
    bias = -(slopes[None, :, None, None] * dilation) * dist[:, None].astype(np.float64)
    tab = np.where(dist[:, None] <= BAND_HALF, bias, NEG_INF)
    return jnp.asarray(tab, F32)


def _band_kernel(q_ref, k_ref, v_ref, tab_ref, o_ref, lse_ref, *, n):
    tq = BAND_TQ
    tkw = tq + 2 * BAND_HALF
    tiles = q_ref.shape[0] // tq
    last_tile = n // tq - 1
    items = [(t, h) for t in range(tiles) for h in range(N_HEADS_A)]
    info = []
    for t in range(tiles):
        i = pl.program_id(2) * tiles + t
        start = pl.multiple_of(jnp.clip(i * tq - BAND_HALF, 0, n - tkw), BAND_HALF)
        variant = jnp.minimum(i, 1) + (i == last_tile).astype(jnp.int32)
        info.append((start, variant))
    pairs = _HeadPairs(q_ref, k_ref, v_ref, tq, tkw)
    held = {}

    def issue(idx):
        t, h = items[idx]
        return pairs.scores(t * tq, info[t][0], h)

    def consume(idx, s):
        t, h = items[idx]
        s = s + tab_ref[info[t][1], h]
        m = jnp.max(s, axis=-1, keepdims=True)
        return jnp.exp(s - m).astype(BF16), m

    def finish(idx, res):
        t, h = items[idx]
        p, m = res
        out, den = pairs.values(info[t][0], h, p)
        held[h % 2] = (out / den, m + jnp.log(den))
        if h % 2 == 1:
            cols = pairs.cols(h)
            o_ref[t * tq:(t + 1) * tq, cols] = pairs.merge(held[0][0], held[1][0])
            lse_ref[t * tq:(t + 1) * tq, cols] = pairs.merge(held[0][1], held[1][1])

    _run_pipeline(len(items), BAND_LOOKAHEAD, issue, consume, finish)


def _band_call(pa_view, dilation):
    b, n, _ = pa_view.shape
    rows = BAND_TQ * BAND_TILES_PER_STEP
    w = WIDTH_A
    tab = _band_tables(dilation)
    qspec = pl.BlockSpec((None, rows, w), lambda bb, r, i: (bb, i, 3 * r))
    kspec = pl.BlockSpec((None, n, w), lambda bb, r, i: (bb, 0, 3 * r + 1))
    vspec = pl.BlockSpec((None, n, w), lambda bb, r, i: (bb, 0, 3 * r + 2))
    tspec = pl.BlockSpec(tab.shape, lambda bb, r, i: (0, 0, 0, 0))
    ospec = pl.BlockSpec((None, rows, w), lambda bb, r, i: (bb, i, r))
    shape = jax.ShapeDtypeStruct((b, n, dilation * w), F32)
    return pl.pallas_call(
        functools.partial(_band_kernel, n=n),
        grid=(b, dilation, n // rows),
        in_specs=[qspec, kspec, vspec, tspec],
        out_specs=[ospec, ospec],
        out_shape=[shape, shape],
        compiler_params=_cparams(("parallel", "parallel", "parallel")),
        name=f"band_d{dilation}",
    )(pa_view, pa_view, pa_view, tab)


def _mla_kernel(q_ref, k_ref, vt_ref, o_ref):
    tq = q_ref.shape[0]
    seq = k_ref.shape[0]
    tk = MLA_TK
    nh = MLA_HEADS_PER_STEP
    n_chunks = seq // tk
    items = [(u, h) for u in range(MLA_UNROLL) for h in range(nh)]
    look = MLA_LOOKAHEAD
    ones_rows = jnp.ones((MLA_DEN_ROWS, tk), BF16)

    def key_slice(chunk):
        return pl.ds(pl.multiple_of(chunk * tk, tk), tk)

    def score_matmul(chunk, h):
        k = k_ref[key_slice(chunk), h * LANES:(h + 1) * LANES]
        return _dot_nt(k, q_ref[:, h * LANES:(h + 1) * LANES])

    def value_matmul(chunk, h, p):
        vt = jnp.concatenate([vt_ref[h * V_HEAD:(h + 1) * V_HEAD, key_slice(chunk)], ones_rows], axis=0)
        return _dot(vt, p)

    def body(j, carry):
        state = list(carry[:2 * nh])
        scores = dict(zip(items[:look], carry[2 * nh:2 * nh + look]))
        pend_p, pend_alpha = carry[2 * nh + look:]
        pending = (jnp.maximum(j * MLA_UNROLL - 1, 0), nh - 1, pend_p, pend_alpha)
        ahead = []
        for idx, (u, h) in enumerate(items):
            la = idx + look
            if la < len(items):
                lu, lh = items[la]
                scores[lu, lh] = score_matmul(j * MLA_UNROLL + lu, lh)
            else:
                lu, lh = items[la - len(items)]
                ahead.append(score_matmul(jnp.minimum((j + 1) * MLA_UNROLL + lu, n_chunks - 1), lh))
            pc, ph, pp, pa = pending
            state[2 * ph + 1] = pa * state[2 * ph + 1] + value_matmul(pc, ph, pp)
            s = scores.pop((u, h))
            m_new = jnp.maximum(state[2 * h], jnp.max(s, axis=0, keepdims=True))
            alpha = jnp.exp2(state[2 * h] - m_new)
            state[2 * h] = m_new
            pending = (j * MLA_UNROLL + u, h, jnp.exp2(s - m_new).astype(BF16), alpha)
        return tuple(state) + tuple(ahead) + (pending[2], pending[3])

    init = (jnp.full((1, tq), NEG_INF, F32), jnp.zeros((V_HEAD + MLA_DEN_ROWS, tq), F32)) * nh
    init += tuple(score_matmul(u, h) for u, h in items[:look])
    init += (jnp.zeros((tk, tq), BF16), jnp.ones((1, tq), F32))
    res = lax.fori_loop(0, n_chunks // MLA_UNROLL, body, init)
    accs = [res[2 * h + 1] for h in range(nh)]
    accs[nh - 1] = res[-1] * accs[nh - 1] + value_matmul(n_chunks - 1, nh - 1, res[-2])
    out_t = jnp.concatenate([a[:V_HEAD] / a[V_HEAD:V_HEAD + 1] for a in accs], axis=0)
    o_ref[...] = out_t.T


def _mla_call(qm, km, vt, batch, seq):
    tq = MLA_TQ
    nh = MLA_HEADS_PER_STEP
    qspec = pl.BlockSpec((None, tq, nh * LANES), lambda b, g, i: (b, i, g))
    kspec = pl.BlockSpec((None, seq, nh * LANES), lambda b, g, i: (b, 0, g))
    vspec = pl.BlockSpec((nh * V_HEAD, seq), lambda b, g, i: (g, b))
    ospec = pl.BlockSpec((None, tq, nh * V_HEAD), lambda b, g, i: (b, i, g))
    return pl.pallas_call(
        _mla_kernel,
        grid=(batch, N_HEADS_B // nh, seq // tq),
        in_specs=[qspec, kspec, vspec],
        out_specs=ospec,
        out_shape=jax.ShapeDtypeStruct((batch, seq, WIDTH_B), F32),
        compiler_params=_cparams(("parallel", "parallel", "parallel")),
        name="mla",
    )(qm.reshape(batch, seq, -1), km.reshape(batch, seq, -1), vt)


def _na_variant(i, n_tiles):
    return jnp.minimum(i, 2) + jnp.maximum(i - (n_tiles - 3), 0)


def _na_kernel(q_ref, k_ref, v_ref, tab_ref, o_ref, *, rows):
    tq = NA_TILE_ROWS * GRID_W
    tkw = NA_KEY_ROWS * GRID_W
    tiles = q_ref.shape[0] // tq
    n_tiles = rows // NA_TILE_ROWS
    items = [(t, h) for t in range(tiles) for h in range(N_HEADS_C)]
    info = []
    for t in range(tiles):
        i = pl.program_id(1) * tiles + t
        base = jnp.clip(i * NA_TILE_ROWS - NA_ROWS // 2, 0, rows - NA_KEY_ROWS)
        info.append((pl.multiple_of(base * GRID_W, GRID_W), _na_variant(i, n_tiles)))
    pairs = _HeadPairs(q_ref, k_ref, v_ref, tq, tkw)
    held = {}

    def issue(idx):
        t, h = items[idx]
        return pairs.scores(t * tq, info[t][0], h)

    def consume(idx, s):
        t, h = items[idx]
        s = s + tab_ref[info[t][1], h]
        return jnp.exp(s - jnp.max(s, axis=-1, keepdims=True)).astype(BF16)

    def finish(idx, p):
        t, h = items[idx]
        out, den = pairs.values(info[t][0], h, p)
        held[h % 2] = out / den
        if h % 2 == 1:
            o_ref[t * tq:(t + 1) * tq, pairs.cols(h)] = pairs.merge(held[0], held[1])

    _run_pipeline(len(items), NA_LOOKAHEAD, issue, consume, finish)


def _na_tables(rpb, rows):
    r0 = np.array([0, 2, 4, rows - 4, rows - 2])
    base = np.clip(r0 - NA_ROWS // 2, 0, rows - NA_KEY_ROWS)
    r = r0[:, None] + np.arange(NA_TILE_ROWS)[None, :]
    row_start = np.clip(r - NA_ROWS // 2, 0, rows - NA_ROWS)
    krow = base[:, None] + np.arange(NA_KEY_ROWS)[None, :]
    drow = krow[:, None, :] - r[:, :, None]
    row_ok = (krow[:, None, :] >= row_start[:, :, None]) & (krow[:, None, :] < row_start[:, :, None] + NA_ROWS)
    c = np.arange(GRID_W)
    win_start = np.clip(c - NA_COLS // 2, 0, GRID_W - NA_COLS)
    col_ok = (c[None, :] >= win_start[:, None]) & (c[None, :] < win_start[:, None] + NA_COLS)
    dcol = np.clip(c[None, :] - c[:, None], -(NA_COLS - 1), NA_COLS - 1)
    ok = row_ok[:, :, None, :, None] & col_ok[None, None, :, None, :]
    di = np.clip(drow, -(NA_ROWS - 1), NA_ROWS - 1) + (NA_ROWS - 1)
    pick_col = (dcol[:, :, None] + NA_COLS - 1 == np.arange(2 * NA_COLS - 1)).astype(np.float32)
    pick_row = (di[..., None] == np.arange(2 * NA_ROWS - 1)).astype(np.float32)
    hi = lax.Precision.HIGHEST
    toeplitz = jnp.einsum("hab,cjb->hacj", rpb.astype(F32), pick_col, precision=hi)
    bias = jnp.einsum("vqka,hacj->vhqckj", pick_row, toeplitz, precision=hi)
    tab = jnp.where(jnp.asarray(ok)[:, None], bias, NEG_INF)
    return tab.reshape(5, N_HEADS_C, NA_TILE_ROWS * GRID_W, NA_KEY_ROWS * GRID_W)


def _na_call(pc, tab, batch, seq):
    rows = seq // GRID_W
    tq = NA_TILE_ROWS * GRID_W * NA_TILES_PER_STEP
    w = WIDTH_C
    qspec = pl.BlockSpec((None, tq, w), lambda b, i: (b, i, 0))
    kspec = pl.BlockSpec((None, seq, w), lambda b, i: (b, 0, 1))
    vspec = pl.BlockSpec((None, seq, w), lambda b, i: (b, 0, 2))
    tspec = pl.BlockSpec(tab.shape, lambda b, i: (0, 0, 0, 0))
    ospec = pl.BlockSpec((None, tq, w), lambda b, i: (b, i, 0))
    pc3 = pc.reshape(batch, seq, 3 * w)
    return pl.pallas_call(
        functools.partial(_na_kernel, rows=rows),
        grid=(batch, seq // tq),
        in_specs=[qspec, kspec, vspec, tspec],
        out_specs=ospec,
        out_shape=jax.ShapeDtypeStruct((batch, seq, w), F32),
        compiler_params=_cparams(("parallel", "parallel")),
        name="natten",
    )(pc3, pc3, pc3, tab)


def _mix_out_kernel(x_ref, o1_ref, o2_ref, o3_ref, l1_ref, l2_ref, l3_ref, ob_ref, oc_ref,
                    ga_ref, gb_ref, gc_ref, wo_ref, out_ref, o2_scr, o3_scr, l2_scr, l3_scr):
    tm = x_ref.shape[0]
    n_chunks = WIDTH_A // LANES

    def natural(view_ref, scr, dil):
        for r in range(dil):
            for c in range(n_chunks):
                col = r * WIDTH_A + c * LANES
                scr[c, pl.ds(r, tm // dil, stride=dil), :] = view_ref[:, col:col + LANES]
        return jnp.concatenate([scr[c] for c in range(n_chunks)], axis=-1)

    l1, o1 = l1_ref[...], o1_ref[...]
    l2, o2 = natural(l2_ref, l2_scr, DILATIONS[1]), natural(o2_ref, o2_scr, DILATIONS[1])
    l3, o3 = natural(l3_ref, l3_scr, DILATIONS[2]), natural(o3_ref, o3_scr, DILATIONS[2])
    mx = jnp.maximum(jnp.maximum(l1, l2), l3)
    w1, w2, w3 = jnp.exp(l1 - mx), jnp.exp(l2 - mx), jnp.exp(l3 - mx)
    oa = (w1 * o1 + w2 * o2 + w3 * o3) / (w1 + w2 + w3)
    ya = _rms(oa, ga_ref[...]).astype(BF16)
    yb = _rms(ob_ref[...], gb_ref[...]).astype(BF16)
    yc = _rms(oc_ref[...], gc_ref[...]).astype(BF16)
    y = _dot(ya, wo_ref[:WIDTH_A, :])
    y = y + _dot(yb, wo_ref[WIDTH_A:WIDTH_A + WIDTH_B, :])
    y = y + _dot(yc, wo_ref[WIDTH_A + WIDTH_B:, :])
    out_ref[...] = x_ref[...] + y


def _mix_out_call(x, o_parts, lse_parts, ob, oc, ga, gb, gc, wo):
    n, d = x.shape
    tm = ROW_TILE
    full = lambda a: pl.BlockSpec(a.shape, lambda i: (0,) * a.ndim)
    row = lambda w: pl.BlockSpec((tm, w), lambda i: (i, 0))
    view = lambda dil: pl.BlockSpec((tm // dil, dil * WIDTH_A), lambda i: (i, 0))
    return pl.pallas_call(
        _mix_out_kernel,
        grid=(n // tm,),
        in_specs=[row(d)] + [view(dil) for dil in DILATIONS] * 2
                 + [row(WIDTH_B), row(WIDTH_C), full(ga), full(gb), full(gc), full(wo)],
        out_specs=row(d),
        out_shape=jax.ShapeDtypeStruct((n, d), F32),
        scratch_shapes=[pltpu.VMEM((WIDTH_A // LANES, tm, LANES), F32)] * 4,
        compiler_params=_cparams(("parallel",)),
        name="mix_out",
    )(x, *o_parts, *lse_parts, ob, oc, ga, gb, gc, wo)


def _silu(u):
    return u * (1.0 / (1.0 + jnp.exp(-u)))


def _ffn_kernel(x_ref, g_ref, w1_ref, w3_ref, w2_ref, out_ref):
    x = x_ref[...]
    h = _rms(x, g_ref[...]).astype(BF16)
    a = (_silu(_dot(h, w1_ref[...])) * _dot(h, w3_ref[...])).astype(BF16)
    out_ref[...] = x + _dot(a, w2_ref[...])


def _ffn_call(x, g, w1, w3, w2):
    n, d = x.shape
    tm = ROW_TILE
    resident = lambda a: pl.BlockSpec(a.shape, lambda i: (0, 0), pipeline_mode=pl.Buffered(1))
    return pl.pallas_call(
        _ffn_kernel,
        grid=(n // tm,),
        in_specs=[pl.BlockSpec((tm, d), lambda i: (i, 0)), pl.BlockSpec((1, d), lambda i: (0, 0)),
                  resident(w1), resident(w3), resident(w2)],
        out_specs=pl.BlockSpec((tm, d), lambda i: (i, 0)),
        out_shape=jax.ShapeDtypeStruct((n, d), F32),
        compiler_params=_cparams(("parallel",)),
        name="ffn",
    )(x, g, w1, w3, w2)


def _router_kernel(x_ref, g_ref, wr_ref, h_ref, e_ref, gate_ref):
    h = _rms(x_ref[...], g_ref[...])
    _to_token_tiles(h_ref, h)
    logits = jnp.dot(h, wr_ref[...], precision=lax.Precision.HIGHEST, preferred_element_type=F32)
    lane = lax.broadcasted_iota(jnp.int32, logits.shape, 1).astype(F32)
    logits = jnp.where(lane < N_EXPERTS, logits, -jnp.inf)
    m1 = jnp.max(logits, axis=-1, keepdims=True)
    i1 = jnp.min(jnp.where(logits == m1, lane, float(LANES)), axis=-1, keepdims=True)
    rest = jnp.where(lane == i1, -jnp.inf, logits)
    m2 = jnp.max(rest, axis=-1, keepdims=True)
    i2 = jnp.min(jnp.where(rest == m2, lane, float(LANES)), axis=-1, keepdims=True)
    e = jnp.exp(m2 - m1)
    den = 1.0 + e
    e_ref[...] = jnp.where(lane == 0.0, i1, jnp.where(lane == 1.0, i2, 0.0)).astype(jnp.int32)
    gate_ref[...] = jnp.where(lane == 0.0, 1.0 / den, jnp.where(lane == 1.0, e / den, 0.0))


def _router_call(x, g, wr_pad):
    n, d = x.shape
    tm = ROW_TILE
    row = lambda w: pl.BlockSpec((tm, w), lambda i: (i, 0))
    full = lambda a: pl.BlockSpec(a.shape, lambda i: (0,) * a.ndim)
    return pl.pallas_call(
        _router_kernel,
        grid=(n // tm,),
        in_specs=[row(d), full(g), full(wr_pad)],
        out_specs=[pl.BlockSpec((tm * TOKEN_ROWS, LANES), lambda i: (i, 0)), row(LANES), row(LANES)],
        out_shape=[jax.ShapeDtypeStruct((n * TOKEN_ROWS, LANES), F32), jax.ShapeDtypeStruct((n, LANES), jnp.int32),
                   jax.ShapeDtypeStruct((n, LANES), F32)],
        compiler_params=_cparams(("parallel",)),
        name="router",
    )(x, g, wr_pad)


TOKEN_ROWS = 8


def _to_token_tiles(dst_ref, x):
    tm, d = x.shape
    assert d == TOKEN_ROWS * LANES
    for j in range(TOKEN_ROWS):
        dst_ref[pl.ds(j, tm, stride=TOKEN_ROWS), :] = x[:, j * LANES:(j + 1) * LANES]


def _from_token_tiles(src_ref, tm):
    return jnp.concatenate([src_ref[pl.ds(j, tm, stride=TOKEN_ROWS), :] for j in range(TOKEN_ROWS)], axis=-1)


def _row_copy(src_hbm, row, dst_ref, r, sem):
    src = src_hbm.at[pl.ds(pl.multiple_of(row * TOKEN_ROWS, TOKEN_ROWS), TOKEN_ROWS)]
    return pltpu.make_async_copy(src, dst_ref.at[pl.ds(pl.multiple_of(r * TOKEN_ROWS, TOKEN_ROWS), TOKEN_ROWS)], sem)


def _start_row_gather(idx_ref, base, src_hbm, dst_ref, sem, count, priorities=1):
    def body(g, c):
        for p in range(priorities):
            r = g * priorities + p
            _row_copy(src_hbm, idx_ref[base + r], dst_ref, r, sem).start(priority=p)
        return c
    lax.fori_loop(0, count // priorities, body, 0, unroll=GATHER_UNROLL // priorities)


def _wait_row_gather(src_hbm, dst_ref, sem, count):
    pltpu.make_async_copy(src_hbm.at[pl.ds(0, count * TOKEN_ROWS)], dst_ref, sem).wait()


def _moe_ffn_kernel(tile_e_ref, tile_ok_ref, row_tok_ref, h_hbm, w1_ref, w3_ref, w2_ref, out_ref,
                    xg_ref, acc_ref, sem):
    i = pl.program_id(0)
    f = pl.program_id(1)
    n_tiles = pl.num_programs(0)
    tm = MOE_TM
    slot = i % 2

    @pl.when(f == 0)
    def _():
        @pl.when(i == 0)
        def _():
            acc_ref[...] = jnp.zeros_like(acc_ref)
            _start_row_gather(row_tok_ref, 0, h_hbm, xg_ref.at[0], sem.at[0], tm, priorities=2)

        @pl.when(tile_ok_ref[i] != 0)
        def _():
            _wait_row_gather(h_hbm, xg_ref.at[slot], sem.at[slot], tm)

        @pl.when(jnp.logical_and(i + 1 < n_tiles, tile_ok_ref[jnp.minimum(i + 1, n_tiles - 1)] != 0))
        def _():
            _start_row_gather(row_tok_ref, (i + 1) * tm, h_hbm, xg_ref.at[1 - slot], sem.at[1 - slot], tm,
                              priorities=2)

    @pl.when(tile_ok_ref[i] != 0)
    def _():
        h = _from_token_tiles(xg_ref.at[slot], tm).astype(BF16)
        a = (_silu(_dot(h, w1_ref[...])) * _dot(h, w3_ref[...])).astype(BF16)
        total = _dot(a, w2_ref[...]) + jnp.where(f > 0, acc_ref[...], 0.0)
        acc_ref[...] = total
        _to_token_tiles(out_ref, total)

    @pl.when(tile_ok_ref[i] == 0)
    def _():
        out_ref[...] = jnp.zeros_like(out_ref)


def _moe_ffn_call(tile_e, tile_ok, row_tok, h, w1, w3, w2, layer):
    d = w1.shape[2]
    n_tiles = tile_e.shape[0]
    ff = w1.shape[3]
    tm, tf = MOE_TM, MOE_TF

    def ff_block(i, f, tile_ok):
        return jnp.where(tile_ok[i] != 0, f, ff // tf - 1)

    grid_spec = pltpu.PrefetchScalarGridSpec(
        num_scalar_prefetch=3,
        grid=(n_tiles, ff // tf),
        in_specs=[pl.BlockSpec(memory_space=pl.ANY),
                  pl.BlockSpec((None, None, d, tf), lambda i, f, te, tv, rt: (layer, te[i], 0, ff_block(i, f, tv))),
                  pl.BlockSpec((None, None, d, tf), lambda i, f, te, tv, rt: (layer, te[i], 0, ff_block(i, f, tv))),
                  pl.BlockSpec((None, None, tf, d), lambda i, f, te, tv, rt: (layer, te[i], ff_block(i, f, tv), 0))],
        out_specs=pl.BlockSpec((tm * TOKEN_ROWS, LANES), lambda i, f, te, tv, rt: (i, 0)),
        scratch_shapes=[pltpu.VMEM((2, tm * TOKEN_ROWS, LANES), F32), pltpu.VMEM((tm, d), F32),
                        pltpu.SemaphoreType.DMA((2,))],
    )
    return pl.pallas_call(
        _moe_ffn_kernel,
        grid_spec=grid_spec,
        out_shape=jax.ShapeDtypeStruct((n_tiles * tm * TOKEN_ROWS, LANES), F32),
        compiler_params=_cparams(("arbitrary", "arbitrary")),
        name="moe_ffn",
    )(tile_e, tile_ok, row_tok, h, w1, w3, w2)


def _combine_kernel(d0_ref, d1_ref, y_hbm, x_ref, gate_ref, gain_ref, out_ref, b0_ref, b1_ref, sem, *, final_norm):
    i = pl.program_id(0)
    n_tiles = pl.num_programs(0)
    tm = COMBINE_TM
    slot = i % 2

    def start(tile, s):
        _start_row_gather(d0_ref, tile * tm, y_hbm, b0_ref.at[s], sem.at[0, s], tm, priorities=2)
        _start_row_gather(d1_ref, tile * tm, y_hbm, b1_ref.at[s], sem.at[1, s], tm, priorities=2)

    @pl.when(i == 0)
    def _():
        start(0, 0)

    _wait_row_gather(y_hbm, b0_ref.at[slot], sem.at[0, slot], tm)
    _wait_row_gather(y_hbm, b1_ref.at[slot], sem.at[1, slot], tm)

    @pl.when(i + 1 < n_tiles)
    def _():
        start(i + 1, 1 - slot)

    gates = gate_ref[...]
    y0 = _from_token_tiles(b0_ref.at[slot], tm)
    y1 = _from_token_tiles(b1_ref.at[slot], tm)
    out = x_ref[...] + (gates[:, 0:1] * y0 + gates[:, 1:2] * y1)
    out_ref[...] = _rms(out, gain_ref[...]) if final_norm else out


def _combine_call(d0, d1, y, x, gates, gain, final_norm):
    n, d = x.shape
    tm = COMBINE_TM
    grid_spec = pltpu.PrefetchScalarGridSpec(
        num_scalar_prefetch=2,
        grid=(n // tm,),
        in_specs=[pl.BlockSpec(memory_space=pl.ANY),
                  pl.BlockSpec((tm, d), lambda i, a, b: (i, 0)),
                  pl.BlockSpec((tm, LANES), lambda i, a, b: (i, 0)),
                  pl.BlockSpec((1, d), lambda i, a, b: (0, 0))],
        out_specs=pl.BlockSpec((tm, d), lambda i, a, b: (i, 0)),
        scratch_shapes=[pltpu.VMEM((2, tm * TOKEN_ROWS, LANES), F32)] * 2 + [pltpu.SemaphoreType.DMA((2, 2))],
    )
    return pl.pallas_call(
        functools.partial(_combine_kernel, final_norm=final_norm),
        grid_spec=grid_spec,
        out_shape=jax.ShapeDtypeStruct((n, d), F32),
        compiler_params=_cparams(("arbitrary",)),
        name="moe_combine",
    )(d0, d1, y, x, gates, gain)


def _moe_plan(top_e, n_tiles):
    n_assign = top_e.shape[0] * TOP_K
    flat_e = top_e.reshape(n_assign)
    onehot = (flat_e[:, None] == jnp.arange(N_EXPERTS, dtype=jnp.int32)[None, :]).astype(jnp.int32)
    csum = jnp.cumsum(onehot, axis=0)
    rank = jnp.take_along_axis(csum, flat_e[:, None], axis=1)[:, 0] - 1
    counts = csum[-1]
    padded = (counts + MOE_TM - 1) // MOE_TM * MOE_TM
    pend = jnp.cumsum(padded)
    pstart = pend - padded
    dest = pstart[flat_e] + rank
    tile_start = jnp.arange(n_tiles, dtype=jnp.int32) * MOE_TM
    tile_e = jnp.minimum(jnp.searchsorted(pend, tile_start, side="right"), N_EXPERTS - 1).astype(jnp.int32)
    tile_ok = (tile_start < pend[-1]).astype(jnp.int32)
    order = jnp.argsort(flat_e, stable=True).astype(jnp.int32)
    row_e = jnp.repeat(tile_e, MOE_TM)
    r = jnp.arange(n_tiles * MOE_TM, dtype=jnp.int32) - pstart[row_e]
    src = jnp.clip((jnp.cumsum(counts) - counts)[row_e] + r, 0, n_assign - 1)
    row_tok = jnp.where(r < counts[row_e], order[src] // TOP_K, 0).astype(jnp.int32)
    dest = dest.reshape(-1, TOP_K).astype(jnp.int32)
    return tile_e, tile_ok, row_tok, dest[:, 0], dest[:, 1]


def _moe_layer(x, g, wr_pad, w1, w3, w2, layer, gain, final_norm):
    n = x.shape[0]
    h, top_e, gates = _router_call(x, g, wr_pad)
    n_tiles = -(-(n * TOP_K + N_EXPERTS * (MOE_TM - 1)) // MOE_TM)
    tile_e, tile_ok, row_tok, d0, d1 = _moe_plan(top_e[:, :TOP_K], n_tiles)
    y = _moe_ffn_call(tile_e, tile_ok, row_tok, h, w1, w3, w2, layer)
    return _combine_call(d0, d1, y, x, gates, gain, final_norm)


def _norm_kernel(x_ref, g_ref, o_ref):
    o_ref[...] = _rms(x_ref[...], g_ref[...])


def _norm_call(x, g):
    n, d = x.shape
    tm = ROW_TILE
    return pl.pallas_call(
        _norm_kernel,
        grid=(n // tm,),
        in_specs=[pl.BlockSpec((tm, d), lambda i: (i, 0)), pl.BlockSpec((1, d), lambda i: (0, 0))],
        out_specs=pl.BlockSpec((tm, d), lambda i: (i, 0)),
        out_shape=jax.ShapeDtypeStruct((n, d), F32),
        compiler_params=_cparams(("parallel",)),
        name="final_norm",
    )(x, g)


def _rope_tables(seq):
    inv = ROPE_BASE ** (-jnp.arange(0, QK_ROPE, 2, dtype=F32) / QK_ROPE)
    ang = jnp.arange(seq)[:, None].astype(F32) * inv[None, :]
    cos, sin = jnp.cos(ang), jnp.sin(ang)
    pad = LANES - QK_NOPE - QK_ROPE
    zeros = lambda w: jnp.zeros((seq, w), F32)
    ctab = jnp.concatenate([jnp.ones((seq, QK_NOPE), F32), cos, cos, zeros(pad)], axis=1)
    stab = jnp.concatenate([zeros(QK_NOPE), sin, sin, zeros(pad)], axis=1)
    sa = jnp.concatenate([zeros(QK_NOPE), -sin, zeros(QK_ROPE // 2 + pad)], axis=1)
    sb = jnp.concatenate([zeros(QK_NOPE + QK_ROPE // 2), sin, zeros(pad)], axis=1)
    return ctab, stab, sa, sb


def _split_w_in(w_in):
    d = w_in.shape[0]
    bounds = np.cumsum([WIDTH_A, WIDTH_A, WIDTH_A, Q_LORA, KV_LORA, QK_ROPE, WIDTH_C, WIDTH_C])
    qa, ka, va, cq, ckv, kr, qc, kc, vc = jnp.split(w_in, bounds.tolist(), axis=1)
    half = QK_ROPE // 2
    z_lo = jnp.zeros((d, QK_NOPE), w_in.dtype)
    z_hi = jnp.zeros((d, LANES - QK_NOPE - QK_ROPE), w_in.dtype)
    rope_blk = jnp.concatenate([z_lo, kr, z_hi], axis=1)
    swap_blk = jnp.concatenate([z_lo, -kr[:, half:], kr[:, :half], z_hi], axis=1)
    score_scale = HEAD_DIM ** -0.5
    wa = jnp.concatenate([qa * score_scale, ka, va], axis=1).astype(BF16)
    wc = jnp.concatenate([qc * score_scale, kc, vc], axis=1).astype(BF16)
    wb = jnp.concatenate([cq, ckv, rope_blk, swap_blk], axis=1).astype(BF16)
    return jnp.concatenate([wa, wc, wb], axis=1)


def _split_w_uq(w_uq):
    r = w_uq.shape[0]
    w = w_uq.reshape(r, N_HEADS_B, QK_NOPE + QK_ROPE)
    nope, rope = w[..., :QK_NOPE], w[..., QK_NOPE:]
    z_hi = jnp.zeros((r, N_HEADS_B, LANES - QK_NOPE - QK_ROPE), w_uq.dtype)
    return jnp.concatenate([nope, rope, z_hi], axis=-1).reshape(r, -1).astype(BF16)


def _split_w_ukv(w_ukv):
    r = w_ukv.shape[0]
    w = w_ukv.reshape(r, N_HEADS_B, QK_NOPE + V_HEAD)
    k_nope, v = w[..., :QK_NOPE], w[..., QK_NOPE:]
    wk = jnp.concatenate([k_nope, jnp.zeros((r, N_HEADS_B, LANES - QK_NOPE), w_ukv.dtype)], axis=-1)
    return wk.reshape(r, -1).astype(BF16), v.reshape(r, -1).T.astype(BF16)


def kernel(x, g_mix, w_in, g_q, g_kv, w_uq, w_ukv, rpb, g_out_a, g_out_b, g_out_c, w_o, g_ffn, w1, w3, w2,
           w_router, e_w1, e_w3, e_w2, g_final):
    batch, seq, d = x.shape
    n = batch * seq
    depth = g_mix.shape[0]
    rows = seq // GRID_W
    rope_tabs = _rope_tables(seq)
    ew1, ew3, ew2 = e_w1.astype(BF16), e_w3.astype(BF16), e_w2.astype(BF16)
    xf = x.reshape(n, d)
    for layer in range(depth):
        win = _split_w_in(w_in[layer])
        wq = _split_w_uq(w_uq[layer])
        wk, wvt = _split_w_ukv(w_ukv[layer])
        pa, pa4, pa16, pc, qm, km, vt = _proj_call(xf, g_mix[layer][None], win, g_q[layer][None], g_kv[layer][None],
                                                   wq, wk, wvt, rope_tabs, seq)
        o_parts, lse_parts = [], []
        for dil, view in zip(DILATIONS, (pa, pa4, pa16)):
            o, lse = _band_call(view.reshape(batch, seq // dil, dil * 3 * WIDTH_A), dil)
            o_parts.append(o.reshape(n // dil, dil * WIDTH_A))
            lse_parts.append(lse.reshape(n // dil, dil * WIDTH_A))
        ob = _mla_call(qm, km, vt, batch, seq).reshape(n, WIDTH_B)
        oc = _na_call(pc, _na_tables(rpb[layer], rows), batch, seq).reshape(n, WIDTH_C)
        xf = _mix_out_call(xf, o_parts, lse_parts, ob, oc, g_out_a[layer][None], g_out_b[layer][None],
                           g_out_c[layer][None], w_o[layer].astype(BF16))
        j = layer // 2
        if layer % 2 == 0:
            xf = _ffn_call(xf, g_ffn[layer][None], w1[j].astype(BF16), w3[j].astype(BF16), w2[j].astype(BF16))
        else:
            wr_pad = jnp.pad(w_router[j], ((0, 0), (0, LANES - N_EXPERTS)))
            xf = _moe_layer(xf, g_ffn[layer][None], wr_pad, ew1, ew3, ew2, j, g_final[None], layer == depth - 1)
    if depth % 2 == 1:
        xf = _norm_call(xf, g_final[None])
    return xf.reshape(batch, seq, d)
```

```python
import functools
import math

import numpy as np
import jax
import jax.numpy as jnp
from jax import lax
from jax.experimental import pallas as pl
from jax.experimental.pallas import tpu as pltpu

F32 = jnp.float32
BF16 = jnp.bfloat16

LANES = 128
V7X_VMEM_LIMIT_BYTES = 52 * 1024 * 1024

HEAD_DIM = 64
N_HEADS_A = 6
DILATIONS = (1, 4, 16)
BAND_HALF = 64
N_HEADS_B = 6
Q_LORA = 384
KV_LORA = 256
QK_NOPE = 64
QK_ROPE = 32
V_HEAD = 64
ROPE_BASE = 10000.0
N_HEADS_C = 4
GRID_W = 64
NA_ROWS = 8
NA_COLS = 16
WIDTH_A = N_HEADS_A * HEAD_DIM
WIDTH_B = N_HEADS_B * V_HEAD
WIDTH_C = N_HEADS_C * HEAD_DIM
N_EXPERTS = 8
TOP_K = 2
RMS_EPS = 1e-6
NEG_INF = -1e30

ROW_TILE = 512
MLA_TQ = 256
MLA_TK = 256
MLA_UNROLL = 32
MLA_LOOKAHEAD = 4
MLA_HEADS_PER_STEP = 6
MLA_DEN_ROWS = 16
MLA_Q_PRESCALE = (QK_NOPE + QK_ROPE) ** -0.5 * math.log2(math.e)
BAND_TQ = 128
BAND_TILES_PER_STEP = 4
BAND_LOOKAHEAD = 6
NA_TILE_ROWS = 2
NA_KEY_ROWS = 10
NA_TILES_PER_STEP = 4
NA_LOOKAHEAD = 16
MOE_TM = 512
MOE_TF = 1792
COMBINE_TM = 512
GATHER_UNROLL = 8


def _cparams(semantics):
    return pltpu.CompilerParams(dimension_semantics=semantics, vmem_limit_bytes=V7X_VMEM_LIMIT_BYTES)


def _rms(x, g):
    return x * lax.rsqrt(jnp.mean(x * x, axis=-1, keepdims=True) + RMS_EPS) * g


def _dot(a, b):
    return jnp.dot(a, b, preferred_element_type=F32)


def _dot_nt(a, b):
    return lax.dot_general(a, b, (((1,), (1,)), ((), ())), preferred_element_type=F32)


def _proj_kernel(x_ref, g_ref, win_ref, gq_ref, gkv_ref, wq_ref, wk_ref, wvt_ref,
                 ct_ref, st_ref, sa_ref, sb_ref, pa_ref, pa4_ref, pa16_ref, pc_ref, qm_ref, km_ref, vt_ref, pa_scr):
    tm = x_ref.shape[0]
    h = _rms(x_ref[...], g_ref[...]).astype(BF16)
    proj = _dot(h, win_ref[...])
    pa = proj[:, :3 * WIDTH_A]
    pb = proj[:, 3 * (WIDTH_A + WIDTH_C):]
    pa_ref[...] = pa.astype(BF16)
    n_chunks = pa.shape[1] // LANES
    for c in range(n_chunks):
        pa_scr[c] = pa[:, c * LANES:(c + 1) * LANES]
    for dil, view_ref in ((DILATIONS[1], pa4_ref), (DILATIONS[2], pa16_ref)):
        for r in range(dil):
            for c in range(n_chunks):
                col = r * 3 * WIDTH_A + c * LANES
                view_ref[:, col:col + LANES] = pa_scr[c, pl.ds(r, tm // dil, stride=dil), :].astype(BF16)
    pc_ref[...] = proj[:, 3 * WIDTH_A:3 * (WIDTH_A + WIDTH_C)].astype(BF16)
    hq = _rms(pb[:, :Q_LORA], gq_ref[...]).astype(BF16)
    hkv = _rms(pb[:, Q_LORA:Q_LORA + KV_LORA], gkv_ref[...]).astype(BF16)
    r1 = pb[:, Q_LORA + KV_LORA:Q_LORA + KV_LORA + LANES]
    r2 = pb[:, Q_LORA + KV_LORA + LANES:]
    ct = ct_ref[...]
    st = st_ref[...]
    sa = sa_ref[...]
    sb = sb_ref[...]
    half = QK_ROPE // 2
    qa = _dot(hq, wq_ref[...])
    kn = _dot(hkv, wk_ref[...])
    kr = r1 * ct + r2 * st
    for hd in range(N_HEADS_B):
        sl = slice(hd * LANES, (hd + 1) * LANES)
        q = qa[:, sl]
        q = q * ct + pltpu.roll(q, LANES - half, 1) * sa + pltpu.roll(q, half, 1) * sb
        qm_ref[:, sl] = (q * MLA_Q_PRESCALE).astype(BF16)
        km_ref[:, sl] = (kn[:, sl] + kr).astype(BF16)
    vt_ref[...] = _dot_nt(wvt_ref[...], hkv).astype(BF16)


def _proj_call(x, g, win, gq, gkv, wq, wk, wvt, tabs, seq):
    n, d = x.shape
    tm = ROW_TILE
    tiles_per_seq = seq // tm
    full = lambda a: pl.BlockSpec(a.shape, lambda i: (0,) * a.ndim)
    row = lambda w: pl.BlockSpec((tm, w), lambda i: (i, 0))
    tab = pl.BlockSpec((tm, LANES), lambda i: (i % tiles_per_seq, 0))
    view = lambda dil: pl.BlockSpec((tm // dil, dil * 3 * WIDTH_A), lambda i: (i, 0))
    hb = N_HEADS_B * LANES
    return pl.pallas_call(
        _proj_kernel,
        grid=(n // tm,),
        in_specs=[row(d), full(g), full(win), full(gq), full(gkv), full(wq), full(wk), full(wvt)] + [tab] * 4,
        out_specs=[row(3 * WIDTH_A)] + [view(dil) for dil in DILATIONS[1:]] + [row(3 * WIDTH_C), row(hb), row(hb),
                   pl.BlockSpec((WIDTH_B, tm), lambda i: (0, i))],
        out_shape=[jax.ShapeDtypeStruct((n, 3 * WIDTH_A), BF16)]
                  + [jax.ShapeDtypeStruct((n // dil, dil * 3 * WIDTH_A), BF16) for dil in DILATIONS[1:]]
                  + [jax.ShapeDtypeStruct((n, 3 * WIDTH_C), BF16),
                   jax.ShapeDtypeStruct((n, hb), BF16), jax.ShapeDtypeStruct((n, hb), BF16),
                   jax.ShapeDtypeStruct((WIDTH_B, n), BF16)],
        scratch_shapes=[pltpu.VMEM((3 * WIDTH_A // LANES, tm, LANES), F32)],
        compiler_params=_cparams(("parallel",)),
        name="proj",
    )(x, g, win, gq, gkv, wq, wk, wvt, *tabs)


def _run_pipeline(n_items, look, issue, consume, finish):
    if look >= n_items:
        results = [consume(i, s) for i, s in enumerate([issue(i) for i in range(n_items)])]
        for i, res in enumerate(results):
            finish(i, res)
        return
    inflight = {i: issue(i) for i in range(min(look, n_items))}
    pending = None
    for i in range(n_items):
        if i + look < n_items:
            inflight[i + look] = issue(i + look)
        if pending is not None:
            finish(*pending)
        pending = (i, consume(i, inflight.pop(i)))
    finish(*pending)


class _HeadPairs:
    def __init__(self, q_ref, k_ref, v_ref, tq, tkw):
        self.q_ref, self.k_ref, self.v_ref, self.tq, self.tkw = q_ref, k_ref, v_ref, tq, tkw
        lane = lax.broadcasted_iota(jnp.int32, (tq, LANES), 1)
        self.low = lane < HEAD_DIM
        self.ones = jnp.ones((tkw, LANES), BF16)

    @staticmethod
    def cols(h):
        return slice((h // 2) * LANES, (h // 2 + 1) * LANES)

    def scores(self, q_start, k_start, h):
        q = self.q_ref[q_start:q_start + self.tq, self.cols(h)]
        q = jnp.where(self.low if h % 2 == 0 else ~self.low, q, jnp.zeros_like(q))
        return _dot_nt(q, self.k_ref[pl.ds(k_start, self.tkw), self.cols(h)])

    def values(self, k_start, h, p):
        v = jnp.concatenate([self.v_ref[pl.ds(k_start, self.tkw), self.cols(h)], self.ones], axis=-1)
        o = _dot(p, v)
        return o[:, :LANES], o[:, LANES:]

    def merge(self, even, odd):
        return jnp.where(self.low, even, odd)


def _band_tables(dilation):
    tq, tkw = BAND_TQ, BAND_TQ + 2 * BAND_HALF
    shift = np.array([0, -BAND_HALF, -2 * BAND_HALF])
    rel = shift[:, None, None] + np.arange(tkw)[None, None, :] - np.arange(tq)[None, :, None]
    dist = np.abs(rel)
    slopes = 2.0 ** (-8.0 * np.arange(1, N_HEADS_A + 1) / N_HEADS_A)
    bias = -(slopes[None, :, None, None] * dilation) * dist[:, None].astype(np.float64)
    tab = np.where(dist[:, None] <= BAND_HALF, bias, NEG_INF)
    return jnp.asarray(tab, F32)


def _band_kernel(q_ref, k_ref, v_ref, tab_ref, o_ref, lse_ref, *, n):
    tq = BAND_TQ
    tkw = tq + 2 * BAND_HALF
    tiles = q_ref.shape[0] // tq
    last_tile = n // tq - 1
    items = [(t, h) for t in range(tiles) for h in range(N_HEADS_A)]
    info = []
    for t in range(tiles):
        i = pl.program_id(2) * tiles + t
        start = pl.multiple_of(jnp.clip(i * tq - BAND_HALF, 0, n - tkw), BAND_HALF)
        variant = jnp.minimum(i, 1) + (i == last_tile).astype(jnp.int32)
        info.append((start, variant))
    pairs = _HeadPairs(q_ref, k_ref, v_ref, tq, tkw)
    held = {}

    def issue(idx):
        t, h = items[idx]
        return pairs.scores(t * tq, info[t][0], h)

    def consume(idx, s):
        t, h = items[idx]
        s = s + tab_ref[info[t][1], h]
        m = jnp.max(s, axis=-1, keepdims=True)
        return jnp.exp(s - m).astype(BF16), m

    def finish(idx, res):
        t, h = items[idx]
        p, m = res
        out, den = pairs.values(info[t][0], h, p)
        held[h % 2] = (out / den, m + jnp.log(den))
        if h % 2 == 1:
            cols = pairs.cols(h)
            o_ref[t * tq:(t + 1) * tq, cols] = pairs.merge(held[0][0], held[1][0])
            lse_ref[t * tq:(t + 1) * tq, cols] = pairs.merge(held[0][1], held[1][1])

    _run_pipeline(len(items), BAND_LOOKAHEAD, issue, consume, finish)


def _band_call(pa_view, dilation):
    b, n, _ = pa_view.shape
    rows = BAND_TQ * BAND_TILES_PER_STEP
    w = WIDTH_A
    tab = _band_tables(dilation)
    qspec = pl.BlockSpec((None, rows, w), lambda bb, r, i: (bb, i, 3 * r))
    kspec = pl.BlockSpec((None, n, w), lambda bb, r, i: (bb, 0, 3 * r + 1))
    vspec = pl.BlockSpec((None, n, w), lambda bb, r, i: (bb, 0, 3 * r + 2))
    tspec = pl.BlockSpec(tab.shape, lambda bb, r, i: (0, 0, 0, 0))
    ospec = pl.BlockSpec((None, rows, w), lambda bb, r, i: (bb, i, r))
    shape = jax.ShapeDtypeStruct((b, n, dilation * w), F32)
    return pl.pallas_call(
        functools.partial(_band_kernel, n=n),
        grid=(b, dilation, n // rows),
        in_specs=[qspec, kspec, vspec, tspec],
        out_specs=[ospec, ospec],
        out_shape=[shape, shape],
        compiler_params=_cparams(("parallel", "parallel", "parallel")),
        name=f"band_d{dilation}",
    )(pa_view, pa_view, pa_view, tab)


def _mla_kernel(q_ref, k_ref, vt_ref, o_ref):
    tq = q_ref.shape[0]
    seq = k_ref.shape[0]
    tk = MLA_TK
    nh = MLA_HEADS_PER_STEP
    n_chunks = seq // tk
    items = [(u, h) for u in range(MLA_UNROLL) for h in range(nh)]
    look = MLA_LOOKAHEAD
    ones_rows = jnp.ones((MLA_DEN_ROWS, tk), BF16)

    def key_slice(chunk):
        return pl.ds(pl.multiple_of(chunk * tk, tk), tk)

    def score_matmul(chunk, h):
        k = k_ref[key_slice(chunk), h * LANES:(h + 1) * LANES]
        return _dot_nt(k, q_ref[:, h * LANES:(h + 1) * LANES])

    def value_matmul(chunk, h, p):
        vt = jnp.concatenate([vt_ref[h * V_HEAD:(h + 1) * V_HEAD, key_slice(chunk)], ones_rows], axis=0)
        return _dot(vt, p)

    def body(j, carry):
        state = list(carry[:2 * nh])
        scores = dict(zip(items[:look], carry[2 * nh:2 * nh + look]))
        pend_p, pend_alpha = carry[2 * nh + look:]
        pending = (jnp.maximum(j * MLA_UNROLL - 1, 0), nh - 1, pend_p, pend_alpha)
        ahead = []
        for idx, (u, h) in enumerate(items):
            la = idx + look
            if la < len(items):
                lu, lh = items[la]
                scores[lu, lh] = score_matmul(j * MLA_UNROLL + lu, lh)
            else:
                lu, lh = items[la - len(items)]
                ahead.append(score_matmul(jnp.minimum((j + 1) * MLA_UNROLL + lu, n_chunks - 1), lh))
            pc, ph, pp, pa = pending
            state[2 * ph + 1] = pa * state[2 * ph + 1] + value_matmul(pc, ph, pp)
            s = scores.pop((u, h))
            m_new = jnp.maximum(state[2 * h], jnp.max(s, axis=0, keepdims=True))
            alpha = jnp.exp2(state[2 * h] - m_new)
            state[2 * h] = m_new
            pending = (j * MLA_UNROLL + u, h, jnp.exp2(s - m_new).astype(BF16), alpha)
        return tuple(state) + tuple(ahead) + (pending[2], pending[3])

    init = (jnp.full((1, tq), NEG_INF, F32), jnp.zeros((V_HEAD + MLA_DEN_ROWS, tq), F32)) * nh
    init += tuple(score_matmul(u, h) for u, h in items[:look])
    init += (jnp.zeros((tk, tq), BF16), jnp.ones((1, tq), F32))
    res = lax.fori_loop(0, n_chunks // MLA_UNROLL, body, init)
    accs = [res[2 * h + 1] for h in range(nh)]
    accs[nh - 1] = res[-1] * accs[nh - 1] + value_matmul(n_chunks - 1, nh - 1, res[-2])
    out_t = jnp.concatenate([a[:V_HEAD] / a[V_HEAD:V_HEAD + 1] for a in accs], axis=0)
    o_ref[...] = out_t.T


def _mla_call(qm, km, vt, batch, seq):
    tq = MLA_TQ
    nh = MLA_HEADS_PER_STEP
    qspec = pl.BlockSpec((None, tq, nh * LANES), lambda b, g, i: (b, i, g))
    kspec = pl.BlockSpec((None, seq, nh * LANES), lambda b, g, i: (b, 0, g))
    vspec = pl.BlockSpec((nh * V_HEAD, seq), lambda b, g, i: (g, b))
    ospec = pl.BlockSpec((None, tq, nh * V_HEAD), lambda b, g, i: (b, i, g))
    return pl.pallas_call(
        _mla_kernel,
        grid=(batch, N_HEADS_B // nh, seq // tq),
        in_specs=[qspec, kspec, vspec],
        out_specs=ospec,
        out_shape=jax.ShapeDtypeStruct((batch, seq, WIDTH_B), F32),
        compiler_params=_cparams(("parallel", "parallel", "parallel")),
        name="mla",
    )(qm.reshape(batch, seq, -1), km.reshape(batch, seq, -1), vt)


def _na_variant(i, n_tiles):
    return jnp.minimum(i, 2) + jnp.maximum(i - (n_tiles - 3), 0)


def _na_kernel(q_ref, k_ref, v_ref, tab_ref, o_ref, *, rows):
    tq = NA_TILE_ROWS * GRID_W
    tkw = NA_KEY_ROWS * GRID_W
    tiles = q_ref.shape[0] // tq
    n_tiles = rows // NA_TILE_ROWS
    items = [(t, h) for t in range(tiles) for h in range(N_HEADS_C)]
    info = []
    for t in range(tiles):
        i = pl.program_id(1) * tiles + t
        base = jnp.clip(i * NA_TILE_ROWS - NA_ROWS // 2, 0, rows - NA_KEY_ROWS)
        info.append((pl.multiple_of(base * GRID_W, GRID_W), _na_variant(i, n_tiles)))
    pairs = _HeadPairs(q_ref, k_ref, v_ref, tq, tkw)
    held = {}

    def issue(idx):
        t, h = items[idx]
        return pairs.scores(t * tq, info[t][0], h)

    def consume(idx, s):
        t, h = items[idx]
        s = s + tab_ref[info[t][1], h]
        return jnp.exp(s - jnp.max(s, axis=-1, keepdims=True)).astype(BF16)

    def finish(idx, p):
        t, h = items[idx]
        out, den = pairs.values(info[t][0], h, p)
        held[h % 2] = out / den
        if h % 2 == 1:
            o_ref[t * tq:(t + 1) * tq, pairs.cols(h)] = pairs.merge(held[0], held[1])

    _run_pipeline(len(items), NA_LOOKAHEAD, issue, consume, finish)


def _na_tables(rpb, rows):
    r0 = np.array([0, 2, 4, rows - 4, rows - 2])
    base = np.clip(r0 - NA_ROWS // 2, 0, rows - NA_KEY_ROWS)
    r = r0[:, None] + np.arange(NA_TILE_ROWS)[None, :]
    row_start = np.clip(r - NA_ROWS // 2, 0, rows - NA_ROWS)
    krow = base[:, None] + np.arange(NA_KEY_ROWS)[None, :]
    drow = krow[:, None, :] - r[:, :, None]
    row_ok = (krow[:, None, :] >= row_start[:, :, None]) & (krow[:, None, :] < row_start[:, :, None] + NA_ROWS)
    c = np.arange(GRID_W)
    win_start = np.clip(c - NA_COLS // 2, 0, GRID_W - NA_COLS)
    col_ok = (c[None, :] >= win_start[:, None]) & (c[None, :] < win_start[:, None] + NA_COLS)
    dcol = np.clip(c[None, :] - c[:, None], -(NA_COLS - 1), NA_COLS - 1)
    ok = row_ok[:, :, None, :, None] & col_ok[None, None, :, None, :]
    di = np.clip(drow, -(NA_ROWS - 1), NA_ROWS - 1) + (NA_ROWS - 1)
    pick_col = (dcol[:, :, None] + NA_COLS - 1 == np.arange(2 * NA_COLS - 1)).astype(np.float32)
    pick_row = (di[..., None] == np.arange(2 * NA_ROWS - 1)).astype(np.float32)
    hi = lax.Precision.HIGHEST
    toeplitz = jnp.einsum("hab,cjb->hacj", rpb.astype(F32), pick_col, precision=hi)
    bias = jnp.einsum("vqka,hacj->vhqckj", pick_row, toeplitz, precision=hi)
    tab = jnp.where(jnp.asarray(ok)[:, None], bias, NEG_INF)
    return tab.reshape(5, N_HEADS_C, NA_TILE_ROWS * GRID_W, NA_KEY_ROWS * GRID_W)


def _na_call(pc, tab, batch, seq):
    rows = seq // GRID_W
    tq = NA_TILE_ROWS * GRID_W * NA_TILES_PER_STEP
    w = WIDTH_C
    qspec = pl.BlockSpec((None, tq, w), lambda b, i: (b, i, 0))
    kspec = pl.BlockSpec((None, seq, w), lambda b, i: (b, 0, 1))
    vspec = pl.BlockSpec((None, seq, w), lambda b, i: (b, 0, 2))
    tspec = pl.BlockSpec(tab.shape, lambda b, i: (0, 0, 0, 0))
    ospec = pl.BlockSpec((None, tq, w), lambda b, i: (b, i, 0))
    pc3 = pc.reshape(batch, seq, 3 * w)
    return pl.pallas_call(
        functools.partial(_na_kernel, rows=rows),
        grid=(batch, seq // tq),
        in_specs=[qspec, kspec, vspec, tspec],
        out_specs=ospec,
        out_shape=jax.ShapeDtypeStruct((batch, seq, w), F32),
        compiler_params=_cparams(("parallel", "parallel")),
        name="natten",
    )(pc3, pc3, pc3, tab)


def _mix_out_kernel(x_ref, o1_ref, o2_ref, o3_ref, l1_ref, l2_ref, l3_ref, ob_ref, oc_ref,
                    ga_ref, gb_ref, gc_ref, wo_ref, out_ref, o2_scr, o3_scr, l2_scr, l3_scr):
    tm = x_ref.shape[0]
    n_chunks = WIDTH_A // LANES

    def natural(view_ref, scr, dil):
        for r in range(dil):
            for c in range(n_chunks):
                col = r * WIDTH_A + c * LANES
                scr[c, pl.ds(r, tm // dil, stride=dil), :] = view_ref[:, col:col + LANES]
        return jnp.concatenate([scr[c] for c in range(n_chunks)], axis=-1)

    l1, o1 = l1_ref[...], o1_ref[...]
    l2, o2 = natural(l2_ref, l2_scr, DILATIONS[1]), natural(o2_ref, o2_scr, DILATIONS[1])
    l3, o3 = natural(l3_ref, l3_scr, DILATIONS[2]), natural(o3_ref, o3_scr, DILATIONS[2])
    mx = jnp.maximum(jnp.maximum(l1, l2), l3)
    w1, w2, w3 = jnp.exp(l1 - mx), jnp.exp(l2 - mx), jnp.exp(l3 - mx)
    oa = (w1 * o1 + w2 * o2 + w3 * o3) / (w1 + w2 + w3)
    ya = _rms(oa, ga_ref[...]).astype(BF16)
    yb = _rms(ob_ref[...], gb_ref[...]).astype(BF16)
    yc = _rms(oc_ref[...], gc_ref[...]).astype(BF16)
    y = _dot(ya, wo_ref[:WIDTH_A, :])
    y = y + _dot(yb, wo_ref[WIDTH_A:WIDTH_A + WIDTH_B, :])
    y = y + _dot(yc, wo_ref[WIDTH_A + WIDTH_B:, :])
    out_ref[...] = x_ref[...] + y


def _mix_out_call(x, o_parts, lse_parts, ob, oc, ga, gb, gc, wo):
    n, d = x.shape
    tm = ROW_TILE
    full = lambda a: pl.BlockSpec(a.shape, lambda i: (0,) * a.ndim)
    row = lambda w: pl.BlockSpec((tm, w), lambda i: (i, 0))
    view = lambda dil: pl.BlockSpec((tm // dil, dil * WIDTH_A), lambda i: (i, 0))
    return pl.pallas_call(
        _mix_out_kernel,
        grid=(n // tm,),
        in_specs=[row(d)] + [view(dil) for dil in DILATIONS] * 2
                 + [row(WIDTH_B), row(WIDTH_C), full(ga), full(gb), full(gc), full(wo)],
        out_specs=row(d),
        out_shape=jax.ShapeDtypeStruct((n, d), F32),
        scratch_shapes=[pltpu.VMEM((WIDTH_A // LANES, tm, LANES), F32)] * 4,
        compiler_params=_cparams(("parallel",)),
        name="mix_out",
    )(x, *o_parts, *lse_parts, ob, oc, ga, gb, gc, wo)


def _silu(u):
    return u * (1.0 / (1.0 + jnp.exp(-u)))


def _ffn_kernel(x_ref, g_ref, w1_ref, w3_ref, w2_ref, out_ref):
    x = x_ref[...]
    h = _rms(x, g_ref[...]).astype(BF16)
    a = (_silu(_dot(h, w1_ref[...])) * _dot(h, w3_ref[...])).astype(BF16)
    out_ref[...] = x + _dot(a, w2_ref[...])


def _ffn_call(x, g, w1, w3, w2):
    n, d = x.shape
    tm = ROW_TILE
    resident = lambda a: pl.BlockSpec(a.shape, lambda i: (0, 0), pipeline_mode=pl.Buffered(1))
    return pl.pallas_call(
        _ffn_kernel,
        grid=(n // tm,),
        in_specs=[pl.BlockSpec((tm, d), lambda i: (i, 0)), pl.BlockSpec((1, d), lambda i: (0, 0)),
                  resident(w1), resident(w3), resident(w2)],
        out_specs=pl.BlockSpec((tm, d), lambda i: (i, 0)),
        out_shape=jax.ShapeDtypeStruct((n, d), F32),
        compiler_params=_cparams(("parallel",)),
        name="ffn",
    )(x, g, w1, w3, w2)


def _router_kernel(x_ref, g_ref, wr_ref, h_ref, e_ref, gate_ref):
    h = _rms(x_ref[...], g_ref[...])
    _to_token_tiles(h_ref, h)
    logits = jnp.dot(h, wr_ref[...], precision=lax.Precision.HIGHEST, preferred_element_type=F32)
    lane = lax.broadcasted_iota(jnp.int32, logits.shape, 1).astype(F32)
    logits = jnp.where(lane < N_EXPERTS, logits, -jnp.inf)
    m1 = jnp.max(logits, axis=-1, keepdims=True)
    i1 = jnp.min(jnp.where(logits == m1, lane, float(LANES)), axis=-1, keepdims=True)
    rest = jnp.where(lane == i1, -jnp.inf, logits)
    m2 = jnp.max(rest, axis=-1, keepdims=True)
    i2 = jnp.min(jnp.where(rest == m2, lane, float(LANES)), axis=-1, keepdims=True)
    e = jnp.exp(m2 - m1)
    den = 1.0 + e
    e_ref[...] = jnp.where(lane == 0.0, i1, jnp.where(lane == 1.0, i2, 0.0)).astype(jnp.int32)
    gate_ref[...] = jnp.where(lane == 0.0, 1.0 / den, jnp.where(lane == 1.0, e / den, 0.0))


def _router_call(x, g, wr_pad):
    n, d = x.shape
    tm = ROW_TILE
    row = lambda w: pl.BlockSpec((tm, w), lambda i: (i, 0))
    full = lambda a: pl.BlockSpec(a.shape, lambda i: (0,) * a.ndim)
    return pl.pallas_call(
        _router_kernel,
        grid=(n // tm,),
        in_specs=[row(d), full(g), full(wr_pad)],
        out_specs=[pl.BlockSpec((tm * TOKEN_ROWS, LANES), lambda i: (i, 0)), row(LANES), row(LANES)],
        out_shape=[jax.ShapeDtypeStruct((n * TOKEN_ROWS, LANES), F32), jax.ShapeDtypeStruct((n, LANES), jnp.int32),
                   jax.ShapeDtypeStruct((n, LANES), F32)],
        compiler_params=_cparams(("parallel",)),
        name="router",
    )(x, g, wr_pad)


TOKEN_ROWS = 8


def _to_token_tiles(dst_ref, x):
    tm, d = x.shape
    assert d == TOKEN_ROWS * LANES
    for j in range(TOKEN_ROWS):
        dst_ref[pl.ds(j, tm, stride=TOKEN_ROWS), :] = x[:, j * LANES:(j + 1) * LANES]


def _from_token_tiles(src_ref, tm):
    return jnp.concatenate([src_ref[pl.ds(j, tm, stride=TOKEN_ROWS), :] for j in range(TOKEN_ROWS)], axis=-1)


def _row_copy(src_hbm, row, dst_ref, r, sem):
    src = src_hbm.at[pl.ds(pl.multiple_of(row * TOKEN_ROWS, TOKEN_ROWS), TOKEN_ROWS)]
    return pltpu.make_async_copy(src, dst_ref.at[pl.ds(pl.multiple_of(r * TOKEN_ROWS, TOKEN_ROWS), TOKEN_ROWS)], sem)


def _start_row_gather(idx_ref, base, src_hbm, dst_ref, sem, count, priorities=1):
    def body(g, c):
        for p in range(priorities):
            r = g * priorities + p
            _row_copy(src_hbm, idx_ref[base + r], dst_ref, r, sem).start(priority=p)
        return c
    lax.fori_loop(0, count // priorities, body, 0, unroll=GATHER_UNROLL // priorities)


def _wait_row_gather(src_hbm, dst_ref, sem, count):
    pltpu.make_async_copy(src_hbm.at[pl.ds(0, count * TOKEN_ROWS)], dst_ref, sem).wait()


def _moe_ffn_kernel(tile_e_ref, tile_ok_ref, row_tok_ref, h_hbm, w1_ref, w3_ref, w2_ref, out_ref,
                    xg_ref, acc_ref, sem):
    i = pl.program_id(0)
    f = pl.program_id(1)
    n_tiles = pl.num_programs(0)
    tm = MOE_TM
    slot = i % 2

    @pl.when(f == 0)
    def _():
        @pl.when(i == 0)
        def _():
            acc_ref[...] = jnp.zeros_like(acc_ref)
            _start_row_gather(row_tok_ref, 0, h_hbm, xg_ref.at[0], sem.at[0], tm, priorities=2)

        @pl.when(tile_ok_ref[i] != 0)
        def _():
            _wait_row_gather(h_hbm, xg_ref.at[slot], sem.at[slot], tm)

        @pl.when(jnp.logical_and(i + 1 < n_tiles, tile_ok_ref[jnp.minimum(i + 1, n_tiles - 1)] != 0))
        def _():
            _start_row_gather(row_tok_ref, (i + 1) * tm, h_hbm, xg_ref.at[1 - slot], sem.at[1 - slot], tm,
                              priorities=2)

    @pl.when(tile_ok_ref[i] != 0)
    def _():
        h = _from_token_tiles(xg_ref.at[slot], tm).astype(BF16)
        a = (_silu(_dot(h, w1_ref[...])) * _dot(h, w3_ref[...])).astype(BF16)
        total = _dot(a, w2_ref[...]) + jnp.where(f > 0, acc_ref[...], 0.0)
        acc_ref[...] = total
        _to_token_tiles(out_ref, total)

    @pl.when(tile_ok_ref[i] == 0)
    def _():
        out_ref[...] = jnp.zeros_like(out_ref)


def _moe_ffn_call(tile_e, tile_ok, row_tok, h, w1, w3, w2, layer):
    d = w1.shape[2]
    n_tiles = tile_e.shape[0]
    ff = w1.shape[3]
    tm, tf = MOE_TM, MOE_TF

    def ff_block(i, f, tile_ok):
        return jnp.where(tile_ok[i] != 0, f, ff // tf - 1)

    grid_spec = pltpu.PrefetchScalarGridSpec(
        num_scalar_prefetch=3,
        grid=(n_tiles, ff // tf),
        in_specs=[pl.BlockSpec(memory_space=pl.ANY),
                  pl.BlockSpec((None, None, d, tf), lambda i, f, te, tv, rt: (layer, te[i], 0, ff_block(i, f, tv))),
                  pl.BlockSpec((None, None, d, tf), lambda i, f, te, tv, rt: (layer, te[i], 0, ff_block(i, f, tv))),
                  pl.BlockSpec((None, None, tf, d), lambda i, f, te, tv, rt: (layer, te[i], ff_block(i, f, tv), 0))],
        out_specs=pl.BlockSpec((tm * TOKEN_ROWS, LANES), lambda i, f, te, tv, rt: (i, 0)),
        scratch_shapes=[pltpu.VMEM((2, tm * TOKEN_ROWS, LANES), F32), pltpu.VMEM((tm, d), F32),
                        pltpu.SemaphoreType.DMA((2,))],
    )
    return pl.pallas_call(
        _moe_ffn_kernel,
        grid_spec=grid_spec,
        out_shape=jax.ShapeDtypeStruct((n_tiles * tm * TOKEN_ROWS, LANES), F32),
        compiler_params=_cparams(("arbitrary", "arbitrary")),
        name="moe_ffn",
    )(tile_e, tile_ok, row_tok, h, w1, w3, w2)


def _combine_kernel(d0_ref, d1_ref, y_hbm, x_ref, gate_ref, gain_ref, out_ref, b0_ref, b1_ref, sem, *, final_norm):
    i = pl.program_id(0)
    n_tiles = pl.num_programs(0)
    tm = COMBINE_TM
    slot = i % 2

    def start(tile, s):
        _start_row_gather(d0_ref, tile * tm, y_hbm, b0_ref.at[s], sem.at[0, s], tm, priorities=2)
        _start_row_gather(d1_ref, tile * tm, y_hbm, b1_ref.at[s], sem.at[1, s], tm, priorities=2)

    @pl.when(i == 0)
    def _():
        start(0, 0)

    _wait_row_gather(y_hbm, b0_ref.at[slot], sem.at[0, slot], tm)
    _wait_row_gather(y_hbm, b1_ref.at[slot], sem.at[1, slot], tm)

    @pl.when(i + 1 < n_tiles)
    def _():
        start(i + 1, 1 - slot)

    gates = gate_ref[...]
    y0 = _from_token_tiles(b0_ref.at[slot], tm)
    y1 = _from_token_tiles(b1_ref.at[slot], tm)
    out = x_ref[...] + (gates[:, 0:1] * y0 + gates[:, 1:2] * y1)
    out_ref[...] = _rms(out, gain_ref[...]) if final_norm else out


def _combine_call(d0, d1, y, x, gates, gain, final_norm):
    n, d = x.shape
    tm = COMBINE_TM
    grid_spec = pltpu.PrefetchScalarGridSpec(
        num_scalar_prefetch=2,
        grid=(n // tm,),
        in_specs=[pl.BlockSpec(memory_space=pl.ANY),
                  pl.BlockSpec((tm, d), lambda i, a, b: (i, 0)),
                  pl.BlockSpec((tm, LANES), lambda i, a, b: (i, 0)),
                  pl.BlockSpec((1, d), lambda i, a, b: (0, 0))],
        out_specs=pl.BlockSpec((tm, d), lambda i, a, b: (i, 0)),
        scratch_shapes=[pltpu.VMEM((2, tm * TOKEN_ROWS, LANES), F32)] * 2 + [pltpu.SemaphoreType.DMA((2, 2))],
    )
    return pl.pallas_call(
        functools.partial(_combine_kernel, final_norm=final_norm),
        grid_spec=grid_spec,
        out_shape=jax.ShapeDtypeStruct((n, d), F32),
        compiler_params=_cparams(("arbitrary",)),
        name="moe_combine",
    )(d0, d1, y, x, gates, gain)


def _moe_plan(top_e, n_tiles):
    n_assign = top_e.shape[0] * TOP_K
    flat_e = top_e.reshape(n_assign)
    onehot = (flat_e[:, None] == jnp.arange(N_EXPERTS, dtype=jnp.int32)[None, :]).astype(jnp.int32)
    csum = jnp.cumsum(onehot, axis=0)
    rank = jnp.take_along_axis(csum, flat_e[:, None], axis=1)[:, 0] - 1
    counts = csum[-1]
    padded = (counts + MOE_TM - 1) // MOE_TM * MOE_TM
    pend = jnp.cumsum(padded)
    pstart = pend - padded
    dest = pstart[flat_e] + rank
    tile_start = jnp.arange(n_tiles, dtype=jnp.int32) * MOE_TM
    tile_e = jnp.minimum(jnp.searchsorted(pend, tile_start, side="right"), N_EXPERTS - 1).astype(jnp.int32)
    tile_ok = (tile_start < pend[-1]).astype(jnp.int32)
    order = jnp.argsort(flat_e, stable=True).astype(jnp.int32)
    row_e = jnp.repeat(tile_e, MOE_TM)
    r = jnp.arange(n_tiles * MOE_TM, dtype=jnp.int32) - pstart[row_e]
    src = jnp.clip((jnp.cumsum(counts) - counts)[row_e] + r, 0, n_assign - 1)
    row_tok = jnp.where(r < counts[row_e], order[src] // TOP_K, 0).astype(jnp.int32)
    dest = dest.reshape(-1, TOP_K).astype(jnp.int32)
    return tile_e, tile_ok, row_tok, dest[:, 0], dest[:, 1]


def _moe_layer(x, g, wr_pad, w1, w3, w2, layer, gain, final_norm):
    n = x.shape[0]
    h, top_e, gates = _router_call(x, g, wr_pad)
    n_tiles = -(-(n * TOP_K + N_EXPERTS * (MOE_TM - 1)) // MOE_TM)
    tile_e, tile_ok, row_tok, d0, d1 = _moe_plan(top_e[:, :TOP_K], n_tiles)
    y = _moe_ffn_call(tile_e, tile_ok, row_tok, h, w1, w3, w2, layer)
    return _combine_call(d0, d1, y, x, gates, gain, final_norm)


def _norm_kernel(x_ref, g_ref, o_ref):
    o_ref[...] = _rms(x_ref[...], g_ref[...])


def _norm_call(x, g):
    n, d = x.shape
    tm = ROW_TILE
    return pl.pallas_call(
        _norm_kernel,
        grid=(n // tm,),
        in_specs=[pl.BlockSpec((tm, d), lambda i: (i, 0)), pl.BlockSpec((1, d), lambda i: (0, 0))],
        out_specs=pl.BlockSpec((tm, d), lambda i: (i, 0)),
        out_shape=jax.ShapeDtypeStruct((n, d), F32),
        compiler_params=_cparams(("parallel",)),
        name="final_norm",
    )(x, g)


def _rope_tables(seq):
    inv = ROPE_BASE ** (-jnp.arange(0, QK_ROPE, 2, dtype=F32) / QK_ROPE)
    ang = jnp.arange(seq)[:, None].astype(F32) * inv[None, :]
    cos, sin = jnp.cos(ang), jnp.sin(ang)
    pad = LANES - QK_NOPE - QK_ROPE
    zeros = lambda w: jnp.zeros((seq, w), F32)
    ctab = jnp.concatenate([jnp.ones((seq, QK_NOPE), F32), cos, cos, zeros(pad)], axis=1)
    stab = jnp.concatenate([zeros(QK_NOPE), sin, sin, zeros(pad)], axis=1)
    sa = jnp.concatenate([zeros(QK_NOPE), -sin, zeros(QK_ROPE // 2 + pad)], axis=1)
    sb = jnp.concatenate([zeros(QK_NOPE + QK_ROPE // 2), sin, zeros(pad)], axis=1)
    return ctab, stab, sa, sb


def _split_w_in(w_in):
    d = w_in.shape[0]
    bounds = np.cumsum([WIDTH_A, WIDTH_A, WIDTH_A, Q_LORA, KV_LORA, QK_ROPE, WIDTH_C, WIDTH_C])
    qa, ka, va, cq, ckv, kr, qc, kc, vc = jnp.split(w_in, bounds.tolist(), axis=1)
    half = QK_ROPE // 2
    z_lo = jnp.zeros((d, QK_NOPE), w_in.dtype)
    z_hi = jnp.zeros((d, LANES - QK_NOPE - QK_ROPE), w_in.dtype)
    rope_blk = jnp.concatenate([z_lo, kr, z_hi], axis=1)
    swap_blk = jnp.concatenate([z_lo, -kr[:, half:], kr[:, :half], z_hi], axis=1)
    score_scale = HEAD_DIM ** -0.5
    wa = jnp.concatenate([qa * score_scale, ka, va], axis=1).astype(BF16)
    wc = jnp.concatenate([qc * score_scale, kc, vc], axis=1).astype(BF16)
    wb = jnp.concatenate([cq, ckv, rope_blk, swap_blk], axis=1).astype(BF16)
    return jnp.concatenate([wa, wc, wb], axis=1)


def _split_w_uq(w_uq):
    r = w_uq.shape[0]
    w = w_uq.reshape(r, N_HEADS_B, QK_NOPE + QK_ROPE)
    nope, rope = w[..., :QK_NOPE], w[..., QK_NOPE:]
    z_hi = jnp.zeros((r, N_HEADS_B, LANES - QK_NOPE - QK_ROPE), w_uq.dtype)
    return jnp.concatenate([nope, rope, z_hi], axis=-1).reshape(r, -1).astype(BF16)


def _split_w_ukv(w_ukv):
    r = w_ukv.shape[0]
    w = w_ukv.reshape(r, N_HEADS_B, QK_NOPE + V_HEAD)
    k_nope, v = w[..., :QK_NOPE], w[..., QK_NOPE:]
    wk = jnp.concatenate([k_nope, jnp.zeros((r, N_HEADS_B, LANES - QK_NOPE), w_ukv.dtype)], axis=-1)
    return wk.reshape(r, -1).astype(BF16), v.reshape(r, -1).T.astype(BF16)


def kernel(x, g_mix, w_in, g_q, g_kv, w_uq, w_ukv, rpb, g_out_a, g_out_b, g_out_c, w_o, g_ffn, w1, w3, w2,
           w_router, e_w1, e_w3, e_w2, g_final):
    batch, seq, d = x.shape
    n = batch * seq
    depth = g_mix.shape[0]
    rows = seq // GRID_W
    rope_tabs = _rope_tables(seq)
    ew1, ew3, ew2 = e_w1.astype(BF16), e_w3.astype(BF16), e_w2.astype(BF16)
    xf = x.reshape(n, d)
    for layer in range(depth):
        win = _split_w_in(w_in[layer])
        wq = _split_w_uq(w_uq[layer])
        wk, wvt = _split_w_ukv(w_ukv[layer])
        pa, pa4, pa16, pc, qm, km, vt = _proj_call(xf, g_mix[layer][None], win, g_q[layer][None], g_kv[layer][None],
                                                   wq, wk, wvt, rope_tabs, seq)
        o_parts, lse_parts = [], []
        for dil, view in zip(DILATIONS, (pa, pa4, pa16)):
            o, lse = _band_call(view.reshape(batch, seq // dil, dil * 3 * WIDTH_A), dil)
            o_parts.append(o.reshape(n // dil, dil * WIDTH_A))
            lse_parts.append(lse.reshape(n // dil, dil * WIDTH_A))
        ob = _mla_call(qm, km, vt, batch, seq).reshape(n, WIDTH_B)
        oc = _na_call(pc, _na_tables(rpb[layer], rows), batch, seq).reshape(n, WIDTH_C)
        xf = _mix_out_call(xf, o_parts, lse_parts, ob, oc, g_out_a[layer][None], g_out_b[layer][None],
                           g_out_c[layer][None], w_o[layer].astype(BF16))
        j = layer // 2
        if layer % 2 == 0:
            xf = _ffn_call(xf, g_ffn[layer][None], w1[j].astype(BF16), w3[j].astype(BF16), w2[j].astype(BF16))
        else:
            wr_pad = jnp.pad(w_router[j], ((0, 0), (0, LANES - N_EXPERTS)))
            xf = _moe_layer(xf, g_ffn[layer][None], wr_pad, ew1, ew3, ew2, j, g_final[None], layer == depth - 1)
    if depth % 2 == 1:
        xf = _norm_call(xf, g_final[None])
    return xf.reshape(batch, seq, d)
```
